```python
import jax, jax.numpy as jnp
from jax import lax
import numpy as np

D_MODEL = 2048
BATCH = 4
SEQ = 2048
DEPTH = 1
DEC_BATCH = 32
DEC_SEQ = 32
PAST_LEN = 1024

CHUNK = 64
MIX_WIDTH = D_MODEL
ATTN_WIDTH = MIX_WIDTH // 2
GMLP_WIDTH = MIX_WIDTH - ATTN_WIDTH
N_HEADS = 8
NOPE_DIM = 128
ROPE_DIM = 64
QK_DIM = NOPE_DIM + ROPE_DIM
V_DIM = ATTN_WIDTH // N_HEADS
Q_RANK = 512
KV_RANK = 256
GMLP_GROUPS = 8
GMLP_GROUP_DIM = GMLP_WIDTH // GMLP_GROUPS
GMLP_CHUNK = 128
D_FF = 4 * D_MODEL
IN_WIDTH = Q_RANK + KV_RANK + ROPE_DIM + 2 * GMLP_WIDTH
ROPE_THETA = 10000.0
EPS = 1e-6
Q_BLOCK = 128

kernel_name = "hymba_mla_gmlp_stream_step"


def rmsnorm(x, g):
    xf = x.astype(jnp.float32)
    y = xf * lax.rsqrt(jnp.mean(jnp.square(xf), axis=-1, keepdims=True) + EPS)
    return (y * g.astype(jnp.float32)).astype(x.dtype)


def rope(x, pos):
    half = ROPE_DIM // 2
    inv = ROPE_THETA ** (-jnp.arange(half, dtype=jnp.float32) / half)
    ang = pos.astype(jnp.float32)[:, None] * inv[None, :]
    ang = ang.reshape(ang.shape[:1] + (1,) * (x.ndim - 3) + (half,))
    cos, sin = jnp.cos(ang), jnp.sin(ang)
    xf = x.astype(jnp.float32)
    x1, x2 = xf[..., :half], xf[..., half:]
    return jnp.concatenate([x1 * cos - x2 * sin, x1 * sin + x2 * cos], axis=-1).astype(x.dtype)


def qk_gain(g_nope, g_rope):
    return jnp.concatenate([g_nope, g_rope, g_rope], axis=-1)


def project(x, pos, p):
    b, t, _ = x.shape
    z = rmsnorm(x, p["norm_mix"]) @ p["w_in"]
    s1 = Q_RANK
    s2 = s1 + KV_RANK
    s3 = s2 + ROPE_DIM
    s4 = s3 + GMLP_WIDTH
    q_lat, kv_lat, k_pe, u, v = jnp.split(z, [s1, s2, s3, s4], axis=-1)
    q = (rmsnorm(q_lat, p["q_lat_norm"]) @ p["w_uq"]).reshape(b, t, N_HEADS, QK_DIM)
    q = jnp.concatenate([q[..., :NOPE_DIM], rope(q[..., NOPE_DIM:], pos)], axis=-1)
    q = rmsnorm(q, qk_gain(p["q_norm_nope"], p["q_norm_rope"]))
    c_kv = rmsnorm(kv_lat, p["kv_lat_norm"])
    k_pe = rope(k_pe, pos)
    u = jax.nn.gelu(u).reshape(b, t, GMLP_GROUPS, GMLP_GROUP_DIM)
    v = rmsnorm(jax.nn.gelu(v).reshape(b, t, GMLP_GROUPS, GMLP_GROUP_DIM), p["v_norm"])
    return q, c_kv, k_pe, u, v


def expand_kv(c_kv, k_pe, p):
    b, l, _ = c_kv.shape
    k_nope = (c_kv @ p["w_uk"]).reshape(b, l, N_HEADS, NOPE_DIM)
    k_rot = jnp.broadcast_to(k_pe[:, :, None, :], (b, l, N_HEADS, ROPE_DIM))
    k = rmsnorm(jnp.concatenate([k_nope, k_rot], axis=-1), qk_gain(p["k_norm_nope"], p["k_norm_rope"]))
    v = (c_kv @ p["w_uv"]).reshape(b, l, N_HEADS, V_DIM)
    return k, v


def attend(q, k, v, q_pos, k_pos):
    s = jnp.einsum("bqhd,bkhd->bhqk", q, k).astype(jnp.float32) * (QK_DIM ** -0.5)
    mask = (k_pos[None, :] // CHUNK) <= (q_pos[:, None] // CHUNK)
    s = jnp.where(mask[None, None], s, jnp.finfo(jnp.float32).min)
    w = jax.nn.softmax(s, axis=-1).astype(v.dtype)
    return jnp.einsum("bhqk,bkhd->bqhd", w, v)


def spatial_gate(u, v, w_s, b_s):
    l = v.shape[2]
    w = jnp.tril(w_s[:, :l, :l])
    s = jnp.einsum("gts,bnsgc->bntgc", w, v) + b_s[:, :l].T[None, None, :, :, None]
    return u * s


def finish(x, attn_o, gmlp_o, p):
    m = jnp.concatenate([rmsnorm(attn_o, p["out_norm_attn"]), rmsnorm(gmlp_o, p["out_norm_gmlp"])], axis=-1)
    h = x + m @ p["w_out"]
    f = jnp.square(jax.nn.relu(rmsnorm(h, p["norm_ffn"]) @ p["w_up"])) @ p["w_down"]
    return h + f


def layer_prompt(x, p):
    b, t, _ = x.shape
    pos = jnp.arange(t, dtype=jnp.int32)
    q, c_kv, k_pe, u, v = project(x, pos, p)
    k, vv = expand_kv(c_kv, k_pe, p)
    nb = t // Q_BLOCK
    qb = q.reshape(b, nb, Q_BLOCK, N_HEADS, QK_DIM).transpose(1, 0, 2, 3, 4)
    pb = pos.reshape(nb, Q_BLOCK)
    o = lax.map(lambda a: attend(a[0], k, vv, a[1], pos), (qb, pb))
    attn_o = o.transpose(1, 0, 2, 3, 4).reshape(b, t, ATTN_WIDTH)
    nc = t // GMLP_CHUNK
    shp = (b, nc, GMLP_CHUNK, GMLP_GROUPS, GMLP_GROUP_DIM)
    gmlp_o = spatial_gate(u.reshape(shp), v.reshape(shp), p["w_spatial"], p["b_spatial"]).reshape(b, t, GMLP_WIDTH)
    return finish(x, attn_o, gmlp_o, p), c_kv, k_pe


def layer_sample(x, cache_c_kv, cache_k_rope, p):
    b, t, _ = x.shape
    past = cache_c_kv.shape[1]
    q_pos = past + jnp.arange(t, dtype=jnp.int32)
    k_pos = jnp.arange(past + t, dtype=jnp.int32)
    q, c_kv, k_pe, u, v = project(x, q_pos, p)
    all_c = jnp.concatenate([cache_c_kv.astype(c_kv.dtype), c_kv], axis=1)
    all_pe = jnp.concatenate([cache_k_rope.astype(k_pe.dtype), k_pe], axis=1)
    k, vv = expand_kv(all_c, all_pe, p)
    attn_o = attend(q, k, vv, q_pos, k_pos).reshape(b, t, ATTN_WIDTH)
    gmlp_o = spatial_gate(u[:, None], v[:, None], p["w_spatial"], p["b_spatial"]).reshape(b, t, GMLP_WIDTH)
    return finish(x, attn_o, gmlp_o, p), c_kv, k_pe, v.reshape(b, t, GMLP_WIDTH)


def setup_inputs(seed: int = 0) -> dict:
    key = jax.random.key(seed)
    ks = list(jax.random.split(key, 32))
    f32 = jnp.float32

    def w(k, shape, fan_in):
        return jax.random.normal(k, (DEPTH,) + shape, f32) * (fan_in ** -0.5)

    def gain(k, shape):
        return 1.0 + 0.05 * jax.random.normal(k, (DEPTH,) + shape, f32)

    return {
        "x_prompt": jax.random.normal(ks[0], (BATCH, SEQ, D_MODEL), f32),
        "x_sample": jax.random.normal(ks[1], (DEC_BATCH, DEC_SEQ, D_MODEL), f32),
        "cache_c_kv": jax.random.normal(ks[2], (DEPTH, DEC_BATCH, PAST_LEN, KV_RANK), f32),
        "cache_k_rope": jax.random.normal(ks[3], (DEPTH, DEC_BATCH, PAST_LEN, ROPE_DIM), f32),
        "norm_mix": gain(ks[4], (D_MODEL,)),
        "w_in": w(ks[5], (D_MODEL, IN_WIDTH), D_MODEL),
        "q_lat_norm": gain(ks[6], (Q_RANK,)),
        "kv_lat_norm": gain(ks[7], (KV_RANK,)),
        "w_uq": w(ks[8], (Q_RANK, N_HEADS * QK_DIM), Q_RANK),
        "w_uk": w(ks[9], (KV_RANK, N_HEADS * NOPE_DIM), KV_RANK),
        "w_uv": w(ks[10], (KV_RANK, N_HEADS * V_DIM), KV_RANK),
        "q_norm_nope": gain(ks[11], (NOPE_DIM,)),
        "q_norm_rope": gain(ks[12], (ROPE_DIM // 2,)),
        "k_norm_nope": gain(ks[13], (NOPE_DIM,)),
        "k_norm_rope": gain(ks[14], (ROPE_DIM // 2,)),
        "v_norm": gain(ks[15], (GMLP_GROUPS, GMLP_GROUP_DIM)),
        "w_spatial": w(ks[16], (GMLP_GROUPS, GMLP_CHUNK, GMLP_CHUNK), GMLP_CHUNK),
        "b_spatial": gain(ks[17], (GMLP_GROUPS, GMLP_CHUNK)),
        "out_norm_attn": gain(ks[18], (ATTN_WIDTH,)),
        "out_norm_gmlp": gain(ks[19], (GMLP_WIDTH,)),
        "w_out": w(ks[20], (MIX_WIDTH, D_MODEL), MIX_WIDTH),
        "norm_ffn": gain(ks[21], (D_MODEL,)),
        "w_up": w(ks[22], (D_MODEL, D_FF), D_MODEL),
        "w_down": w(ks[23], (D_FF, D_MODEL), D_FF),
    }


def reference(x_prompt, x_sample, cache_c_kv, cache_k_rope, norm_mix, w_in, q_lat_norm, kv_lat_norm,
              w_uq, w_uk, w_uv, q_norm_nope, q_norm_rope, k_norm_nope, k_norm_rope, v_norm,
              w_spatial, b_spatial, out_norm_attn, out_norm_gmlp, w_out, norm_ffn, w_up, w_down):
    xp, xs = x_prompt, x_sample
    ckv_p, kpe_p, ckv_s, kpe_s, v_s = [], [], [], [], []
    for i in range(DEPTH):
        p = {
            "norm_mix": norm_mix[i], "w_in": w_in[i], "q_lat_norm": q_lat_norm[i],
            "kv_lat_norm": kv_lat_norm[i], "w_uq": w_uq[i], "w_uk": w_uk[i], "w_uv": w_uv[i],
            "q_norm_nope": q_norm_nope[i], "q_norm_rope": q_norm_rope[i],
            "k_norm_nope": k_norm_nope[i], "k_norm_rope": k_norm_rope[i], "v_norm": v_norm[i],
            "w_spatial": w_spatial[i], "b_spatial": b_spatial[i],
            "out_norm_attn": out_norm_attn[i], "out_norm_gmlp": out_norm_gmlp[i],
            "w_out": w_out[i], "norm_ffn": norm_ffn[i], "w_up": w_up[i], "w_down": w_down[i],
        }
        xp, c_p, k_p = layer_prompt(xp, p)
        xs, c_s, k_s, vrow = layer_sample(xs, cache_c_kv[i], cache_k_rope[i], p)
        ckv_p.append(c_p)
        kpe_p.append(k_p)
        ckv_s.append(c_s)
        kpe_s.append(k_s)
        v_s.append(vrow)
    new_c_kv_prompt = jnp.stack(ckv_p)
    new_k_rope_prompt = jnp.stack(kpe_p)
    new_c_kv_sample = jnp.stack(ckv_s)
    new_k_rope_sample = jnp.stack(kpe_s)
    new_gmlp_v_sample = jnp.stack(v_s)
    return (xp, xs, new_c_kv_prompt, new_k_rope_prompt, new_c_kv_sample, new_k_rope_sample, new_gmlp_v_sample)
```

```python
import functools
import math

import jax
import jax.numpy as jnp
from jax import lax
from jax.experimental import pallas as pl
from jax.experimental.pallas import tpu as pltpu

D_MODEL = 2048
CHUNK = 64
N_HEADS = 8
NOPE_DIM = 128
ROPE_DIM = 64
HALF_ROPE = ROPE_DIM // 2
QK_DIM = NOPE_DIM + ROPE_DIM
V_DIM = 128
ATTN_WIDTH = N_HEADS * V_DIM
Q_RANK = 512
KV_RANK = 256
GMLP_GROUPS = 8
GMLP_GROUP_DIM = 128
GMLP_WIDTH = GMLP_GROUPS * GMLP_GROUP_DIM
GMLP_CHUNK = 128
D_FF = 4 * D_MODEL
ROPE_THETA = 10000.0
EPS = 1e-6

LANES = 128
HEAD_PAD = 2 * LANES
QK_WIDTH = N_HEADS * HEAD_PAD
C_Q0, C_Q1 = 0, Q_RANK
C_KV0, C_KV1 = C_Q1, C_Q1 + KV_RANK
C_PE0, C_PE1 = C_KV1, C_KV1 + LANES
C_U0, C_U1 = C_PE1, C_PE1 + GMLP_WIDTH
C_V0, C_V1 = C_U1, C_U1 + GMLP_WIDTH
IN_PAD = C_V1
VMEM_LIMIT = 56 * 1024 * 1024

BF16 = jnp.bfloat16
F32 = jnp.float32


def _dot(a, b):
    return jnp.dot(a, b, preferred_element_type=F32)


def _dot_nt(a, b):
    return lax.dot_general(a, b, (((1,), (1,)), ((), ())), preferred_element_type=F32)


def _sumsq(x):
    return jnp.sum(x * x, axis=-1, keepdims=True)


def _rms(x):
    return x * lax.rsqrt(jnp.mean(x * x, axis=-1, keepdims=True) + EPS)


def _gelu(x):
    c = math.sqrt(2.0 / math.pi)
    return 0.5 * x * (1.0 + jnp.tanh(c * (x + 0.044715 * (x * x * x))))


def _rope(t, cos, s1, s2):
    return t * cos + pltpu.roll(t, LANES - HALF_ROPE, 1) * s1 + pltpu.roll(t, HALF_ROPE, 1) * s2


def _const_spec(shape):
    nd = len(shape)
    return pl.BlockSpec(shape, lambda *_: (0,) * nd, pipeline_mode=pl.Buffered(1))


def _proj_kernel(chunk_len, emit_v, x_ref, gmix_ref, win_ref, gql_ref, gkv_ref, wuq_ref, wuk_ref,
                 wuv_ref, gq_ref, gk_ref, cos_ref, s1_ref, s2_ref, vn_ref, ws_ref, bst_ref, gog_ref,
                 q_ref, k_ref, v_ref, ckv_ref, kpe_ref, gm_ref, *rest):
    if emit_v:
        vg_ref, gate_scr, vgb_scr = rest
    else:
        vg_ref = None
        gate_scr, vgb_scr = rest
    tm = x_ref.shape[0]
    xn = (_rms(x_ref[...]) * gmix_ref[...]).astype(BF16)
    cos, s1, s2 = cos_ref[...], s1_ref[...], s2_ref[...]

    zq = _dot(xn, win_ref[:, C_Q0:C_Q1])
    qln = (_rms(zq) * gql_ref[...]).astype(BF16)
    qraw = _dot(qln, wuq_ref[...])
    for h in range(N_HEADS):
        lo = h * HEAD_PAD
        nope = qraw[:, lo:lo + LANES]
        rp = qraw[:, lo + LANES:lo + HEAD_PAD]
        r = lax.rsqrt((_sumsq(nope) + _sumsq(rp)) * (1.0 / QK_DIM) + EPS)
        rp = _rope(rp, cos, s1, s2)
        q_ref[:, lo:lo + LANES] = (nope * r * gq_ref[:, :LANES]).astype(BF16)
        q_ref[:, lo + LANES:lo + HEAD_PAD] = (rp * r * gq_ref[:, LANES:]).astype(BF16)

    ckv = _rms(_dot(xn, win_ref[:, C_KV0:C_KV1])) * gkv_ref[...]
    ckv_ref[...] = ckv
    cb = ckv.astype(BF16)
    pe = _rope(_dot(xn, win_ref[:, C_PE0:C_PE1]), cos, s1, s2)
    kpe_ref[...] = pe[:, :ROPE_DIM]
    ss_pe = _sumsq(pe)
    knope = _dot(cb, wuk_ref[...])
    v_ref[...] = _dot(cb, wuv_ref[...]).astype(BF16)
    for h in range(N_HEADS):
        nope = knope[:, h * LANES:(h + 1) * LANES]
        r = lax.rsqrt((_sumsq(nope) + ss_pe) * (1.0 / QK_DIM) + EPS)
        lo = h * HEAD_PAD
        k_ref[:, lo:lo + LANES] = (nope * r * gk_ref[:, :LANES]).astype(BF16)
        k_ref[:, lo + LANES:lo + HEAD_PAD] = (pe * r * gk_ref[:, LANES:]).astype(BF16)

    gv = _gelu(_dot(xn, win_ref[:, C_V0:C_V1]))
    for g in range(GMLP_GROUPS):
        blk = gv[:, g * LANES:(g + 1) * LANES]
        vg = _rms(blk) * vn_ref[g:g + 1, :]
        if emit_v:
            vg_ref[:, g * LANES:(g + 1) * LANES] = vg
        vgb_scr[:, g * LANES:(g + 1) * LANES] = vg.astype(BF16)
    u = _gelu(_dot(xn, win_ref[:, C_U0:C_U1]))
    row = lax.broadcasted_iota(jnp.int32, (GMLP_CHUNK, GMLP_CHUNK), 0)
    col = lax.broadcasted_iota(jnp.int32, (GMLP_CHUNK, GMLP_CHUNK), 1)
    causal = (row // chunk_len == col // chunk_len) & (col <= row)
    for g in range(GMLP_GROUPS):
        wm = jnp.where(causal, ws_ref[g], 0.0).astype(BF16)
        bias = bst_ref[:, g:g + 1]
        for c in range(tm // GMLP_CHUNK):
            rows = slice(c * GMLP_CHUNK, (c + 1) * GMLP_CHUNK)
            cols = slice(g * LANES, (g + 1) * LANES)
            s = _dot(wm, vgb_scr[rows, cols]) + bias
            gate_scr[rows, cols] = u[rows, cols] * s
    gm_ref[...] = (_rms(gate_scr[...]) * gog_ref[...]).astype(BF16)


def _proj(x2d, tabs, wp, chunk_len, emit_v, tm):
    m = x2d.shape[0]
    cos, s1, s2 = tabs
    tab_blocks = cos.shape[0] // tm
    row_spec = lambda w: pl.BlockSpec((tm, w), lambda i: (i, 0))
    tab_spec = pl.BlockSpec((tm, LANES), lambda i: (i % tab_blocks, 0))
    consts = [wp["gmix"], wp["w_in"], wp["gql"], wp["gkv"], wp["w_uq"], wp["w_uk"], wp["w_uv"],
              wp["gq"], wp["gk"]]
    consts2 = [wp["v_norm"], wp["ws"], wp["bst"], wp["gog"]]
    in_specs = ([row_spec(D_MODEL)] + [_const_spec(c.shape) for c in consts]
                + [tab_spec] * 3 + [_const_spec(c.shape) for c in consts2])
    out_shape = [jax.ShapeDtypeStruct((m, QK_WIDTH), BF16), jax.ShapeDtypeStruct((m, QK_WIDTH), BF16),
                 jax.ShapeDtypeStruct((m, ATTN_WIDTH), BF16), jax.ShapeDtypeStruct((m, KV_RANK), F32),
                 jax.ShapeDtypeStruct((m, ROPE_DIM), F32), jax.ShapeDtypeStruct((m, GMLP_WIDTH), BF16)]
    out_specs = [row_spec(QK_WIDTH), row_spec(QK_WIDTH), row_spec(ATTN_WIDTH), row_spec(KV_RANK),
                 row_spec(ROPE_DIM), row_spec(GMLP_WIDTH)]
    if emit_v:
        out_shape.append(jax.ShapeDtypeStruct((m, GMLP_WIDTH), F32))
        out_specs.append(row_spec(GMLP_WIDTH))
    return pl.pallas_call(
        functools.partial(_proj_kernel, chunk_len, emit_v),
        out_shape=out_shape,
        grid=(m // tm,),
        in_specs=in_specs,
        out_specs=out_specs,
        scratch_shapes=[pltpu.VMEM((tm, GMLP_WIDTH), F32), pltpu.VMEM((tm, GMLP_WIDTH), BF16)],
        compiler_params=pltpu.CompilerParams(dimension_semantics=("parallel",),
                                             vmem_limit_bytes=VMEM_LIMIT),
        name="proj",
    )(x2d, *consts, cos, s1, s2, *consts2)


def _expand_kernel(c_ref, kr_ref, wuk_ref, wuv_ref, gk_ref, k_ref, v_ref):
    cb = c_ref[...].astype(BF16)
    kr = kr_ref[...]
    ss_pe = _sumsq(kr)
    knope = _dot(cb, wuk_ref[...])
    v_ref[...] = _dot(cb, wuv_ref[...]).astype(BF16)
    zeros = jnp.zeros((c_ref.shape[0], HEAD_PAD - QK_DIM), BF16)
    for h in range(N_HEADS):
        nope = knope[:, h * LANES:(h + 1) * LANES]
        r = lax.rsqrt((_sumsq(nope) + ss_pe) * (1.0 / QK_DIM) + EPS)
        lo = h * HEAD_PAD
        k_ref[:, lo:lo + LANES] = (nope * r * gk_ref[:, :LANES]).astype(BF16)
        k_ref[:, lo + LANES:lo + QK_DIM] = (kr * r * gk_ref[:, LANES:QK_DIM]).astype(BF16)
        k_ref[:, lo + QK_DIM:lo + HEAD_PAD] = zeros


def _expand(c2d, kr2d, wp, tm):
    m = c2d.shape[0]
    row_spec = lambda w: pl.BlockSpec((tm, w), lambda i: (i, 0))
    consts = [wp["w_uk"], wp["w_uv"], wp["gk"]]
    return pl.pallas_call(
        _expand_kernel,
        out_shape=[jax.ShapeDtypeStruct((m, QK_WIDTH), BF16), jax.ShapeDtypeStruct((m, ATTN_WIDTH), BF16)],
        grid=(m // tm,),
        in_specs=[row_spec(KV_RANK), row_spec(ROPE_DIM)] + [_const_spec(c.shape) for c in consts],
        out_specs=[row_spec(QK_WIDTH), row_spec(ATTN_WIDTH)],
        compiler_params=pltpu.CompilerParams(dimension_semantics=("parallel",),
                                             vmem_limit_bytes=VMEM_LIMIT),
        name="expand",
    )(c2d, kr2d, *consts)


NEG = -1e30


def _softmax_step(carry, s, v):
    m, l, acc = carry
    m_new = jnp.maximum(m, jnp.max(s, axis=-1, keepdims=True))
    alpha = jnp.exp(m - m_new)
    p = jnp.exp(s - m_new)
    l = alpha * l + jnp.sum(p, axis=-1, keepdims=True)
    acc = alpha * acc + _dot(p.astype(BF16), v)
    return m_new, l, acc


def _attn_prompt_kernel(q_ref, k_ref, v_ref, go_ref, o_ref, acc_scr):
    t = q_ref.shape[0]
    qi = pl.program_id(1)
    row = lax.broadcasted_iota(jnp.int32, (t, t), 0)
    col = lax.broadcasted_iota(jnp.int32, (t, t), 1)
    diag_ok = (col // CHUNK) <= (row // CHUNK)
    for h in range(N_HEADS):
        qs = slice(h * HEAD_PAD, (h + 1) * HEAD_PAD)
        vs = slice(h * V_DIM, (h + 1) * V_DIM)
        q = q_ref[:, qs]

        def body(kb, carry, qs=qs, vs=vs, q=q):
            rows = pl.ds(pl.multiple_of(kb * t, t), t)
            return _softmax_step(carry, _dot_nt(q, k_ref[rows, qs]), v_ref[rows, vs])

        init = (jnp.full((t, 1), NEG, F32), jnp.zeros((t, 1), F32), jnp.zeros((t, V_DIM), F32))
        carry = lax.fori_loop(0, qi, body, init)
        rows = pl.ds(pl.multiple_of(qi * t, t), t)
        s = jnp.where(diag_ok, _dot_nt(q, k_ref[rows, qs]), NEG)
        _, l, acc = _softmax_step(carry, s, v_ref[rows, vs])
        acc_scr[:, vs] = acc / l
    o_ref[...] = (_rms(acc_scr[...]) * go_ref[...]).astype(BF16)


def _attn_prompt(q, k, v, go, batch, seq, t):
    nq = seq // t
    return pl.pallas_call(
        _attn_prompt_kernel,
        out_shape=jax.ShapeDtypeStruct((batch * seq, ATTN_WIDTH), BF16),
        grid=(batch, nq),
        in_specs=[pl.BlockSpec((t, QK_WIDTH), lambda b, i: (b * nq + i, 0)),
                  pl.BlockSpec((seq, QK_WIDTH), lambda b, i: (b, 0)),
                  pl.BlockSpec((seq, ATTN_WIDTH), lambda b, i: (b, 0)),
                  _const_spec(go.shape)],
        out_specs=pl.BlockSpec((t, ATTN_WIDTH), lambda b, i: (b * nq + i, 0)),
        scratch_shapes=[pltpu.VMEM((t, ATTN_WIDTH), F32)],
        compiler_params=pltpu.CompilerParams(dimension_semantics=("parallel", "arbitrary"),
                                             vmem_limit_bytes=VMEM_LIMIT),
        name="attn_prompt",
    )(q, k, v, go)


def _attn_sample_kernel(past, q_ref, kc_ref, vc_ref, kn_ref, vn_ref, go_ref, o_ref, acc_scr):
    t = q_ref.shape[0]
    row = lax.broadcasted_iota(jnp.int32, (t, t), 0)
    col = lax.broadcasted_iota(jnp.int32, (t, t), 1)
    new_ok = ((past + col) // CHUNK) <= ((past + row) // CHUNK)
    for h in range(N_HEADS):
        qs = slice(h * HEAD_PAD, (h + 1) * HEAD_PAD)
        vs = slice(h * V_DIM, (h + 1) * V_DIM)
        q = q_ref[:, qs]
        s_c = _dot_nt(q, kc_ref[:, qs])
        s_n = jnp.where(new_ok, _dot_nt(q, kn_ref[:, qs]), NEG)
        m = jnp.maximum(jnp.max(s_c, axis=-1, keepdims=True), jnp.max(s_n, axis=-1, keepdims=True))
        p_c = jnp.exp(s_c - m)
        p_n = jnp.exp(s_n - m)
        l = jnp.sum(p_c, axis=-1, keepdims=True) + jnp.sum(p_n, axis=-1, keepdims=True)
        acc = _dot(p_c.astype(BF16), vc_ref[:, vs]) + _dot(p_n.astype(BF16), vn_ref[:, vs])
        acc_scr[:, vs] = acc / l
    o_ref[...] = (_rms(acc_scr[...]) * go_ref[...]).astype(BF16)


def _attn_sample(q, kc, vc, kn, vn, go, batch, t, past):
    return pl.pallas_call(
        functools.partial(_attn_sample_kernel, past),
        out_shape=jax.ShapeDtypeStruct((batch * t, ATTN_WIDTH), BF16),
        grid=(batch,),
        in_specs=[pl.BlockSpec((t, QK_WIDTH), lambda b: (b, 0)),
                  pl.BlockSpec((past, QK_WIDTH), lambda b: (b, 0)),
                  pl.BlockSpec((past, ATTN_WIDTH), lambda b: (b, 0)),
                  pl.BlockSpec((t, QK_WIDTH), lambda b: (b, 0)),
                  pl.BlockSpec((t, ATTN_WIDTH), lambda b: (b, 0)),
                  _const_spec(go.shape)],
        out_specs=pl.BlockSpec((t, ATTN_WIDTH), lambda b: (b, 0)),
        scratch_shapes=[pltpu.VMEM((t, ATTN_WIDTH), F32)],
        compiler_params=pltpu.CompilerParams(dimension_semantics=("parallel",),
                                             vmem_limit_bytes=VMEM_LIMIT),
        name="attn_sample",
    )(q, kc, vc, kn, vn, go)


def _outproj_kernel(an_ref, gm_ref, x_ref, wo_ref, gffn_ref, h_ref, hn_ref):
    h = (x_ref[...] + _dot(an_ref[...], wo_ref[:ATTN_WIDTH, :])
         + _dot(gm_ref[...], wo_ref[ATTN_WIDTH:, :]))
    h_ref[...] = h
    hn_ref[...] = (_rms(h) * gffn_ref[...]).astype(BF16)


def _outproj(an, gm, x2d, wp, tm):
    m = x2d.shape[0]
    row_spec = lambda w: pl.BlockSpec((tm, w), lambda i: (i, 0))
    return pl.pallas_call(
        _outproj_kernel,
        out_shape=[jax.ShapeDtypeStruct((m, D_MODEL), F32), jax.ShapeDtypeStruct((m, D_MODEL), BF16)],
        grid=(m // tm,),
        in_specs=[row_spec(ATTN_WIDTH), row_spec(GMLP_WIDTH), row_spec(D_MODEL),
                  _const_spec(wp["w_out"].shape), _const_spec(wp["gffn"].shape)],
        out_specs=[row_spec(D_MODEL), row_spec(D_MODEL)],
        compiler_params=pltpu.CompilerParams(dimension_semantics=("parallel",),
                                             vmem_limit_bytes=VMEM_LIMIT),
        name="outproj",
    )(an, gm, x2d, wp["w_out"], wp["gffn"])


def _ffn_kernel(h_ref, hn_ref, wu_ref, wd_ref, y_ref):
    f = pl.program_id(1)

    @pl.when(f == 0)
    def _():
        y_ref[...] = h_ref[...]

    a = jnp.maximum(_dot(hn_ref[...], wu_ref[...]), 0.0)
    y_ref[...] += _dot((a * a).astype(BF16), wd_ref[...])


def _ffn(h, hn, wp, tm, tf):
    m = h.shape[0]
    return pl.pallas_call(
        _ffn_kernel,
        out_shape=jax.ShapeDtypeStruct((m, D_MODEL), F32),
        grid=(m // tm, D_FF // tf),
        in_specs=[pl.BlockSpec((tm, D_MODEL), lambda i, f: (i, 0), pipeline_mode=pl.Buffered(1)),
                  pl.BlockSpec((tm, D_MODEL), lambda i, f: (i, 0)),
                  pl.BlockSpec((D_MODEL, tf), lambda i, f: (0, f)),
                  pl.BlockSpec((tf, D_MODEL), lambda i, f: (f, 0))],
        out_specs=pl.BlockSpec((tm, D_MODEL), lambda i, f: (i, 0)),
        compiler_params=pltpu.CompilerParams(dimension_semantics=("parallel", "arbitrary"),
                                             vmem_limit_bytes=VMEM_LIMIT),
        name="ffn",
    )(h, hn, wp["w_up"], wp["w_down"])


def _rope_tables(pos):
    inv = ROPE_THETA ** (-jnp.arange(HALF_ROPE, dtype=F32) / HALF_ROPE)
    ang = pos.astype(F32)[:, None] * inv[None, :]
    cos, sin = jnp.cos(ang), jnp.sin(ang)
    z = jnp.zeros_like(cos)
    z2 = jnp.zeros((pos.shape[0], LANES - ROPE_DIM), F32)
    return (jnp.concatenate([cos, cos, z2], axis=1), jnp.concatenate([-sin, z, z2], axis=1),
            jnp.concatenate([z, sin, z2], axis=1))


def _head_gain(g_nope, g_rope, scale):
    pad = jnp.zeros((HEAD_PAD - QK_DIM,), F32)
    return (jnp.concatenate([g_nope, g_rope, g_rope, pad]) * scale)[None, :]


def _prep_weights(norm_mix, w_in, q_lat_norm, kv_lat_norm, w_uq, w_uk, w_uv, q_norm_nope, q_norm_rope,
                  k_norm_nope, k_norm_rope, v_norm, w_spatial, b_spatial, out_norm_attn, out_norm_gmlp,
                  w_out, norm_ffn, w_up, w_down, chunk_len):
    s1, s2, s3 = Q_RANK, Q_RANK + KV_RANK, Q_RANK + KV_RANK + ROPE_DIM
    w_in_p = jnp.concatenate(
        [w_in[:, :s3], jnp.zeros((D_MODEL, LANES - ROPE_DIM), F32), w_in[:, s3:]], axis=1).astype(BF16)
    wq = w_uq.reshape(Q_RANK, N_HEADS, QK_DIM)
    wq = jnp.pad(wq, ((0, 0), (0, 0), (0, HEAD_PAD - QK_DIM))).reshape(Q_RANK, QK_WIDTH).astype(BF16)
    reps = GMLP_CHUNK // chunk_len
    ws = jnp.tile(w_spatial[:, :chunk_len, :chunk_len], (1, reps, reps))
    bst = jnp.tile(b_spatial[:, :chunk_len], (1, reps)).T
    return {
        "gmix": norm_mix[None, :], "w_in": w_in_p, "gql": q_lat_norm[None, :], "gkv": kv_lat_norm[None, :],
        "w_uq": wq, "w_uk": w_uk.astype(BF16), "w_uv": w_uv.astype(BF16),
        "gq": _head_gain(q_norm_nope, q_norm_rope, QK_DIM ** -0.5),
        "gk": _head_gain(k_norm_nope, k_norm_rope, 1.0),
        "v_norm": v_norm, "ws": ws, "bst": bst, "gog": out_norm_gmlp[None, :],
        "goa": out_norm_attn[None, :], "w_out": w_out.astype(BF16), "gffn": norm_ffn[None, :],
        "w_up": w_up.astype(BF16), "w_down": w_down.astype(BF16),
    }


def _finish(an, gm, x2d, wp, tm_ffn):
    h, hn = _outproj(an, gm, x2d, wp, 512)
    return _ffn(h, hn, wp, tm_ffn, 512)


def kernel(x_prompt, x_sample, cache_c_kv, cache_k_rope, norm_mix, w_in, q_lat_norm, kv_lat_norm, w_uq, w_uk, w_uv, q_norm_nope, q_norm_rope, k_norm_nope, k_norm_rope, v_norm, w_spatial, b_spatial, out_norm_attn, out_norm_gmlp, w_out, norm_ffn, w_up, w_down):
    depth = w_in.shape[0]
    assert depth == 1
    batch, seq, _ = x_prompt.shape
    dec_batch, dec_seq, _ = x_sample.shape
    past = cache_c_kv.shape[2]
    assert past % CHUNK == 0 and dec_seq <= CHUNK and GMLP_CHUNK % dec_seq == 0

    weights = (norm_mix[0], w_in[0], q_lat_norm[0], kv_lat_norm[0], w_uq[0], w_uk[0], w_uv[0],
               q_norm_nope[0], q_norm_rope[0], k_norm_nope[0], k_norm_rope[0], v_norm[0], w_spatial[0],
               b_spatial[0], out_norm_attn[0], out_norm_gmlp[0], w_out[0], norm_ffn[0], w_up[0], w_down[0])
    wp = _prep_weights(*weights, chunk_len=GMLP_CHUNK)
    reps = GMLP_CHUNK // dec_seq
    ws_s = jnp.tile(w_spatial[0][:, :dec_seq, :dec_seq], (1, reps, reps))
    bst_s = jnp.tile(b_spatial[0][:, :dec_seq], (1, reps)).T
    wp_s = dict(wp, ws=ws_s, bst=bst_s)

    tm_p = 256
    xp = x_prompt.reshape(batch * seq, D_MODEL)
    tabs_p = _rope_tables(jnp.arange(seq, dtype=jnp.int32))
    q, k, v, ckv_p, kpe_p, gm = _proj(xp, tabs_p, wp, GMLP_CHUNK, False, tm_p)
    an = _attn_prompt(q, k, v, wp["goa"], batch, seq, 256)
    y_p = _finish(an, gm, xp, wp, 1024)

    tm_s = 256
    xs = x_sample.reshape(dec_batch * dec_seq, D_MODEL)
    pos_s = past + (jnp.arange(tm_s, dtype=jnp.int32) % dec_seq)
    tabs_s = _rope_tables(pos_s)
    qs, kn, vn, ckv_s, kpe_s, gms, vg_s = _proj(xs, tabs_s, wp_s, dec_seq, True, tm_s)
    kc, vc = _expand(cache_c_kv[0].reshape(dec_batch * past, KV_RANK),
                     cache_k_rope[0].reshape(dec_batch * past, ROPE_DIM), wp, 512)
    ans = _attn_sample(qs, kc, vc, kn, vn, wp["goa"], dec_batch, dec_seq, past)
    y_s = _finish(ans, gms, xs, wp, 1024)

    return (y_p.reshape(batch, seq, D_MODEL),
            y_s.reshape(dec_batch, dec_seq, D_MODEL),
            ckv_p.reshape(1, batch, seq, KV_RANK),
            kpe_p.reshape(1, batch, seq, ROPE_DIM),
            ckv_s.reshape(1, dec_batch, dec_seq, KV_RANK),
            kpe_s.reshape(1, dec_batch, dec_seq, ROPE_DIM),
            vg_s.reshape(1, dec_batch, dec_seq, GMLP_WIDTH))
```

```python
import functools
import math

import jax
import jax.numpy as jnp
from jax import lax
from jax.experimental import pallas as pl
from jax.experimental.pallas import tpu as pltpu

D_MODEL = 2048
CHUNK = 64
N_HEADS = 8
NOPE_DIM = 128
ROPE_DIM = 64
HALF_ROPE = ROPE_DIM // 2
QK_DIM = NOPE_DIM + ROPE_DIM
V_DIM = 128
ATTN_WIDTH = N_HEADS * V_DIM
Q_RANK = 512
KV_RANK = 256
GMLP_GROUPS = 8
GMLP_GROUP_DIM = 128
GMLP_WIDTH = GMLP_GROUPS * GMLP_GROUP_DIM
GMLP_CHUNK = 128
D_FF = 4 * D_MODEL
ROPE_THETA = 10000.0
EPS = 1e-6

LANES = 128
HEAD_PAD = 2 * LANES
QK_WIDTH = N_HEADS * HEAD_PAD
C_Q0, C_Q1 = 0, Q_RANK
C_KV0, C_KV1 = C_Q1, C_Q1 + KV_RANK
C_PE0, C_PE1 = C_KV1, C_KV1 + LANES
C_U0, C_U1 = C_PE1, C_PE1 + GMLP_WIDTH
C_V0, C_V1 = C_U1, C_U1 + GMLP_WIDTH
IN_PAD = C_V1
VMEM_LIMIT = 56 * 1024 * 1024

BF16 = jnp.bfloat16
F32 = jnp.float32


def _dot(a, b):
    return jnp.dot(a, b, preferred_element_type=F32)


def _dot_nt(a, b):
    return lax.dot_general(a, b, (((1,), (1,)), ((), ())), preferred_element_type=F32)


def _sumsq(x):
    return jnp.sum(x * x, axis=-1, keepdims=True)


def _rms(x):
    return x * lax.rsqrt(jnp.mean(x * x, axis=-1, keepdims=True) + EPS)


def _gelu(x):
    c = math.sqrt(2.0 / math.pi)
    return 0.5 * x * (1.0 + jnp.tanh(c * (x + 0.044715 * (x * x * x))))


def _rope(t, cos, s1, s2):
    return t * cos + pltpu.roll(t, LANES - HALF_ROPE, 1) * s1 + pltpu.roll(t, HALF_ROPE, 1) * s2


def _const_spec(shape):
    nd = len(shape)
    return pl.BlockSpec(shape, lambda *_: (0,) * nd, pipeline_mode=pl.Buffered(1))


def _proj_kernel(chunk_len, emit_v, v_transposed, x_ref, gmix_ref, win_ref, gql_ref, gkv_ref, wuq_ref, wuk_ref,
                 wuv_ref, gq_ref, gk_ref, cos_ref, s1_ref, s2_ref, vn_ref, ws_ref, bst_ref, gog_ref,
                 q_ref, k_ref, v_ref, ckv_ref, kpe_ref, gm_ref, *rest):
    if emit_v:
        vg_ref, gate_scr, vgb_scr = rest
    else:
        vg_ref = None
        gate_scr, vgb_scr = rest
    tm = x_ref.shape[0]
    xn = (_rms(x_ref[...]) * gmix_ref[...]).astype(BF16)
    cos, s1, s2 = cos_ref[...], s1_ref[...], s2_ref[...]

    zq = _dot(xn, win_ref[:, C_Q0:C_Q1])
    qln = (_rms(zq) * gql_ref[...]).astype(BF16)
    qraw = _dot(qln, wuq_ref[...])
    for h in range(N_HEADS):
        lo = h * HEAD_PAD
        nope = qraw[:, lo:lo + LANES]
        rp = qraw[:, lo + LANES:lo + HEAD_PAD]
        r = lax.rsqrt((_sumsq(nope) + _sumsq(rp)) * (1.0 / QK_DIM) + EPS)
        rp = _rope(rp, cos, s1, s2)
        q_ref[:, lo:lo + LANES] = (nope * r * gq_ref[:, :LANES]).astype(BF16)
        q_ref[:, lo + LANES:lo + HEAD_PAD] = (rp * r * gq_ref[:, LANES:]).astype(BF16)

    ckv = _rms(_dot(xn, win_ref[:, C_KV0:C_KV1])) * gkv_ref[...]
    ckv_ref[...] = ckv
    cb = ckv.astype(BF16)
    pe = _rope(_dot(xn, win_ref[:, C_PE0:C_PE1]), cos, s1, s2)
    kpe_ref[...] = pe[:, :ROPE_DIM]
    ss_pe = _sumsq(pe)
    knope = _dot(cb, wuk_ref[...])
    if v_transposed:
        v_ref[...] = _dot_nt(wuv_ref[...], cb).astype(BF16)
    else:
        v_ref[...] = _dot(cb, wuv_ref[...]).astype(BF16)
    for h in range(N_HEADS):
        nope = knope[:, h * LANES:(h + 1) * LANES]
        r = lax.rsqrt((_sumsq(nope) + ss_pe) * (1.0 / QK_DIM) + EPS)
        lo = h * HEAD_PAD
        k_ref[:, lo:lo + LANES] = (nope * r * gk_ref[:, :LANES]).astype(BF16)
        k_ref[:, lo + LANES:lo + HEAD_PAD] = (pe * r * gk_ref[:, LANES:]).astype(BF16)

    gv = _gelu(_dot(xn, win_ref[:, C_V0:C_V1]))
    for g in range(GMLP_GROUPS):
        blk = gv[:, g * LANES:(g + 1) * LANES]
        vg = _rms(blk) * vn_ref[g:g + 1, :]
        if emit_v:
            vg_ref[:, g * LANES:(g + 1) * LANES] = vg
        vgb_scr[:, g * LANES:(g + 1) * LANES] = vg.astype(BF16)
    u = _gelu(_dot(xn, win_ref[:, C_U0:C_U1]))
    row = lax.broadcasted_iota(jnp.int32, (GMLP_CHUNK, GMLP_CHUNK), 0)
    col = lax.broadcasted_iota(jnp.int32, (GMLP_CHUNK, GMLP_CHUNK), 1)
    causal = (row // chunk_len == col // chunk_len) & (col <= row)
    for g in range(GMLP_GROUPS):
        wm = jnp.where(causal, ws_ref[g], 0.0).astype(BF16)
        bias = bst_ref[:, g:g + 1]
        for c in range(tm // GMLP_CHUNK):
            rows = slice(c * GMLP_CHUNK, (c + 1) * GMLP_CHUNK)
            cols = slice(g * LANES, (g + 1) * LANES)
            s = _dot(wm, vgb_scr[rows, cols]) + bias
            gate_scr[rows, cols] = u[rows, cols] * s
    gm_ref[...] = (_rms(gate_scr[...]) * gog_ref[...]).astype(BF16)


def _proj(x2d, tabs, wp, chunk_len, emit_v, v_transposed, tm):
    m = x2d.shape[0]
    cos, s1, s2 = tabs
    tab_blocks = cos.shape[0] // tm
    row_spec = lambda w: pl.BlockSpec((tm, w), lambda i: (i, 0))
    tab_spec = pl.BlockSpec((tm, LANES), lambda i: (i % tab_blocks, 0))
    consts = [wp["gmix"], wp["w_in"], wp["gql"], wp["gkv"], wp["w_uq"], wp["w_uk"],
              wp["w_uv_t"] if v_transposed else wp["w_uv"], wp["gq"], wp["gk"]]
    consts2 = [wp["v_norm"], wp["ws"], wp["bst"], wp["gog"]]
    in_specs = ([row_spec(D_MODEL)] + [_const_spec(c.shape) for c in consts]
                + [tab_spec] * 3 + [_const_spec(c.shape) for c in consts2])
    v_shape = (m // tm * ATTN_WIDTH, tm) if v_transposed else (m, ATTN_WIDTH)
    v_spec = pl.BlockSpec((ATTN_WIDTH, tm), lambda i: (i, 0)) if v_transposed else row_spec(ATTN_WIDTH)
    out_shape = [jax.ShapeDtypeStruct((m, QK_WIDTH), BF16), jax.ShapeDtypeStruct((m, QK_WIDTH), BF16),
                 jax.ShapeDtypeStruct(v_shape, BF16), jax.ShapeDtypeStruct((m, KV_RANK), F32),
                 jax.ShapeDtypeStruct((m, ROPE_DIM), F32), jax.ShapeDtypeStruct((m, GMLP_WIDTH), BF16)]
    out_specs = [row_spec(QK_WIDTH), row_spec(QK_WIDTH), v_spec, row_spec(KV_RANK),
                 row_spec(ROPE_DIM), row_spec(GMLP_WIDTH)]
    if emit_v:
        out_shape.append(jax.ShapeDtypeStruct((m, GMLP_WIDTH), F32))
        out_specs.append(row_spec(GMLP_WIDTH))
    return pl.pallas_call(
        functools.partial(_proj_kernel, chunk_len, emit_v, v_transposed),
        out_shape=out_shape,
        grid=(m // tm,),
        in_specs=in_specs,
        out_specs=out_specs,
        scratch_shapes=[pltpu.VMEM((tm, GMLP_WIDTH), F32), pltpu.VMEM((tm, GMLP_WIDTH), BF16)],
        compiler_params=pltpu.CompilerParams(dimension_semantics=("parallel",),
                                             vmem_limit_bytes=VMEM_LIMIT),
        name="proj",
    )(x2d, *consts, cos, s1, s2, *consts2)


def _expand_kernel(c_ref, kr_ref, wuk_ref, wuv_ref, gk_ref, k_ref, v_ref):
    cb = c_ref[...].astype(BF16)
    kr = kr_ref[...]
    ss_pe = _sumsq(kr)
    knope = _dot(cb, wuk_ref[...])
    v_ref[...] = _dot(cb, wuv_ref[...]).astype(BF16)
    zeros = jnp.zeros((c_ref.shape[0], HEAD_PAD - QK_DIM), BF16)
    for h in range(N_HEADS):
        nope = knope[:, h * LANES:(h + 1) * LANES]
        r = lax.rsqrt((_sumsq(nope) + ss_pe) * (1.0 / QK_DIM) + EPS)
        lo = h * HEAD_PAD
        k_ref[:, lo:lo + LANES] = (nope * r * gk_ref[:, :LANES]).astype(BF16)
        k_ref[:, lo + LANES:lo + QK_DIM] = (kr * r * gk_ref[:, LANES:QK_DIM]).astype(BF16)
        k_ref[:, lo + QK_DIM:lo + HEAD_PAD] = zeros


def _expand(c2d, kr2d, wp, tm):
    m = c2d.shape[0]
    row_spec = lambda w: pl.BlockSpec((tm, w), lambda i: (i, 0))
    consts = [wp["w_uk"], wp["w_uv"], wp["gk"]]
    return pl.pallas_call(
        _expand_kernel,
        out_shape=[jax.ShapeDtypeStruct((m, QK_WIDTH), BF16), jax.ShapeDtypeStruct((m, ATTN_WIDTH), BF16)],
        grid=(m // tm,),
        in_specs=[row_spec(KV_RANK), row_spec(ROPE_DIM)] + [_const_spec(c.shape) for c in consts],
        out_specs=[row_spec(QK_WIDTH), row_spec(ATTN_WIDTH)],
        compiler_params=pltpu.CompilerParams(dimension_semantics=("parallel",),
                                             vmem_limit_bytes=VMEM_LIMIT),
        name="expand",
    )(c2d, kr2d, *consts)


NEG = -1e30


def _softmax_step(carry, s, v):
    m, l, acc = carry
    m_new = jnp.maximum(m, jnp.max(s, axis=-1, keepdims=True))
    alpha = jnp.exp(m - m_new)
    p = jnp.exp(s - m_new)
    l = alpha * l + jnp.sum(p, axis=-1, keepdims=True)
    acc = alpha * acc + _dot(p.astype(BF16), v)
    return m_new, l, acc


SOFTMAX_KEY_CHUNKS = 4


def _attn_prompt_kernel(q_ref, k_ref, vt_ref, go_ref, o_ref, acc_scr, m_scr, l_scr, s_scr, p_scr):
    t = q_ref.shape[0]
    ck = t // SOFTMAX_KEY_CHUNKS
    qi = pl.program_id(1)
    m_scr[...] = jnp.full(m_scr.shape, NEG, F32)
    l_scr[...] = jnp.zeros(l_scr.shape, F32)
    acc_scr[...] = jnp.zeros(acc_scr.shape, F32)

    def block(kb, masked):
        rows = pl.ds(pl.multiple_of(kb * t, t), t)
        if masked:
            key = lax.broadcasted_iota(jnp.int32, (t, t), 0)
            qry = lax.broadcasted_iota(jnp.int32, (t, t), 1)
            ok = (key // CHUNK) <= (qry // CHUNK)

        def scores(h):
            qs = slice(h * HEAD_PAD, (h + 1) * HEAD_PAD)
            s = _dot_nt(k_ref[rows, qs], q_ref[:, qs])
            s_scr[h % 2] = jnp.where(ok, s, NEG) if masked else s

        scores(0)
        for h in range(N_HEADS):
            if h + 1 < N_HEADS:
                scores(h + 1)
            slot = h % 2
            hs = slice(h * V_DIM, (h + 1) * V_DIM)
            chunks = [slice(c * ck, (c + 1) * ck) for c in range(SOFTMAX_KEY_CHUNKS)]
            m_old = m_scr[h]
            m_new = m_old
            for c in chunks:
                m_new = jnp.maximum(m_new, jnp.max(s_scr[slot, c, :], axis=0, keepdims=True))
            alpha = jnp.exp(m_old - m_new)
            l_new = alpha * l_scr[h]
            for c in chunks:
                p = jnp.exp(s_scr[slot, c, :] - m_new)
                l_new = l_new + jnp.sum(p, axis=0, keepdims=True)
                p_scr[slot, c, :] = p.astype(BF16)
            l_scr[h] = l_new
            m_scr[h] = m_new
            vt = vt_ref[pl.ds(pl.multiple_of(kb * ATTN_WIDTH, ATTN_WIDTH) + h * V_DIM, V_DIM), :]
            acc_scr[hs, :] = alpha * acc_scr[hs, :] + _dot(vt, p_scr[slot])

    def body(kb, carry):
        block(kb, False)
        return carry

    lax.fori_loop(0, qi, body, 0)
    block(qi, True)
    for h in range(N_HEADS):
        hs = slice(h * V_DIM, (h + 1) * V_DIM)
        acc_scr[hs, :] = acc_scr[hs, :] / l_scr[h]
    o = acc_scr[...].T
    o_ref[...] = (_rms(o) * go_ref[...]).astype(BF16)


def _attn_prompt(q, k, vt, go, batch, seq, t):
    nq = seq // t
    assert vt.shape == (batch * nq * ATTN_WIDTH, t)
    return pl.pallas_call(
        _attn_prompt_kernel,
        out_shape=jax.ShapeDtypeStruct((batch * seq, ATTN_WIDTH), BF16),
        grid=(batch, nq),
        in_specs=[pl.BlockSpec((t, QK_WIDTH), lambda b, i: (b * nq + i, 0)),
                  pl.BlockSpec((seq, QK_WIDTH), lambda b, i: (b, 0)),
                  pl.BlockSpec((nq * ATTN_WIDTH, t), lambda b, i: (b, 0)),
                  _const_spec(go.shape)],
        out_specs=pl.BlockSpec((t, ATTN_WIDTH), lambda b, i: (b * nq + i, 0)),
        scratch_shapes=[pltpu.VMEM((ATTN_WIDTH, t), F32), pltpu.VMEM((N_HEADS, 1, t), F32),
                        pltpu.VMEM((N_HEADS, 1, t), F32), pltpu.VMEM((2, t, t), F32),
                        pltpu.VMEM((2, t, t), BF16)],
        compiler_params=pltpu.CompilerParams(dimension_semantics=("parallel", "arbitrary"),
                                             vmem_limit_bytes=VMEM_LIMIT),
        name="attn_prompt",
    )(q, k, vt, go)


def _attn_sample_kernel(past, q_ref, kc_ref, vc_ref, kn_ref, vn_ref, go_ref, o_ref, acc_scr):
    t = q_ref.shape[0]
    row = lax.broadcasted_iota(jnp.int32, (t, t), 0)
    col = lax.broadcasted_iota(jnp.int32, (t, t), 1)
    new_ok = ((past + col) // CHUNK) <= ((past + row) // CHUNK)
    for h in range(N_HEADS):
        qs = slice(h * HEAD_PAD, (h + 1) * HEAD_PAD)
        vs = slice(h * V_DIM, (h + 1) * V_DIM)
        q = q_ref[:, qs]
        s_c = _dot_nt(q, kc_ref[:, qs])
        s_n = jnp.where(new_ok, _dot_nt(q, kn_ref[:, qs]), NEG)
        m = jnp.maximum(jnp.max(s_c, axis=-1, keepdims=True), jnp.max(s_n, axis=-1, keepdims=True))
        p_c = jnp.exp(s_c - m)
        p_n = jnp.exp(s_n - m)
        l = jnp.sum(p_c, axis=-1, keepdims=True) + jnp.sum(p_n, axis=-1, keepdims=True)
        acc = _dot(p_c.astype(BF16), vc_ref[:, vs]) + _dot(p_n.astype(BF16), vn_ref[:, vs])
        acc_scr[:, vs] = acc / l
    o_ref[...] = (_rms(acc_scr[...]) * go_ref[...]).astype(BF16)


def _attn_sample(q, kc, vc, kn, vn, go, batch, t, past):
    return pl.pallas_call(
        functools.partial(_attn_sample_kernel, past),
        out_shape=jax.ShapeDtypeStruct((batch * t, ATTN_WIDTH), BF16),
        grid=(batch,),
        in_specs=[pl.BlockSpec((t, QK_WIDTH), lambda b: (b, 0)),
                  pl.BlockSpec((past, QK_WIDTH), lambda b: (b, 0)),
                  pl.BlockSpec((past, ATTN_WIDTH), lambda b: (b, 0)),
                  pl.BlockSpec((t, QK_WIDTH), lambda b: (b, 0)),
                  pl.BlockSpec((t, ATTN_WIDTH), lambda b: (b, 0)),
                  _const_spec(go.shape)],
        out_specs=pl.BlockSpec((t, ATTN_WIDTH), lambda b: (b, 0)),
        scratch_shapes=[pltpu.VMEM((t, ATTN_WIDTH), F32)],
        compiler_params=pltpu.CompilerParams(dimension_semantics=("parallel",),
                                             vmem_limit_bytes=VMEM_LIMIT),
        name="attn_sample",
    )(q, kc, vc, kn, vn, go)


def _outproj_kernel(an_ref, gm_ref, x_ref, wo_ref, gffn_ref, h_ref, hn_ref):
    h = (x_ref[...] + _dot(an_ref[...], wo_ref[:ATTN_WIDTH, :])
         + _dot(gm_ref[...], wo_ref[ATTN_WIDTH:, :]))
    h_ref[...] = h
    hn_ref[...] = (_rms(h) * gffn_ref[...]).astype(BF16)


def _outproj(an, gm, x2d, wp, tm):
    m = x2d.shape[0]
    row_spec = lambda w: pl.BlockSpec((tm, w), lambda i: (i, 0))
    return pl.pallas_call(
        _outproj_kernel,
        out_shape=[jax.ShapeDtypeStruct((m, D_MODEL), F32), jax.ShapeDtypeStruct((m, D_MODEL), BF16)],
        grid=(m // tm,),
        in_specs=[row_spec(ATTN_WIDTH), row_spec(GMLP_WIDTH), row_spec(D_MODEL),
                  _const_spec(wp["w_out"].shape), _const_spec(wp["gffn"].shape)],
        out_specs=[row_spec(D_MODEL), row_spec(D_MODEL)],
        compiler_params=pltpu.CompilerParams(dimension_semantics=("parallel",),
                                             vmem_limit_bytes=VMEM_LIMIT),
        name="outproj",
    )(an, gm, x2d, wp["w_out"], wp["gffn"])


def _ffn_kernel(h_ref, hn_ref, wu_ref, wd_ref, y_ref):
    f = pl.program_id(1)

    @pl.when(f == 0)
    def _():
        y_ref[...] = h_ref[...]

    a = jnp.maximum(_dot(hn_ref[...], wu_ref[...]), 0.0)
    y_ref[...] += _dot((a * a).astype(BF16), wd_ref[...])


def _ffn(h, hn, wp, tm, tf):
    m = h.shape[0]
    return pl.pallas_call(
        _ffn_kernel,
        out_shape=jax.ShapeDtypeStruct((m, D_MODEL), F32),
        grid=(m // tm, D_FF // tf),
        in_specs=[pl.BlockSpec((tm, D_MODEL), lambda i, f: (i, 0), pipeline_mode=pl.Buffered(1)),
                  pl.BlockSpec((tm, D_MODEL), lambda i, f: (i, 0)),
                  pl.BlockSpec((D_MODEL, tf), lambda i, f: (0, f)),
                  pl.BlockSpec((tf, D_MODEL), lambda i, f: (f, 0))],
        out_specs=pl.BlockSpec((tm, D_MODEL), lambda i, f: (i, 0)),
        compiler_params=pltpu.CompilerParams(dimension_semantics=("parallel", "arbitrary"),
                                             vmem_limit_bytes=VMEM_LIMIT),
        name="ffn",
    )(h, hn, wp["w_up"], wp["w_down"])


def _rope_tables(pos):
    inv = ROPE_THETA ** (-jnp.arange(HALF_ROPE, dtype=F32) / HALF_ROPE)
    ang = pos.astype(F32)[:, None] * inv[None, :]
    cos, sin = jnp.cos(ang), jnp.sin(ang)
    z = jnp.zeros_like(cos)
    z2 = jnp.zeros((pos.shape[0], LANES - ROPE_DIM), F32)
    return (jnp.concatenate([cos, cos, z2], axis=1), jnp.concatenate([-sin, z, z2], axis=1),
            jnp.concatenate([z, sin, z2], axis=1))


def _head_gain(g_nope, g_rope, scale):
    pad = jnp.zeros((HEAD_PAD - QK_DIM,), F32)
    return (jnp.concatenate([g_nope, g_rope, g_rope, pad]) * scale)[None, :]


def _prep_weights(norm_mix, w_in, q_lat_norm, kv_lat_norm, w_uq, w_uk, w_uv, q_norm_nope, q_norm_rope,
                  k_norm_nope, k_norm_rope, v_norm, w_spatial, b_spatial, out_norm_attn, out_norm_gmlp,
                  w_out, norm_ffn, w_up, w_down, chunk_len):
    s1, s2, s3 = Q_RANK, Q_RANK + KV_RANK, Q_RANK + KV_RANK + ROPE_DIM
    w_in_p = jnp.concatenate(
        [w_in[:, :s3], jnp.zeros((D_MODEL, LANES - ROPE_DIM), F32), w_in[:, s3:]], axis=1).astype(BF16)
    wq = w_uq.reshape(Q_RANK, N_HEADS, QK_DIM)
    wq = jnp.pad(wq, ((0, 0), (0, 0), (0, HEAD_PAD - QK_DIM))).reshape(Q_RANK, QK_WIDTH).astype(BF16)
    reps = GMLP_CHUNK // chunk_len
    ws = jnp.tile(w_spatial[:, :chunk_len, :chunk_len], (1, reps, reps))
    bst = jnp.tile(b_spatial[:, :chunk_len], (1, reps)).T
    return {
        "gmix": norm_mix[None, :], "w_in": w_in_p, "gql": q_lat_norm[None, :], "gkv": kv_lat_norm[None, :],
        "w_uq": wq, "w_uk": w_uk.astype(BF16), "w_uv": w_uv.astype(BF16), "w_uv_t": w_uv.T.astype(BF16),
        "gq": _head_gain(q_norm_nope, q_norm_rope, QK_DIM ** -0.5),
        "gk": _head_gain(k_norm_nope, k_norm_rope, 1.0),
        "v_norm": v_norm, "ws": ws, "bst": bst, "gog": out_norm_gmlp[None, :],
        "goa": out_norm_attn[None, :], "w_out": w_out.astype(BF16), "gffn": norm_ffn[None, :],
        "w_up": w_up.astype(BF16), "w_down": w_down.astype(BF16),
    }


def _finish(an, gm, x2d, wp, tm_ffn):
    h, hn = _outproj(an, gm, x2d, wp, 512)
    return _ffn(h, hn, wp, tm_ffn, 512)


def kernel(x_prompt, x_sample, cache_c_kv, cache_k_rope, norm_mix, w_in, q_lat_norm, kv_lat_norm, w_uq, w_uk, w_uv, q_norm_nope, q_norm_rope, k_norm_nope, k_norm_rope, v_norm, w_spatial, b_spatial, out_norm_attn, out_norm_gmlp, w_out, norm_ffn, w_up, w_down):
    depth = w_in.shape[0]
    assert depth == 1
    batch, seq, _ = x_prompt.shape
    dec_batch, dec_seq, _ = x_sample.shape
    past = cache_c_kv.shape[2]
    assert past % CHUNK == 0 and dec_seq <= CHUNK and GMLP_CHUNK % dec_seq == 0

    weights = (norm_mix[0], w_in[0], q_lat_norm[0], kv_lat_norm[0], w_uq[0], w_uk[0], w_uv[0],
               q_norm_nope[0], q_norm_rope[0], k_norm_nope[0], k_norm_rope[0], v_norm[0], w_spatial[0],
               b_spatial[0], out_norm_attn[0], out_norm_gmlp[0], w_out[0], norm_ffn[0], w_up[0], w_down[0])
    wp = _prep_weights(*weights, chunk_len=GMLP_CHUNK)
    reps = GMLP_CHUNK // dec_seq
    ws_s = jnp.tile(w_spatial[0][:, :dec_seq, :dec_seq], (1, reps, reps))
    bst_s = jnp.tile(b_spatial[0][:, :dec_seq], (1, reps)).T
    wp_s = dict(wp, ws=ws_s, bst=bst_s)

    tm_p = 256
    xp = x_prompt.reshape(batch * seq, D_MODEL)
    tabs_p = _rope_tables(jnp.arange(seq, dtype=jnp.int32))
    q, k, vt, ckv_p, kpe_p, gm = _proj(xp, tabs_p, wp, GMLP_CHUNK, False, True, tm_p)
    an = _attn_prompt(q, k, vt, wp["goa"], batch, seq, tm_p)
    y_p = _finish(an, gm, xp, wp, 1024)

    tm_s = 256
    xs = x_sample.reshape(dec_batch * dec_seq, D_MODEL)
    pos_s = past + (jnp.arange(tm_s, dtype=jnp.int32) % dec_seq)
    tabs_s = _rope_tables(pos_s)
    qs, kn, vn, ckv_s, kpe_s, gms, vg_s = _proj(xs, tabs_s, wp_s, dec_seq, True, False, tm_s)
    kc, vc = _expand(cache_c_kv[0].reshape(dec_batch * past, KV_RANK),
                     cache_k_rope[0].reshape(dec_batch * past, ROPE_DIM), wp, 512)
    ans = _attn_sample(qs, kc, vc, kn, vn, wp["goa"], dec_batch, dec_seq, past)
    y_s = _finish(ans, gms, xs, wp, 1024)

    return (y_p.reshape(batch, seq, D_MODEL),
            y_s.reshape(dec_batch, dec_seq, D_MODEL),
            ckv_p.reshape(1, batch, seq, KV_RANK),
            kpe_p.reshape(1, batch, seq, ROPE_DIM),
            ckv_s.reshape(1, dec_batch, dec_seq, KV_RANK),
            kpe_s.reshape(1, dec_batch, dec_seq, ROPE_DIM),
            vg_s.reshape(1, dec_batch, dec_seq, GMLP_WIDTH))
```

```python
import functools
import math

import jax
import jax.numpy as jnp
from jax import lax
from jax.experimental import pallas as pl
from jax.experimental.pallas import tpu as pltpu

D_MODEL = 2048
CHUNK = 64
N_HEADS = 8
NOPE_DIM = 128
ROPE_DIM = 64
HALF_ROPE = ROPE_DIM // 2
QK_DIM = NOPE_DIM + ROPE_DIM
V_DIM = 128
ATTN_WIDTH = N_HEADS * V_DIM
Q_RANK = 512
KV_RANK = 256
GMLP_GROUPS = 8
GMLP_GROUP_DIM = 128
GMLP_WIDTH = GMLP_GROUPS * GMLP_GROUP_DIM
GMLP_CHUNK = 128
D_FF = 4 * D_MODEL
ROPE_THETA = 10000.0
EPS = 1e-6

LANES = 128
SUBLANES = 8
HEAD_PAD = 2 * LANES
QK_WIDTH = N_HEADS * HEAD_PAD
C_Q0, C_Q1 = 0, Q_RANK
C_KV0, C_KV1 = C_Q1, C_Q1 + KV_RANK
C_PE0, C_PE1 = C_KV1, C_KV1 + LANES
C_U0, C_U1 = C_PE1, C_PE1 + GMLP_WIDTH
C_V0, C_V1 = C_U1, C_U1 + GMLP_WIDTH
IN_PAD = C_V1
VMEM_LIMIT = 56 * 1024 * 1024

BF16 = jnp.bfloat16
F32 = jnp.float32


def _dot(a, b):
    return jnp.dot(a, b, preferred_element_type=F32)


def _dot_nt(a, b):
    return lax.dot_general(a, b, (((1,), (1,)), ((), ())), preferred_element_type=F32)


def _sumsq(x):
    return jnp.sum(x * x, axis=-1, keepdims=True)


def _rms(x):
    return x * lax.rsqrt(jnp.mean(x * x, axis=-1, keepdims=True) + EPS)


def _gelu(x):
    c = math.sqrt(2.0 / math.pi)
    return 0.5 * x * (1.0 + jnp.tanh(c * (x + 0.044715 * (x * x * x))))


def _rope(t, cos, s1, s2):
    return t * cos + pltpu.roll(t, LANES - HALF_ROPE, 1) * s1 + pltpu.roll(t, HALF_ROPE, 1) * s2


def _const_spec(shape):
    nd = len(shape)
    return pl.BlockSpec(shape, lambda *_: (0,) * nd, pipeline_mode=pl.Buffered(1))


def _proj_kernel(chunk_len, prompt, x_ref, gmix_ref, win_ref, gql_ref, gkv_ref, wuq_ref, wuk_ref,
                 wuv_ref, gq_ref, gk_ref, cos_ref, s1_ref, s2_ref, vn_ref, ws_ref, bst_ref, gog_ref, *rest):
    v_transposed = prompt
    emit_v = not prompt
    if prompt:
        (wo_ref, wu_ref, wd_ref, q_ref, k_ref, v_ref, ckv_ref, kpe_ref, gm_ref,
         wo_out, wu_out, wd_out, gate_scr, vgb_scr) = rest
        wo_out[...] = wo_ref[...].astype(BF16)
        wu_out[...] = wu_ref[...].astype(BF16)
        wd_out[...] = wd_ref[...].astype(BF16)
    else:
        q_ref, k_ref, v_ref, ckv_ref, kpe_ref, gm_ref, vg_ref, gate_scr, vgb_scr = rest
    tm = x_ref.shape[0]
    xn = (_rms(x_ref[...]) * gmix_ref[...]).astype(BF16)
    cos, s1, s2 = cos_ref[...], s1_ref[...], s2_ref[...]

    zq = _dot(xn, win_ref[:, C_Q0:C_Q1])
    qln = (_rms(zq) * gql_ref[...]).astype(BF16)
    qraw = _dot(qln, wuq_ref[...])
    for h in range(N_HEADS):
        lo = h * HEAD_PAD
        nope = qraw[:, lo:lo + LANES]
        rp = qraw[:, lo + LANES:lo + HEAD_PAD]
        r = lax.rsqrt((_sumsq(nope) + _sumsq(rp)) * (1.0 / QK_DIM) + EPS)
        rp = _rope(rp, cos, s1, s2)
        q_ref[:, lo:lo + LANES] = (nope * r * gq_ref[:, :LANES]).astype(BF16)
        q_ref[:, lo + LANES:lo + HEAD_PAD] = (rp * r * gq_ref[:, LANES:]).astype(BF16)

    ckv = _rms(_dot(xn, win_ref[:, C_KV0:C_KV1])) * gkv_ref[...]
    ckv_ref[...] = ckv
    cb = ckv.astype(BF16)
    pe = _rope(_dot(xn, win_ref[:, C_PE0:C_PE1]), cos, s1, s2)
    kpe_ref[...] = pe[:, :ROPE_DIM]
    ss_pe = _sumsq(pe)
    knope = _dot(cb, wuk_ref[...])
    if v_transposed:
        v_ref[...] = _dot_nt(wuv_ref[...], cb).astype(BF16)
    else:
        v_ref[...] = _dot(cb, wuv_ref[...]).astype(BF16)
    for h in range(N_HEADS):
        nope = knope[:, h * LANES:(h + 1) * LANES]
        r = lax.rsqrt((_sumsq(nope) + ss_pe) * (1.0 / QK_DIM) + EPS)
        lo = h * HEAD_PAD
        k_ref[:, lo:lo + LANES] = (nope * r * gk_ref[:, :LANES]).astype(BF16)
        k_ref[:, lo + LANES:lo + HEAD_PAD] = (pe * r * gk_ref[:, LANES:]).astype(BF16)

    gv = _gelu(_dot(xn, win_ref[:, C_V0:C_V1]))
    for g in range(GMLP_GROUPS):
        blk = gv[:, g * LANES:(g + 1) * LANES]
        vg = _rms(blk) * vn_ref[g:g + 1, :]
        if emit_v:
            vg_ref[:, g * LANES:(g + 1) * LANES] = vg
        vgb_scr[:, g * LANES:(g + 1) * LANES] = vg.astype(BF16)
    u = _gelu(_dot(xn, win_ref[:, C_U0:C_U1]))
    row = lax.broadcasted_iota(jnp.int32, (GMLP_CHUNK, GMLP_CHUNK), 0)
    col = lax.broadcasted_iota(jnp.int32, (GMLP_CHUNK, GMLP_CHUNK), 1)
    causal = (row // chunk_len == col // chunk_len) & (col <= row)
    for g in range(GMLP_GROUPS):
        wm = jnp.where(causal, ws_ref[g], 0.0).astype(BF16)
        bias = bst_ref[:, g:g + 1]
        for c in range(tm // GMLP_CHUNK):
            rows = slice(c * GMLP_CHUNK, (c + 1) * GMLP_CHUNK)
            cols = slice(g * LANES, (g + 1) * LANES)
            s = _dot(wm, vgb_scr[rows, cols]) + bias
            gate_scr[rows, cols] = u[rows, cols] * s
    gm_ref[...] = (_rms(gate_scr[...]) * gog_ref[...]).astype(BF16)


def _proj(x2d, tabs, wp, chunk_len, tm, f32_weights=None):
    prompt = f32_weights is not None
    m = x2d.shape[0]
    steps = m // tm
    cos, s1, s2 = tabs
    tab_blocks = cos.shape[0] // tm
    row_spec = lambda w: pl.BlockSpec((tm, w), lambda i: (i, 0))
    tab_spec = pl.BlockSpec((tm, LANES), lambda i: (i % tab_blocks, 0))
    consts = [wp["gmix"], wp["w_in"], wp["gql"], wp["gkv"], wp["w_uq"], wp["w_uk"],
              wp["w_uv_t"] if prompt else wp["w_uv"], wp["gq"], wp["gk"]]
    consts2 = [wp["v_norm"], wp["ws"], wp["bst"], wp["gog"]]
    in_specs = ([row_spec(D_MODEL)] + [_const_spec(c.shape) for c in consts]
                + [tab_spec] * 3 + [_const_spec(c.shape) for c in consts2])
    v_shape = (steps * ATTN_WIDTH, tm) if prompt else (m, ATTN_WIDTH)
    v_spec = pl.BlockSpec((ATTN_WIDTH, tm), lambda i: (i, 0)) if prompt else row_spec(ATTN_WIDTH)
    out_shape = [jax.ShapeDtypeStruct((m, QK_WIDTH), BF16), jax.ShapeDtypeStruct((m, QK_WIDTH), BF16),
                 jax.ShapeDtypeStruct(v_shape, BF16), jax.ShapeDtypeStruct((m, KV_RANK), F32),
                 jax.ShapeDtypeStruct((m, ROPE_DIM), F32), jax.ShapeDtypeStruct((m, GMLP_WIDTH), BF16)]
    out_specs = [row_spec(QK_WIDTH), row_spec(QK_WIDTH), v_spec, row_spec(KV_RANK),
                 row_spec(ROPE_DIM), row_spec(GMLP_WIDTH)]
    extra_in = []
    if prompt:
        w_out, w_up, w_down = f32_weights
        slabs = [pl.BlockSpec((D_MODEL // steps, D_MODEL), lambda i: (i, 0)),
                 pl.BlockSpec((D_MODEL, D_FF // steps), lambda i: (0, i)),
                 pl.BlockSpec((D_FF // steps, D_MODEL), lambda i: (i, 0))]
        extra_in = [w_out, w_up, w_down]
        in_specs += slabs
        out_shape += [jax.ShapeDtypeStruct(w.shape, BF16) for w in extra_in]
        out_specs += slabs
    else:
        out_shape.append(jax.ShapeDtypeStruct((m, GMLP_WIDTH), F32))
        out_specs.append(row_spec(GMLP_WIDTH))
    return pl.pallas_call(
        functools.partial(_proj_kernel, chunk_len, prompt),
        out_shape=out_shape,
        grid=(steps,),
        in_specs=in_specs,
        out_specs=out_specs,
        scratch_shapes=[pltpu.VMEM((tm, GMLP_WIDTH), F32), pltpu.VMEM((tm, GMLP_WIDTH), BF16)],
        compiler_params=pltpu.CompilerParams(dimension_semantics=("parallel",),
                                             vmem_limit_bytes=VMEM_LIMIT),
        name="proj",
    )(x2d, *consts, cos, s1, s2, *consts2, *extra_in)


NEG = -1e30
SOFTMAX_KEY_CHUNKS = 4


def _attn_prompt_kernel(q_ref, k_ref, vt_ref, go_ref, o_ref, acc_scr, m_scr, l_scr, s_scr, p_scr):
    t = q_ref.shape[0]
    ck = t // SOFTMAX_KEY_CHUNKS
    qi = pl.program_id(1)
    m_scr[...] = jnp.full(m_scr.shape, NEG, F32)
    l_scr[...] = jnp.zeros(l_scr.shape, F32)
    acc_scr[...] = jnp.zeros(acc_scr.shape, F32)

    def block(kb, masked):
        rows = pl.ds(pl.multiple_of(kb * t, t), t)
        if masked:
            key = lax.broadcasted_iota(jnp.int32, (t, t), 0)
            qry = lax.broadcasted_iota(jnp.int32, (t, t), 1)
            ok = (key // CHUNK) <= (qry // CHUNK)

        def scores(h):
            qs = slice(h * HEAD_PAD, (h + 1) * HEAD_PAD)
            s = _dot_nt(k_ref[rows, qs], q_ref[:, qs])
            s_scr[h % 2] = jnp.where(ok, s, NEG) if masked else s

        scores(0)
        for h in range(N_HEADS):
            if h + 1 < N_HEADS:
                scores(h + 1)
            slot = h % 2
            hs = slice(h * V_DIM, (h + 1) * V_DIM)
            chunks = [slice(c * ck, (c + 1) * ck) for c in range(SOFTMAX_KEY_CHUNKS)]
            m8 = None
            for c in chunks:
                mc = jnp.max(s_scr[slot, c, :].reshape(ck // SUBLANES, SUBLANES, t), axis=0)
                m8 = mc if m8 is None else jnp.maximum(m8, mc)
            m_old = m_scr[h]
            m_new = jnp.maximum(m_old, jnp.max(m8, axis=0, keepdims=True))
            alpha = jnp.exp2(m_old - m_new)
            l8 = None
            for c in chunks:
                p = jnp.exp2(s_scr[slot, c, :] - m_new)
                lc = jnp.sum(p.reshape(ck // SUBLANES, SUBLANES, t), axis=0)
                l8 = lc if l8 is None else l8 + lc
                p_scr[slot, c, :] = p.astype(BF16)
            l_scr[h] = alpha * l_scr[h] + jnp.sum(l8, axis=0, keepdims=True)
            m_scr[h] = m_new
            vt = vt_ref[pl.ds(pl.multiple_of(kb * ATTN_WIDTH, ATTN_WIDTH) + h * V_DIM, V_DIM), :]
            acc_scr[hs, :] = alpha * acc_scr[hs, :] + _dot(vt, p_scr[slot])

    def body(kb, carry):
        block(kb, False)
        return carry

    lax.fori_loop(0, qi, body, 0)
    block(qi, True)
    for h in range(N_HEADS):
        hs = slice(h * V_DIM, (h + 1) * V_DIM)
        acc_scr[hs, :] = acc_scr[hs, :] / l_scr[h]
    o = acc_scr[...].T
    o_ref[...] = (_rms(o) * go_ref[...]).astype(BF16)


def _attn_prompt(q, k, vt, go, batch, seq, t):
    nq = seq // t
    assert vt.shape == (batch * nq * ATTN_WIDTH, t)
    return pl.pallas_call(
        _attn_prompt_kernel,
        out_shape=jax.ShapeDtypeStruct((batch * seq, ATTN_WIDTH), BF16),
        grid=(batch, nq),
        in_specs=[pl.BlockSpec((t, QK_WIDTH), lambda b, i: (b * nq + i, 0)),
                  pl.BlockSpec((seq, QK_WIDTH), lambda b, i: (b, 0)),
                  pl.BlockSpec((nq * ATTN_WIDTH, t), lambda b, i: (b, 0)),
                  _const_spec(go.shape)],
        out_specs=pl.BlockSpec((t, ATTN_WIDTH), lambda b, i: (b * nq + i, 0)),
        scratch_shapes=[pltpu.VMEM((ATTN_WIDTH, t), F32), pltpu.VMEM((N_HEADS, 1, t), F32),
                        pltpu.VMEM((N_HEADS, 1, t), F32), pltpu.VMEM((2, t, t), F32),
                        pltpu.VMEM((2, t, t), BF16)],
        compiler_params=pltpu.CompilerParams(dimension_semantics=("parallel", "arbitrary"),
                                             vmem_limit_bytes=VMEM_LIMIT),
        name="attn_prompt",
    )(q, k, vt, go)


def _attn_sample_kernel(past, c_ref, kr_ref, q_ref, kn_ref, vn_ref, wuk_ref, wuv_ref, gk_ref, go_ref,
                        o_ref, kc_scr, vc_scr, acc_scr):
    t = q_ref.shape[0]
    cb = c_ref[...].astype(BF16)
    kr = kr_ref[...]
    ss_pe = _sumsq(kr)
    knope = _dot(cb, wuk_ref[...])
    vc_scr[...] = _dot(cb, wuv_ref[...]).astype(BF16)
    zeros = jnp.zeros((past, HEAD_PAD - QK_DIM), BF16)
    for h in range(N_HEADS):
        nope = knope[:, h * LANES:(h + 1) * LANES]
        r = lax.rsqrt((_sumsq(nope) + ss_pe) * (1.0 / QK_DIM) + EPS)
        lo = h * HEAD_PAD
        kc_scr[:, lo:lo + LANES] = (nope * r * gk_ref[:, :LANES]).astype(BF16)
        kc_scr[:, lo + LANES:lo + QK_DIM] = (kr * r * gk_ref[:, LANES:QK_DIM]).astype(BF16)
        kc_scr[:, lo + QK_DIM:lo + HEAD_PAD] = zeros

    row = lax.broadcasted_iota(jnp.int32, (t, t), 0)
    col = lax.broadcasted_iota(jnp.int32, (t, t), 1)
    new_ok = ((past + col) // CHUNK) <= ((past + row) // CHUNK)
    for h in range(N_HEADS):
        qs = slice(h * HEAD_PAD, (h + 1) * HEAD_PAD)
        vs = slice(h * V_DIM, (h + 1) * V_DIM)
        q = q_ref[:, qs]
        s_c = _dot_nt(q, kc_scr[:, qs])
        s_n = jnp.where(new_ok, _dot_nt(q, kn_ref[:, qs]), NEG)
        m = jnp.maximum(jnp.max(s_c, axis=-1, keepdims=True), jnp.max(s_n, axis=-1, keepdims=True))
        p_c = jnp.exp2(s_c - m)
        p_n = jnp.exp2(s_n - m)
        l = jnp.sum(p_c, axis=-1, keepdims=True) + jnp.sum(p_n, axis=-1, keepdims=True)
        acc = _dot(p_c.astype(BF16), vc_scr[:, vs]) + _dot(p_n.astype(BF16), vn_ref[:, vs])
        acc_scr[:, vs] = acc / l
    o_ref[...] = (_rms(acc_scr[...]) * go_ref[...]).astype(BF16)


def _attn_sample(c2d, kr2d, q, kn, vn, wp, batch, t, past):
    consts = [wp["w_uk"], wp["w_uv"], wp["gk"], wp["goa"]]
    return pl.pallas_call(
        functools.partial(_attn_sample_kernel, past),
        out_shape=jax.ShapeDtypeStruct((batch * t, ATTN_WIDTH), BF16),
        grid=(batch,),
        in_specs=[pl.BlockSpec((past, KV_RANK), lambda b: (b, 0)),
                  pl.BlockSpec((past, ROPE_DIM), lambda b: (b, 0)),
                  pl.BlockSpec((t, QK_WIDTH), lambda b: (b, 0)),
                  pl.BlockSpec((t, QK_WIDTH), lambda b: (b, 0)),
                  pl.BlockSpec((t, ATTN_WIDTH), lambda b: (b, 0))]
                 + [_const_spec(c.shape) for c in consts],
        out_specs=pl.BlockSpec((t, ATTN_WIDTH), lambda b: (b, 0)),
        scratch_shapes=[pltpu.VMEM((past, QK_WIDTH), BF16), pltpu.VMEM((past, ATTN_WIDTH), BF16),
                        pltpu.VMEM((t, ATTN_WIDTH), F32)],
        compiler_params=pltpu.CompilerParams(dimension_semantics=("parallel",),
                                             vmem_limit_bytes=VMEM_LIMIT),
        name="attn_sample",
    )(c2d, kr2d, q, kn, vn, *consts)


def _outproj_kernel(an_ref, gm_ref, x_ref, wo_ref, gffn_ref, h_ref, hn_ref):
    h = (x_ref[...] + _dot(an_ref[...], wo_ref[:ATTN_WIDTH, :])
         + _dot(gm_ref[...], wo_ref[ATTN_WIDTH:, :]))
    h_ref[...] = h
    hn_ref[...] = (_rms(h) * gffn_ref[...]).astype(BF16)


def _outproj(an, gm, x2d, wp, tm):
    m = x2d.shape[0]
    row_spec = lambda w: pl.BlockSpec((tm, w), lambda i: (i, 0))
    return pl.pallas_call(
        _outproj_kernel,
        out_shape=[jax.ShapeDtypeStruct((m, D_MODEL), F32), jax.ShapeDtypeStruct((m, D_MODEL), BF16)],
        grid=(m // tm,),
        in_specs=[row_spec(ATTN_WIDTH), row_spec(GMLP_WIDTH), row_spec(D_MODEL),
                  _const_spec(wp["w_out"].shape), _const_spec(wp["gffn"].shape)],
        out_specs=[row_spec(D_MODEL), row_spec(D_MODEL)],
        compiler_params=pltpu.CompilerParams(dimension_semantics=("parallel",),
                                             vmem_limit_bytes=VMEM_LIMIT),
        name="outproj",
    )(an, gm, x2d, wp["w_out"], wp["gffn"])


def _ffn_kernel(h_ref, hn_ref, wu_ref, wd_ref, y_ref):
    f = pl.program_id(1)

    @pl.when(f == 0)
    def _():
        y_ref[...] = h_ref[...]

    a = jnp.maximum(_dot(hn_ref[...], wu_ref[...]), 0.0)
    y_ref[...] += _dot((a * a).astype(BF16), wd_ref[...])


def _ffn(h, hn, wp, tm, tf):
    m = h.shape[0]
    return pl.pallas_call(
        _ffn_kernel,
        out_shape=jax.ShapeDtypeStruct((m, D_MODEL), F32),
        grid=(m // tm, D_FF // tf),
        in_specs=[pl.BlockSpec((tm, D_MODEL), lambda i, f: (i, 0), pipeline_mode=pl.Buffered(1)),
                  pl.BlockSpec((tm, D_MODEL), lambda i, f: (i, 0)),
                  pl.BlockSpec((D_MODEL, tf), lambda i, f: (0, f)),
                  pl.BlockSpec((tf, D_MODEL), lambda i, f: (f, 0))],
        out_specs=pl.BlockSpec((tm, D_MODEL), lambda i, f: (i, 0)),
        compiler_params=pltpu.CompilerParams(dimension_semantics=("parallel", "arbitrary"),
                                             vmem_limit_bytes=VMEM_LIMIT),
        name="ffn",
    )(h, hn, wp["w_up"], wp["w_down"])


def _rope_tables(pos):
    inv = ROPE_THETA ** (-jnp.arange(HALF_ROPE, dtype=F32) / HALF_ROPE)
    ang = pos.astype(F32)[:, None] * inv[None, :]
    cos, sin = jnp.cos(ang), jnp.sin(ang)
    z = jnp.zeros_like(cos)
    z2 = jnp.zeros((pos.shape[0], LANES - ROPE_DIM), F32)
    return (jnp.concatenate([cos, cos, z2], axis=1), jnp.concatenate([-sin, z, z2], axis=1),
            jnp.concatenate([z, sin, z2], axis=1))


def _head_gain(g_nope, g_rope, scale):
    pad = jnp.zeros((HEAD_PAD - QK_DIM,), F32)
    return (jnp.concatenate([g_nope, g_rope, g_rope, pad]) * scale)[None, :]


def _pad_w_in_kernel(w_ref, o_ref):
    w = w_ref[...]
    rows = w.shape[0]
    split = C_PE0 + ROPE_DIM
    o_ref[:, :split] = w[:, :split].astype(BF16)
    o_ref[:, split:C_PE1] = jnp.zeros((rows, C_PE1 - split), BF16)
    o_ref[:, C_PE1:] = w[:, split:].astype(BF16)


def _pad_w_in(w_in, tr):
    rows, width = w_in.shape
    assert width + LANES - ROPE_DIM == IN_PAD
    return pl.pallas_call(
        _pad_w_in_kernel,
        out_shape=jax.ShapeDtypeStruct((rows, IN_PAD), BF16),
        grid=(rows // tr,),
        in_specs=[pl.BlockSpec((tr, width), lambda i: (i, 0))],
        out_specs=pl.BlockSpec((tr, IN_PAD), lambda i: (i, 0)),
        compiler_params=pltpu.CompilerParams(dimension_semantics=("parallel",)),
        name="pad_w_in",
    )(w_in)


def _prep_weights(norm_mix, w_in, q_lat_norm, kv_lat_norm, w_uq, w_uk, w_uv, q_norm_nope, q_norm_rope,
                  k_norm_nope, k_norm_rope, v_norm, w_spatial, b_spatial, out_norm_attn, out_norm_gmlp,
                  norm_ffn, chunk_len):
    w_in_p = _pad_w_in(w_in, 256)
    wq = w_uq.reshape(Q_RANK, N_HEADS, QK_DIM)
    wq = jnp.pad(wq, ((0, 0), (0, 0), (0, HEAD_PAD - QK_DIM))).reshape(Q_RANK, QK_WIDTH).astype(BF16)
    reps = GMLP_CHUNK // chunk_len
    ws = jnp.tile(w_spatial[:, :chunk_len, :chunk_len], (1, reps, reps))
    bst = jnp.tile(b_spatial[:, :chunk_len], (1, reps)).T
    return {
        "gmix": norm_mix[None, :], "w_in": w_in_p, "gql": q_lat_norm[None, :], "gkv": kv_lat_norm[None, :],
        "w_uq": wq, "w_uk": w_uk.astype(BF16), "w_uv": w_uv.astype(BF16), "w_uv_t": w_uv.T.astype(BF16),
        "gq": _head_gain(q_norm_nope, q_norm_rope, QK_DIM ** -0.5 * math.log2(math.e)),
        "gk": _head_gain(k_norm_nope, k_norm_rope, 1.0),
        "v_norm": v_norm, "ws": ws, "bst": bst, "gog": out_norm_gmlp[None, :],
        "goa": out_norm_attn[None, :], "gffn": norm_ffn[None, :],
    }


def _finish(an, gm, x2d, wp, tm_ffn):
    h, hn = _outproj(an, gm, x2d, wp, 512)
    return _ffn(h, hn, wp, tm_ffn, 512)


def kernel(x_prompt, x_sample, cache_c_kv, cache_k_rope, norm_mix, w_in, q_lat_norm, kv_lat_norm, w_uq, w_uk, w_uv, q_norm_nope, q_norm_rope, k_norm_nope, k_norm_rope, v_norm, w_spatial, b_spatial, out_norm_attn, out_norm_gmlp, w_out, norm_ffn, w_up, w_down):
    depth = w_in.shape[0]
    assert depth == 1
    batch, seq, _ = x_prompt.shape
    dec_batch, dec_seq, _ = x_sample.shape
    past = cache_c_kv.shape[2]
    assert past % CHUNK == 0 and dec_seq <= CHUNK and GMLP_CHUNK % dec_seq == 0

    weights = (norm_mix[0], w_in[0], q_lat_norm[0], kv_lat_norm[0], w_uq[0], w_uk[0], w_uv[0],
               q_norm_nope[0], q_norm_rope[0], k_norm_nope[0], k_norm_rope[0], v_norm[0], w_spatial[0],
               b_spatial[0], out_norm_attn[0], out_norm_gmlp[0], norm_ffn[0])
    wp = _prep_weights(*weights, chunk_len=GMLP_CHUNK)
    reps = GMLP_CHUNK // dec_seq
    ws_s = jnp.tile(w_spatial[0][:, :dec_seq, :dec_seq], (1, reps, reps))
    bst_s = jnp.tile(b_spatial[0][:, :dec_seq], (1, reps)).T

    tm_p = 256
    xp = x_prompt.reshape(batch * seq, D_MODEL)
    tabs_p = _rope_tables(jnp.arange(seq, dtype=jnp.int32))
    q, k, vt, ckv_p, kpe_p, gm, wo_b, wu_b, wd_b = _proj(
        xp, tabs_p, wp, GMLP_CHUNK, tm_p, f32_weights=(w_out[0], w_up[0], w_down[0]))
    wp = dict(wp, w_out=wo_b, w_up=wu_b, w_down=wd_b)
    wp_s = dict(wp, ws=ws_s, bst=bst_s)
    an = _attn_prompt(q, k, vt, wp["goa"], batch, seq, tm_p)
    y_p = _finish(an, gm, xp, wp, 1024)

    tm_s = 256
    xs = x_sample.reshape(dec_batch * dec_seq, D_MODEL)
    pos_s = past + (jnp.arange(tm_s, dtype=jnp.int32) % dec_seq)
    tabs_s = _rope_tables(pos_s)
    qs, kn, vn, ckv_s, kpe_s, gms, vg_s = _proj(xs, tabs_s, wp_s, dec_seq, tm_s)
    ans = _attn_sample(cache_c_kv[0].reshape(dec_batch * past, KV_RANK),
                       cache_k_rope[0].reshape(dec_batch * past, ROPE_DIM),
                       qs, kn, vn, wp, dec_batch, dec_seq, past)
    y_s = _finish(ans, gms, xs, wp, 1024)

    return (y_p.reshape(batch, seq, D_MODEL),
            y_s.reshape(dec_batch, dec_seq, D_MODEL),
            ckv_p.reshape(1, batch, seq, KV_RANK),
            kpe_p.reshape(1, batch, seq, ROPE_DIM),
            ckv_s.reshape(1, dec_batch, dec_seq, KV_RANK),
            kpe_s.reshape(1, dec_batch, dec_seq, ROPE_DIM),
            vg_s.reshape(1, dec_batch, dec_seq, GMLP_WIDTH))
```

```python
import functools
import math

import jax
import jax.numpy as jnp
from jax import lax
from jax.experimental import pallas as pl
from jax.experimental.pallas import tpu as pltpu

D_MODEL = 2048
CHUNK = 64
N_HEADS = 8
NOPE_DIM = 128
ROPE_DIM = 64
HALF_ROPE = ROPE_DIM // 2
QK_DIM = NOPE_DIM + ROPE_DIM
V_DIM = 128
ATTN_WIDTH = N_HEADS * V_DIM
Q_RANK = 512
KV_RANK = 256
GMLP_GROUPS = 8
GMLP_GROUP_DIM = 128
GMLP_WIDTH = GMLP_GROUPS * GMLP_GROUP_DIM
GMLP_CHUNK = 128
D_FF = 4 * D_MODEL
ROPE_THETA = 10000.0
EPS = 1e-6

LANES = 128
SUBLANES = 8
HEAD_PAD = 2 * LANES
QK_WIDTH = N_HEADS * HEAD_PAD
C_Q0, C_Q1 = 0, Q_RANK
C_KV0, C_KV1 = C_Q1, C_Q1 + KV_RANK
C_PE0, C_PE1 = C_KV1, C_KV1 + LANES
C_U0, C_U1 = C_PE1, C_PE1 + GMLP_WIDTH
C_V0, C_V1 = C_U1, C_U1 + GMLP_WIDTH
IN_PAD = C_V1
VMEM_LIMIT = 56 * 1024 * 1024

BF16 = jnp.bfloat16
F32 = jnp.float32


def _dot(a, b):
    return jnp.dot(a, b, preferred_element_type=F32)


def _dot_nt(a, b):
    return lax.dot_general(a, b, (((1,), (1,)), ((), ())), preferred_element_type=F32)


def _sumsq(x):
    return jnp.sum(x * x, axis=-1, keepdims=True)


def _rms(x):
    return x * lax.rsqrt(jnp.mean(x * x, axis=-1, keepdims=True) + EPS)


def _gelu(x):
    c = math.sqrt(2.0 / math.pi)
    return 0.5 * x * (1.0 + jnp.tanh(c * (x + 0.044715 * (x * x * x))))


def _rope(t, cos, s1, s2):
    return t * cos + pltpu.roll(t, LANES - HALF_ROPE, 1) * s1 + pltpu.roll(t, HALF_ROPE, 1) * s2


def _const_spec(shape):
    nd = len(shape)
    return pl.BlockSpec(shape, lambda *_: (0,) * nd, pipeline_mode=pl.Buffered(1))


def _proj_kernel(chunk_len, prompt, x_ref, gmix_ref, win_ref, gql_ref, gkv_ref, wuq_ref, wuk_ref,
                 wuv_ref, gq_ref, cos_ref, s1_ref, s2_ref, vn_ref, ws_ref, bst_ref, gog_ref, *rest):
    v_transposed = prompt
    emit_v = not prompt
    if prompt:
        (wo_ref, wu_ref, wd_ref, q_ref, k_ref, v_ref, ckv_ref, kpe_ref, gm_ref,
         wo_out, wu_out, wd_out, gate_scr, vgb_scr) = rest
        wo_out[...] = wo_ref[...].astype(BF16)
        wu_out[...] = wu_ref[...].astype(BF16)
        wd_out[...] = wd_ref[...].astype(BF16)
    else:
        q_ref, k_ref, v_ref, ckv_ref, kpe_ref, gm_ref, vg_ref, gate_scr, vgb_scr = rest
    tm = x_ref.shape[0]
    xn = (_rms(x_ref[...]) * gmix_ref[...]).astype(BF16)
    cos, s1, s2 = cos_ref[...], s1_ref[...], s2_ref[...]

    zq = _dot(xn, win_ref[:, C_Q0:C_Q1])
    qln = (_rms(zq) * gql_ref[...]).astype(BF16)
    qraw = _dot(qln, wuq_ref[...])
    for h in range(N_HEADS):
        lo = h * HEAD_PAD
        nope = qraw[:, lo:lo + LANES]
        rp = qraw[:, lo + LANES:lo + HEAD_PAD]
        r = lax.rsqrt((_sumsq(nope) + _sumsq(rp)) * (1.0 / QK_DIM) + EPS)
        rp = _rope(rp, cos, s1, s2)
        q_ref[:, lo:lo + LANES] = (nope * r * gq_ref[:, :LANES]).astype(BF16)
        q_ref[:, lo + LANES:lo + HEAD_PAD] = (rp * r * gq_ref[:, LANES:]).astype(BF16)

    ckv = _rms(_dot(xn, win_ref[:, C_KV0:C_KV1])) * gkv_ref[...]
    ckv_ref[...] = ckv
    cb = ckv.astype(BF16)
    pe = _rope(_dot(xn, win_ref[:, C_PE0:C_PE1]), cos, s1, s2)
    kpe_ref[...] = pe[:, :ROPE_DIM]
    ss_pe = _sumsq(pe)
    knope = _dot(cb, wuk_ref[...])
    if v_transposed:
        v_ref[...] = _dot_nt(wuv_ref[...], cb).astype(BF16)
    else:
        v_ref[...] = _dot(cb, wuv_ref[...]).astype(BF16)
    for h in range(N_HEADS):
        nope = knope[:, h * LANES:(h + 1) * LANES]
        r = lax.rsqrt((_sumsq(nope) + ss_pe) * (1.0 / QK_DIM) + EPS)
        lo = h * HEAD_PAD
        k_ref[:, lo:lo + LANES] = (nope * r).astype(BF16)
        k_ref[:, lo + LANES:lo + HEAD_PAD] = (pe * r).astype(BF16)

    gv = _gelu(_dot(xn, win_ref[:, C_V0:C_V1]))
    for g in range(GMLP_GROUPS):
        blk = gv[:, g * LANES:(g + 1) * LANES]
        vg = _rms(blk) * vn_ref[g:g + 1, :]
        if emit_v:
            vg_ref[:, g * LANES:(g + 1) * LANES] = vg
        vgb_scr[:, g * LANES:(g + 1) * LANES] = vg.astype(BF16)
    u = _gelu(_dot(xn, win_ref[:, C_U0:C_U1]))
    row = lax.broadcasted_iota(jnp.int32, (GMLP_CHUNK, GMLP_CHUNK), 0)
    col = lax.broadcasted_iota(jnp.int32, (GMLP_CHUNK, GMLP_CHUNK), 1)
    causal = (row // chunk_len == col // chunk_len) & (col <= row)
    for g in range(GMLP_GROUPS):
        wm = jnp.where(causal, ws_ref[g], 0.0).astype(BF16)
        bias = bst_ref[:, g:g + 1]
        for c in range(tm // GMLP_CHUNK):
            rows = slice(c * GMLP_CHUNK, (c + 1) * GMLP_CHUNK)
            cols = slice(g * LANES, (g + 1) * LANES)
            s = _dot(wm, vgb_scr[rows, cols]) + bias
            gate_scr[rows, cols] = u[rows, cols] * s
    gm_ref[...] = (_rms(gate_scr[...]) * gog_ref[...]).astype(BF16)


def _proj(x2d, tabs, wp, chunk_len, tm, f32_weights=None):
    prompt = f32_weights is not None
    m = x2d.shape[0]
    steps = m // tm
    cos, s1, s2 = tabs
    tab_blocks = cos.shape[0] // tm
    row_spec = lambda w: pl.BlockSpec((tm, w), lambda i: (i, 0))
    tab_spec = pl.BlockSpec((tm, LANES), lambda i: (i % tab_blocks, 0))
    consts = [wp["gmix"], wp["w_in"], wp["gql"], wp["gkv"], wp["w_uq"], wp["w_uk"],
              wp["w_uv_t"] if prompt else wp["w_uv"], wp["gq"]]
    consts2 = [wp["v_norm"], wp["ws"], wp["bst"], wp["gog"]]
    in_specs = ([row_spec(D_MODEL)] + [_const_spec(c.shape) for c in consts]
                + [tab_spec] * 3 + [_const_spec(c.shape) for c in consts2])
    v_shape = (steps * ATTN_WIDTH, tm) if prompt else (m, ATTN_WIDTH)
    v_spec = pl.BlockSpec((ATTN_WIDTH, tm), lambda i: (i, 0)) if prompt else row_spec(ATTN_WIDTH)
    out_shape = [jax.ShapeDtypeStruct((m, QK_WIDTH), BF16), jax.ShapeDtypeStruct((m, QK_WIDTH), BF16),
                 jax.ShapeDtypeStruct(v_shape, BF16), jax.ShapeDtypeStruct((m, KV_RANK), F32),
                 jax.ShapeDtypeStruct((m, ROPE_DIM), F32), jax.ShapeDtypeStruct((m, GMLP_WIDTH), BF16)]
    out_specs = [row_spec(QK_WIDTH), row_spec(QK_WIDTH), v_spec, row_spec(KV_RANK),
                 row_spec(ROPE_DIM), row_spec(GMLP_WIDTH)]
    extra_in = []
    if prompt:
        w_out, w_up, w_down = f32_weights
        slabs = [pl.BlockSpec((D_MODEL // steps, D_MODEL), lambda i: (i, 0)),
                 pl.BlockSpec((D_MODEL, D_FF // steps), lambda i: (0, i)),
                 pl.BlockSpec((D_FF // steps, D_MODEL), lambda i: (i, 0))]
        extra_in = [w_out, w_up, w_down]
        in_specs += slabs
        out_shape += [jax.ShapeDtypeStruct(w.shape, BF16) for w in extra_in]
        out_specs += slabs
    else:
        out_shape.append(jax.ShapeDtypeStruct((m, GMLP_WIDTH), F32))
        out_specs.append(row_spec(GMLP_WIDTH))
    return pl.pallas_call(
        functools.partial(_proj_kernel, chunk_len, prompt),
        out_shape=out_shape,
        grid=(steps,),
        in_specs=in_specs,
        out_specs=out_specs,
        scratch_shapes=[pltpu.VMEM((tm, GMLP_WIDTH), F32), pltpu.VMEM((tm, GMLP_WIDTH), BF16)],
        compiler_params=pltpu.CompilerParams(dimension_semantics=("parallel",),
                                             vmem_limit_bytes=VMEM_LIMIT),
        name="proj",
    )(x2d, *consts, cos, s1, s2, *consts2, *extra_in)


NEG = -1e30
SOFTMAX_KEY_CHUNKS = 4


def _attn_prompt_kernel(q_ref, k_ref, vt_ref, go_ref, o_ref, acc_scr, m_scr, l_scr, s_scr, p_scr):
    t = q_ref.shape[0]
    ck = t // SOFTMAX_KEY_CHUNKS
    qi = pl.program_id(1)
    m_scr[...] = jnp.full(m_scr.shape, NEG, F32)
    l_scr[...] = jnp.zeros(l_scr.shape, F32)
    acc_scr[...] = jnp.zeros(acc_scr.shape, F32)

    def block(kb, masked):
        rows = pl.ds(pl.multiple_of(kb * t, t), t)
        if masked:
            key = lax.broadcasted_iota(jnp.int32, (t, t), 0)
            qry = lax.broadcasted_iota(jnp.int32, (t, t), 1)
            ok = (key // CHUNK) <= (qry // CHUNK)

        def scores(h):
            qs = slice(h * HEAD_PAD, (h + 1) * HEAD_PAD)
            s = _dot_nt(k_ref[rows, qs], q_ref[:, qs])
            s_scr[h % 2] = jnp.where(ok, s, NEG) if masked else s

        scores(0)
        for h in range(N_HEADS):
            if h + 1 < N_HEADS:
                scores(h + 1)
            slot = h % 2
            hs = slice(h * V_DIM, (h + 1) * V_DIM)
            chunks = [slice(c * ck, (c + 1) * ck) for c in range(SOFTMAX_KEY_CHUNKS)]
            m8 = None
            for c in chunks:
                mc = jnp.max(s_scr[slot, c, :].reshape(ck // SUBLANES, SUBLANES, t), axis=0)
                m8 = mc if m8 is None else jnp.maximum(m8, mc)
            m_old = m_scr[h]
            m_new = jnp.maximum(m_old, jnp.max(m8, axis=0, keepdims=True))
            alpha = jnp.exp2(m_old - m_new)
            l8 = None
            for c in chunks:
                p = jnp.exp2(s_scr[slot, c, :] - m_new)
                lc = jnp.sum(p.reshape(ck // SUBLANES, SUBLANES, t), axis=0)
                l8 = lc if l8 is None else l8 + lc
                p_scr[slot, c, :] = p.astype(BF16)
            l_scr[h] = alpha * l_scr[h] + jnp.sum(l8, axis=0, keepdims=True)
            m_scr[h] = m_new
            vt = vt_ref[pl.ds(pl.multiple_of(kb * ATTN_WIDTH, ATTN_WIDTH) + h * V_DIM, V_DIM), :]
            acc_scr[hs, :] = alpha * acc_scr[hs, :] + _dot(vt, p_scr[slot])

    def body(kb, carry):
        block(kb, False)
        return carry

    lax.fori_loop(0, qi, body, 0)
    block(qi, True)
    for h in range(N_HEADS):
        hs = slice(h * V_DIM, (h + 1) * V_DIM)
        acc_scr[hs, :] = acc_scr[hs, :] / l_scr[h]
    o = acc_scr[...].T
    o_ref[...] = (_rms(o) * go_ref[...]).astype(BF16)


def _attn_prompt(q, k, vt, go, batch, seq, t):
    nq = seq // t
    assert vt.shape == (batch * nq * ATTN_WIDTH, t)
    return pl.pallas_call(
        _attn_prompt_kernel,
        out_shape=jax.ShapeDtypeStruct((batch * seq, ATTN_WIDTH), BF16),
        grid=(batch, nq),
        in_specs=[pl.BlockSpec((t, QK_WIDTH), lambda b, i: (b * nq + i, 0)),
                  pl.BlockSpec((seq, QK_WIDTH), lambda b, i: (b, 0)),
                  pl.BlockSpec((nq * ATTN_WIDTH, t), lambda b, i: (b, 0)),
                  _const_spec(go.shape)],
        out_specs=pl.BlockSpec((t, ATTN_WIDTH), lambda b, i: (b * nq + i, 0)),
        scratch_shapes=[pltpu.VMEM((ATTN_WIDTH, t), F32), pltpu.VMEM((N_HEADS, 1, t), F32),
                        pltpu.VMEM((N_HEADS, 1, t), F32), pltpu.VMEM((2, t, t), F32),
                        pltpu.VMEM((2, t, t), BF16)],
        compiler_params=pltpu.CompilerParams(dimension_semantics=("parallel", "arbitrary"),
                                             vmem_limit_bytes=VMEM_LIMIT),
        name="attn_prompt",
    )(q, k, vt, go)


def _attn_sample_kernel(past, c_ref, krt_ref, q_ref, kn_ref, vn_ref, wukt_ref, wuv_ref, go_ref,
                        o_ref, kct_scr, vc_scr, acc_scr):
    t = q_ref.shape[0]
    cb = c_ref[...].astype(BF16)
    krt = krt_ref[...]
    ss_pe = jnp.sum(krt * krt, axis=0, keepdims=True)
    knt = _dot_nt(wukt_ref[...], cb)
    vc_scr[...] = _dot(cb, wuv_ref[...]).astype(BF16)
    zeros = jnp.zeros((HEAD_PAD - QK_DIM, past), BF16)
    for h in range(N_HEADS):
        nope = knt[h * LANES:(h + 1) * LANES, :]
        ss = jnp.sum(nope * nope, axis=0, keepdims=True) + ss_pe
        r = lax.rsqrt(ss * (1.0 / QK_DIM) + EPS)
        lo = h * HEAD_PAD
        kct_scr[lo:lo + LANES, :] = (nope * r).astype(BF16)
        kct_scr[lo + LANES:lo + QK_DIM, :] = (krt * r).astype(BF16)
        kct_scr[lo + QK_DIM:lo + HEAD_PAD, :] = zeros

    row = lax.broadcasted_iota(jnp.int32, (t, t), 0)
    col = lax.broadcasted_iota(jnp.int32, (t, t), 1)
    new_ok = ((past + col) // CHUNK) <= ((past + row) // CHUNK)
    for h in range(N_HEADS):
        qs = slice(h * HEAD_PAD, (h + 1) * HEAD_PAD)
        vs = slice(h * V_DIM, (h + 1) * V_DIM)
        q = q_ref[:, qs]
        s_c = _dot(q, kct_scr[qs, :])
        s_n = jnp.where(new_ok, _dot_nt(q, kn_ref[:, qs]), NEG)
        m = jnp.maximum(jnp.max(s_c, axis=-1, keepdims=True), jnp.max(s_n, axis=-1, keepdims=True))
        p_c = jnp.exp2(s_c - m)
        p_n = jnp.exp2(s_n - m)
        l = jnp.sum(p_c, axis=-1, keepdims=True) + jnp.sum(p_n, axis=-1, keepdims=True)
        acc = _dot(p_c.astype(BF16), vc_scr[:, vs]) + _dot(p_n.astype(BF16), vn_ref[:, vs])
        acc_scr[:, vs] = acc / l
    o_ref[...] = (_rms(acc_scr[...]) * go_ref[...]).astype(BF16)


def _attn_sample(c2d, krt, q, kn, vn, wp, batch, t, past):
    consts = [wp["w_uk_t"], wp["w_uv"], wp["goa"]]
    return pl.pallas_call(
        functools.partial(_attn_sample_kernel, past),
        out_shape=jax.ShapeDtypeStruct((batch * t, ATTN_WIDTH), BF16),
        grid=(batch,),
        in_specs=[pl.BlockSpec((past, KV_RANK), lambda b: (b, 0)),
                  pl.BlockSpec((None, ROPE_DIM, past), lambda b: (b, 0, 0)),
                  pl.BlockSpec((t, QK_WIDTH), lambda b: (b, 0)),
                  pl.BlockSpec((t, QK_WIDTH), lambda b: (b, 0)),
                  pl.BlockSpec((t, ATTN_WIDTH), lambda b: (b, 0))]
                 + [_const_spec(c.shape) for c in consts],
        out_specs=pl.BlockSpec((t, ATTN_WIDTH), lambda b: (b, 0)),
        scratch_shapes=[pltpu.VMEM((QK_WIDTH, past), BF16), pltpu.VMEM((past, ATTN_WIDTH), BF16),
                        pltpu.VMEM((t, ATTN_WIDTH), F32)],
        compiler_params=pltpu.CompilerParams(dimension_semantics=("parallel",),
                                             vmem_limit_bytes=VMEM_LIMIT),
        name="attn_sample",
    )(c2d, krt, q, kn, vn, *consts)


def _outproj_kernel(an_ref, gm_ref, x_ref, wo_ref, gffn_ref, h_ref, hn_ref):
    h = (x_ref[...] + _dot(an_ref[...], wo_ref[:ATTN_WIDTH, :])
         + _dot(gm_ref[...], wo_ref[ATTN_WIDTH:, :]))
    h_ref[...] = h
    hn_ref[...] = (_rms(h) * gffn_ref[...]).astype(BF16)


def _outproj(an, gm, x2d, wp, tm):
    m = x2d.shape[0]
    row_spec = lambda w: pl.BlockSpec((tm, w), lambda i: (i, 0))
    return pl.pallas_call(
        _outproj_kernel,
        out_shape=[jax.ShapeDtypeStruct((m, D_MODEL), F32), jax.ShapeDtypeStruct((m, D_MODEL), BF16)],
        grid=(m // tm,),
        in_specs=[row_spec(ATTN_WIDTH), row_spec(GMLP_WIDTH), row_spec(D_MODEL),
                  _const_spec(wp["w_out"].shape), _const_spec(wp["gffn"].shape)],
        out_specs=[row_spec(D_MODEL), row_spec(D_MODEL)],
        compiler_params=pltpu.CompilerParams(dimension_semantics=("parallel",),
                                             vmem_limit_bytes=VMEM_LIMIT),
        name="outproj",
    )(an, gm, x2d, wp["w_out"], wp["gffn"])


def _ffn_kernel(h_ref, hn_ref, wu_ref, wd_ref, y_ref):
    f = pl.program_id(1)

    @pl.when(f == 0)
    def _():
        y_ref[...] = h_ref[...]

    a = jnp.maximum(_dot(hn_ref[...], wu_ref[...]), 0.0)
    y_ref[...] += _dot((a * a).astype(BF16), wd_ref[...])


def _ffn(h, hn, wp, tm, tf):
    m = h.shape[0]
    return pl.pallas_call(
        _ffn_kernel,
        out_shape=jax.ShapeDtypeStruct((m, D_MODEL), F32),
        grid=(m // tm, D_FF // tf),
        in_specs=[pl.BlockSpec((tm, D_MODEL), lambda i, f: (i, 0), pipeline_mode=pl.Buffered(1)),
                  pl.BlockSpec((tm, D_MODEL), lambda i, f: (i, 0)),
                  pl.BlockSpec((D_MODEL, tf), lambda i, f: (0, f)),
                  pl.BlockSpec((tf, D_MODEL), lambda i, f: (f, 0))],
        out_specs=pl.BlockSpec((tm, D_MODEL), lambda i, f: (i, 0)),
        compiler_params=pltpu.CompilerParams(dimension_semantics=("parallel", "arbitrary"),
                                             vmem_limit_bytes=VMEM_LIMIT),
        name="ffn",
    )(h, hn, wp["w_up"], wp["w_down"])


def _rope_tables(pos):
    inv = ROPE_THETA ** (-jnp.arange(HALF_ROPE, dtype=F32) / HALF_ROPE)
    ang = pos.astype(F32)[:, None] * inv[None, :]
    cos, sin = jnp.cos(ang), jnp.sin(ang)
    z = jnp.zeros_like(cos)
    z2 = jnp.zeros((pos.shape[0], LANES - ROPE_DIM), F32)
    return (jnp.concatenate([cos, cos, z2], axis=1), jnp.concatenate([-sin, z, z2], axis=1),
            jnp.concatenate([z, sin, z2], axis=1))


def _head_gain(g_nope, g_rope, scale):
    pad = jnp.zeros((HEAD_PAD - QK_DIM,), F32)
    return (jnp.concatenate([g_nope, g_rope, g_rope, pad]) * scale)[None, :]


def _pad_w_in_kernel(wt_ref, o_ref):
    tr = wt_ref.shape[1]
    split = C_PE0 + ROPE_DIM
    o_ref[:, :C_PE0] = wt_ref[:C_PE0, :].T.astype(BF16)
    pe = wt_ref[C_PE0:C_PE1, :].T
    lane = lax.broadcasted_iota(jnp.int32, (tr, LANES), 1)
    o_ref[:, C_PE0:C_PE1] = jnp.where(lane < ROPE_DIM, pe, 0.0).astype(BF16)
    o_ref[:, C_PE1:] = wt_ref[split:, :].T.astype(BF16)


def _pad_w_in(w_in_t, tr):
    width, rows = w_in_t.shape
    assert width + LANES - ROPE_DIM == IN_PAD
    return pl.pallas_call(
        _pad_w_in_kernel,
        out_shape=jax.ShapeDtypeStruct((rows, IN_PAD), BF16),
        grid=(rows // tr,),
        in_specs=[pl.BlockSpec((width, tr), lambda i: (0, i))],
        out_specs=pl.BlockSpec((tr, IN_PAD), lambda i: (i, 0)),
        compiler_params=pltpu.CompilerParams(dimension_semantics=("parallel",)),
        name="pad_w_in",
    )(w_in_t)


def _prep_weights(norm_mix, w_in, q_lat_norm, kv_lat_norm, w_uq, w_uk, w_uv, q_norm_nope, q_norm_rope,
                  k_norm_nope, k_norm_rope, v_norm, w_spatial, b_spatial, out_norm_attn, out_norm_gmlp,
                  norm_ffn, chunk_len):
    w_in_p = _pad_w_in(w_in.T, 256)
    wq = w_uq.reshape(Q_RANK, N_HEADS, QK_DIM)
    wq = jnp.pad(wq, ((0, 0), (0, 0), (0, HEAD_PAD - QK_DIM))).reshape(Q_RANK, QK_WIDTH).astype(BF16)
    reps = GMLP_CHUNK // chunk_len
    ws = jnp.tile(w_spatial[:, :chunk_len, :chunk_len], (1, reps, reps))
    bst = jnp.tile(b_spatial[:, :chunk_len], (1, reps)).T
    return {
        "gmix": norm_mix[None, :], "w_in": w_in_p, "gql": q_lat_norm[None, :], "gkv": kv_lat_norm[None, :],
        "w_uq": wq, "w_uk": w_uk.astype(BF16), "w_uk_t": w_uk.T.astype(BF16),
        "w_uv": w_uv.astype(BF16), "w_uv_t": w_uv.T.astype(BF16),
        "gq": (_head_gain(q_norm_nope, q_norm_rope, QK_DIM ** -0.5 * math.log2(math.e))
               * _head_gain(k_norm_nope, k_norm_rope, 1.0)),
        "v_norm": v_norm, "ws": ws, "bst": bst, "gog": out_norm_gmlp[None, :],
        "goa": out_norm_attn[None, :], "gffn": norm_ffn[None, :],
    }


def _finish(an, gm, x2d, wp, tm_ffn):
    h, hn = _outproj(an, gm, x2d, wp, 512)
    return _ffn(h, hn, wp, tm_ffn, 512)


def kernel(x_prompt, x_sample, cache_c_kv, cache_k_rope, norm_mix, w_in, q_lat_norm, kv_lat_norm, w_uq, w_uk, w_uv, q_norm_nope, q_norm_rope, k_norm_nope, k_norm_rope, v_norm, w_spatial, b_spatial, out_norm_attn, out_norm_gmlp, w_out, norm_ffn, w_up, w_down):
    depth = w_in.shape[0]
    assert depth == 1
    batch, seq, _ = x_prompt.shape
    dec_batch, dec_seq, _ = x_sample.shape
    past = cache_c_kv.shape[2]
    assert past % CHUNK == 0 and dec_seq <= CHUNK and GMLP_CHUNK % dec_seq == 0

    weights = (norm_mix[0], w_in[0], q_lat_norm[0], kv_lat_norm[0], w_uq[0], w_uk[0], w_uv[0],
               q_norm_nope[0], q_norm_rope[0], k_norm_nope[0], k_norm_rope[0], v_norm[0], w_spatial[0],
               b_spatial[0], out_norm_attn[0], out_norm_gmlp[0], norm_ffn[0])
    wp = _prep_weights(*weights, chunk_len=GMLP_CHUNK)
    reps = GMLP_CHUNK // dec_seq
    ws_s = jnp.tile(w_spatial[0][:, :dec_seq, :dec_seq], (1, reps, reps))
    bst_s = jnp.tile(b_spatial[0][:, :dec_seq], (1, reps)).T

    tm_p = 256
    xp = x_prompt.reshape(batch * seq, D_MODEL)
    tabs_p = _rope_tables(jnp.arange(seq, dtype=jnp.int32))
    q, k, vt, ckv_p, kpe_p, gm, wo_b, wu_b, wd_b = _proj(
        xp, tabs_p, wp, GMLP_CHUNK, tm_p, f32_weights=(w_out[0], w_up[0], w_down[0]))
    wp = dict(wp, w_out=wo_b, w_up=wu_b, w_down=wd_b)
    wp_s = dict(wp, ws=ws_s, bst=bst_s)
    an = _attn_prompt(q, k, vt, wp["goa"], batch, seq, tm_p)
    y_p = _finish(an, gm, xp, wp, 1024)

    tm_s = 256
    xs = x_sample.reshape(dec_batch * dec_seq, D_MODEL)
    pos_s = past + (jnp.arange(tm_s, dtype=jnp.int32) % dec_seq)
    tabs_s = _rope_tables(pos_s)
    qs, kn, vn, ckv_s, kpe_s, gms, vg_s = _proj(xs, tabs_s, wp_s, dec_seq, tm_s)
    ans = _attn_sample(cache_c_kv[0].reshape(dec_batch * past, KV_RANK),
                       jnp.swapaxes(cache_k_rope[0], 1, 2),
                       qs, kn, vn, wp, dec_batch, dec_seq, past)
    y_s = _finish(ans, gms, xs, wp, 1024)

    return (y_p.reshape(batch, seq, D_MODEL),
            y_s.reshape(dec_batch, dec_seq, D_MODEL),
            ckv_p.reshape(1, batch, seq, KV_RANK),
            kpe_p.reshape(1, batch, seq, ROPE_DIM),
            ckv_s.reshape(1, dec_batch, dec_seq, KV_RANK),
            kpe_s.reshape(1, dec_batch, dec_seq, ROPE_DIM),
            vg_s.reshape(1, dec_batch, dec_seq, GMLP_WIDTH))
```

```python
import functools
import math

import jax
import jax.numpy as jnp
from jax import lax
from jax.experimental import pallas as pl
from jax.experimental.pallas import tpu as pltpu

D_MODEL = 2048
CHUNK = 64
N_HEADS = 8
NOPE_DIM = 128
ROPE_DIM = 64
HALF_ROPE = ROPE_DIM // 2
QK_DIM = NOPE_DIM + ROPE_DIM
V_DIM = 128
ATTN_WIDTH = N_HEADS * V_DIM
Q_RANK = 512
KV_RANK = 256
GMLP_GROUPS = 8
GMLP_GROUP_DIM = 128
GMLP_WIDTH = GMLP_GROUPS * GMLP_GROUP_DIM
GMLP_CHUNK = 128
D_FF = 4 * D_MODEL
ROPE_THETA = 10000.0
EPS = 1e-6

LANES = 128
SUBLANES = 8
BF16_SUBLANES = 2 * SUBLANES
HEAD_PAD = 2 * LANES
QK_WIDTH = N_HEADS * HEAD_PAD
C_Q0, C_Q1 = 0, Q_RANK
C_KV0, C_KV1 = C_Q1, C_Q1 + KV_RANK
C_PE0, C_PE1 = C_KV1, C_KV1 + LANES
C_U0, C_U1 = C_PE1, C_PE1 + GMLP_WIDTH
C_V0, C_V1 = C_U1, C_U1 + GMLP_WIDTH
IN_PAD = C_V1
VMEM_LIMIT = 56 * 1024 * 1024

BF16 = jnp.bfloat16
F32 = jnp.float32


def _dot(a, b):
    return jnp.dot(a, b, preferred_element_type=F32)


def _dot_nt(a, b):
    return lax.dot_general(a, b, (((1,), (1,)), ((), ())), preferred_element_type=F32)


def _sumsq(x):
    return jnp.sum(x * x, axis=-1, keepdims=True)


def _rms(x):
    return x * lax.rsqrt(jnp.mean(x * x, axis=-1, keepdims=True) + EPS)


def _gelu(x):
    c = math.sqrt(2.0 / math.pi)
    return 0.5 * x * (1.0 + jnp.tanh(c * (x + 0.044715 * (x * x * x))))


def _rope(t, cos, s1, s2):
    return t * cos + pltpu.roll(t, LANES - HALF_ROPE, 1) * s1 + pltpu.roll(t, HALF_ROPE, 1) * s2


def _const_spec(shape):
    nd = len(shape)
    return pl.BlockSpec(shape, lambda *_: (0,) * nd, pipeline_mode=pl.Buffered(1))


def _proj_kernel(chunk_len, prompt, x_ref, gmix_ref, win_ref, gql_ref, gkv_ref, wuq_ref, wuk_ref,
                 wuv_ref, gq_ref, cos_ref, s1_ref, s2_ref, vn_ref, ws_ref, bst_ref, gog_ref, *rest):
    v_transposed = prompt
    emit_v = not prompt
    if prompt:
        (wo_ref, wu_ref, wd_ref, q_ref, k_ref, v_ref, ckv_ref, kpe_ref, gm_ref,
         wo_out, wu_out, wd_out, gate_scr, vgb_scr) = rest
        wo_out[...] = wo_ref[...].astype(BF16)
        wu_out[...] = wu_ref[...].astype(BF16)
        wd_out[...] = wd_ref[...].astype(BF16)
    else:
        q_ref, k_ref, v_ref, ckv_ref, kpe_ref, gm_ref, vg_ref, gate_scr, vgb_scr = rest
    tm = x_ref.shape[0]
    xn = (_rms(x_ref[...]) * gmix_ref[...]).astype(BF16)
    cos, s1, s2 = cos_ref[...], s1_ref[...], s2_ref[...]

    zq = _dot(xn, win_ref[:, C_Q0:C_Q1])
    qln = (_rms(zq) * gql_ref[...]).astype(BF16)
    qraw = _dot(qln, wuq_ref[...])
    for h in range(N_HEADS):
        lo = h * HEAD_PAD
        nope = qraw[:, lo:lo + LANES]
        rp = qraw[:, lo + LANES:lo + HEAD_PAD]
        r = lax.rsqrt((_sumsq(nope) + _sumsq(rp)) * (1.0 / QK_DIM) + EPS)
        rp = _rope(rp, cos, s1, s2)
        q_ref[:, lo:lo + LANES] = (nope * r * gq_ref[:, :LANES]).astype(BF16)
        q_ref[:, lo + LANES:lo + HEAD_PAD] = (rp * r * gq_ref[:, LANES:]).astype(BF16)

    ckv = _rms(_dot(xn, win_ref[:, C_KV0:C_KV1])) * gkv_ref[...]
    ckv_ref[...] = ckv
    cb = ckv.astype(BF16)
    pe = _rope(_dot(xn, win_ref[:, C_PE0:C_PE1]), cos, s1, s2)
    kpe_ref[...] = pe[:, :ROPE_DIM]
    ss_pe = _sumsq(pe)
    knope = _dot(cb, wuk_ref[...])
    if v_transposed:
        v_ref[...] = _dot_nt(wuv_ref[...], cb).astype(BF16)
    else:
        v_ref[...] = _dot(cb, wuv_ref[...]).astype(BF16)
    for h in range(N_HEADS):
        nope = knope[:, h * LANES:(h + 1) * LANES]
        r = lax.rsqrt((_sumsq(nope) + ss_pe) * (1.0 / QK_DIM) + EPS)
        lo = h * HEAD_PAD
        k_ref[:, lo:lo + LANES] = (nope * r).astype(BF16)
        k_ref[:, lo + LANES:lo + HEAD_PAD] = (pe * r).astype(BF16)

    gv = _gelu(_dot(xn, win_ref[:, C_V0:C_V1]))
    for g in range(GMLP_GROUPS):
        blk = gv[:, g * LANES:(g + 1) * LANES]
        vg = _rms(blk) * vn_ref[g:g + 1, :]
        if emit_v:
            vg_ref[:, g * LANES:(g + 1) * LANES] = vg
        vgb_scr[:, g * LANES:(g + 1) * LANES] = vg.astype(BF16)
    u = _gelu(_dot(xn, win_ref[:, C_U0:C_U1]))
    row = lax.broadcasted_iota(jnp.int32, (GMLP_CHUNK, GMLP_CHUNK), 0)
    col = lax.broadcasted_iota(jnp.int32, (GMLP_CHUNK, GMLP_CHUNK), 1)
    causal = (row // chunk_len == col // chunk_len) & (col <= row)
    for g in range(GMLP_GROUPS):
        wm = jnp.where(causal, ws_ref[g], 0.0).astype(BF16)
        bias = bst_ref[:, g:g + 1]
        for c in range(tm // GMLP_CHUNK):
            rows = slice(c * GMLP_CHUNK, (c + 1) * GMLP_CHUNK)
            cols = slice(g * LANES, (g + 1) * LANES)
            s = _dot(wm, vgb_scr[rows, cols]) + bias
            gate_scr[rows, cols] = u[rows, cols] * s
    gm_ref[...] = (_rms(gate_scr[...]) * gog_ref[...]).astype(BF16)


def _proj(x2d, tabs, wp, chunk_len, tm, f32_weights=None):
    prompt = f32_weights is not None
    m = x2d.shape[0]
    steps = m // tm
    cos, s1, s2 = tabs
    tab_blocks = cos.shape[0] // tm
    row_spec = lambda w: pl.BlockSpec((tm, w), lambda i: (i, 0))
    tab_spec = pl.BlockSpec((tm, LANES), lambda i: (i % tab_blocks, 0))
    consts = [wp["gmix"], wp["w_in"], wp["gql"], wp["gkv"], wp["w_uq"], wp["w_uk"],
              wp["w_uv_t"] if prompt else wp["w_uv"], wp["gq"]]
    consts2 = [wp["v_norm"], wp["ws"], wp["bst"], wp["gog"]]
    in_specs = ([row_spec(D_MODEL)] + [_const_spec(c.shape) for c in consts]
                + [tab_spec] * 3 + [_const_spec(c.shape) for c in consts2])
    v_shape = (steps * ATTN_WIDTH, tm) if prompt else (m, ATTN_WIDTH)
    v_spec = pl.BlockSpec((ATTN_WIDTH, tm), lambda i: (i, 0)) if prompt else row_spec(ATTN_WIDTH)
    out_shape = [jax.ShapeDtypeStruct((m, QK_WIDTH), BF16), jax.ShapeDtypeStruct((m, QK_WIDTH), BF16),
                 jax.ShapeDtypeStruct(v_shape, BF16), jax.ShapeDtypeStruct((m, KV_RANK), F32),
                 jax.ShapeDtypeStruct((m, ROPE_DIM), F32), jax.ShapeDtypeStruct((m, GMLP_WIDTH), BF16)]
    out_specs = [row_spec(QK_WIDTH), row_spec(QK_WIDTH), v_spec, row_spec(KV_RANK),
                 row_spec(ROPE_DIM), row_spec(GMLP_WIDTH)]
    extra_in = []
    if prompt:
        w_out, w_up, w_down = f32_weights
        slabs = [pl.BlockSpec((D_MODEL // steps, D_MODEL), lambda i: (i, 0)),
                 pl.BlockSpec((D_MODEL, D_FF // steps), lambda i: (0, i)),
                 pl.BlockSpec((D_FF // steps, D_MODEL), lambda i: (i, 0))]
        extra_in = [w_out, w_up, w_down]
        in_specs += slabs
        out_shape += [jax.ShapeDtypeStruct(w.shape, BF16) for w in extra_in]
        out_specs += slabs
    else:
        out_shape.append(jax.ShapeDtypeStruct((m, GMLP_WIDTH), F32))
        out_specs.append(row_spec(GMLP_WIDTH))
    return pl.pallas_call(
        functools.partial(_proj_kernel, chunk_len, prompt),
        out_shape=out_shape,
        grid=(steps,),
        in_specs=in_specs,
        out_specs=out_specs,
        scratch_shapes=[pltpu.VMEM((tm, GMLP_WIDTH), F32), pltpu.VMEM((tm, GMLP_WIDTH), BF16)],
        compiler_params=pltpu.CompilerParams(dimension_semantics=("parallel",),
                                             vmem_limit_bytes=VMEM_LIMIT),
        name="proj",
    )(x2d, *consts, cos, s1, s2, *consts2, *extra_in)


NEG = -1e30
SOFTMAX_KEY_CHUNKS = 4


def _attn_prompt_kernel(q_ref, k_ref, vt_ref, go_ref, o_ref, acc_scr, m_scr, l_scr, s_scr, p_scr):
    t = q_ref.shape[0]
    ck = t // SOFTMAX_KEY_CHUNKS
    qi = pl.program_id(1)
    m_scr[...] = jnp.full(m_scr.shape, NEG, F32)
    l_scr[...] = jnp.zeros(l_scr.shape, F32)
    acc_scr[...] = jnp.zeros(acc_scr.shape, F32)

    def block(kb, masked):
        rows = pl.ds(pl.multiple_of(kb * t, t), t)
        if masked:
            key = lax.broadcasted_iota(jnp.int32, (t, t), 0)
            qry = lax.broadcasted_iota(jnp.int32, (t, t), 1)
            ok = (key // CHUNK) <= (qry // CHUNK)

        m8 = []
        for h in range(N_HEADS):
            qs = slice(h * HEAD_PAD, (h + 1) * HEAD_PAD)
            s = _dot_nt(k_ref[rows, qs], q_ref[:, qs])
            if masked:
                s = jnp.where(ok, s, NEG)
            s_scr[h] = s
            m8.append(jnp.max(s.reshape(t // SUBLANES, SUBLANES, t), axis=0))

        ones = jnp.ones((BF16_SUBLANES, t), BF16)
        for h in range(N_HEADS):
            hs = slice(h * V_DIM, (h + 1) * V_DIM)
            m_old = m_scr[h]
            m_new = jnp.maximum(m_old, jnp.max(m8[h], axis=0, keepdims=True))
            alpha = jnp.exp2(m_old - m_new)
            for c in range(SOFTMAX_KEY_CHUNKS):
                cs = slice(c * ck, (c + 1) * ck)
                p_scr[h, cs, :] = jnp.exp2(s_scr[h, cs, :] - m_new).astype(BF16)
            m_scr[h] = m_new
            vt = vt_ref[pl.ds(pl.multiple_of(kb * ATTN_WIDTH, ATTN_WIDTH) + h * V_DIM, V_DIM), :]
            pv = _dot(jnp.concatenate([vt, ones], axis=0), p_scr[h])
            acc_scr[hs, :] = alpha * acc_scr[hs, :] + pv[:V_DIM]
            l_scr[h] = alpha * l_scr[h] + pv[V_DIM:V_DIM + 1]

    def body(kb, carry):
        block(kb, False)
        return carry

    lax.fori_loop(0, qi, body, 0)
    block(qi, True)
    for h in range(N_HEADS):
        hs = slice(h * V_DIM, (h + 1) * V_DIM)
        acc_scr[hs, :] = acc_scr[hs, :] / l_scr[h]
    o = acc_scr[...].T
    o_ref[...] = (_rms(o) * go_ref[...]).astype(BF16)


def _attn_prompt(q, k, vt, go, batch, seq, t):
    nq = seq // t
    assert vt.shape == (batch * nq * ATTN_WIDTH, t)
    return pl.pallas_call(
        _attn_prompt_kernel,
        out_shape=jax.ShapeDtypeStruct((batch * seq, ATTN_WIDTH), BF16),
        grid=(batch, nq),
        in_specs=[pl.BlockSpec((t, QK_WIDTH), lambda b, i: (b * nq + i, 0)),
                  pl.BlockSpec((seq, QK_WIDTH), lambda b, i: (b, 0)),
                  pl.BlockSpec((nq * ATTN_WIDTH, t), lambda b, i: (b, 0)),
                  _const_spec(go.shape)],
        out_specs=pl.BlockSpec((t, ATTN_WIDTH), lambda b, i: (b * nq + i, 0)),
        scratch_shapes=[pltpu.VMEM((ATTN_WIDTH, t), F32), pltpu.VMEM((N_HEADS, 1, t), F32),
                        pltpu.VMEM((N_HEADS, 1, t), F32), pltpu.VMEM((N_HEADS, t, t), F32),
                        pltpu.VMEM((N_HEADS, t, t), BF16)],
        compiler_params=pltpu.CompilerParams(dimension_semantics=("parallel", "arbitrary"),
                                             vmem_limit_bytes=VMEM_LIMIT),
        name="attn_prompt",
    )(q, k, vt, go)


def _attn_sample_kernel(past, c_ref, krt_ref, q_ref, kn_ref, vn_ref, wukt_ref, wuv_ref, go_ref,
                        o_ref, kct_scr, vc_scr, acc_scr):
    t = q_ref.shape[0]
    cb = c_ref[...].astype(BF16)
    krt = krt_ref[...]
    ss_pe = jnp.sum(krt * krt, axis=0, keepdims=True)
    knt = _dot_nt(wukt_ref[...], cb)
    vc_scr[...] = _dot(cb, wuv_ref[...]).astype(BF16)
    zeros = jnp.zeros((HEAD_PAD - QK_DIM, past), BF16)
    for h in range(N_HEADS):
        nope = knt[h * LANES:(h + 1) * LANES, :]
        ss = jnp.sum(nope * nope, axis=0, keepdims=True) + ss_pe
        r = lax.rsqrt(ss * (1.0 / QK_DIM) + EPS)
        lo = h * HEAD_PAD
        kct_scr[lo:lo + LANES, :] = (nope * r).astype(BF16)
        kct_scr[lo + LANES:lo + QK_DIM, :] = (krt * r).astype(BF16)
        kct_scr[lo + QK_DIM:lo + HEAD_PAD, :] = zeros

    row = lax.broadcasted_iota(jnp.int32, (t, t), 0)
    col = lax.broadcasted_iota(jnp.int32, (t, t), 1)
    new_ok = ((past + col) // CHUNK) <= ((past + row) // CHUNK)
    for h in range(N_HEADS):
        qs = slice(h * HEAD_PAD, (h + 1) * HEAD_PAD)
        vs = slice(h * V_DIM, (h + 1) * V_DIM)
        q = q_ref[:, qs]
        s_c = _dot(q, kct_scr[qs, :])
        s_n = jnp.where(new_ok, _dot_nt(q, kn_ref[:, qs]), NEG)
        m = jnp.maximum(jnp.max(s_c, axis=-1, keepdims=True), jnp.max(s_n, axis=-1, keepdims=True))
        p_c = jnp.exp2(s_c - m)
        p_n = jnp.exp2(s_n - m)
        l = jnp.sum(p_c, axis=-1, keepdims=True) + jnp.sum(p_n, axis=-1, keepdims=True)
        acc = _dot(p_c.astype(BF16), vc_scr[:, vs]) + _dot(p_n.astype(BF16), vn_ref[:, vs])
        acc_scr[:, vs] = acc / l
    o_ref[...] = (_rms(acc_scr[...]) * go_ref[...]).astype(BF16)


def _attn_sample(c2d, krt, q, kn, vn, wp, batch, t, past):
    consts = [wp["w_uk_t"], wp["w_uv"], wp["goa"]]
    return pl.pallas_call(
        functools.partial(_attn_sample_kernel, past),
        out_shape=jax.ShapeDtypeStruct((batch * t, ATTN_WIDTH), BF16),
        grid=(batch,),
        in_specs=[pl.BlockSpec((past, KV_RANK), lambda b: (b, 0)),
                  pl.BlockSpec((None, ROPE_DIM, past), lambda b: (b, 0, 0)),
                  pl.BlockSpec((t, QK_WIDTH), lambda b: (b, 0)),
                  pl.BlockSpec((t, QK_WIDTH), lambda b: (b, 0)),
                  pl.BlockSpec((t, ATTN_WIDTH), lambda b: (b, 0))]
                 + [_const_spec(c.shape) for c in consts],
        out_specs=pl.BlockSpec((t, ATTN_WIDTH), lambda b: (b, 0)),
        scratch_shapes=[pltpu.VMEM((QK_WIDTH, past), BF16), pltpu.VMEM((past, ATTN_WIDTH), BF16),
                        pltpu.VMEM((t, ATTN_WIDTH), F32)],
        compiler_params=pltpu.CompilerParams(dimension_semantics=("parallel",),
                                             vmem_limit_bytes=VMEM_LIMIT),
        name="attn_sample",
    )(c2d, krt, q, kn, vn, *consts)


def _outproj_kernel(an_ref, gm_ref, x_ref, wo_ref, gffn_ref, h_ref, hn_ref):
    h = (x_ref[...] + _dot(an_ref[...], wo_ref[:ATTN_WIDTH, :])
         + _dot(gm_ref[...], wo_ref[ATTN_WIDTH:, :]))
    h_ref[...] = h
    hn_ref[...] = (_rms(h) * gffn_ref[...]).astype(BF16)


def _outproj(an, gm, x2d, wp, tm):
    m = x2d.shape[0]
    row_spec = lambda w: pl.BlockSpec((tm, w), lambda i: (i, 0))
    return pl.pallas_call(
        _outproj_kernel,
        out_shape=[jax.ShapeDtypeStruct((m, D_MODEL), F32), jax.ShapeDtypeStruct((m, D_MODEL), BF16)],
        grid=(m // tm,),
        in_specs=[row_spec(ATTN_WIDTH), row_spec(GMLP_WIDTH), row_spec(D_MODEL),
                  _const_spec(wp["w_out"].shape), _const_spec(wp["gffn"].shape)],
        out_specs=[row_spec(D_MODEL), row_spec(D_MODEL)],
        compiler_params=pltpu.CompilerParams(dimension_semantics=("parallel",),
                                             vmem_limit_bytes=VMEM_LIMIT),
        name="outproj",
    )(an, gm, x2d, wp["w_out"], wp["gffn"])


def _ffn_kernel(h_ref, hn_ref, wu_ref, wd_ref, y_ref):
    f = pl.program_id(1)

    @pl.when(f == 0)
    def _():
        y_ref[...] = h_ref[...]

    a = jnp.maximum(_dot(hn_ref[...], wu_ref[...]), 0.0)
    y_ref[...] += _dot((a * a).astype(BF16), wd_ref[...])


def _ffn(h, hn, wp, tm, tf):
    m = h.shape[0]
    return pl.pallas_call(
        _ffn_kernel,
        out_shape=jax.ShapeDtypeStruct((m, D_MODEL), F32),
        grid=(m // tm, D_FF // tf),
        in_specs=[pl.BlockSpec((tm, D_MODEL), lambda i, f: (i, 0), pipeline_mode=pl.Buffered(1)),
                  pl.BlockSpec((tm, D_MODEL), lambda i, f: (i, 0)),
                  pl.BlockSpec((D_MODEL, tf), lambda i, f: (0, f)),
                  pl.BlockSpec((tf, D_MODEL), lambda i, f: (f, 0))],
        out_specs=pl.BlockSpec((tm, D_MODEL), lambda i, f: (i, 0)),
        compiler_params=pltpu.CompilerParams(dimension_semantics=("parallel", "arbitrary"),
                                             vmem_limit_bytes=VMEM_LIMIT),
        name="ffn",
    )(h, hn, wp["w_up"], wp["w_down"])


def _rope_tables(pos):
    inv = ROPE_THETA ** (-jnp.arange(HALF_ROPE, dtype=F32) / HALF_ROPE)
    ang = pos.astype(F32)[:, None] * inv[None, :]
    cos, sin = jnp.cos(ang), jnp.sin(ang)
    z = jnp.zeros_like(cos)
    z2 = jnp.zeros((pos.shape[0], LANES - ROPE_DIM), F32)
    return (jnp.concatenate([cos, cos, z2], axis=1), jnp.concatenate([-sin, z, z2], axis=1),
            jnp.concatenate([z, sin, z2], axis=1))


def _head_gain(g_nope, g_rope, scale):
    pad = jnp.zeros((HEAD_PAD - QK_DIM,), F32)
    return (jnp.concatenate([g_nope, g_rope, g_rope, pad]) * scale)[None, :]


def _pad_w_in_kernel(wt_ref, o_ref):
    tr = wt_ref.shape[1]
    split = C_PE0 + ROPE_DIM
    o_ref[:, :C_PE0] = wt_ref[:C_PE0, :].T.astype(BF16)
    pe = wt_ref[C_PE0:C_PE1, :].T
    lane = lax.broadcasted_iota(jnp.int32, (tr, LANES), 1)
    o_ref[:, C_PE0:C_PE1] = jnp.where(lane < ROPE_DIM, pe, 0.0).astype(BF16)
    o_ref[:, C_PE1:] = wt_ref[split:, :].T.astype(BF16)


def _pad_w_in(w_in_t, tr):
    width, rows = w_in_t.shape
    assert width + LANES - ROPE_DIM == IN_PAD
    return pl.pallas_call(
        _pad_w_in_kernel,
        out_shape=jax.ShapeDtypeStruct((rows, IN_PAD), BF16),
        grid=(rows // tr,),
        in_specs=[pl.BlockSpec((width, tr), lambda i: (0, i))],
        out_specs=pl.BlockSpec((tr, IN_PAD), lambda i: (i, 0)),
        compiler_params=pltpu.CompilerParams(dimension_semantics=("parallel",)),
        name="pad_w_in",
    )(w_in_t)


def _prep_weights(norm_mix, w_in, q_lat_norm, kv_lat_norm, w_uq, w_uk, w_uv, q_norm_nope, q_norm_rope,
                  k_norm_nope, k_norm_rope, v_norm, w_spatial, b_spatial, out_norm_attn, out_norm_gmlp,
                  norm_ffn, chunk_len):
    w_in_p = _pad_w_in(w_in.T, 256)
    wq = w_uq.reshape(Q_RANK, N_HEADS, QK_DIM)
    wq = jnp.pad(wq, ((0, 0), (0, 0), (0, HEAD_PAD - QK_DIM))).reshape(Q_RANK, QK_WIDTH).astype(BF16)
    reps = GMLP_CHUNK // chunk_len
    ws = jnp.tile(w_spatial[:, :chunk_len, :chunk_len], (1, reps, reps))
    bst = jnp.tile(b_spatial[:, :chunk_len], (1, reps)).T
    return {
        "gmix": norm_mix[None, :], "w_in": w_in_p, "gql": q_lat_norm[None, :], "gkv": kv_lat_norm[None, :],
        "w_uq": wq, "w_uk": w_uk.astype(BF16), "w_uk_t": w_uk.T.astype(BF16),
        "w_uv": w_uv.astype(BF16), "w_uv_t": w_uv.T.astype(BF16),
        "gq": (_head_gain(q_norm_nope, q_norm_rope, QK_DIM ** -0.5 * math.log2(math.e))
               * _head_gain(k_norm_nope, k_norm_rope, 1.0)),
        "v_norm": v_norm, "ws": ws, "bst": bst, "gog": out_norm_gmlp[None, :],
        "goa": out_norm_attn[None, :], "gffn": norm_ffn[None, :],
    }


def _finish(an, gm, x2d, wp, tm_ffn):
    h, hn = _outproj(an, gm, x2d, wp, 512)
    return _ffn(h, hn, wp, tm_ffn, 1024)


def kernel(x_prompt, x_sample, cache_c_kv, cache_k_rope, norm_mix, w_in, q_lat_norm, kv_lat_norm, w_uq, w_uk, w_uv, q_norm_nope, q_norm_rope, k_norm_nope, k_norm_rope, v_norm, w_spatial, b_spatial, out_norm_attn, out_norm_gmlp, w_out, norm_ffn, w_up, w_down):
    depth = w_in.shape[0]
    assert depth == 1
    batch, seq, _ = x_prompt.shape
    dec_batch, dec_seq, _ = x_sample.shape
    past = cache_c_kv.shape[2]
    assert past % CHUNK == 0 and dec_seq <= CHUNK and GMLP_CHUNK % dec_seq == 0

    weights = (norm_mix[0], w_in[0], q_lat_norm[0], kv_lat_norm[0], w_uq[0], w_uk[0], w_uv[0],
               q_norm_nope[0], q_norm_rope[0], k_norm_nope[0], k_norm_rope[0], v_norm[0], w_spatial[0],
               b_spatial[0], out_norm_attn[0], out_norm_gmlp[0], norm_ffn[0])
    wp = _prep_weights(*weights, chunk_len=GMLP_CHUNK)
    reps = GMLP_CHUNK // dec_seq
    ws_s = jnp.tile(w_spatial[0][:, :dec_seq, :dec_seq], (1, reps, reps))
    bst_s = jnp.tile(b_spatial[0][:, :dec_seq], (1, reps)).T

    tm_p = 256
    xp = x_prompt.reshape(batch * seq, D_MODEL)
    tabs_p = _rope_tables(jnp.arange(seq, dtype=jnp.int32))
    q, k, vt, ckv_p, kpe_p, gm, wo_b, wu_b, wd_b = _proj(
        xp, tabs_p, wp, GMLP_CHUNK, tm_p, f32_weights=(w_out[0], w_up[0], w_down[0]))
    wp = dict(wp, w_out=wo_b, w_up=wu_b, w_down=wd_b)
    wp_s = dict(wp, ws=ws_s, bst=bst_s)
    an = _attn_prompt(q, k, vt, wp["goa"], batch, seq, tm_p)
    y_p = _finish(an, gm, xp, wp, 1024)

    tm_s = 256
    xs = x_sample.reshape(dec_batch * dec_seq, D_MODEL)
    pos_s = past + (jnp.arange(tm_s, dtype=jnp.int32) % dec_seq)
    tabs_s = _rope_tables(pos_s)
    qs, kn, vn, ckv_s, kpe_s, gms, vg_s = _proj(xs, tabs_s, wp_s, dec_seq, tm_s)
    ans = _attn_sample(cache_c_kv[0].reshape(dec_batch * past, KV_RANK),
                       jnp.swapaxes(cache_k_rope[0], 1, 2),
                       qs, kn, vn, wp, dec_batch, dec_seq, past)
    y_s = _finish(ans, gms, xs, wp, 1024)

    return (y_p.reshape(batch, seq, D_MODEL),
            y_s.reshape(dec_batch, dec_seq, D_MODEL),
            ckv_p.reshape(1, batch, seq, KV_RANK),
            kpe_p.reshape(1, batch, seq, ROPE_DIM),
            ckv_s.reshape(1, dec_batch, dec_seq, KV_RANK),
            kpe_s.reshape(1, dec_batch, dec_seq, ROPE_DIM),
            vg_s.reshape(1, dec_batch, dec_seq, GMLP_WIDTH))
```

```python
import functools
import math

import jax
import jax.numpy as jnp
from jax import lax
from jax.experimental import pallas as pl
from jax.experimental.pallas import tpu as pltpu

D_MODEL = 2048
CHUNK = 64
N_HEADS = 8
NOPE_DIM = 128
ROPE_DIM = 64
HALF_ROPE = ROPE_DIM // 2
QK_DIM = NOPE_DIM + ROPE_DIM
V_DIM = 128
ATTN_WIDTH = N_HEADS * V_DIM
Q_RANK = 512
KV_RANK = 256
GMLP_GROUPS = 8
GMLP_GROUP_DIM = 128
GMLP_WIDTH = GMLP_GROUPS * GMLP_GROUP_DIM
GMLP_CHUNK = 128
D_FF = 4 * D_MODEL
ROPE_THETA = 10000.0
EPS = 1e-6

LANES = 128
SUBLANES = 8
BF16_SUBLANES = 2 * SUBLANES
HEAD_PAD = 2 * LANES
QK_WIDTH = N_HEADS * HEAD_PAD
C_Q0, C_Q1 = 0, Q_RANK
C_KV0, C_KV1 = C_Q1, C_Q1 + KV_RANK
C_PE0, C_PE1 = C_KV1, C_KV1 + LANES
C_U0, C_U1 = C_PE1, C_PE1 + GMLP_WIDTH
C_V0, C_V1 = C_U1, C_U1 + GMLP_WIDTH
IN_PAD = C_V1
VMEM_LIMIT = 56 * 1024 * 1024

BF16 = jnp.bfloat16
F32 = jnp.float32


def _dot(a, b):
    return jnp.dot(a, b, preferred_element_type=F32)


def _dot_nt(a, b):
    return lax.dot_general(a, b, (((1,), (1,)), ((), ())), preferred_element_type=F32)


def _sumsq(x):
    return jnp.sum(x * x, axis=-1, keepdims=True)


def _rms(x):
    return x * lax.rsqrt(jnp.mean(x * x, axis=-1, keepdims=True) + EPS)


def _gelu(x):
    c = math.sqrt(2.0 / math.pi)
    return 0.5 * x * (1.0 + jnp.tanh(c * (x + 0.044715 * (x * x * x))))


def _rope(t, cos, s1, s2):
    return t * cos + pltpu.roll(t, LANES - HALF_ROPE, 1) * s1 + pltpu.roll(t, HALF_ROPE, 1) * s2


def _const_spec(shape):
    nd = len(shape)
    return pl.BlockSpec(shape, lambda *_: (0,) * nd, pipeline_mode=pl.Buffered(1))


def _proj_kernel(chunk_len, prompt, x_ref, gmix_ref, win_ref, gql_ref, gkv_ref, wuq_ref, wuk_ref,
                 wuv_ref, gq_ref, cos_ref, s1_ref, s2_ref, vn_ref, ws_ref, bst_ref, gog_ref, *rest):
    v_transposed = prompt
    emit_v = not prompt
    if prompt:
        (wo_ref, wu_ref, wd_ref, q_ref, k_ref, v_ref, ckv_ref, kpe_ref, gm_ref,
         wo_out, wu_out, wd_out, gate_scr, vgb_scr) = rest
        wo_out[...] = wo_ref[...].astype(BF16)
        wu_out[...] = wu_ref[...].astype(BF16)
        wd_out[...] = wd_ref[...].astype(BF16)
    else:
        q_ref, k_ref, v_ref, ckv_ref, kpe_ref, gm_ref, vg_ref, gate_scr, vgb_scr = rest
    tm = x_ref.shape[0]
    xn = (_rms(x_ref[...]) * gmix_ref[...]).astype(BF16)
    cos, s1, s2 = cos_ref[...], s1_ref[...], s2_ref[...]

    zq = _dot(xn, win_ref[:, C_Q0:C_Q1])
    qln = (_rms(zq) * gql_ref[...]).astype(BF16)
    qraw = _dot(qln, wuq_ref[...])
    for h in range(N_HEADS):
        lo = h * HEAD_PAD
        nope = qraw[:, lo:lo + LANES]
        rp = qraw[:, lo + LANES:lo + HEAD_PAD]
        r = lax.rsqrt((_sumsq(nope) + _sumsq(rp)) * (1.0 / QK_DIM) + EPS)
        rp = _rope(rp, cos, s1, s2)
        q_ref[:, lo:lo + LANES] = (nope * r * gq_ref[:, :LANES]).astype(BF16)
        q_ref[:, lo + LANES:lo + HEAD_PAD] = (rp * r * gq_ref[:, LANES:]).astype(BF16)

    ckv = _rms(_dot(xn, win_ref[:, C_KV0:C_KV1])) * gkv_ref[...]
    ckv_ref[...] = ckv
    cb = ckv.astype(BF16)
    pe = _rope(_dot(xn, win_ref[:, C_PE0:C_PE1]), cos, s1, s2)
    kpe_ref[...] = pe[:, :ROPE_DIM]
    ss_pe = _sumsq(pe)
    knope = _dot(cb, wuk_ref[...])
    if v_transposed:
        v_ref[...] = _dot_nt(wuv_ref[...], cb).astype(BF16)
    else:
        v_ref[...] = _dot(cb, wuv_ref[...]).astype(BF16)
    for h in range(N_HEADS):
        nope = knope[:, h * LANES:(h + 1) * LANES]
        r = lax.rsqrt((_sumsq(nope) + ss_pe) * (1.0 / QK_DIM) + EPS)
        lo = h * HEAD_PAD
        k_ref[:, lo:lo + LANES] = (nope * r).astype(BF16)
        k_ref[:, lo + LANES:lo + HEAD_PAD] = (pe * r).astype(BF16)

    gv = _gelu(_dot(xn, win_ref[:, C_V0:C_V1]))
    for g in range(GMLP_GROUPS):
        blk = gv[:, g * LANES:(g + 1) * LANES]
        vg = _rms(blk) * vn_ref[g:g + 1, :]
        if emit_v:
            vg_ref[:, g * LANES:(g + 1) * LANES] = vg
        vgb_scr[:, g * LANES:(g + 1) * LANES] = vg.astype(BF16)
    u = _gelu(_dot(xn, win_ref[:, C_U0:C_U1]))
    row = lax.broadcasted_iota(jnp.int32, (GMLP_CHUNK, GMLP_CHUNK), 0)
    col = lax.broadcasted_iota(jnp.int32, (GMLP_CHUNK, GMLP_CHUNK), 1)
    causal = (row // chunk_len == col // chunk_len) & (col <= row)
    for g in range(GMLP_GROUPS):
        wm = jnp.where(causal, ws_ref[g], 0.0).astype(BF16)
        bias = bst_ref[:, g:g + 1]
        for c in range(tm // GMLP_CHUNK):
            rows = slice(c * GMLP_CHUNK, (c + 1) * GMLP_CHUNK)
            cols = slice(g * LANES, (g + 1) * LANES)
            s = _dot(wm, vgb_scr[rows, cols]) + bias
            gate_scr[rows, cols] = u[rows, cols] * s
    gm_ref[...] = (_rms(gate_scr[...]) * gog_ref[...]).astype(BF16)


def _proj(x2d, tabs, wp, chunk_len, tm, f32_weights=None):
    prompt = f32_weights is not None
    m = x2d.shape[0]
    steps = m // tm
    cos, s1, s2 = tabs
    tab_blocks = cos.shape[0] // tm
    row_spec = lambda w: pl.BlockSpec((tm, w), lambda i: (i, 0))
    tab_spec = pl.BlockSpec((tm, LANES), lambda i: (i % tab_blocks, 0))
    consts = [wp["gmix"], wp["w_in"], wp["gql"], wp["gkv"], wp["w_uq"], wp["w_uk"],
              wp["w_uv_t"] if prompt else wp["w_uv"], wp["gq"]]
    consts2 = [wp["v_norm"], wp["ws"], wp["bst"], wp["gog"]]
    in_specs = ([row_spec(D_MODEL)] + [_const_spec(c.shape) for c in consts]
                + [tab_spec] * 3 + [_const_spec(c.shape) for c in consts2])
    v_shape = (steps * ATTN_WIDTH, tm) if prompt else (m, ATTN_WIDTH)
    v_spec = pl.BlockSpec((ATTN_WIDTH, tm), lambda i: (i, 0)) if prompt else row_spec(ATTN_WIDTH)
    out_shape = [jax.ShapeDtypeStruct((m, QK_WIDTH), BF16), jax.ShapeDtypeStruct((m, QK_WIDTH), BF16),
                 jax.ShapeDtypeStruct(v_shape, BF16), jax.ShapeDtypeStruct((m, KV_RANK), F32),
                 jax.ShapeDtypeStruct((m, ROPE_DIM), F32), jax.ShapeDtypeStruct((m, GMLP_WIDTH), BF16)]
    out_specs = [row_spec(QK_WIDTH), row_spec(QK_WIDTH), v_spec, row_spec(KV_RANK),
                 row_spec(ROPE_DIM), row_spec(GMLP_WIDTH)]
    extra_in = []
    if prompt:
        w_out, w_up, w_down = f32_weights
        slabs = [pl.BlockSpec((D_MODEL // steps, D_MODEL), lambda i: (i, 0)),
                 pl.BlockSpec((D_MODEL, D_FF // steps), lambda i: (0, i)),
                 pl.BlockSpec((D_FF // steps, D_MODEL), lambda i: (i, 0))]
        extra_in = [w_out, w_up, w_down]
        in_specs += slabs
        out_shape += [jax.ShapeDtypeStruct(w.shape, BF16) for w in extra_in]
        out_specs += slabs
    else:
        out_shape.append(jax.ShapeDtypeStruct((m, GMLP_WIDTH), F32))
        out_specs.append(row_spec(GMLP_WIDTH))
    return pl.pallas_call(
        functools.partial(_proj_kernel, chunk_len, prompt),
        out_shape=out_shape,
        grid=(steps,),
        in_specs=in_specs,
        out_specs=out_specs,
        scratch_shapes=[pltpu.VMEM((tm, GMLP_WIDTH), F32), pltpu.VMEM((tm, GMLP_WIDTH), BF16)],
        compiler_params=pltpu.CompilerParams(dimension_semantics=("parallel",),
                                             vmem_limit_bytes=VMEM_LIMIT),
        name="proj",
    )(x2d, *consts, cos, s1, s2, *consts2, *extra_in)


NEG = -1e30
SOFTMAX_KEY_CHUNK = 64
KEY_BLOCKS_PER_STEP = 2


def _attn_prompt_kernel(q_ref, k_ref, vt_ref, go_ref, o_ref, acc_scr, m_scr, l_scr, s_scr, p_scr):
    t = q_ref.shape[0]
    qi = pl.program_id(1)
    m_scr[...] = jnp.full(m_scr.shape, NEG, F32)
    l_scr[...] = jnp.zeros(l_scr.shape, F32)
    acc_scr[...] = jnp.zeros(acc_scr.shape, F32)

    def step(kb0, nblk, last_masked):
        nk = nblk * t
        if last_masked:
            key = lax.broadcasted_iota(jnp.int32, (t, t), 0)
            qry = lax.broadcasted_iota(jnp.int32, (t, t), 1)
            ok = (key // CHUNK) <= (qry // CHUNK)

        m8 = []
        for h in range(N_HEADS):
            qs = slice(h * HEAD_PAD, (h + 1) * HEAD_PAD)
            mh = None
            for j in range(nblk):
                rows = pl.ds(pl.multiple_of((kb0 + j) * t, t), t)
                s = _dot_nt(k_ref[rows, qs], q_ref[:, qs])
                if last_masked and j == nblk - 1:
                    s = jnp.where(ok, s, NEG)
                s_scr[h, j * t:(j + 1) * t, :] = s
                mj = jnp.max(s.reshape(t // SUBLANES, SUBLANES, t), axis=0)
                mh = mj if mh is None else jnp.maximum(mh, mj)
            m8.append(mh)

        ones = jnp.ones((BF16_SUBLANES, nk), BF16)
        for h in range(N_HEADS):
            hs = slice(h * V_DIM, (h + 1) * V_DIM)
            m_old = m_scr[h]
            m_new = jnp.maximum(m_old, jnp.max(m8[h], axis=0, keepdims=True))
            alpha = jnp.exp2(m_old - m_new)
            for c in range(nk // SOFTMAX_KEY_CHUNK):
                cs = slice(c * SOFTMAX_KEY_CHUNK, (c + 1) * SOFTMAX_KEY_CHUNK)
                p_scr[h, cs, :] = jnp.exp2(s_scr[h, cs, :] - m_new).astype(BF16)
            m_scr[h] = m_new
            vt = [vt_ref[pl.ds(pl.multiple_of((kb0 + j) * ATTN_WIDTH, ATTN_WIDTH) + h * V_DIM, V_DIM), :]
                  for j in range(nblk)]
            vt = vt[0] if nblk == 1 else jnp.concatenate(vt, axis=1)
            pv = _dot(jnp.concatenate([vt, ones], axis=0), p_scr[h, :nk, :])
            acc_scr[hs, :] = alpha * acc_scr[hs, :] + pv[:V_DIM]
            l_scr[h] = alpha * l_scr[h] + pv[V_DIM:V_DIM + 1]

    def body(j, carry):
        step(j * KEY_BLOCKS_PER_STEP, KEY_BLOCKS_PER_STEP, False)
        return carry

    lax.fori_loop(0, qi // KEY_BLOCKS_PER_STEP, body, 0)
    for rem in range(KEY_BLOCKS_PER_STEP):
        @pl.when(qi % KEY_BLOCKS_PER_STEP == rem)
        def _(rem=rem):
            step(qi - rem, rem + 1, True)

    for h in range(N_HEADS):
        hs = slice(h * V_DIM, (h + 1) * V_DIM)
        acc_scr[hs, :] = acc_scr[hs, :] / l_scr[h]
    o = acc_scr[...].T
    o_ref[...] = (_rms(o) * go_ref[...]).astype(BF16)


def _attn_prompt(q, k, vt, go, batch, seq, t):
    nq = seq // t
    assert vt.shape == (batch * nq * ATTN_WIDTH, t)
    tk = KEY_BLOCKS_PER_STEP * t
    return pl.pallas_call(
        _attn_prompt_kernel,
        out_shape=jax.ShapeDtypeStruct((batch * seq, ATTN_WIDTH), BF16),
        grid=(batch, nq),
        in_specs=[pl.BlockSpec((t, QK_WIDTH), lambda b, i: (b * nq + i, 0)),
                  pl.BlockSpec((seq, QK_WIDTH), lambda b, i: (b, 0)),
                  pl.BlockSpec((nq * ATTN_WIDTH, t), lambda b, i: (b, 0)),
                  _const_spec(go.shape)],
        out_specs=pl.BlockSpec((t, ATTN_WIDTH), lambda b, i: (b * nq + i, 0)),
        scratch_shapes=[pltpu.VMEM((ATTN_WIDTH, t), F32), pltpu.VMEM((N_HEADS, 1, t), F32),
                        pltpu.VMEM((N_HEADS, 1, t), F32), pltpu.VMEM((N_HEADS, tk, t), F32),
                        pltpu.VMEM((N_HEADS, tk, t), BF16)],
        compiler_params=pltpu.CompilerParams(dimension_semantics=("parallel", "arbitrary"),
                                             vmem_limit_bytes=VMEM_LIMIT),
        name="attn_prompt",
    )(q, k, vt, go)


def _attn_sample_kernel(past, c_ref, krt_ref, q_ref, kn_ref, vn_ref, wukt_ref, wuv_ref, go_ref,
                        o_ref, kct_scr, vc_scr, sc_scr, sn_scr, acc_scr):
    t = q_ref.shape[0]
    cb = c_ref[...].astype(BF16)
    krt = krt_ref[...]
    ss_pe = jnp.sum(krt * krt, axis=0, keepdims=True)
    knt = _dot_nt(wukt_ref[...], cb)
    vc_scr[...] = _dot(cb, wuv_ref[...]).astype(BF16)
    zeros = jnp.zeros((HEAD_PAD - QK_DIM, past), BF16)
    for h in range(N_HEADS):
        nope = knt[h * LANES:(h + 1) * LANES, :]
        ss = jnp.sum(nope * nope, axis=0, keepdims=True) + ss_pe
        r = lax.rsqrt(ss * (1.0 / QK_DIM) + EPS)
        lo = h * HEAD_PAD
        kct_scr[lo:lo + LANES, :] = (nope * r).astype(BF16)
        kct_scr[lo + LANES:lo + QK_DIM, :] = (krt * r).astype(BF16)
        kct_scr[lo + QK_DIM:lo + HEAD_PAD, :] = zeros

    row = lax.broadcasted_iota(jnp.int32, (t, t), 0)
    col = lax.broadcasted_iota(jnp.int32, (t, t), 1)
    new_ok = ((past + col) // CHUNK) <= ((past + row) // CHUNK)
    for h in range(N_HEADS):
        qs = slice(h * HEAD_PAD, (h + 1) * HEAD_PAD)
        q = q_ref[:, qs]
        sc_scr[h] = _dot(q, kct_scr[qs, :])
        sn_scr[h] = jnp.where(new_ok, _dot_nt(q, kn_ref[:, qs]), NEG)
    for h in range(N_HEADS):
        vs = slice(h * V_DIM, (h + 1) * V_DIM)
        s_c, s_n = sc_scr[h], sn_scr[h]
        m = jnp.maximum(jnp.max(s_c, axis=-1, keepdims=True), jnp.max(s_n, axis=-1, keepdims=True))
        p_c = jnp.exp2(s_c - m)
        p_n = jnp.exp2(s_n - m)
        l = jnp.sum(p_c, axis=-1, keepdims=True) + jnp.sum(p_n, axis=-1, keepdims=True)
        acc = _dot(p_c.astype(BF16), vc_scr[:, vs]) + _dot(p_n.astype(BF16), vn_ref[:, vs])
        acc_scr[:, vs] = acc / l
    o_ref[...] = (_rms(acc_scr[...]) * go_ref[...]).astype(BF16)


def _attn_sample(c2d, krt, q, kn, vn, wp, batch, t, past):
    consts = [wp["w_uk_t"], wp["w_uv"], wp["goa"]]
    return pl.pallas_call(
        functools.partial(_attn_sample_kernel, past),
        out_shape=jax.ShapeDtypeStruct((batch * t, ATTN_WIDTH), BF16),
        grid=(batch,),
        in_specs=[pl.BlockSpec((past, KV_RANK), lambda b: (b, 0)),
                  pl.BlockSpec((None, ROPE_DIM, past), lambda b: (b, 0, 0)),
                  pl.BlockSpec((t, QK_WIDTH), lambda b: (b, 0)),
                  pl.BlockSpec((t, QK_WIDTH), lambda b: (b, 0)),
                  pl.BlockSpec((t, ATTN_WIDTH), lambda b: (b, 0))]
                 + [_const_spec(c.shape) for c in consts],
        out_specs=pl.BlockSpec((t, ATTN_WIDTH), lambda b: (b, 0)),
        scratch_shapes=[pltpu.VMEM((QK_WIDTH, past), BF16), pltpu.VMEM((past, ATTN_WIDTH), BF16),
                        pltpu.VMEM((N_HEADS, t, past), F32), pltpu.VMEM((N_HEADS, t, t), F32),
                        pltpu.VMEM((t, ATTN_WIDTH), F32)],
        compiler_params=pltpu.CompilerParams(dimension_semantics=("parallel",),
                                             vmem_limit_bytes=VMEM_LIMIT),
        name="attn_sample",
    )(c2d, krt, q, kn, vn, *consts)


def _outproj_kernel(an_ref, gm_ref, x_ref, wo_ref, gffn_ref, h_ref, hn_ref):
    h = (x_ref[...] + _dot(an_ref[...], wo_ref[:ATTN_WIDTH, :])
         + _dot(gm_ref[...], wo_ref[ATTN_WIDTH:, :]))
    h_ref[...] = h
    hn_ref[...] = (_rms(h) * gffn_ref[...]).astype(BF16)


def _outproj(an, gm, x2d, wp, tm):
    m = x2d.shape[0]
    row_spec = lambda w: pl.BlockSpec((tm, w), lambda i: (i, 0))
    return pl.pallas_call(
        _outproj_kernel,
        out_shape=[jax.ShapeDtypeStruct((m, D_MODEL), F32), jax.ShapeDtypeStruct((m, D_MODEL), BF16)],
        grid=(m // tm,),
        in_specs=[row_spec(ATTN_WIDTH), row_spec(GMLP_WIDTH), row_spec(D_MODEL),
                  _const_spec(wp["w_out"].shape), _const_spec(wp["gffn"].shape)],
        out_specs=[row_spec(D_MODEL), row_spec(D_MODEL)],
        compiler_params=pltpu.CompilerParams(dimension_semantics=("parallel",),
                                             vmem_limit_bytes=VMEM_LIMIT),
        name="outproj",
    )(an, gm, x2d, wp["w_out"], wp["gffn"])


def _ffn_kernel(h_ref, hn_ref, wu_ref, wd_ref, y_ref):
    f = pl.program_id(1)

    @pl.when(f == 0)
    def _():
        y_ref[...] = h_ref[...]

    a = jnp.maximum(_dot(hn_ref[...], wu_ref[...]), 0.0)
    y_ref[...] += _dot((a * a).astype(BF16), wd_ref[...])


def _ffn(h, hn, wp, tm, tf):
    m = h.shape[0]
    return pl.pallas_call(
        _ffn_kernel,
        out_shape=jax.ShapeDtypeStruct((m, D_MODEL), F32),
        grid=(m // tm, D_FF // tf),
        in_specs=[pl.BlockSpec((tm, D_MODEL), lambda i, f: (i, 0), pipeline_mode=pl.Buffered(1)),
                  pl.BlockSpec((tm, D_MODEL), lambda i, f: (i, 0)),
                  pl.BlockSpec((D_MODEL, tf), lambda i, f: (0, f)),
                  pl.BlockSpec((tf, D_MODEL), lambda i, f: (f, 0))],
        out_specs=pl.BlockSpec((tm, D_MODEL), lambda i, f: (i, 0)),
        compiler_params=pltpu.CompilerParams(dimension_semantics=("parallel", "arbitrary"),
                                             vmem_limit_bytes=VMEM_LIMIT),
        name="ffn",
    )(h, hn, wp["w_up"], wp["w_down"])


def _rope_tables(pos):
    inv = ROPE_THETA ** (-jnp.arange(HALF_ROPE, dtype=F32) / HALF_ROPE)
    ang = pos.astype(F32)[:, None] * inv[None, :]
    cos, sin = jnp.cos(ang), jnp.sin(ang)
    z = jnp.zeros_like(cos)
    z2 = jnp.zeros((pos.shape[0], LANES - ROPE_DIM), F32)
    return (jnp.concatenate([cos, cos, z2], axis=1), jnp.concatenate([-sin, z, z2], axis=1),
            jnp.concatenate([z, sin, z2], axis=1))


def _head_gain(g_nope, g_rope, scale):
    pad = jnp.zeros((HEAD_PAD - QK_DIM,), F32)
    return (jnp.concatenate([g_nope, g_rope, g_rope, pad]) * scale)[None, :]


def _pad_w_in_kernel(wt_ref, o_ref):
    tr = wt_ref.shape[1]
    split = C_PE0 + ROPE_DIM
    o_ref[:, :C_PE0] = wt_ref[:C_PE0, :].T.astype(BF16)
    pe = wt_ref[C_PE0:C_PE1, :].T
    lane = lax.broadcasted_iota(jnp.int32, (tr, LANES), 1)
    o_ref[:, C_PE0:C_PE1] = jnp.where(lane < ROPE_DIM, pe, 0.0).astype(BF16)
    o_ref[:, C_PE1:] = wt_ref[split:, :].T.astype(BF16)


def _pad_w_in(w_in_t, tr):
    width, rows = w_in_t.shape
    assert width + LANES - ROPE_DIM == IN_PAD
    return pl.pallas_call(
        _pad_w_in_kernel,
        out_shape=jax.ShapeDtypeStruct((rows, IN_PAD), BF16),
        grid=(rows // tr,),
        in_specs=[pl.BlockSpec((width, tr), lambda i: (0, i))],
        out_specs=pl.BlockSpec((tr, IN_PAD), lambda i: (i, 0)),
        compiler_params=pltpu.CompilerParams(dimension_semantics=("parallel",)),
        name="pad_w_in",
    )(w_in_t)


def _prep_weights(norm_mix, w_in, q_lat_norm, kv_lat_norm, w_uq, w_uk, w_uv, q_norm_nope, q_norm_rope,
                  k_norm_nope, k_norm_rope, v_norm, w_spatial, b_spatial, out_norm_attn, out_norm_gmlp,
                  norm_ffn, chunk_len):
    w_in_p = _pad_w_in(w_in.T, 256)
    wq = w_uq.reshape(Q_RANK, N_HEADS, QK_DIM)
    wq = jnp.pad(wq, ((0, 0), (0, 0), (0, HEAD_PAD - QK_DIM))).reshape(Q_RANK, QK_WIDTH).astype(BF16)
    reps = GMLP_CHUNK // chunk_len
    ws = jnp.tile(w_spatial[:, :chunk_len, :chunk_len], (1, reps, reps))
    bst = jnp.tile(b_spatial[:, :chunk_len], (1, reps)).T
    return {
        "gmix": norm_mix[None, :], "w_in": w_in_p, "gql": q_lat_norm[None, :], "gkv": kv_lat_norm[None, :],
        "w_uq": wq, "w_uk": w_uk.astype(BF16), "w_uk_t": w_uk.T.astype(BF16),
        "w_uv": w_uv.astype(BF16), "w_uv_t": w_uv.T.astype(BF16),
        "gq": (_head_gain(q_norm_nope, q_norm_rope, QK_DIM ** -0.5 * math.log2(math.e))
               * _head_gain(k_norm_nope, k_norm_rope, 1.0)),
        "v_norm": v_norm, "ws": ws, "bst": bst, "gog": out_norm_gmlp[None, :],
        "goa": out_norm_attn[None, :], "gffn": norm_ffn[None, :],
    }


def _finish(an, gm, x2d, wp, tm_ffn):
    h, hn = _outproj(an, gm, x2d, wp, 512)
    return _ffn(h, hn, wp, tm_ffn, 1024)


def kernel(x_prompt, x_sample, cache_c_kv, cache_k_rope, norm_mix, w_in, q_lat_norm, kv_lat_norm, w_uq, w_uk, w_uv, q_norm_nope, q_norm_rope, k_norm_nope, k_norm_rope, v_norm, w_spatial, b_spatial, out_norm_attn, out_norm_gmlp, w_out, norm_ffn, w_up, w_down):
    depth = w_in.shape[0]
    assert depth == 1
    batch, seq, _ = x_prompt.shape
    dec_batch, dec_seq, _ = x_sample.shape
    past = cache_c_kv.shape[2]
    assert past % CHUNK == 0 and dec_seq <= CHUNK and GMLP_CHUNK % dec_seq == 0

    weights = (norm_mix[0], w_in[0], q_lat_norm[0], kv_lat_norm[0], w_uq[0], w_uk[0], w_uv[0],
               q_norm_nope[0], q_norm_rope[0], k_norm_nope[0], k_norm_rope[0], v_norm[0], w_spatial[0],
               b_spatial[0], out_norm_attn[0], out_norm_gmlp[0], norm_ffn[0])
    wp = _prep_weights(*weights, chunk_len=GMLP_CHUNK)
    reps = GMLP_CHUNK // dec_seq
    ws_s = jnp.tile(w_spatial[0][:, :dec_seq, :dec_seq], (1, reps, reps))
    bst_s = jnp.tile(b_spatial[0][:, :dec_seq], (1, reps)).T

    tm_p = 256
    xp = x_prompt.reshape(batch * seq, D_MODEL)
    tabs_p = _rope_tables(jnp.arange(seq, dtype=jnp.int32))
    q, k, vt, ckv_p, kpe_p, gm, wo_b, wu_b, wd_b = _proj(
        xp, tabs_p, wp, GMLP_CHUNK, tm_p, f32_weights=(w_out[0], w_up[0], w_down[0]))
    wp = dict(wp, w_out=wo_b, w_up=wu_b, w_down=wd_b)
    wp_s = dict(wp, ws=ws_s, bst=bst_s)
    an = _attn_prompt(q, k, vt, wp["goa"], batch, seq, tm_p)
    y_p = _finish(an, gm, xp, wp, 1024)

    tm_s = 256
    xs = x_sample.reshape(dec_batch * dec_seq, D_MODEL)
    pos_s = past + (jnp.arange(tm_s, dtype=jnp.int32) % dec_seq)
    tabs_s = _rope_tables(pos_s)
    qs, kn, vn, ckv_s, kpe_s, gms, vg_s = _proj(xs, tabs_s, wp_s, dec_seq, tm_s)
    ans = _attn_sample(cache_c_kv[0].reshape(dec_batch * past, KV_RANK),
                       jnp.swapaxes(cache_k_rope[0], 1, 2),
                       qs, kn, vn, wp, dec_batch, dec_seq, past)
    y_s = _finish(ans, gms, xs, wp, 1024)

    return (y_p.reshape(batch, seq, D_MODEL),
            y_s.reshape(dec_batch, dec_seq, D_MODEL),
            ckv_p.reshape(1, batch, seq, KV_RANK),
            kpe_p.reshape(1, batch, seq, ROPE_DIM),
            ckv_s.reshape(1, dec_batch, dec_seq, KV_RANK),
            kpe_s.reshape(1, dec_batch, dec_seq, ROPE_DIM),
            vg_s.reshape(1, dec_batch, dec_seq, GMLP_WIDTH))
```

```python
import functools
import math

import jax
import jax.numpy as jnp
from jax import lax
from jax.experimental import pallas as pl
from jax.experimental.pallas import tpu as pltpu

D_MODEL = 2048
CHUNK = 64
N_HEADS = 8
NOPE_DIM = 128
ROPE_DIM = 64
HALF_ROPE = ROPE_DIM // 2
QK_DIM = NOPE_DIM + ROPE_DIM
V_DIM = 128
ATTN_WIDTH = N_HEADS * V_DIM
Q_RANK = 512
KV_RANK = 256
GMLP_GROUPS = 8
GMLP_GROUP_DIM = 128
GMLP_WIDTH = GMLP_GROUPS * GMLP_GROUP_DIM
GMLP_CHUNK = 128
D_FF = 4 * D_MODEL
ROPE_THETA = 10000.0
EPS = 1e-6

LANES = 128
SUBLANES = 8
BF16_SUBLANES = 2 * SUBLANES
HEAD_PAD = 2 * LANES
QK_WIDTH = N_HEADS * HEAD_PAD
C_Q0, C_Q1 = 0, Q_RANK
C_KV0, C_KV1 = C_Q1, C_Q1 + KV_RANK
C_PE0, C_PE1 = C_KV1, C_KV1 + LANES
C_U0, C_U1 = C_PE1, C_PE1 + GMLP_WIDTH
C_V0, C_V1 = C_U1, C_U1 + GMLP_WIDTH
IN_PAD = C_V1
VMEM_LIMIT = 56 * 1024 * 1024

BF16 = jnp.bfloat16
F32 = jnp.float32


def _dot(a, b):
    return jnp.dot(a, b, preferred_element_type=F32)


def _dot_nt(a, b):
    return lax.dot_general(a, b, (((1,), (1,)), ((), ())), preferred_element_type=F32)


def _sumsq(x):
    return jnp.sum(x * x, axis=-1, keepdims=True)


def _rms(x):
    return x * lax.rsqrt(jnp.mean(x * x, axis=-1, keepdims=True) + EPS)


def _gelu(x):
    c = math.sqrt(2.0 / math.pi)
    return 0.5 * x * (1.0 + jnp.tanh(c * (x + 0.044715 * (x * x * x))))


def _rope(t, cos, s1, s2):
    return t * cos + pltpu.roll(t, LANES - HALF_ROPE, 1) * s1 + pltpu.roll(t, HALF_ROPE, 1) * s2


def _const_spec(shape):
    nd = len(shape)
    return pl.BlockSpec(shape, lambda *_: (0,) * nd, pipeline_mode=pl.Buffered(1))


def _proj_kernel(chunk_len, prompt, x_ref, gmix_ref, win_ref, gql_ref, gkv_ref, wuq_ref, wuk_ref,
                 wuv_ref, gq_ref, cos_ref, s1_ref, s2_ref, vn_ref, ws_ref, bst_ref, gog_ref, *rest):
    v_transposed = prompt
    emit_v = not prompt
    if prompt:
        (wo_ref, wu_ref, wd_ref, q_ref, k_ref, v_ref, ckv_ref, kpe_ref, gm_ref,
         wo_out, wu_out, wd_out, gate_scr, vgb_scr) = rest
        wo_out[...] = wo_ref[...].astype(BF16)
        wu_out[...] = wu_ref[...].astype(BF16)
        wd_out[...] = wd_ref[...].astype(BF16)
    else:
        q_ref, k_ref, v_ref, ckv_ref, kpe_ref, gm_ref, vg_ref, gate_scr, vgb_scr = rest
    tm = x_ref.shape[0]
    xn = (_rms(x_ref[...]) * gmix_ref[...]).astype(BF16)
    cos, s1, s2 = cos_ref[...], s1_ref[...], s2_ref[...]

    zq = _dot(xn, win_ref[:, C_Q0:C_Q1])
    qln = (_rms(zq) * gql_ref[...]).astype(BF16)
    qraw = _dot(qln, wuq_ref[...])
    for h in range(N_HEADS):
        lo = h * HEAD_PAD
        nope = qraw[:, lo:lo + LANES]
        rp = qraw[:, lo + LANES:lo + HEAD_PAD]
        r = lax.rsqrt((_sumsq(nope) + _sumsq(rp)) * (1.0 / QK_DIM) + EPS)
        rp = _rope(rp, cos, s1, s2)
        q_ref[:, lo:lo + LANES] = (nope * r * gq_ref[:, :LANES]).astype(BF16)
        q_ref[:, lo + LANES:lo + HEAD_PAD] = (rp * r * gq_ref[:, LANES:]).astype(BF16)

    ckv = _rms(_dot(xn, win_ref[:, C_KV0:C_KV1])) * gkv_ref[...]
    ckv_ref[...] = ckv
    cb = ckv.astype(BF16)
    pe = _rope(_dot(xn, win_ref[:, C_PE0:C_PE1]), cos, s1, s2)
    kpe_ref[...] = pe[:, :ROPE_DIM]
    ss_pe = _sumsq(pe)
    knope = _dot(cb, wuk_ref[...])
    if v_transposed:
        v_ref[...] = _dot_nt(wuv_ref[...], cb).astype(BF16)
    else:
        v_ref[...] = _dot(cb, wuv_ref[...]).astype(BF16)
    for h in range(N_HEADS):
        nope = knope[:, h * LANES:(h + 1) * LANES]
        r = lax.rsqrt((_sumsq(nope) + ss_pe) * (1.0 / QK_DIM) + EPS)
        lo = h * HEAD_PAD
        k_ref[:, lo:lo + LANES] = (nope * r).astype(BF16)
        k_ref[:, lo + LANES:lo + HEAD_PAD] = (pe * r).astype(BF16)

    gv = _gelu(_dot(xn, win_ref[:, C_V0:C_V1]))
    for g in range(GMLP_GROUPS):
        blk = gv[:, g * LANES:(g + 1) * LANES]
        vg = _rms(blk) * vn_ref[g:g + 1, :]
        if emit_v:
            vg_ref[:, g * LANES:(g + 1) * LANES] = vg
        vgb_scr[:, g * LANES:(g + 1) * LANES] = vg.astype(BF16)
    u = _gelu(_dot(xn, win_ref[:, C_U0:C_U1]))
    row = lax.broadcasted_iota(jnp.int32, (GMLP_CHUNK, GMLP_CHUNK), 0)
    col = lax.broadcasted_iota(jnp.int32, (GMLP_CHUNK, GMLP_CHUNK), 1)
    causal = (row // chunk_len == col // chunk_len) & (col <= row)
    for g in range(GMLP_GROUPS):
        wm = jnp.where(causal, ws_ref[g], 0.0).astype(BF16)
        bias = bst_ref[:, g:g + 1]
        for c in range(tm // GMLP_CHUNK):
            rows = slice(c * GMLP_CHUNK, (c + 1) * GMLP_CHUNK)
            cols = slice(g * LANES, (g + 1) * LANES)
            s = _dot(wm, vgb_scr[rows, cols]) + bias
            gate_scr[rows, cols] = u[rows, cols] * s
    gm_ref[...] = (_rms(gate_scr[...]) * gog_ref[...]).astype(BF16)


def _proj(x2d, tabs, wp, chunk_len, tm, f32_weights=None):
    prompt = f32_weights is not None
    m = x2d.shape[0]
    steps = m // tm
    cos, s1, s2 = tabs
    tab_blocks = cos.shape[0] // tm
    row_spec = lambda w: pl.BlockSpec((tm, w), lambda i: (i, 0))
    tab_spec = pl.BlockSpec((tm, LANES), lambda i: (i % tab_blocks, 0))
    consts = [wp["gmix"], wp["w_in"], wp["gql"], wp["gkv"], wp["w_uq"], wp["w_uk"],
              wp["w_uv_t"] if prompt else wp["w_uv"], wp["gq"]]
    consts2 = [wp["v_norm"], wp["ws"], wp["bst"], wp["gog"]]
    in_specs = ([row_spec(D_MODEL)] + [_const_spec(c.shape) for c in consts]
                + [tab_spec] * 3 + [_const_spec(c.shape) for c in consts2])
    v_shape = (steps * ATTN_WIDTH, tm) if prompt else (m, ATTN_WIDTH)
    v_spec = pl.BlockSpec((ATTN_WIDTH, tm), lambda i: (i, 0)) if prompt else row_spec(ATTN_WIDTH)
    out_shape = [jax.ShapeDtypeStruct((m, QK_WIDTH), BF16), jax.ShapeDtypeStruct((m, QK_WIDTH), BF16),
                 jax.ShapeDtypeStruct(v_shape, BF16), jax.ShapeDtypeStruct((m, KV_RANK), F32),
                 jax.ShapeDtypeStruct((m, ROPE_DIM), F32), jax.ShapeDtypeStruct((m, GMLP_WIDTH), BF16)]
    out_specs = [row_spec(QK_WIDTH), row_spec(QK_WIDTH), v_spec, row_spec(KV_RANK),
                 row_spec(ROPE_DIM), row_spec(GMLP_WIDTH)]
    extra_in = []
    if prompt:
        w_out, w_up, w_down = f32_weights
        slabs = [pl.BlockSpec((D_MODEL // steps, D_MODEL), lambda i: (i, 0)),
                 pl.BlockSpec((D_MODEL, D_FF // steps), lambda i: (0, i)),
                 pl.BlockSpec((D_FF // steps, D_MODEL), lambda i: (i, 0))]
        extra_in = [w_out, w_up, w_down]
        in_specs += slabs
        out_shape += [jax.ShapeDtypeStruct(w.shape, BF16) for w in extra_in]
        out_specs += slabs
    else:
        out_shape.append(jax.ShapeDtypeStruct((m, GMLP_WIDTH), F32))
        out_specs.append(row_spec(GMLP_WIDTH))
    return pl.pallas_call(
        functools.partial(_proj_kernel, chunk_len, prompt),
        out_shape=out_shape,
        grid=(steps,),
        in_specs=in_specs,
        out_specs=out_specs,
        scratch_shapes=[pltpu.VMEM((tm, GMLP_WIDTH), F32), pltpu.VMEM((tm, GMLP_WIDTH), BF16)],
        compiler_params=pltpu.CompilerParams(dimension_semantics=("parallel",),
                                             vmem_limit_bytes=VMEM_LIMIT),
        name="proj",
    )(x2d, *consts, cos, s1, s2, *consts2, *extra_in)


NEG = -1e30
SOFTMAX_KEY_CHUNK = 64
KEY_BLOCKS_PER_STEP = 2


def _attn_prompt_kernel(q_ref, k_ref, vt_ref, go_ref, o_ref, acc_scr, m_scr, l_scr, s_scr, p_scr):
    t = q_ref.shape[0]
    qi = pl.program_id(1)
    m_scr[...] = jnp.full(m_scr.shape, NEG, F32)
    l_scr[...] = jnp.zeros(l_scr.shape, F32)
    acc_scr[...] = jnp.zeros(acc_scr.shape, F32)

    def step(kb0, nblk, last_masked):
        nk = nblk * t
        if last_masked:
            key = lax.broadcasted_iota(jnp.int32, (t, t), 0)
            qry = lax.broadcasted_iota(jnp.int32, (t, t), 1)
            ok = (key // CHUNK) <= (qry // CHUNK)

        m8 = []
        for h in range(N_HEADS):
            qs = slice(h * HEAD_PAD, (h + 1) * HEAD_PAD)
            mh = None
            for j in range(nblk):
                rows = pl.ds(pl.multiple_of((kb0 + j) * t, t), t)
                s = _dot_nt(k_ref[rows, qs], q_ref[:, qs])
                if last_masked and j == nblk - 1:
                    s = jnp.where(ok, s, NEG)
                s_scr[h, j * t:(j + 1) * t, :] = s
                mj = jnp.max(s.reshape(t // SUBLANES, SUBLANES, t), axis=0)
                mh = mj if mh is None else jnp.maximum(mh, mj)
            m8.append(mh)

        ones = jnp.ones((BF16_SUBLANES, nk), BF16)
        for h in range(N_HEADS):
            hs = slice(h * V_DIM, (h + 1) * V_DIM)
            m_old = m_scr[h]
            m_new = jnp.maximum(m_old, jnp.max(m8[h], axis=0, keepdims=True))
            alpha = jnp.exp2(m_old - m_new)
            for c in range(nk // SOFTMAX_KEY_CHUNK):
                cs = slice(c * SOFTMAX_KEY_CHUNK, (c + 1) * SOFTMAX_KEY_CHUNK)
                p_scr[h, cs, :] = jnp.exp2(s_scr[h, cs, :] - m_new).astype(BF16)
            m_scr[h] = m_new
            vt = [vt_ref[pl.ds(pl.multiple_of((kb0 + j) * ATTN_WIDTH, ATTN_WIDTH) + h * V_DIM, V_DIM), :]
                  for j in range(nblk)]
            vt = vt[0] if nblk == 1 else jnp.concatenate(vt, axis=1)
            pv = _dot(jnp.concatenate([vt, ones], axis=0), p_scr[h, :nk, :])
            acc_scr[hs, :] = alpha * acc_scr[hs, :] + pv[:V_DIM]
            l_scr[h] = alpha * l_scr[h] + pv[V_DIM:V_DIM + 1]

    def body(j, carry):
        step(j * KEY_BLOCKS_PER_STEP, KEY_BLOCKS_PER_STEP, False)
        return carry

    lax.fori_loop(0, qi // KEY_BLOCKS_PER_STEP, body, 0)
    for rem in range(KEY_BLOCKS_PER_STEP):
        @pl.when(qi % KEY_BLOCKS_PER_STEP == rem)
        def _(rem=rem):
            step(qi - rem, rem + 1, True)

    for h in range(N_HEADS):
        hs = slice(h * V_DIM, (h + 1) * V_DIM)
        acc_scr[hs, :] = acc_scr[hs, :] / l_scr[h]
    o = acc_scr[...].T
    o_ref[...] = (_rms(o) * go_ref[...]).astype(BF16)


def _attn_prompt(q, k, vt, go, batch, seq, t):
    nq = seq // t
    assert vt.shape == (batch * nq * ATTN_WIDTH, t)
    tk = KEY_BLOCKS_PER_STEP * t
    return pl.pallas_call(
        _attn_prompt_kernel,
        out_shape=jax.ShapeDtypeStruct((batch * seq, ATTN_WIDTH), BF16),
        grid=(batch, nq),
        in_specs=[pl.BlockSpec((t, QK_WIDTH), lambda b, i: (b * nq + i, 0)),
                  pl.BlockSpec((seq, QK_WIDTH), lambda b, i: (b, 0)),
                  pl.BlockSpec((nq * ATTN_WIDTH, t), lambda b, i: (b, 0)),
                  _const_spec(go.shape)],
        out_specs=pl.BlockSpec((t, ATTN_WIDTH), lambda b, i: (b * nq + i, 0)),
        scratch_shapes=[pltpu.VMEM((ATTN_WIDTH, t), F32), pltpu.VMEM((N_HEADS, 1, t), F32),
                        pltpu.VMEM((N_HEADS, 1, t), F32), pltpu.VMEM((N_HEADS, tk, t), F32),
                        pltpu.VMEM((N_HEADS, tk, t), BF16)],
        compiler_params=pltpu.CompilerParams(dimension_semantics=("parallel", "arbitrary"),
                                             vmem_limit_bytes=VMEM_LIMIT),
        name="attn_prompt",
    )(q, k, vt, go)


def _attn_sample_kernel(past, c_ref, krt_ref, q_ref, kn_ref, vn_ref, wukt_ref, wuv_ref, go_ref,
                        o_ref, cb_scr, qa_scr, qr_scr, r_scr, sc_scr, sn_scr, pc_scr, acc_scr):
    t = q_ref.shape[0]
    cb_scr[...] = c_ref[...].astype(BF16)
    krt = krt_ref[...]
    ss_pe = jnp.sum(krt * krt, axis=0, keepdims=True)
    knt = _dot_nt(wukt_ref[...], cb_scr[...])
    for h in range(N_HEADS):
        nope = knt[h * LANES:(h + 1) * LANES, :]
        ss = jnp.sum(nope * nope, axis=0, keepdims=True) + ss_pe
        r_scr[h] = lax.rsqrt(ss * (1.0 / QK_DIM) + EPS)

    row = lax.broadcasted_iota(jnp.int32, (t, t), 0)
    col = lax.broadcasted_iota(jnp.int32, (t, t), 1)
    new_ok = ((past + col) // CHUNK) <= ((past + row) // CHUNK)
    for h in range(N_HEADS):
        lo = h * HEAD_PAD
        rows = slice(h * t, (h + 1) * t)
        qa_scr[rows, :] = _dot(q_ref[:, lo:lo + NOPE_DIM],
                               wukt_ref[h * NOPE_DIM:(h + 1) * NOPE_DIM, :]).astype(BF16)
        qr_scr[rows, :] = q_ref[:, lo + NOPE_DIM:lo + QK_DIM]
        sn_scr[h] = jnp.where(new_ok, _dot_nt(q_ref[:, lo:lo + HEAD_PAD], kn_ref[:, lo:lo + HEAD_PAD]), NEG)
    sc_scr[...] = (_dot_nt(qa_scr[...], cb_scr[...]) + _dot(qr_scr[...], krt.astype(BF16)))
    denom, p_new = [], []
    for h in range(N_HEADS):
        s_c, s_n = sc_scr[h * t:(h + 1) * t, :] * r_scr[h], sn_scr[h]
        m = jnp.maximum(jnp.max(s_c, axis=-1, keepdims=True), jnp.max(s_n, axis=-1, keepdims=True))
        p_c = jnp.exp2(s_c - m)
        p_n = jnp.exp2(s_n - m)
        l = jnp.sum(p_c, axis=-1, keepdims=True) + jnp.sum(p_n, axis=-1, keepdims=True)
        denom.append(l)
        p_new.append(p_n.astype(BF16))
        pc_scr[h * t:(h + 1) * t, :] = p_c.astype(BF16)
    lat = _dot(pc_scr[...], cb_scr[...]).astype(BF16)
    for h in range(N_HEADS):
        vs = slice(h * V_DIM, (h + 1) * V_DIM)
        acc = _dot(lat[h * t:(h + 1) * t, :], wuv_ref[:, vs]) + _dot(p_new[h], vn_ref[:, vs])
        acc_scr[:, vs] = acc / denom[h]
    o_ref[...] = (_rms(acc_scr[...]) * go_ref[...]).astype(BF16)


def _attn_sample(c2d, krt, q, kn, vn, wp, batch, t, past):
    consts = [wp["w_uk_t"], wp["w_uv"], wp["goa"]]
    return pl.pallas_call(
        functools.partial(_attn_sample_kernel, past),
        out_shape=jax.ShapeDtypeStruct((batch * t, ATTN_WIDTH), BF16),
        grid=(batch,),
        in_specs=[pl.BlockSpec((past, KV_RANK), lambda b: (b, 0)),
                  pl.BlockSpec((None, ROPE_DIM, past), lambda b: (b, 0, 0)),
                  pl.BlockSpec((t, QK_WIDTH), lambda b: (b, 0)),
                  pl.BlockSpec((t, QK_WIDTH), lambda b: (b, 0)),
                  pl.BlockSpec((t, ATTN_WIDTH), lambda b: (b, 0))]
                 + [_const_spec(c.shape) for c in consts],
        out_specs=pl.BlockSpec((t, ATTN_WIDTH), lambda b: (b, 0)),
        scratch_shapes=[pltpu.VMEM((past, KV_RANK), BF16), pltpu.VMEM((N_HEADS * t, KV_RANK), BF16),
                        pltpu.VMEM((N_HEADS * t, ROPE_DIM), BF16), pltpu.VMEM((N_HEADS, 1, past), F32),
                        pltpu.VMEM((N_HEADS * t, past), F32), pltpu.VMEM((N_HEADS, t, t), F32),
                        pltpu.VMEM((N_HEADS * t, past), BF16), pltpu.VMEM((t, ATTN_WIDTH), F32)],
        compiler_params=pltpu.CompilerParams(dimension_semantics=("parallel",),
                                             vmem_limit_bytes=VMEM_LIMIT),
        name="attn_sample",
    )(c2d, krt, q, kn, vn, *consts)


def _outproj_kernel(an_ref, gm_ref, x_ref, wo_ref, gffn_ref, h_ref, hn_ref):
    h = (x_ref[...] + _dot(an_ref[...], wo_ref[:ATTN_WIDTH, :])
         + _dot(gm_ref[...], wo_ref[ATTN_WIDTH:, :]))
    h_ref[...] = h
    hn_ref[...] = (_rms(h) * gffn_ref[...]).astype(BF16)


def _outproj(an, gm, x2d, wp, tm):
    m = x2d.shape[0]
    row_spec = lambda w: pl.BlockSpec((tm, w), lambda i: (i, 0))
    return pl.pallas_call(
        _outproj_kernel,
        out_shape=[jax.ShapeDtypeStruct((m, D_MODEL), F32), jax.ShapeDtypeStruct((m, D_MODEL), BF16)],
        grid=(m // tm,),
        in_specs=[row_spec(ATTN_WIDTH), row_spec(GMLP_WIDTH), row_spec(D_MODEL),
                  _const_spec(wp["w_out"].shape), _const_spec(wp["gffn"].shape)],
        out_specs=[row_spec(D_MODEL), row_spec(D_MODEL)],
        compiler_params=pltpu.CompilerParams(dimension_semantics=("parallel",),
                                             vmem_limit_bytes=VMEM_LIMIT),
        name="outproj",
    )(an, gm, x2d, wp["w_out"], wp["gffn"])


def _ffn_kernel(h_ref, hn_ref, wu_ref, wd_ref, y_ref):
    f = pl.program_id(1)
    slab = h_ref.shape[1]

    @pl.when(f == 0)
    def _():
        y_ref[...] = jnp.zeros(y_ref.shape, F32)

    for j in range(D_MODEL // slab):
        @pl.when(f == j)
        def _(j=j):
            y_ref[:, j * slab:(j + 1) * slab] += h_ref[...]

    a = jnp.maximum(_dot(hn_ref[...], wu_ref[...]), 0.0)
    y_ref[...] += _dot((a * a).astype(BF16), wd_ref[...])


def _ffn(h, hn, wp, tm, tf):
    m = h.shape[0]
    nf = D_FF // tf
    return pl.pallas_call(
        _ffn_kernel,
        out_shape=jax.ShapeDtypeStruct((m, D_MODEL), F32),
        grid=(m // tm, nf),
        in_specs=[pl.BlockSpec((tm, D_MODEL // nf), lambda i, f: (i, f)),
                  pl.BlockSpec((tm, D_MODEL), lambda i, f: (i, 0)),
                  pl.BlockSpec((D_MODEL, tf), lambda i, f: (0, f)),
                  pl.BlockSpec((tf, D_MODEL), lambda i, f: (f, 0))],
        out_specs=pl.BlockSpec((tm, D_MODEL), lambda i, f: (i, 0)),
        compiler_params=pltpu.CompilerParams(dimension_semantics=("parallel", "arbitrary"),
                                             vmem_limit_bytes=VMEM_LIMIT),
        name="ffn",
    )(h, hn, wp["w_up"], wp["w_down"])


def _rope_tables(pos):
    inv = ROPE_THETA ** (-jnp.arange(HALF_ROPE, dtype=F32) / HALF_ROPE)
    ang = pos.astype(F32)[:, None] * inv[None, :]
    cos, sin = jnp.cos(ang), jnp.sin(ang)
    z = jnp.zeros_like(cos)
    z2 = jnp.zeros((pos.shape[0], LANES - ROPE_DIM), F32)
    return (jnp.concatenate([cos, cos, z2], axis=1), jnp.concatenate([-sin, z, z2], axis=1),
            jnp.concatenate([z, sin, z2], axis=1))


def _head_gain(g_nope, g_rope, scale):
    pad = jnp.zeros((HEAD_PAD - QK_DIM,), F32)
    return (jnp.concatenate([g_nope, g_rope, g_rope, pad]) * scale)[None, :]


def _pad_w_in_kernel(wt_ref, o_ref):
    tr = wt_ref.shape[1]
    split = C_PE0 + ROPE_DIM
    o_ref[:, :C_PE0] = wt_ref[:C_PE0, :].T.astype(BF16)
    pe = wt_ref[C_PE0:C_PE1, :].T
    lane = lax.broadcasted_iota(jnp.int32, (tr, LANES), 1)
    o_ref[:, C_PE0:C_PE1] = jnp.where(lane < ROPE_DIM, pe, 0.0).astype(BF16)
    o_ref[:, C_PE1:] = wt_ref[split:, :].T.astype(BF16)


def _pad_w_in(w_in_t, tr):
    width, rows = w_in_t.shape
    assert width + LANES - ROPE_DIM == IN_PAD
    return pl.pallas_call(
        _pad_w_in_kernel,
        out_shape=jax.ShapeDtypeStruct((rows, IN_PAD), BF16),
        grid=(rows // tr,),
        in_specs=[pl.BlockSpec((width, tr), lambda i: (0, i))],
        out_specs=pl.BlockSpec((tr, IN_PAD), lambda i: (i, 0)),
        compiler_params=pltpu.CompilerParams(dimension_semantics=("parallel",)),
        name="pad_w_in",
    )(w_in_t)


def _prep_weights(norm_mix, w_in, q_lat_norm, kv_lat_norm, w_uq, w_uk, w_uv, q_norm_nope, q_norm_rope,
                  k_norm_nope, k_norm_rope, v_norm, w_spatial, b_spatial, out_norm_attn, out_norm_gmlp,
                  norm_ffn, chunk_len):
    w_in_p = _pad_w_in(w_in.T, 256)
    wq = w_uq.reshape(Q_RANK, N_HEADS, QK_DIM)
    wq = jnp.pad(wq, ((0, 0), (0, 0), (0, HEAD_PAD - QK_DIM))).reshape(Q_RANK, QK_WIDTH).astype(BF16)
    reps = GMLP_CHUNK // chunk_len
    ws = jnp.tile(w_spatial[:, :chunk_len, :chunk_len], (1, reps, reps))
    bst = jnp.tile(b_spatial[:, :chunk_len], (1, reps)).T
    return {
        "gmix": norm_mix[None, :], "w_in": w_in_p, "gql": q_lat_norm[None, :], "gkv": kv_lat_norm[None, :],
        "w_uq": wq, "w_uk": w_uk.astype(BF16), "w_uk_t": w_uk.T.astype(BF16),
        "w_uv": w_uv.astype(BF16), "w_uv_t": w_uv.T.astype(BF16),
        "gq": (_head_gain(q_norm_nope, q_norm_rope, QK_DIM ** -0.5 * math.log2(math.e))
               * _head_gain(k_norm_nope, k_norm_rope, 1.0)),
        "v_norm": v_norm, "ws": ws, "bst": bst, "gog": out_norm_gmlp[None, :],
        "goa": out_norm_attn[None, :], "gffn": norm_ffn[None, :],
    }


def _finish(an, gm, x2d, wp, tm_ffn):
    h, hn = _outproj(an, gm, x2d, wp, 512)
    return _ffn(h, hn, wp, tm_ffn, 1024)


def kernel(x_prompt, x_sample, cache_c_kv, cache_k_rope, norm_mix, w_in, q_lat_norm, kv_lat_norm, w_uq, w_uk, w_uv, q_norm_nope, q_norm_rope, k_norm_nope, k_norm_rope, v_norm, w_spatial, b_spatial, out_norm_attn, out_norm_gmlp, w_out, norm_ffn, w_up, w_down):
    depth = w_in.shape[0]
    assert depth == 1
    batch, seq, _ = x_prompt.shape
    dec_batch, dec_seq, _ = x_sample.shape
    past = cache_c_kv.shape[2]
    assert past % CHUNK == 0 and dec_seq <= CHUNK and GMLP_CHUNK % dec_seq == 0

    weights = (norm_mix[0], w_in[0], q_lat_norm[0], kv_lat_norm[0], w_uq[0], w_uk[0], w_uv[0],
               q_norm_nope[0], q_norm_rope[0], k_norm_nope[0], k_norm_rope[0], v_norm[0], w_spatial[0],
               b_spatial[0], out_norm_attn[0], out_norm_gmlp[0], norm_ffn[0])
    wp = _prep_weights(*weights, chunk_len=GMLP_CHUNK)
    reps = GMLP_CHUNK // dec_seq
    ws_s = jnp.tile(w_spatial[0][:, :dec_seq, :dec_seq], (1, reps, reps))
    bst_s = jnp.tile(b_spatial[0][:, :dec_seq], (1, reps)).T

    tm_p = 256
    xp = x_prompt.reshape(batch * seq, D_MODEL)
    tabs_p = _rope_tables(jnp.arange(seq, dtype=jnp.int32))
    q, k, vt, ckv_p, kpe_p, gm, wo_b, wu_b, wd_b = _proj(
        xp, tabs_p, wp, GMLP_CHUNK, tm_p, f32_weights=(w_out[0], w_up[0], w_down[0]))
    wp = dict(wp, w_out=wo_b, w_up=wu_b, w_down=wd_b)
    wp_s = dict(wp, ws=ws_s, bst=bst_s)
    an = _attn_prompt(q, k, vt, wp["goa"], batch, seq, tm_p)
    y_p = _finish(an, gm, xp, wp, 1024)

    tm_s = 256
    xs = x_sample.reshape(dec_batch * dec_seq, D_MODEL)
    pos_s = past + (jnp.arange(tm_s, dtype=jnp.int32) % dec_seq)
    tabs_s = _rope_tables(pos_s)
    qs, kn, vn, ckv_s, kpe_s, gms, vg_s = _proj(xs, tabs_s, wp_s, dec_seq, tm_s)
    ans = _attn_sample(cache_c_kv[0].reshape(dec_batch * past, KV_RANK),
                       jnp.swapaxes(cache_k_rope[0], 1, 2),
                       qs, kn, vn, wp, dec_batch, dec_seq, past)
    y_s = _finish(ans, gms, xs, wp, 1024)

    return (y_p.reshape(batch, seq, D_MODEL),
            y_s.reshape(dec_batch, dec_seq, D_MODEL),
            ckv_p.reshape(1, batch, seq, KV_RANK),
            kpe_p.reshape(1, batch, seq, ROPE_DIM),
            ckv_s.reshape(1, dec_batch, dec_seq, KV_RANK),
            kpe_s.reshape(1, dec_batch, dec_seq, ROPE_DIM),
            vg_s.reshape(1, dec_batch, dec_seq, GMLP_WIDTH))
```

```python
import functools
import math

import jax
import jax.numpy as jnp
import numpy as np
from jax import lax
from jax.experimental import pallas as pl
from jax.experimental.pallas import tpu as pltpu

D_MODEL = 2048
CHUNK = 64
N_HEADS = 8
NOPE_DIM = 128
ROPE_DIM = 64
HALF_ROPE = ROPE_DIM // 2
QK_DIM = NOPE_DIM + ROPE_DIM
V_DIM = 128
ATTN_WIDTH = N_HEADS * V_DIM
Q_RANK = 512
KV_RANK = 256
GMLP_GROUPS = 8
GMLP_GROUP_DIM = 128
GMLP_WIDTH = GMLP_GROUPS * GMLP_GROUP_DIM
GMLP_CHUNK = 128
D_FF = 4 * D_MODEL
ROPE_THETA = 10000.0
EPS = 1e-6

LANES = 128
SUBLANES = 8
BF16_SUBLANES = 2 * SUBLANES
HEAD_PAD = 2 * LANES
QK_WIDTH = N_HEADS * HEAD_PAD
C_Q0, C_Q1 = 0, Q_RANK
C_KV0, C_KV1 = C_Q1, C_Q1 + KV_RANK
C_PE0, C_PE1 = C_KV1, C_KV1 + LANES
C_U0, C_U1 = C_PE1, C_PE1 + GMLP_WIDTH
C_V0, C_V1 = C_U1, C_U1 + GMLP_WIDTH
IN_PAD = C_V1
VMEM_LIMIT = 56 * 1024 * 1024

BF16 = jnp.bfloat16
F32 = jnp.float32


def _dot(a, b):
    return jnp.dot(a, b, preferred_element_type=F32)


def _dot_nt(a, b):
    return lax.dot_general(a, b, (((1,), (1,)), ((), ())), preferred_element_type=F32)


def _sumsq(x):
    return jnp.sum(x * x, axis=-1, keepdims=True)


def _rms(x):
    return x * lax.rsqrt(jnp.mean(x * x, axis=-1, keepdims=True) + EPS)


def _gelu(x):
    c = math.sqrt(2.0 / math.pi)
    return 0.5 * x * (1.0 + jnp.tanh(c * (x + 0.044715 * (x * x * x))))


def _rope(t, cos, s1, s2):
    return t * cos + pltpu.roll(t, LANES - HALF_ROPE, 1) * s1 + pltpu.roll(t, HALF_ROPE, 1) * s2


def _const_spec(shape):
    nd = len(shape)
    return pl.BlockSpec(shape, lambda *_: (0,) * nd, pipeline_mode=pl.Buffered(1))


def _proj_kernel(chunk_len, prompt, x_ref, gmix_ref, win_ref, gql_ref, gkv_ref, wuq_ref, wuk_ref,
                 wuv_ref, gq_ref, cos_ref, s1_ref, s2_ref, vn_ref, ws_ref, bst_ref, gog_ref, *rest):
    v_transposed = prompt
    emit_v = not prompt
    if prompt:
        (wo_ref, wu_ref, wd_ref, q_ref, k_ref, v_ref, ckv_ref, kpe_ref, gm_ref,
         wo_out, wu_out, wd_out, gate_scr, vgb_scr) = rest
        wo_out[...] = wo_ref[...].astype(BF16)
        wu_out[...] = wu_ref[...].astype(BF16)
        wd_out[...] = wd_ref[...].astype(BF16)
    else:
        q_ref, k_ref, v_ref, ckv_ref, kpe_ref, gm_ref, vg_ref, gate_scr, vgb_scr = rest
    tm = x_ref.shape[0]
    xn = (_rms(x_ref[...]) * gmix_ref[...]).astype(BF16)
    cos, s1, s2 = cos_ref[...], s1_ref[...], s2_ref[...]

    zq = _dot(xn, win_ref[:, C_Q0:C_Q1])
    qln = (_rms(zq) * gql_ref[...]).astype(BF16)
    qraw = _dot(qln, wuq_ref[...])
    for h in range(N_HEADS):
        lo = h * HEAD_PAD
        nope = qraw[:, lo:lo + LANES]
        rp = qraw[:, lo + LANES:lo + HEAD_PAD]
        r = lax.rsqrt((_sumsq(nope) + _sumsq(rp)) * (1.0 / QK_DIM) + EPS)
        rp = _rope(rp, cos, s1, s2)
        q_ref[:, lo:lo + LANES] = (nope * r * gq_ref[:, :LANES]).astype(BF16)
        q_ref[:, lo + LANES:lo + HEAD_PAD] = (rp * r * gq_ref[:, LANES:]).astype(BF16)

    ckv = _rms(_dot(xn, win_ref[:, C_KV0:C_KV1])) * gkv_ref[...]
    ckv_ref[...] = ckv
    cb = ckv.astype(BF16)
    pe = _rope(_dot(xn, win_ref[:, C_PE0:C_PE1]), cos, s1, s2)
    kpe_ref[...] = pe.T[:ROPE_DIM, :] if prompt else pe[:, :ROPE_DIM]
    ss_pe = _sumsq(pe)
    knope = _dot(cb, wuk_ref[...])
    if v_transposed:
        v_ref[...] = _dot_nt(wuv_ref[...], cb).astype(BF16)
    else:
        v_ref[...] = _dot(cb, wuv_ref[...]).astype(BF16)
    for h in range(N_HEADS):
        nope = knope[:, h * LANES:(h + 1) * LANES]
        r = lax.rsqrt((_sumsq(nope) + ss_pe) * (1.0 / QK_DIM) + EPS)
        lo = h * HEAD_PAD
        k_ref[:, lo:lo + LANES] = (nope * r).astype(BF16)
        k_ref[:, lo + LANES:lo + HEAD_PAD] = (pe * r).astype(BF16)

    gv = _gelu(_dot(xn, win_ref[:, C_V0:C_V1]))
    for g in range(GMLP_GROUPS):
        blk = gv[:, g * LANES:(g + 1) * LANES]
        vg = _rms(blk) * vn_ref[g:g + 1, :]
        if emit_v:
            vg_ref[:, g * LANES:(g + 1) * LANES] = vg
        vgb_scr[:, g * LANES:(g + 1) * LANES] = vg.astype(BF16)
    u = _gelu(_dot(xn, win_ref[:, C_U0:C_U1]))
    row = lax.broadcasted_iota(jnp.int32, (GMLP_CHUNK, GMLP_CHUNK), 0)
    col = lax.broadcasted_iota(jnp.int32, (GMLP_CHUNK, GMLP_CHUNK), 1)
    causal = (row // chunk_len == col // chunk_len) & (col <= row)
    for g in range(GMLP_GROUPS):
        wm = jnp.where(causal, ws_ref[g], 0.0).astype(BF16)
        bias = bst_ref[:, g:g + 1]
        for c in range(tm // GMLP_CHUNK):
            rows = slice(c * GMLP_CHUNK, (c + 1) * GMLP_CHUNK)
            cols = slice(g * LANES, (g + 1) * LANES)
            s = _dot(wm, vgb_scr[rows, cols]) + bias
            gate_scr[rows, cols] = u[rows, cols] * s
    gm_ref[...] = (_rms(gate_scr[...]) * gog_ref[...]).astype(BF16)


def _proj(x2d, tabs, wp, chunk_len, tm, f32_weights=None):
    prompt = f32_weights is not None
    m = x2d.shape[0]
    steps = m // tm
    cos, s1, s2 = tabs
    tab_blocks = cos.shape[0] // tm
    row_spec = lambda w: pl.BlockSpec((tm, w), lambda i: (i, 0))
    tab_spec = pl.BlockSpec((tm, LANES), lambda i: (i % tab_blocks, 0))
    consts = [wp["gmix"], wp["w_in"], wp["gql"], wp["gkv"], wp["w_uq"], wp["w_uk"],
              wp["w_uv_t"] if prompt else wp["w_uv"], wp["gq"]]
    consts2 = [wp["v_norm"], wp["ws"], wp["bst"], wp["gog"]]
    in_specs = ([row_spec(D_MODEL)] + [_const_spec(c.shape) for c in consts]
                + [tab_spec] * 3 + [_const_spec(c.shape) for c in consts2])
    v_shape = (steps * ATTN_WIDTH, tm) if prompt else (m, ATTN_WIDTH)
    v_spec = pl.BlockSpec((ATTN_WIDTH, tm), lambda i: (i, 0)) if prompt else row_spec(ATTN_WIDTH)
    kpe_shape = (m // cos.shape[0], ROPE_DIM, cos.shape[0]) if prompt else (m, ROPE_DIM)
    kpe_spec = (pl.BlockSpec((None, ROPE_DIM, tm), lambda i: (i // tab_blocks, 0, i % tab_blocks))
                if prompt else row_spec(ROPE_DIM))
    out_shape = [jax.ShapeDtypeStruct((m, QK_WIDTH), BF16), jax.ShapeDtypeStruct((m, QK_WIDTH), BF16),
                 jax.ShapeDtypeStruct(v_shape, BF16), jax.ShapeDtypeStruct((m, KV_RANK), F32),
                 jax.ShapeDtypeStruct(kpe_shape, F32), jax.ShapeDtypeStruct((m, GMLP_WIDTH), BF16)]
    out_specs = [row_spec(QK_WIDTH), row_spec(QK_WIDTH), v_spec, row_spec(KV_RANK),
                 kpe_spec, row_spec(GMLP_WIDTH)]
    extra_in = []
    if prompt:
        w_out, w_up, w_down = f32_weights
        slabs = [pl.BlockSpec((D_MODEL // steps, D_MODEL), lambda i: (i, 0)),
                 pl.BlockSpec((D_MODEL, D_FF // steps), lambda i: (0, i)),
                 pl.BlockSpec((D_FF // steps, D_MODEL), lambda i: (i, 0))]
        extra_in = [w_out, w_up, w_down]
        in_specs += slabs
        out_shape += [jax.ShapeDtypeStruct(w.shape, BF16) for w in extra_in]
        out_specs += slabs
    else:
        out_shape.append(jax.ShapeDtypeStruct((m, GMLP_WIDTH), F32))
        out_specs.append(row_spec(GMLP_WIDTH))
    return pl.pallas_call(
        functools.partial(_proj_kernel, chunk_len, prompt),
        out_shape=out_shape,
        grid=(steps,),
        in_specs=in_specs,
        out_specs=out_specs,
        scratch_shapes=[pltpu.VMEM((tm, GMLP_WIDTH), F32), pltpu.VMEM((tm, GMLP_WIDTH), BF16)],
        compiler_params=pltpu.CompilerParams(dimension_semantics=("parallel",),
                                             vmem_limit_bytes=VMEM_LIMIT),
        name="proj",
    )(x2d, *consts, cos, s1, s2, *consts2, *extra_in)


NEG = -1e30
SOFTMAX_KEY_CHUNK = 64
KEY_BLOCKS_PER_STEP = 2
SCORE_LEAD = 4


def _attn_prompt_kernel(q_ref, k_ref, vt_ref, go_ref, o_ref, acc_scr, m_scr, l_scr, s_scr, p_scr):
    t = q_ref.shape[0]
    qi = pl.program_id(1)
    m_scr[...] = jnp.full(m_scr.shape, NEG, F32)
    l_scr[...] = jnp.zeros(l_scr.shape, F32)
    acc_scr[...] = jnp.zeros(acc_scr.shape, F32)

    def step(kb0, nblk, last_masked):
        nk = nblk * t
        if last_masked:
            key = lax.broadcasted_iota(jnp.int32, (t, t), 0)
            qry = lax.broadcasted_iota(jnp.int32, (t, t), 1)
            ok = (key // CHUNK) <= (qry // CHUNK)

        m8 = {}

        def scores(h):
            qs = slice(h * HEAD_PAD, (h + 1) * HEAD_PAD)
            mh = None
            for j in range(nblk):
                rows = pl.ds(pl.multiple_of((kb0 + j) * t, t), t)
                s = _dot_nt(k_ref[rows, qs], q_ref[:, qs])
                if last_masked and j == nblk - 1:
                    s = jnp.where(ok, s, NEG)
                s_scr[h, j * t:(j + 1) * t, :] = s
                mj = jnp.max(s.reshape(t // SUBLANES, SUBLANES, t), axis=0)
                mh = mj if mh is None else jnp.maximum(mh, mj)
            m8[h] = mh

        ones = jnp.ones((BF16_SUBLANES, nk), BF16)
        for h in range(min(SCORE_LEAD, N_HEADS)):
            scores(h)
        for h in range(N_HEADS):
            if h + SCORE_LEAD < N_HEADS:
                scores(h + SCORE_LEAD)
            hs = slice(h * V_DIM, (h + 1) * V_DIM)
            m_old = m_scr[h]
            m_new = jnp.maximum(m_old, jnp.max(m8[h], axis=0, keepdims=True))
            alpha = jnp.exp2(m_old - m_new)
            for c in range(nk // SOFTMAX_KEY_CHUNK):
                cs = slice(c * SOFTMAX_KEY_CHUNK, (c + 1) * SOFTMAX_KEY_CHUNK)
                p_scr[h, cs, :] = jnp.exp2(s_scr[h, cs, :] - m_new).astype(BF16)
            m_scr[h] = m_new
            vt = [vt_ref[pl.ds(pl.multiple_of((kb0 + j) * ATTN_WIDTH, ATTN_WIDTH) + h * V_DIM, V_DIM), :]
                  for j in range(nblk)]
            vt = vt[0] if nblk == 1 else jnp.concatenate(vt, axis=1)
            pv = _dot(jnp.concatenate([vt, ones], axis=0), p_scr[h, :nk, :])
            acc_scr[hs, :] = alpha * acc_scr[hs, :] + pv[:V_DIM]
            l_scr[h] = alpha * l_scr[h] + pv[V_DIM:V_DIM + 1]

    def body(j, carry):
        step(j * KEY_BLOCKS_PER_STEP, KEY_BLOCKS_PER_STEP, False)
        return carry

    lax.fori_loop(0, qi // KEY_BLOCKS_PER_STEP, body, 0)
    for rem in range(KEY_BLOCKS_PER_STEP):
        @pl.when(qi % KEY_BLOCKS_PER_STEP == rem)
        def _(rem=rem):
            step(qi - rem, rem + 1, True)

    for h in range(N_HEADS):
        hs = slice(h * V_DIM, (h + 1) * V_DIM)
        acc_scr[hs, :] = acc_scr[hs, :] / l_scr[h]
    o = acc_scr[...].T
    o_ref[...] = (_rms(o) * go_ref[...]).astype(BF16)


def _attn_prompt(q, k, vt, go, batch, seq, t):
    nq = seq // t
    assert vt.shape == (batch * nq * ATTN_WIDTH, t)
    tk = KEY_BLOCKS_PER_STEP * t
    return pl.pallas_call(
        _attn_prompt_kernel,
        out_shape=jax.ShapeDtypeStruct((batch * seq, ATTN_WIDTH), BF16),
        grid=(batch, nq),
        in_specs=[pl.BlockSpec((t, QK_WIDTH), lambda b, i: (b * nq + i, 0)),
                  pl.BlockSpec((seq, QK_WIDTH), lambda b, i: (b, 0)),
                  pl.BlockSpec((nq * ATTN_WIDTH, t), lambda b, i: (b, 0)),
                  _const_spec(go.shape)],
        out_specs=pl.BlockSpec((t, ATTN_WIDTH), lambda b, i: (b * nq + i, 0)),
        scratch_shapes=[pltpu.VMEM((ATTN_WIDTH, t), F32), pltpu.VMEM((N_HEADS, 1, t), F32),
                        pltpu.VMEM((N_HEADS, 1, t), F32), pltpu.VMEM((N_HEADS, tk, t), F32),
                        pltpu.VMEM((N_HEADS, tk, t), BF16)],
        compiler_params=pltpu.CompilerParams(dimension_semantics=("parallel", "arbitrary"),
                                             vmem_limit_bytes=VMEM_LIMIT),
        name="attn_prompt",
    )(q, k, vt, go)


def _attn_sample_kernel(past, c_ref, krt_ref, q_ref, kn_ref, vn_ref, wukt_ref, wuv_ref, go_ref,
                        o_ref, cb_scr, qa_scr, qr_scr, r_scr, sc_scr, sn_scr, pc_scr, acc_scr):
    t = q_ref.shape[0]
    cb_scr[...] = c_ref[...].astype(BF16)
    krt = krt_ref[...]
    ss_pe = jnp.sum(krt * krt, axis=0, keepdims=True)
    knt = _dot_nt(wukt_ref[...], cb_scr[...])
    for h in range(N_HEADS):
        nope = knt[h * LANES:(h + 1) * LANES, :]
        ss = jnp.sum(nope * nope, axis=0, keepdims=True) + ss_pe
        r_scr[h] = lax.rsqrt(ss * (1.0 / QK_DIM) + EPS)

    row = lax.broadcasted_iota(jnp.int32, (t, t), 0)
    col = lax.broadcasted_iota(jnp.int32, (t, t), 1)
    new_ok = ((past + col) // CHUNK) <= ((past + row) // CHUNK)
    for h in range(N_HEADS):
        lo = h * HEAD_PAD
        rows = slice(h * t, (h + 1) * t)
        qa_scr[rows, :] = _dot(q_ref[:, lo:lo + NOPE_DIM],
                               wukt_ref[h * NOPE_DIM:(h + 1) * NOPE_DIM, :]).astype(BF16)
        qr_scr[rows, :] = q_ref[:, lo + NOPE_DIM:lo + QK_DIM]
        sn_scr[h] = jnp.where(new_ok, _dot_nt(q_ref[:, lo:lo + HEAD_PAD], kn_ref[:, lo:lo + HEAD_PAD]), NEG)
    sc_scr[...] = (_dot_nt(qa_scr[...], cb_scr[...]) + _dot(qr_scr[...], krt.astype(BF16)))
    denom, p_new = [], []
    for h in range(N_HEADS):
        s_c, s_n = sc_scr[h * t:(h + 1) * t, :] * r_scr[h], sn_scr[h]
        m = jnp.maximum(jnp.max(s_c, axis=-1, keepdims=True), jnp.max(s_n, axis=-1, keepdims=True))
        p_c = jnp.exp2(s_c - m)
        p_n = jnp.exp2(s_n - m)
        l = jnp.sum(p_c, axis=-1, keepdims=True) + jnp.sum(p_n, axis=-1, keepdims=True)
        denom.append(l)
        p_new.append(p_n.astype(BF16))
        pc_scr[h * t:(h + 1) * t, :] = p_c.astype(BF16)
    lat = _dot(pc_scr[...], cb_scr[...]).astype(BF16)
    for h in range(N_HEADS):
        vs = slice(h * V_DIM, (h + 1) * V_DIM)
        acc = _dot(lat[h * t:(h + 1) * t, :], wuv_ref[:, vs]) + _dot(p_new[h], vn_ref[:, vs])
        acc_scr[:, vs] = acc / denom[h]
    o_ref[...] = (_rms(acc_scr[...]) * go_ref[...]).astype(BF16)


def _attn_sample(c2d, krt, q, kn, vn, wp, batch, t, past):
    consts = [wp["w_uk_t"], wp["w_uv"], wp["goa"]]
    return pl.pallas_call(
        functools.partial(_attn_sample_kernel, past),
        out_shape=jax.ShapeDtypeStruct((batch * t, ATTN_WIDTH), BF16),
        grid=(batch,),
        in_specs=[pl.BlockSpec((past, KV_RANK), lambda b: (b, 0)),
                  pl.BlockSpec((None, ROPE_DIM, past), lambda b: (b, 0, 0)),
                  pl.BlockSpec((t, QK_WIDTH), lambda b: (b, 0)),
                  pl.BlockSpec((t, QK_WIDTH), lambda b: (b, 0)),
                  pl.BlockSpec((t, ATTN_WIDTH), lambda b: (b, 0))]
                 + [_const_spec(c.shape) for c in consts],
        out_specs=pl.BlockSpec((t, ATTN_WIDTH), lambda b: (b, 0)),
        scratch_shapes=[pltpu.VMEM((past, KV_RANK), BF16), pltpu.VMEM((N_HEADS * t, KV_RANK), BF16),
                        pltpu.VMEM((N_HEADS * t, ROPE_DIM), BF16), pltpu.VMEM((N_HEADS, 1, past), F32),
                        pltpu.VMEM((N_HEADS * t, past), F32), pltpu.VMEM((N_HEADS, t, t), F32),
                        pltpu.VMEM((N_HEADS * t, past), BF16), pltpu.VMEM((t, ATTN_WIDTH), F32)],
        compiler_params=pltpu.CompilerParams(dimension_semantics=("parallel",),
                                             vmem_limit_bytes=VMEM_LIMIT),
        name="attn_sample",
    )(c2d, krt, q, kn, vn, *consts)


def _outproj_kernel(an_ref, gm_ref, x_ref, wo_ref, gffn_ref, h_ref, hn_ref):
    h = (x_ref[...] + _dot(an_ref[...], wo_ref[:ATTN_WIDTH, :])
         + _dot(gm_ref[...], wo_ref[ATTN_WIDTH:, :]))
    h_ref[...] = h
    hn_ref[...] = (_rms(h) * gffn_ref[...]).astype(BF16)


def _outproj(an, gm, x2d, wp, tm):
    m = x2d.shape[0]
    row_spec = lambda w: pl.BlockSpec((tm, w), lambda i: (i, 0))
    return pl.pallas_call(
        _outproj_kernel,
        out_shape=[jax.ShapeDtypeStruct((m, D_MODEL), F32), jax.ShapeDtypeStruct((m, D_MODEL), BF16)],
        grid=(m // tm,),
        in_specs=[row_spec(ATTN_WIDTH), row_spec(GMLP_WIDTH), row_spec(D_MODEL),
                  _const_spec(wp["w_out"].shape), _const_spec(wp["gffn"].shape)],
        out_specs=[row_spec(D_MODEL), row_spec(D_MODEL)],
        compiler_params=pltpu.CompilerParams(dimension_semantics=("parallel",),
                                             vmem_limit_bytes=VMEM_LIMIT),
        name="outproj",
    )(an, gm, x2d, wp["w_out"], wp["gffn"])


def _ffn_kernel(h_ref, hn_ref, wu_ref, wd_ref, y_ref):
    f = pl.program_id(1)
    slab = h_ref.shape[1]

    @pl.when(f == 0)
    def _():
        y_ref[...] = jnp.zeros(y_ref.shape, F32)

    for j in range(D_MODEL // slab):
        @pl.when(f == j)
        def _(j=j):
            y_ref[:, j * slab:(j + 1) * slab] += h_ref[...]

    a = jnp.maximum(_dot(hn_ref[...], wu_ref[...]), 0.0)
    y_ref[...] += _dot((a * a).astype(BF16), wd_ref[...])


def _ffn(h, hn, wp, tm, tf):
    m = h.shape[0]
    nf = D_FF // tf
    return pl.pallas_call(
        _ffn_kernel,
        out_shape=jax.ShapeDtypeStruct((m, D_MODEL), F32),
        grid=(m // tm, nf),
        in_specs=[pl.BlockSpec((tm, D_MODEL // nf), lambda i, f: (i, f)),
                  pl.BlockSpec((tm, D_MODEL), lambda i, f: (i, 0)),
                  pl.BlockSpec((D_MODEL, tf), lambda i, f: (0, f)),
                  pl.BlockSpec((tf, D_MODEL), lambda i, f: (f, 0))],
        out_specs=pl.BlockSpec((tm, D_MODEL), lambda i, f: (i, 0)),
        compiler_params=pltpu.CompilerParams(dimension_semantics=("parallel", "arbitrary"),
                                             vmem_limit_bytes=VMEM_LIMIT),
        name="ffn",
    )(h, hn, wp["w_up"], wp["w_down"])


def _rope_tables(pos):
    inv = ROPE_THETA ** (-np.arange(HALF_ROPE, dtype=np.float64) / HALF_ROPE)
    ang = np.asarray(pos, np.float64)[:, None] * inv[None, :]
    cos, sin = np.cos(ang), np.sin(ang)
    z = np.zeros_like(cos)
    z2 = np.zeros((ang.shape[0], LANES - ROPE_DIM))
    tabs = (np.concatenate([cos, cos, z2], axis=1), np.concatenate([-sin, z, z2], axis=1),
            np.concatenate([z, sin, z2], axis=1))
    return tuple(jnp.asarray(t, F32) for t in tabs)


def _head_gain(g_nope, g_rope, scale):
    pad = jnp.zeros((HEAD_PAD - QK_DIM,), F32)
    return (jnp.concatenate([g_nope, g_rope, g_rope, pad]) * scale)[None, :]


def _pad_w_in_kernel(wt_ref, o_ref):
    tr = wt_ref.shape[1]
    split = C_PE0 + ROPE_DIM
    o_ref[:, :C_PE0] = wt_ref[:C_PE0, :].T.astype(BF16)
    pe = wt_ref[C_PE0:C_PE1, :].T
    lane = lax.broadcasted_iota(jnp.int32, (tr, LANES), 1)
    o_ref[:, C_PE0:C_PE1] = jnp.where(lane < ROPE_DIM, pe, 0.0).astype(BF16)
    o_ref[:, C_PE1:] = wt_ref[split:, :].T.astype(BF16)


def _pad_w_in(w_in_t, tr):
    width, rows = w_in_t.shape
    assert width + LANES - ROPE_DIM == IN_PAD
    return pl.pallas_call(
        _pad_w_in_kernel,
        out_shape=jax.ShapeDtypeStruct((rows, IN_PAD), BF16),
        grid=(rows // tr,),
        in_specs=[pl.BlockSpec((width, tr), lambda i: (0, i))],
        out_specs=pl.BlockSpec((tr, IN_PAD), lambda i: (i, 0)),
        compiler_params=pltpu.CompilerParams(dimension_semantics=("parallel",)),
        name="pad_w_in",
    )(w_in_t)


def _prep_weights(norm_mix, w_in, q_lat_norm, kv_lat_norm, w_uq, w_uk, w_uv, q_norm_nope, q_norm_rope,
                  k_norm_nope, k_norm_rope, v_norm, w_spatial, b_spatial, out_norm_attn, out_norm_gmlp,
                  norm_ffn, chunk_len):
    w_in_p = _pad_w_in(w_in.T, 256)
    wq = w_uq.reshape(Q_RANK, N_HEADS, QK_DIM)
    wq = jnp.pad(wq, ((0, 0), (0, 0), (0, HEAD_PAD - QK_DIM))).reshape(Q_RANK, QK_WIDTH).astype(BF16)
    reps = GMLP_CHUNK // chunk_len
    ws = jnp.tile(w_spatial[:, :chunk_len, :chunk_len], (1, reps, reps))
    bst = jnp.tile(b_spatial[:, :chunk_len], (1, reps)).T
    return {
        "gmix": norm_mix[None, :], "w_in": w_in_p, "gql": q_lat_norm[None, :], "gkv": kv_lat_norm[None, :],
        "w_uq": wq, "w_uk": w_uk.astype(BF16), "w_uk_t": w_uk.T.astype(BF16),
        "w_uv": w_uv.astype(BF16), "w_uv_t": w_uv.T.astype(BF16),
        "gq": (_head_gain(q_norm_nope, q_norm_rope, QK_DIM ** -0.5 * math.log2(math.e))
               * _head_gain(k_norm_nope, k_norm_rope, 1.0)),
        "v_norm": v_norm, "ws": ws, "bst": bst, "gog": out_norm_gmlp[None, :],
        "goa": out_norm_attn[None, :], "gffn": norm_ffn[None, :],
    }


def _finish(an, gm, x2d, wp, tm_ffn):
    h, hn = _outproj(an, gm, x2d, wp, 512)
    return _ffn(h, hn, wp, tm_ffn, 1024)


def kernel(x_prompt, x_sample, cache_c_kv, cache_k_rope, norm_mix, w_in, q_lat_norm, kv_lat_norm, w_uq, w_uk, w_uv, q_norm_nope, q_norm_rope, k_norm_nope, k_norm_rope, v_norm, w_spatial, b_spatial, out_norm_attn, out_norm_gmlp, w_out, norm_ffn, w_up, w_down):
    depth = w_in.shape[0]
    assert depth == 1
    batch, seq, _ = x_prompt.shape
    dec_batch, dec_seq, _ = x_sample.shape
    past = cache_c_kv.shape[2]
    assert past % CHUNK == 0 and dec_seq <= CHUNK and GMLP_CHUNK % dec_seq == 0

    weights = (norm_mix[0], w_in[0], q_lat_norm[0], kv_lat_norm[0], w_uq[0], w_uk[0], w_uv[0],
               q_norm_nope[0], q_norm_rope[0], k_norm_nope[0], k_norm_rope[0], v_norm[0], w_spatial[0],
               b_spatial[0], out_norm_attn[0], out_norm_gmlp[0], norm_ffn[0])
    wp = _prep_weights(*weights, chunk_len=GMLP_CHUNK)
    reps = GMLP_CHUNK // dec_seq
    ws_s = jnp.tile(w_spatial[0][:, :dec_seq, :dec_seq], (1, reps, reps))
    bst_s = jnp.tile(b_spatial[0][:, :dec_seq], (1, reps)).T

    tm_p = 256
    xp = x_prompt.reshape(batch * seq, D_MODEL)
    tabs_p = _rope_tables(np.arange(seq))
    q, k, vt, ckv_p, kpe_p, gm, wo_b, wu_b, wd_b = _proj(
        xp, tabs_p, wp, GMLP_CHUNK, tm_p, f32_weights=(w_out[0], w_up[0], w_down[0]))
    wp = dict(wp, w_out=wo_b, w_up=wu_b, w_down=wd_b)
    wp_s = dict(wp, ws=ws_s, bst=bst_s)
    an = _attn_prompt(q, k, vt, wp["goa"], batch, seq, tm_p)
    y_p = _finish(an, gm, xp, wp, 1024)

    tm_s = 256
    xs = x_sample.reshape(dec_batch * dec_seq, D_MODEL)
    tabs_s = _rope_tables(past + np.arange(tm_s) % dec_seq)
    qs, kn, vn, ckv_s, kpe_s, gms, vg_s = _proj(xs, tabs_s, wp_s, dec_seq, tm_s)
    ans = _attn_sample(cache_c_kv[0].reshape(dec_batch * past, KV_RANK),
                       jnp.swapaxes(cache_k_rope[0], 1, 2),
                       qs, kn, vn, wp, dec_batch, dec_seq, past)
    y_s = _finish(ans, gms, xs, wp, 1024)

    return (y_p.reshape(batch, seq, D_MODEL),
            y_s.reshape(dec_batch, dec_seq, D_MODEL),
            ckv_p.reshape(1, batch, seq, KV_RANK),
            jnp.swapaxes(kpe_p, 1, 2)[None],
            ckv_s.reshape(1, dec_batch, dec_seq, KV_RANK),
            kpe_s.reshape(1, dec_batch, dec_seq, ROPE_DIM),
            vg_s.reshape(1, dec_batch, dec_seq, GMLP_WIDTH))
```

```python
import functools
import math

import jax
import jax.numpy as jnp
import numpy as np
from jax import lax
from jax.experimental import pallas as pl
from jax.experimental.pallas import tpu as pltpu

D_MODEL = 2048
CHUNK = 64
N_HEADS = 8
NOPE_DIM = 128
ROPE_DIM = 64
HALF_ROPE = ROPE_DIM // 2
QK_DIM = NOPE_DIM + ROPE_DIM
V_DIM = 128
ATTN_WIDTH = N_HEADS * V_DIM
Q_RANK = 512
KV_RANK = 256
GMLP_GROUPS = 8
GMLP_GROUP_DIM = 128
GMLP_WIDTH = GMLP_GROUPS * GMLP_GROUP_DIM
GMLP_CHUNK = 128
D_FF = 4 * D_MODEL
ROPE_THETA = 10000.0
EPS = 1e-6

LANES = 128
SUBLANES = 8
BF16_SUBLANES = 2 * SUBLANES
HEAD_PAD = 2 * LANES
QK_WIDTH = N_HEADS * HEAD_PAD
C_Q0, C_Q1 = 0, Q_RANK
C_KV0, C_KV1 = C_Q1, C_Q1 + KV_RANK
C_PE0, C_PE1 = C_KV1, C_KV1 + LANES
C_U0, C_U1 = C_PE1, C_PE1 + GMLP_WIDTH
C_V0, C_V1 = C_U1, C_U1 + GMLP_WIDTH
IN_PAD = C_V1
VMEM_LIMIT = 56 * 1024 * 1024

BF16 = jnp.bfloat16
F32 = jnp.float32


def _dot(a, b):
    return jnp.dot(a, b, preferred_element_type=F32)


def _dot_nt(a, b):
    return lax.dot_general(a, b, (((1,), (1,)), ((), ())), preferred_element_type=F32)


def _sumsq(x):
    return jnp.sum(x * x, axis=-1, keepdims=True)


def _rms(x):
    return x * lax.rsqrt(jnp.mean(x * x, axis=-1, keepdims=True) + EPS)


def _gelu(x):
    c = math.sqrt(2.0 / math.pi)
    return 0.5 * x * (1.0 + jnp.tanh(c * (x + 0.044715 * (x * x * x))))


def _rope(t, cos, s1, s2):
    return t * cos + pltpu.roll(t, LANES - HALF_ROPE, 1) * s1 + pltpu.roll(t, HALF_ROPE, 1) * s2


def _const_spec(shape):
    nd = len(shape)
    return pl.BlockSpec(shape, lambda *_: (0,) * nd, pipeline_mode=pl.Buffered(1))


def _proj_kernel(chunk_len, prompt, x_ref, gmix_ref, win_ref, gql_ref, gkv_ref, wuq_ref, wuk_ref,
                 wuv_ref, gq_ref, cos_ref, s1_ref, s2_ref, vn_ref, ws_ref, bst_ref, gog_ref, *rest):
    v_transposed = prompt
    emit_v = not prompt
    if prompt:
        (wo_ref, wu_ref, wd_ref, q_ref, k_ref, v_ref, ckv_ref, kpe_ref, gm_ref,
         wo_out, wu_out, wd_out, gate_scr, vgb_scr) = rest
        wo_out[...] = wo_ref[...].astype(BF16)
        wu_out[...] = wu_ref[...].astype(BF16)
        wd_out[...] = wd_ref[...].astype(BF16)
    else:
        q_ref, k_ref, v_ref, ckv_ref, kpe_ref, gm_ref, vg_ref, gate_scr, vgb_scr = rest
    tm = x_ref.shape[0]
    xn = (_rms(x_ref[...]) * gmix_ref[...]).astype(BF16)
    cos, s1, s2 = cos_ref[...], s1_ref[...], s2_ref[...]

    zq = _dot(xn, win_ref[:, C_Q0:C_Q1])
    qln = (_rms(zq) * gql_ref[...]).astype(BF16)
    qraw = _dot(qln, wuq_ref[...])
    for h in range(N_HEADS):
        lo = h * HEAD_PAD
        nope = qraw[:, lo:lo + LANES]
        rp = qraw[:, lo + LANES:lo + HEAD_PAD]
        r = lax.rsqrt((_sumsq(nope) + _sumsq(rp)) * (1.0 / QK_DIM) + EPS)
        rp = _rope(rp, cos, s1, s2)
        q_ref[:, lo:lo + LANES] = (nope * r * gq_ref[:, :LANES]).astype(BF16)
        q_ref[:, lo + LANES:lo + HEAD_PAD] = (rp * r * gq_ref[:, LANES:]).astype(BF16)

    ckv = _rms(_dot(xn, win_ref[:, C_KV0:C_KV1])) * gkv_ref[...]
    ckv_ref[...] = ckv
    cb = ckv.astype(BF16)
    pe = _rope(_dot(xn, win_ref[:, C_PE0:C_PE1]), cos, s1, s2)
    kpe_ref[...] = pe.T[:ROPE_DIM, :] if prompt else pe[:, :ROPE_DIM]
    ss_pe = _sumsq(pe)
    knope = _dot(cb, wuk_ref[...])
    if v_transposed:
        v_ref[...] = _dot_nt(wuv_ref[...], cb).astype(BF16)
    else:
        v_ref[...] = _dot(cb, wuv_ref[...]).astype(BF16)
    for h in range(N_HEADS):
        nope = knope[:, h * LANES:(h + 1) * LANES]
        r = lax.rsqrt((_sumsq(nope) + ss_pe) * (1.0 / QK_DIM) + EPS)
        lo = h * HEAD_PAD
        k_ref[:, lo:lo + LANES] = (nope * r).astype(BF16)
        k_ref[:, lo + LANES:lo + HEAD_PAD] = (pe * r).astype(BF16)

    gv = _gelu(_dot(xn, win_ref[:, C_V0:C_V1]))
    for g in range(GMLP_GROUPS):
        blk = gv[:, g * LANES:(g + 1) * LANES]
        vg = _rms(blk) * vn_ref[g:g + 1, :]
        if emit_v:
            vg_ref[:, g * LANES:(g + 1) * LANES] = vg
        vgb_scr[:, g * LANES:(g + 1) * LANES] = vg.astype(BF16)
    u = _gelu(_dot(xn, win_ref[:, C_U0:C_U1]))
    row = lax.broadcasted_iota(jnp.int32, (GMLP_CHUNK, GMLP_CHUNK), 0)
    col = lax.broadcasted_iota(jnp.int32, (GMLP_CHUNK, GMLP_CHUNK), 1)
    causal = (row // chunk_len == col // chunk_len) & (col <= row)
    for g in range(GMLP_GROUPS):
        wm = jnp.where(causal, ws_ref[g], 0.0).astype(BF16)
        bias = bst_ref[:, g:g + 1]
        for c in range(tm // GMLP_CHUNK):
            rows = slice(c * GMLP_CHUNK, (c + 1) * GMLP_CHUNK)
            cols = slice(g * LANES, (g + 1) * LANES)
            s = _dot(wm, vgb_scr[rows, cols]) + bias
            gate_scr[rows, cols] = u[rows, cols] * s
    gm_ref[...] = (_rms(gate_scr[...]) * gog_ref[...]).astype(BF16)


def _proj(x2d, tabs, wp, chunk_len, tm, f32_weights=None):
    prompt = f32_weights is not None
    m = x2d.shape[0]
    steps = m // tm
    cos, s1, s2 = tabs
    tab_blocks = cos.shape[0] // tm
    row_spec = lambda w: pl.BlockSpec((tm, w), lambda i: (i, 0))
    tab_spec = pl.BlockSpec((tm, LANES), lambda i: (i % tab_blocks, 0))
    consts = [wp["gmix"], wp["w_in"], wp["gql"], wp["gkv"], wp["w_uq"], wp["w_uk"],
              wp["w_uv_t"] if prompt else wp["w_uv"], wp["gq"]]
    consts2 = [wp["v_norm"], wp["ws"], wp["bst"], wp["gog"]]
    in_specs = ([row_spec(D_MODEL)] + [_const_spec(c.shape) for c in consts]
                + [tab_spec] * 3 + [_const_spec(c.shape) for c in consts2])
    v_shape = (steps * ATTN_WIDTH, tm) if prompt else (m, ATTN_WIDTH)
    v_spec = pl.BlockSpec((ATTN_WIDTH, tm), lambda i: (i, 0)) if prompt else row_spec(ATTN_WIDTH)
    kpe_shape = (m // cos.shape[0], ROPE_DIM, cos.shape[0]) if prompt else (m, ROPE_DIM)
    kpe_spec = (pl.BlockSpec((None, ROPE_DIM, tm), lambda i: (i // tab_blocks, 0, i % tab_blocks))
                if prompt else row_spec(ROPE_DIM))
    out_shape = [jax.ShapeDtypeStruct((m, QK_WIDTH), BF16), jax.ShapeDtypeStruct((m, QK_WIDTH), BF16),
                 jax.ShapeDtypeStruct(v_shape, BF16), jax.ShapeDtypeStruct((m, KV_RANK), F32),
                 jax.ShapeDtypeStruct(kpe_shape, F32), jax.ShapeDtypeStruct((m, GMLP_WIDTH), BF16)]
    out_specs = [row_spec(QK_WIDTH), row_spec(QK_WIDTH), v_spec, row_spec(KV_RANK),
                 kpe_spec, row_spec(GMLP_WIDTH)]
    extra_in = []
    if prompt:
        w_out, w_up, w_down = f32_weights
        slabs = [pl.BlockSpec((D_MODEL // steps, D_MODEL), lambda i: (i, 0)),
                 pl.BlockSpec((D_MODEL, D_FF // steps), lambda i: (0, i)),
                 pl.BlockSpec((D_FF // steps, D_MODEL), lambda i: (i, 0))]
        extra_in = [w_out, w_up, w_down]
        in_specs += slabs
        out_shape += [jax.ShapeDtypeStruct(w.shape, BF16) for w in extra_in]
        out_specs += slabs
    else:
        out_shape.append(jax.ShapeDtypeStruct((m, GMLP_WIDTH), F32))
        out_specs.append(row_spec(GMLP_WIDTH))
    return pl.pallas_call(
        functools.partial(_proj_kernel, chunk_len, prompt),
        out_shape=out_shape,
        grid=(steps,),
        in_specs=in_specs,
        out_specs=out_specs,
        scratch_shapes=[pltpu.VMEM((tm, GMLP_WIDTH), F32), pltpu.VMEM((tm, GMLP_WIDTH), BF16)],
        compiler_params=pltpu.CompilerParams(dimension_semantics=("parallel",),
                                             vmem_limit_bytes=VMEM_LIMIT),
        name="proj",
    )(x2d, *consts, cos, s1, s2, *consts2, *extra_in)


NEG = -1e30
SOFTMAX_KEY_CHUNK = 64
KEY_BLOCKS_PER_STEP = 2
SCORE_LEAD = 4


def _attn_prompt_kernel(q_ref, k_ref, vt_ref, go_ref, o_ref, acc_scr, m_scr, l_scr, s_scr, p_scr):
    t = q_ref.shape[0]
    qi = pl.program_id(1)
    m_scr[...] = jnp.full(m_scr.shape, NEG, F32)
    l_scr[...] = jnp.zeros(l_scr.shape, F32)
    acc_scr[...] = jnp.zeros(acc_scr.shape, F32)

    def step(kb0, nblk, last_masked):
        nk = nblk * t
        if last_masked:
            key = lax.broadcasted_iota(jnp.int32, (t, t), 0)
            qry = lax.broadcasted_iota(jnp.int32, (t, t), 1)
            ok = (key // CHUNK) <= (qry // CHUNK)

        m8 = {}

        def scores(h):
            qs = slice(h * HEAD_PAD, (h + 1) * HEAD_PAD)
            mh = None
            for j in range(nblk):
                rows = pl.ds(pl.multiple_of((kb0 + j) * t, t), t)
                s = _dot_nt(k_ref[rows, qs], q_ref[:, qs])
                if last_masked and j == nblk - 1:
                    s = jnp.where(ok, s, NEG)
                s_scr[h, j * t:(j + 1) * t, :] = s
                mj = jnp.max(s.reshape(t // SUBLANES, SUBLANES, t), axis=0)
                mh = mj if mh is None else jnp.maximum(mh, mj)
            m8[h] = mh

        ones = jnp.ones((BF16_SUBLANES, nk), BF16)
        for h in range(min(SCORE_LEAD, N_HEADS)):
            scores(h)
        for h in range(N_HEADS):
            if h + SCORE_LEAD < N_HEADS:
                scores(h + SCORE_LEAD)
            hs = slice(h * V_DIM, (h + 1) * V_DIM)
            m_old = m_scr[h]
            m_new = jnp.maximum(m_old, jnp.max(m8[h], axis=0, keepdims=True))
            alpha = jnp.exp2(m_old - m_new)
            for c in range(nk // SOFTMAX_KEY_CHUNK):
                cs = slice(c * SOFTMAX_KEY_CHUNK, (c + 1) * SOFTMAX_KEY_CHUNK)
                p_scr[h, cs, :] = jnp.exp2(s_scr[h, cs, :] - m_new).astype(BF16)
            m_scr[h] = m_new
            vt = [vt_ref[pl.ds(pl.multiple_of((kb0 + j) * ATTN_WIDTH, ATTN_WIDTH) + h * V_DIM, V_DIM), :]
                  for j in range(nblk)]
            vt = vt[0] if nblk == 1 else jnp.concatenate(vt, axis=1)
            pv = _dot(jnp.concatenate([vt, ones], axis=0), p_scr[h, :nk, :])
            acc_scr[hs, :] = alpha * acc_scr[hs, :] + pv[:V_DIM]
            l_scr[h] = alpha * l_scr[h] + pv[V_DIM:V_DIM + 1]

    def body(j, carry):
        step(j * KEY_BLOCKS_PER_STEP, KEY_BLOCKS_PER_STEP, False)
        return carry

    lax.fori_loop(0, qi // KEY_BLOCKS_PER_STEP, body, 0)
    for rem in range(KEY_BLOCKS_PER_STEP):
        @pl.when(qi % KEY_BLOCKS_PER_STEP == rem)
        def _(rem=rem):
            step(qi - rem, rem + 1, True)

    for h in range(N_HEADS):
        hs = slice(h * V_DIM, (h + 1) * V_DIM)
        acc_scr[hs, :] = acc_scr[hs, :] / l_scr[h]
    o = acc_scr[...].T
    o_ref[...] = (_rms(o) * go_ref[...]).astype(BF16)


def _attn_prompt(q, k, vt, go, batch, seq, t):
    nq = seq // t
    assert vt.shape == (batch * nq * ATTN_WIDTH, t)
    tk = KEY_BLOCKS_PER_STEP * t
    return pl.pallas_call(
        _attn_prompt_kernel,
        out_shape=jax.ShapeDtypeStruct((batch * seq, ATTN_WIDTH), BF16),
        grid=(batch, nq),
        in_specs=[pl.BlockSpec((t, QK_WIDTH), lambda b, i: (b * nq + i, 0)),
                  pl.BlockSpec((seq, QK_WIDTH), lambda b, i: (b, 0)),
                  pl.BlockSpec((nq * ATTN_WIDTH, t), lambda b, i: (b, 0)),
                  _const_spec(go.shape)],
        out_specs=pl.BlockSpec((t, ATTN_WIDTH), lambda b, i: (b * nq + i, 0)),
        scratch_shapes=[pltpu.VMEM((ATTN_WIDTH, t), F32), pltpu.VMEM((N_HEADS, 1, t), F32),
                        pltpu.VMEM((N_HEADS, 1, t), F32), pltpu.VMEM((N_HEADS, tk, t), F32),
                        pltpu.VMEM((N_HEADS, tk, t), BF16)],
        compiler_params=pltpu.CompilerParams(dimension_semantics=("parallel", "arbitrary"),
                                             vmem_limit_bytes=VMEM_LIMIT),
        name="attn_prompt",
    )(q, k, vt, go)


def _attn_sample_kernel(past, c_ref, krt_ref, q_ref, kn_ref, vn_ref, wukt_ref, wuv_ref, go_ref,
                        o_ref, cb_scr, qa_scr, qr_scr, r_scr, sc_scr, sn_scr, pc_scr, acc_scr):
    nb = krt_ref.shape[0]
    t = q_ref.shape[0] // nb
    row = lax.broadcasted_iota(jnp.int32, (t, t), 0)
    col = lax.broadcasted_iota(jnp.int32, (t, t), 1)
    new_ok = ((past + col) // CHUNK) <= ((past + row) // CHUNK)
    denom = [[None] * N_HEADS for _ in range(nb)]
    p_new = [[None] * N_HEADS for _ in range(nb)]

    def norm_factors(j):
        cb_scr[j] = c_ref[j * past:(j + 1) * past, :].astype(BF16)
        krt = krt_ref[j]
        ss_pe = jnp.sum(krt * krt, axis=0, keepdims=True)
        knt = _dot_nt(wukt_ref[...], cb_scr[j])
        for h in range(N_HEADS):
            nope = knt[h * LANES:(h + 1) * LANES, :]
            ss = jnp.sum(nope * nope, axis=0, keepdims=True) + ss_pe
            r_scr[j, h] = lax.rsqrt(ss * (1.0 / QK_DIM) + EPS)

    def scores(j):
        qj = q_ref.at[j * t:(j + 1) * t, :]
        knj = kn_ref.at[j * t:(j + 1) * t, :]
        for h in range(N_HEADS):
            lo = h * HEAD_PAD
            rows = slice(h * t, (h + 1) * t)
            qa_scr[j, rows, :] = _dot(qj[:, lo:lo + NOPE_DIM],
                                      wukt_ref[h * NOPE_DIM:(h + 1) * NOPE_DIM, :]).astype(BF16)
            qr_scr[j, rows, :] = qj[:, lo + NOPE_DIM:lo + QK_DIM]
            sn_scr[j, h] = jnp.where(new_ok, _dot_nt(qj[:, lo:lo + HEAD_PAD], knj[:, lo:lo + HEAD_PAD]), NEG)
        sc_scr[j] = _dot_nt(qa_scr[j], cb_scr[j]) + _dot(qr_scr[j], krt_ref[j].astype(BF16))

    def softmax(j):
        for h in range(N_HEADS):
            s_c, s_n = sc_scr[j, h * t:(h + 1) * t, :] * r_scr[j, h], sn_scr[j, h]
            m = jnp.maximum(jnp.max(s_c, axis=-1, keepdims=True), jnp.max(s_n, axis=-1, keepdims=True))
            p_c = jnp.exp2(s_c - m)
            p_n = jnp.exp2(s_n - m)
            denom[j][h] = jnp.sum(p_c, axis=-1, keepdims=True) + jnp.sum(p_n, axis=-1, keepdims=True)
            p_new[j][h] = p_n.astype(BF16)
            pc_scr[j, h * t:(h + 1) * t, :] = p_c.astype(BF16)

    def values(j):
        vnj = vn_ref.at[j * t:(j + 1) * t, :]
        lat = _dot(pc_scr[j], cb_scr[j]).astype(BF16)
        for h in range(N_HEADS):
            vs = slice(h * V_DIM, (h + 1) * V_DIM)
            acc = _dot(lat[h * t:(h + 1) * t, :], wuv_ref[:, vs]) + _dot(p_new[j][h], vnj[:, vs])
            acc_scr[j, :, vs] = acc / denom[j][h]
        o_ref[j * t:(j + 1) * t, :] = (_rms(acc_scr[j]) * go_ref[...]).astype(BF16)

    stages = (norm_factors, scores, softmax, values)
    for tick in range(nb + len(stages) - 1):
        for j in range(nb):
            if 0 <= tick - j < len(stages):
                stages[tick - j](j)


def _attn_sample(c2d, krt, q, kn, vn, wp, batch, t, past, nb):
    consts = [wp["w_uk_t"], wp["w_uv"], wp["goa"]]
    ht = N_HEADS * t
    return pl.pallas_call(
        functools.partial(_attn_sample_kernel, past),
        out_shape=jax.ShapeDtypeStruct((batch * t, ATTN_WIDTH), BF16),
        grid=(batch // nb,),
        in_specs=[pl.BlockSpec((nb * past, KV_RANK), lambda b: (b, 0)),
                  pl.BlockSpec((nb, ROPE_DIM, past), lambda b: (b, 0, 0)),
                  pl.BlockSpec((nb * t, QK_WIDTH), lambda b: (b, 0)),
                  pl.BlockSpec((nb * t, QK_WIDTH), lambda b: (b, 0)),
                  pl.BlockSpec((nb * t, ATTN_WIDTH), lambda b: (b, 0))]
                 + [_const_spec(c.shape) for c in consts],
        out_specs=pl.BlockSpec((nb * t, ATTN_WIDTH), lambda b: (b, 0)),
        scratch_shapes=[pltpu.VMEM((nb, past, KV_RANK), BF16), pltpu.VMEM((nb, ht, KV_RANK), BF16),
                        pltpu.VMEM((nb, ht, ROPE_DIM), BF16), pltpu.VMEM((nb, N_HEADS, 1, past), F32),
                        pltpu.VMEM((nb, ht, past), F32), pltpu.VMEM((nb, N_HEADS, t, t), F32),
                        pltpu.VMEM((nb, ht, past), BF16), pltpu.VMEM((nb, t, ATTN_WIDTH), F32)],
        compiler_params=pltpu.CompilerParams(dimension_semantics=("parallel",),
                                             vmem_limit_bytes=VMEM_LIMIT),
        name="attn_sample",
    )(c2d, krt, q, kn, vn, *consts)


def _outproj_kernel(an_ref, gm_ref, x_ref, wo_ref, gffn_ref, h_ref, hn_ref):
    h = (x_ref[...] + _dot(an_ref[...], wo_ref[:ATTN_WIDTH, :])
         + _dot(gm_ref[...], wo_ref[ATTN_WIDTH:, :]))
    h_ref[...] = h
    hn_ref[...] = (_rms(h) * gffn_ref[...]).astype(BF16)


def _outproj(an, gm, x2d, wp, tm):
    m = x2d.shape[0]
    row_spec = lambda w: pl.BlockSpec((tm, w), lambda i: (i, 0))
    return pl.pallas_call(
        _outproj_kernel,
        out_shape=[jax.ShapeDtypeStruct((m, D_MODEL), F32), jax.ShapeDtypeStruct((m, D_MODEL), BF16)],
        grid=(m // tm,),
        in_specs=[row_spec(ATTN_WIDTH), row_spec(GMLP_WIDTH), row_spec(D_MODEL),
                  _const_spec(wp["w_out"].shape), _const_spec(wp["gffn"].shape)],
        out_specs=[row_spec(D_MODEL), row_spec(D_MODEL)],
        compiler_params=pltpu.CompilerParams(dimension_semantics=("parallel",),
                                             vmem_limit_bytes=VMEM_LIMIT),
        name="outproj",
    )(an, gm, x2d, wp["w_out"], wp["gffn"])


def _ffn_kernel(h_ref, hn_ref, wu_ref, wd_ref, y_ref):
    f = pl.program_id(1)
    slab = h_ref.shape[1]

    @pl.when(f == 0)
    def _():
        y_ref[...] = jnp.zeros(y_ref.shape, F32)

    for j in range(D_MODEL // slab):
        @pl.when(f == j)
        def _(j=j):
            y_ref[:, j * slab:(j + 1) * slab] += h_ref[...]

    a = jnp.maximum(_dot(hn_ref[...], wu_ref[...]), 0.0)
    y_ref[...] += _dot((a * a).astype(BF16), wd_ref[...])


def _ffn(h, hn, wp, tm, tf):
    m = h.shape[0]
    nf = D_FF // tf
    return pl.pallas_call(
        _ffn_kernel,
        out_shape=jax.ShapeDtypeStruct((m, D_MODEL), F32),
        grid=(m // tm, nf),
        in_specs=[pl.BlockSpec((tm, D_MODEL // nf), lambda i, f: (i, f)),
                  pl.BlockSpec((tm, D_MODEL), lambda i, f: (i, 0)),
                  pl.BlockSpec((D_MODEL, tf), lambda i, f: (0, f)),
                  pl.BlockSpec((tf, D_MODEL), lambda i, f: (f, 0))],
        out_specs=pl.BlockSpec((tm, D_MODEL), lambda i, f: (i, 0)),
        compiler_params=pltpu.CompilerParams(dimension_semantics=("parallel", "arbitrary"),
                                             vmem_limit_bytes=VMEM_LIMIT),
        name="ffn",
    )(h, hn, wp["w_up"], wp["w_down"])


def _rope_tables(pos):
    inv = ROPE_THETA ** (-np.arange(HALF_ROPE, dtype=np.float64) / HALF_ROPE)
    ang = np.asarray(pos, np.float64)[:, None] * inv[None, :]
    cos, sin = np.cos(ang), np.sin(ang)
    z = np.zeros_like(cos)
    z2 = np.zeros((ang.shape[0], LANES - ROPE_DIM))
    tabs = (np.concatenate([cos, cos, z2], axis=1), np.concatenate([-sin, z, z2], axis=1),
            np.concatenate([z, sin, z2], axis=1))
    return tuple(jnp.asarray(t, F32) for t in tabs)


def _head_gain(g_nope, g_rope, scale):
    pad = jnp.zeros((HEAD_PAD - QK_DIM,), F32)
    return (jnp.concatenate([g_nope, g_rope, g_rope, pad]) * scale)[None, :]


def _pad_w_in_kernel(wt_ref, o_ref):
    tr = wt_ref.shape[1]
    split = C_PE0 + ROPE_DIM
    o_ref[:, :C_PE0] = wt_ref[:C_PE0, :].T.astype(BF16)
    pe = wt_ref[C_PE0:C_PE1, :].T
    lane = lax.broadcasted_iota(jnp.int32, (tr, LANES), 1)
    o_ref[:, C_PE0:C_PE1] = jnp.where(lane < ROPE_DIM, pe, 0.0).astype(BF16)
    o_ref[:, C_PE1:] = wt_ref[split:, :].T.astype(BF16)


def _pad_w_in(w_in_t, tr):
    width, rows = w_in_t.shape
    assert width + LANES - ROPE_DIM == IN_PAD
    return pl.pallas_call(
        _pad_w_in_kernel,
        out_shape=jax.ShapeDtypeStruct((rows, IN_PAD), BF16),
        grid=(rows // tr,),
        in_specs=[pl.BlockSpec((width, tr), lambda i: (0, i))],
        out_specs=pl.BlockSpec((tr, IN_PAD), lambda i: (i, 0)),
        compiler_params=pltpu.CompilerParams(dimension_semantics=("parallel",)),
        name="pad_w_in",
    )(w_in_t)


def _prep_weights(norm_mix, w_in, q_lat_norm, kv_lat_norm, w_uq, w_uk, w_uv, q_norm_nope, q_norm_rope,
                  k_norm_nope, k_norm_rope, v_norm, w_spatial, b_spatial, out_norm_attn, out_norm_gmlp,
                  norm_ffn, chunk_len):
    w_in_p = _pad_w_in(w_in.T, 256)
    wq = w_uq.reshape(Q_RANK, N_HEADS, QK_DIM)
    wq = jnp.pad(wq, ((0, 0), (0, 0), (0, HEAD_PAD - QK_DIM))).reshape(Q_RANK, QK_WIDTH).astype(BF16)
    reps = GMLP_CHUNK // chunk_len
    ws = jnp.tile(w_spatial[:, :chunk_len, :chunk_len], (1, reps, reps))
    bst = jnp.tile(b_spatial[:, :chunk_len], (1, reps)).T
    return {
        "gmix": norm_mix[None, :], "w_in": w_in_p, "gql": q_lat_norm[None, :], "gkv": kv_lat_norm[None, :],
        "w_uq": wq, "w_uk": w_uk.astype(BF16), "w_uk_t": w_uk.T.astype(BF16),
        "w_uv": w_uv.astype(BF16), "w_uv_t": w_uv.T.astype(BF16),
        "gq": (_head_gain(q_norm_nope, q_norm_rope, QK_DIM ** -0.5 * math.log2(math.e))
               * _head_gain(k_norm_nope, k_norm_rope, 1.0)),
        "v_norm": v_norm, "ws": ws, "bst": bst, "gog": out_norm_gmlp[None, :],
        "goa": out_norm_attn[None, :], "gffn": norm_ffn[None, :],
    }


def _finish(an, gm, x2d, wp, tm_ffn):
    h, hn = _outproj(an, gm, x2d, wp, 512)
    return _ffn(h, hn, wp, tm_ffn, 1024)


def kernel(x_prompt, x_sample, cache_c_kv, cache_k_rope, norm_mix, w_in, q_lat_norm, kv_lat_norm, w_uq, w_uk, w_uv, q_norm_nope, q_norm_rope, k_norm_nope, k_norm_rope, v_norm, w_spatial, b_spatial, out_norm_attn, out_norm_gmlp, w_out, norm_ffn, w_up, w_down):
    depth = w_in.shape[0]
    assert depth == 1
    batch, seq, _ = x_prompt.shape
    dec_batch, dec_seq, _ = x_sample.shape
    past = cache_c_kv.shape[2]
    assert past % CHUNK == 0 and dec_seq <= CHUNK and GMLP_CHUNK % dec_seq == 0

    weights = (norm_mix[0], w_in[0], q_lat_norm[0], kv_lat_norm[0], w_uq[0], w_uk[0], w_uv[0],
               q_norm_nope[0], q_norm_rope[0], k_norm_nope[0], k_norm_rope[0], v_norm[0], w_spatial[0],
               b_spatial[0], out_norm_attn[0], out_norm_gmlp[0], norm_ffn[0])
    wp = _prep_weights(*weights, chunk_len=GMLP_CHUNK)
    reps = GMLP_CHUNK // dec_seq
    ws_s = jnp.tile(w_spatial[0][:, :dec_seq, :dec_seq], (1, reps, reps))
    bst_s = jnp.tile(b_spatial[0][:, :dec_seq], (1, reps)).T

    tm_p = 256
    xp = x_prompt.reshape(batch * seq, D_MODEL)
    tabs_p = _rope_tables(np.arange(seq))
    q, k, vt, ckv_p, kpe_p, gm, wo_b, wu_b, wd_b = _proj(
        xp, tabs_p, wp, GMLP_CHUNK, tm_p, f32_weights=(w_out[0], w_up[0], w_down[0]))
    wp = dict(wp, w_out=wo_b, w_up=wu_b, w_down=wd_b)
    wp_s = dict(wp, ws=ws_s, bst=bst_s)
    an = _attn_prompt(q, k, vt, wp["goa"], batch, seq, tm_p)
    y_p = _finish(an, gm, xp, wp, 1024)

    tm_s = 256
    xs = x_sample.reshape(dec_batch * dec_seq, D_MODEL)
    tabs_s = _rope_tables(past + np.arange(tm_s) % dec_seq)
    qs, kn, vn, ckv_s, kpe_s, gms, vg_s = _proj(xs, tabs_s, wp_s, dec_seq, tm_s)
    ans = _attn_sample(cache_c_kv[0].reshape(dec_batch * past, KV_RANK),
                       jnp.swapaxes(cache_k_rope[0], 1, 2),
                       qs, kn, vn, wp, dec_batch, dec_seq, past, nb=4)
    y_s = _finish(ans, gms, xs, wp, 1024)

    return (y_p.reshape(batch, seq, D_MODEL),
            y_s.reshape(dec_batch, dec_seq, D_MODEL),
            ckv_p.reshape(1, batch, seq, KV_RANK),
            jnp.swapaxes(kpe_p, 1, 2)[None],
            ckv_s.reshape(1, dec_batch, dec_seq, KV_RANK),
            kpe_s.reshape(1, dec_batch, dec_seq, ROPE_DIM),
            vg_s.reshape(1, dec_batch, dec_seq, GMLP_WIDTH))
```

```python
import functools
import math

import jax
import jax.numpy as jnp
import numpy as np
from jax import lax
from jax.experimental import pallas as pl
from jax.experimental.pallas import tpu as pltpu

D_MODEL = 2048
CHUNK = 64
N_HEADS = 8
NOPE_DIM = 128
ROPE_DIM = 64
HALF_ROPE = ROPE_DIM // 2
QK_DIM = NOPE_DIM + ROPE_DIM
V_DIM = 128
ATTN_WIDTH = N_HEADS * V_DIM
Q_RANK = 512
KV_RANK = 256
GMLP_GROUPS = 8
GMLP_GROUP_DIM = 128
GMLP_WIDTH = GMLP_GROUPS * GMLP_GROUP_DIM
GMLP_CHUNK = 128
D_FF = 4 * D_MODEL
ROPE_THETA = 10000.0
EPS = 1e-6

LANES = 128
SUBLANES = 8
BF16_SUBLANES = 2 * SUBLANES
HEAD_PAD = 2 * LANES
QK_WIDTH = N_HEADS * HEAD_PAD
C_Q0, C_Q1 = 0, Q_RANK
C_KV0, C_KV1 = C_Q1, C_Q1 + KV_RANK
C_PE0, C_PE1 = C_KV1, C_KV1 + LANES
C_U0, C_U1 = C_PE1, C_PE1 + GMLP_WIDTH
C_V0, C_V1 = C_U1, C_U1 + GMLP_WIDTH
IN_PAD = C_V1
VMEM_LIMIT = 56 * 1024 * 1024

BF16 = jnp.bfloat16
F32 = jnp.float32


def _dot(a, b):
    return jnp.dot(a, b, preferred_element_type=F32)


def _dot_nt(a, b):
    return lax.dot_general(a, b, (((1,), (1,)), ((), ())), preferred_element_type=F32)


def _sumsq(x):
    return jnp.sum(x * x, axis=-1, keepdims=True)


def _rms(x):
    return x * lax.rsqrt(jnp.mean(x * x, axis=-1, keepdims=True) + EPS)


def _gelu(x):
    c = math.sqrt(2.0 / math.pi)
    return 0.5 * x * (1.0 + jnp.tanh(c * (x + 0.044715 * (x * x * x))))


def _rope(t, cos, s1, s2):
    return t * cos + pltpu.roll(t, LANES - HALF_ROPE, 1) * s1 + pltpu.roll(t, HALF_ROPE, 1) * s2


def _const_spec(shape):
    nd = len(shape)
    return pl.BlockSpec(shape, lambda *_: (0,) * nd, pipeline_mode=pl.Buffered(1))


def _proj_kernel(chunk_len, prompt, tm, x_ref, gmix_ref, win_ref, gql_ref, gkv_ref, wuq_ref, wuk_ref,
                 wuv_ref, gq_ref, cos_ref, s1_ref, s2_ref, vn_ref, ws_ref, bst_ref, gog_ref, *rest):
    if prompt:
        (wo_ref, q_ref, k_ref, v_ref, ckv_ref, kpe_ref, gm_ref,
         wo_out, xn_scr, gate_scr, vgb_scr) = rest
        wo_out[...] = wo_ref[...].astype(BF16)
    else:
        q_ref, k_ref, v_ref, ckv_ref, kpe_ref, gm_ref, vg_ref, xn_scr, gate_scr, vgb_scr = rest
    nt = x_ref.shape[0] // tm

    def input_norm(j):
        rows = slice(j * tm, (j + 1) * tm)
        xn_scr[j] = (_rms(x_ref[rows, :]) * gmix_ref[...]).astype(BF16)

    def mla(j):
        rows = slice(j * tm, (j + 1) * tm)
        cos, s1, s2 = cos_ref[rows, :], s1_ref[rows, :], s2_ref[rows, :]
        zq = _dot(xn_scr[j], win_ref[:, C_Q0:C_Q1])
        qln = (_rms(zq) * gql_ref[...]).astype(BF16)
        qraw = _dot(qln, wuq_ref[...])
        for h in range(N_HEADS):
            lo = h * HEAD_PAD
            nope = qraw[:, lo:lo + LANES]
            rp = qraw[:, lo + LANES:lo + HEAD_PAD]
            r = lax.rsqrt((_sumsq(nope) + _sumsq(rp)) * (1.0 / QK_DIM) + EPS)
            rp = _rope(rp, cos, s1, s2)
            q_ref[rows, lo:lo + LANES] = (nope * r * gq_ref[:, :LANES]).astype(BF16)
            q_ref[rows, lo + LANES:lo + HEAD_PAD] = (rp * r * gq_ref[:, LANES:]).astype(BF16)
        ckv = _rms(_dot(xn_scr[j], win_ref[:, C_KV0:C_KV1])) * gkv_ref[...]
        ckv_ref[rows, :] = ckv
        cb = ckv.astype(BF16)
        pe = _rope(_dot(xn_scr[j], win_ref[:, C_PE0:C_PE1]), cos, s1, s2)
        if prompt:
            kpe_ref[:, rows] = pe.T[:ROPE_DIM, :]
            v_ref[j * ATTN_WIDTH:(j + 1) * ATTN_WIDTH, :] = _dot_nt(wuv_ref[...], cb).astype(BF16)
        else:
            kpe_ref[rows, :] = pe[:, :ROPE_DIM]
            v_ref[rows, :] = _dot(cb, wuv_ref[...]).astype(BF16)
        ss_pe = _sumsq(pe)
        knope = _dot(cb, wuk_ref[...])
        for h in range(N_HEADS):
            nope = knope[:, h * LANES:(h + 1) * LANES]
            r = lax.rsqrt((_sumsq(nope) + ss_pe) * (1.0 / QK_DIM) + EPS)
            lo = h * HEAD_PAD
            k_ref[rows, lo:lo + LANES] = (nope * r).astype(BF16)
            k_ref[rows, lo + LANES:lo + HEAD_PAD] = (pe * r).astype(BF16)

    def gmlp(j):
        rows = slice(j * tm, (j + 1) * tm)
        gv = _gelu(_dot(xn_scr[j], win_ref[:, C_V0:C_V1]))
        for g in range(GMLP_GROUPS):
            blk = gv[:, g * LANES:(g + 1) * LANES]
            vg = _rms(blk) * vn_ref[g:g + 1, :]
            if not prompt:
                vg_ref[rows, g * LANES:(g + 1) * LANES] = vg
            vgb_scr[j, :, g * LANES:(g + 1) * LANES] = vg.astype(BF16)
        u = _gelu(_dot(xn_scr[j], win_ref[:, C_U0:C_U1]))
        row = lax.broadcasted_iota(jnp.int32, (GMLP_CHUNK, GMLP_CHUNK), 0)
        col = lax.broadcasted_iota(jnp.int32, (GMLP_CHUNK, GMLP_CHUNK), 1)
        causal = (row // chunk_len == col // chunk_len) & (col <= row)
        for g in range(GMLP_GROUPS):
            wm = jnp.where(causal, ws_ref[g], 0.0).astype(BF16)
            bias = bst_ref[:, g:g + 1]
            for c in range(tm // GMLP_CHUNK):
                pos = slice(c * GMLP_CHUNK, (c + 1) * GMLP_CHUNK)
                cols = slice(g * LANES, (g + 1) * LANES)
                s = _dot(wm, vgb_scr[j, pos, cols]) + bias
                gate_scr[j, pos, cols] = u[pos, cols] * s
        gm_ref[rows, :] = (_rms(gate_scr[j]) * gog_ref[...]).astype(BF16)

    stages = (input_norm, mla, gmlp)
    for tick in range(nt + len(stages) - 1):
        for j in range(nt):
            if 0 <= tick - j < len(stages):
                stages[tick - j](j)


def _proj(x2d, tabs, wp, chunk_len, tm, nt, f32_weights=None):
    prompt = f32_weights is not None
    m = x2d.shape[0]
    tr = nt * tm
    steps = m // tr
    cos, s1, s2 = tabs
    tab_blocks = cos.shape[0] // tr
    row_spec = lambda w: pl.BlockSpec((tr, w), lambda i: (i, 0))
    tab_spec = pl.BlockSpec((tr, LANES), lambda i: (i % tab_blocks, 0))
    consts = [wp["gmix"], wp["w_in"], wp["gql"], wp["gkv"], wp["w_uq"], wp["w_uk"],
              wp["w_uv_t"] if prompt else wp["w_uv"], wp["gq"]]
    consts2 = [wp["v_norm"], wp["ws"], wp["bst"], wp["gog"]]
    in_specs = ([row_spec(D_MODEL)] + [_const_spec(c.shape) for c in consts]
                + [tab_spec] * 3 + [_const_spec(c.shape) for c in consts2])
    v_shape = (m // tm * ATTN_WIDTH, tm) if prompt else (m, ATTN_WIDTH)
    v_spec = pl.BlockSpec((nt * ATTN_WIDTH, tm), lambda i: (i, 0)) if prompt else row_spec(ATTN_WIDTH)
    kpe_shape = (m // cos.shape[0], ROPE_DIM, cos.shape[0]) if prompt else (m, ROPE_DIM)
    kpe_spec = (pl.BlockSpec((None, ROPE_DIM, tr), lambda i: (i // tab_blocks, 0, i % tab_blocks))
                if prompt else row_spec(ROPE_DIM))
    out_shape = [jax.ShapeDtypeStruct((m, QK_WIDTH), BF16), jax.ShapeDtypeStruct((m, QK_WIDTH), BF16),
                 jax.ShapeDtypeStruct(v_shape, BF16), jax.ShapeDtypeStruct((m, KV_RANK), F32),
                 jax.ShapeDtypeStruct(kpe_shape, F32), jax.ShapeDtypeStruct((m, GMLP_WIDTH), BF16)]
    out_specs = [row_spec(QK_WIDTH), row_spec(QK_WIDTH), v_spec, row_spec(KV_RANK),
                 kpe_spec, row_spec(GMLP_WIDTH)]
    extra_in = []
    if prompt:
        (w_out,) = f32_weights
        slabs = [pl.BlockSpec((D_MODEL // steps, D_MODEL), lambda i: (i, 0))]
        extra_in = [w_out]
        in_specs += slabs
        out_shape += [jax.ShapeDtypeStruct(w.shape, BF16) for w in extra_in]
        out_specs += slabs
    else:
        out_shape.append(jax.ShapeDtypeStruct((m, GMLP_WIDTH), F32))
        out_specs.append(row_spec(GMLP_WIDTH))
    return pl.pallas_call(
        functools.partial(_proj_kernel, chunk_len, prompt, tm),
        out_shape=out_shape,
        grid=(steps,),
        in_specs=in_specs,
        out_specs=out_specs,
        scratch_shapes=[pltpu.VMEM((nt, tm, D_MODEL), BF16), pltpu.VMEM((nt, tm, GMLP_WIDTH), F32),
                        pltpu.VMEM((nt, tm, GMLP_WIDTH), BF16)],
        compiler_params=pltpu.CompilerParams(dimension_semantics=("parallel",),
                                             vmem_limit_bytes=VMEM_LIMIT),
        name="proj",
    )(x2d, *consts, cos, s1, s2, *consts2, *extra_in)


NEG = -1e30
SOFTMAX_KEY_CHUNK = 64
KEY_BLOCKS_PER_STEP = 2
SCORE_LEAD = 4


def _attn_prompt_kernel(q_ref, k_ref, vt_ref, go_ref, wu_ref, wd_ref, o_ref, wu_out, wd_out,
                        acc_scr, m_scr, l_scr, s_scr, p_scr):
    wu_out[...] = wu_ref[...].astype(BF16)
    wd_out[...] = wd_ref[...].astype(BF16)
    t = q_ref.shape[0]
    qi = pl.program_id(1)
    m_scr[...] = jnp.full(m_scr.shape, NEG, F32)
    l_scr[...] = jnp.zeros(l_scr.shape, F32)
    acc_scr[...] = jnp.zeros(acc_scr.shape, F32)

    def step(kb0, nblk, last_masked):
        nk = nblk * t
        if last_masked:
            key = lax.broadcasted_iota(jnp.int32, (t, t), 0)
            qry = lax.broadcasted_iota(jnp.int32, (t, t), 1)
            ok = (key // CHUNK) <= (qry // CHUNK)

        m8 = {}

        def scores(h):
            qs = slice(h * HEAD_PAD, (h + 1) * HEAD_PAD)
            mh = None
            for j in range(nblk):
                rows = pl.ds(pl.multiple_of((kb0 + j) * t, t), t)
                s = _dot_nt(k_ref[rows, qs], q_ref[:, qs])
                if last_masked and j == nblk - 1:
                    s = jnp.where(ok, s, NEG)
                s_scr[h, j * t:(j + 1) * t, :] = s
                mj = jnp.max(s.reshape(t // SUBLANES, SUBLANES, t), axis=0)
                mh = mj if mh is None else jnp.maximum(mh, mj)
            m8[h] = mh

        ones = jnp.ones((BF16_SUBLANES, nk), BF16)
        for h in range(min(SCORE_LEAD, N_HEADS)):
            scores(h)
        for h in range(N_HEADS):
            if h + SCORE_LEAD < N_HEADS:
                scores(h + SCORE_LEAD)
            hs = slice(h * V_DIM, (h + 1) * V_DIM)
            m_old = m_scr[h]
            m_new = jnp.maximum(m_old, jnp.max(m8[h], axis=0, keepdims=True))
            alpha = jnp.exp2(m_old - m_new)
            for c in range(nk // SOFTMAX_KEY_CHUNK):
                cs = slice(c * SOFTMAX_KEY_CHUNK, (c + 1) * SOFTMAX_KEY_CHUNK)
                p_scr[h, cs, :] = jnp.exp2(s_scr[h, cs, :] - m_new).astype(BF16)
            m_scr[h] = m_new
            vt = [vt_ref[pl.ds(pl.multiple_of((kb0 + j) * ATTN_WIDTH, ATTN_WIDTH) + h * V_DIM, V_DIM), :]
                  for j in range(nblk)]
            vt = vt[0] if nblk == 1 else jnp.concatenate(vt, axis=1)
            pv = _dot(jnp.concatenate([vt, ones], axis=0), p_scr[h, :nk, :])
            acc_scr[hs, :] = alpha * acc_scr[hs, :] + pv[:V_DIM]
            l_scr[h] = alpha * l_scr[h] + pv[V_DIM:V_DIM + 1]

    def body(j, carry):
        step(j * KEY_BLOCKS_PER_STEP, KEY_BLOCKS_PER_STEP, False)
        return carry

    lax.fori_loop(0, qi // KEY_BLOCKS_PER_STEP, body, 0)
    for rem in range(KEY_BLOCKS_PER_STEP):
        @pl.when(qi % KEY_BLOCKS_PER_STEP == rem)
        def _(rem=rem):
            step(qi - rem, rem + 1, True)

    for h in range(N_HEADS):
        hs = slice(h * V_DIM, (h + 1) * V_DIM)
        acc_scr[hs, :] = acc_scr[hs, :] / l_scr[h]
    o = acc_scr[...].T
    o_ref[...] = (_rms(o) * go_ref[...]).astype(BF16)


def _attn_prompt(q, k, vt, go, w_up, w_down, batch, seq, t):
    nq = seq // t
    steps = batch * nq
    assert vt.shape == (steps * ATTN_WIDTH, t)
    tk = KEY_BLOCKS_PER_STEP * t
    wu_spec = pl.BlockSpec((D_MODEL, D_FF // steps), lambda b, i: (0, b * nq + i))
    wd_spec = pl.BlockSpec((D_FF // steps, D_MODEL), lambda b, i: (b * nq + i, 0))
    return pl.pallas_call(
        _attn_prompt_kernel,
        out_shape=[jax.ShapeDtypeStruct((batch * seq, ATTN_WIDTH), BF16),
                   jax.ShapeDtypeStruct(w_up.shape, BF16), jax.ShapeDtypeStruct(w_down.shape, BF16)],
        grid=(batch, nq),
        in_specs=[pl.BlockSpec((t, QK_WIDTH), lambda b, i: (b * nq + i, 0)),
                  pl.BlockSpec((seq, QK_WIDTH), lambda b, i: (b, 0)),
                  pl.BlockSpec((nq * ATTN_WIDTH, t), lambda b, i: (b, 0)),
                  _const_spec(go.shape), wu_spec, wd_spec],
        out_specs=[pl.BlockSpec((t, ATTN_WIDTH), lambda b, i: (b * nq + i, 0)), wu_spec, wd_spec],
        scratch_shapes=[pltpu.VMEM((ATTN_WIDTH, t), F32), pltpu.VMEM((N_HEADS, 1, t), F32),
                        pltpu.VMEM((N_HEADS, 1, t), F32), pltpu.VMEM((N_HEADS, tk, t), F32),
                        pltpu.VMEM((N_HEADS, tk, t), BF16)],
        compiler_params=pltpu.CompilerParams(dimension_semantics=("parallel", "arbitrary"),
                                             vmem_limit_bytes=VMEM_LIMIT),
        name="attn_prompt",
    )(q, k, vt, go, w_up, w_down)


def _attn_sample_kernel(past, c_ref, krt_ref, q_ref, kn_ref, vn_ref, wukt_ref, wuv_ref, go_ref,
                        o_ref, cb_scr, qa_scr, qr_scr, r_scr, sc_scr, sn_scr, pc_scr, acc_scr):
    nb = krt_ref.shape[0]
    t = q_ref.shape[0] // nb
    row = lax.broadcasted_iota(jnp.int32, (t, t), 0)
    col = lax.broadcasted_iota(jnp.int32, (t, t), 1)
    new_ok = ((past + col) // CHUNK) <= ((past + row) // CHUNK)
    denom = [[None] * N_HEADS for _ in range(nb)]
    p_new = [[None] * N_HEADS for _ in range(nb)]

    def norm_factors(j):
        cb_scr[j] = c_ref[j * past:(j + 1) * past, :].astype(BF16)
        krt = krt_ref[j]
        ss_pe = jnp.sum(krt * krt, axis=0, keepdims=True)
        knt = _dot_nt(wukt_ref[...], cb_scr[j])
        for h in range(N_HEADS):
            nope = knt[h * LANES:(h + 1) * LANES, :]
            ss = jnp.sum(nope * nope, axis=0, keepdims=True) + ss_pe
            r_scr[j, h] = lax.rsqrt(ss * (1.0 / QK_DIM) + EPS)

    def scores(j):
        qj = q_ref.at[j * t:(j + 1) * t, :]
        knj = kn_ref.at[j * t:(j + 1) * t, :]
        for h in range(N_HEADS):
            lo = h * HEAD_PAD
            rows = slice(h * t, (h + 1) * t)
            qa_scr[j, rows, :] = _dot(qj[:, lo:lo + NOPE_DIM],
                                      wukt_ref[h * NOPE_DIM:(h + 1) * NOPE_DIM, :]).astype(BF16)
            qr_scr[j, rows, :] = qj[:, lo + NOPE_DIM:lo + QK_DIM]
            sn_scr[j, h] = jnp.where(new_ok, _dot_nt(qj[:, lo:lo + HEAD_PAD], knj[:, lo:lo + HEAD_PAD]), NEG)
        sc_scr[j] = _dot_nt(qa_scr[j], cb_scr[j]) + _dot(qr_scr[j], krt_ref[j].astype(BF16))

    def softmax(j):
        for h in range(N_HEADS):
            s_c, s_n = sc_scr[j, h * t:(h + 1) * t, :] * r_scr[j, h], sn_scr[j, h]
            m = jnp.maximum(jnp.max(s_c, axis=-1, keepdims=True), jnp.max(s_n, axis=-1, keepdims=True))
            p_c = jnp.exp2(s_c - m)
            p_n = jnp.exp2(s_n - m)
            denom[j][h] = jnp.sum(p_c, axis=-1, keepdims=True) + jnp.sum(p_n, axis=-1, keepdims=True)
            p_new[j][h] = p_n.astype(BF16)
            pc_scr[j, h * t:(h + 1) * t, :] = p_c.astype(BF16)

    def values(j):
        vnj = vn_ref.at[j * t:(j + 1) * t, :]
        lat = _dot(pc_scr[j], cb_scr[j]).astype(BF16)
        for h in range(N_HEADS):
            vs = slice(h * V_DIM, (h + 1) * V_DIM)
            acc = _dot(lat[h * t:(h + 1) * t, :], wuv_ref[:, vs]) + _dot(p_new[j][h], vnj[:, vs])
            acc_scr[j, :, vs] = acc / denom[j][h]
        o_ref[j * t:(j + 1) * t, :] = (_rms(acc_scr[j]) * go_ref[...]).astype(BF16)

    stages = (norm_factors, scores, softmax, values)
    for tick in range(nb + len(stages) - 1):
        for j in range(nb):
            if 0 <= tick - j < len(stages):
                stages[tick - j](j)


def _attn_sample(c2d, krt, q, kn, vn, wp, batch, t, past, nb):
    consts = [wp["w_uk_t"], wp["w_uv"], wp["goa"]]
    ht = N_HEADS * t
    return pl.pallas_call(
        functools.partial(_attn_sample_kernel, past),
        out_shape=jax.ShapeDtypeStruct((batch * t, ATTN_WIDTH), BF16),
        grid=(batch // nb,),
        in_specs=[pl.BlockSpec((nb * past, KV_RANK), lambda b: (b, 0)),
                  pl.BlockSpec((nb, ROPE_DIM, past), lambda b: (b, 0, 0)),
                  pl.BlockSpec((nb * t, QK_WIDTH), lambda b: (b, 0)),
                  pl.BlockSpec((nb * t, QK_WIDTH), lambda b: (b, 0)),
                  pl.BlockSpec((nb * t, ATTN_WIDTH), lambda b: (b, 0))]
                 + [_const_spec(c.shape) for c in consts],
        out_specs=pl.BlockSpec((nb * t, ATTN_WIDTH), lambda b: (b, 0)),
        scratch_shapes=[pltpu.VMEM((nb, past, KV_RANK), BF16), pltpu.VMEM((nb, ht, KV_RANK), BF16),
                        pltpu.VMEM((nb, ht, ROPE_DIM), BF16), pltpu.VMEM((nb, N_HEADS, 1, past), F32),
                        pltpu.VMEM((nb, ht, past), F32), pltpu.VMEM((nb, N_HEADS, t, t), F32),
                        pltpu.VMEM((nb, ht, past), BF16), pltpu.VMEM((nb, t, ATTN_WIDTH), F32)],
        compiler_params=pltpu.CompilerParams(dimension_semantics=("parallel",),
                                             vmem_limit_bytes=VMEM_LIMIT),
        name="attn_sample",
    )(c2d, krt, q, kn, vn, *consts)


def _outproj_kernel(an_ref, gm_ref, x_ref, wo_ref, gffn_ref, h_ref, hn_ref):
    h = (x_ref[...] + _dot(an_ref[...], wo_ref[:ATTN_WIDTH, :])
         + _dot(gm_ref[...], wo_ref[ATTN_WIDTH:, :]))
    h_ref[...] = h
    hn_ref[...] = (_rms(h) * gffn_ref[...]).astype(BF16)


def _outproj(an, gm, x2d, wp, tm):
    m = x2d.shape[0]
    row_spec = lambda w: pl.BlockSpec((tm, w), lambda i: (i, 0))
    return pl.pallas_call(
        _outproj_kernel,
        out_shape=[jax.ShapeDtypeStruct((m, D_MODEL), F32), jax.ShapeDtypeStruct((m, D_MODEL), BF16)],
        grid=(m // tm,),
        in_specs=[row_spec(ATTN_WIDTH), row_spec(GMLP_WIDTH), row_spec(D_MODEL),
                  _const_spec(wp["w_out"].shape), _const_spec(wp["gffn"].shape)],
        out_specs=[row_spec(D_MODEL), row_spec(D_MODEL)],
        compiler_params=pltpu.CompilerParams(dimension_semantics=("parallel",),
                                             vmem_limit_bytes=VMEM_LIMIT),
        name="outproj",
    )(an, gm, x2d, wp["w_out"], wp["gffn"])


def _ffn_kernel(h_ref, hn_ref, wu_ref, wd_ref, y_ref):
    f = pl.program_id(1)
    slab = h_ref.shape[1]

    @pl.when(f == 0)
    def _():
        y_ref[...] = jnp.zeros(y_ref.shape, F32)

    for j in range(D_MODEL // slab):
        @pl.when(f == j)
        def _(j=j):
            y_ref[:, j * slab:(j + 1) * slab] += h_ref[...]

    a = jnp.maximum(_dot(hn_ref[...], wu_ref[...]), 0.0)
    y_ref[...] += _dot((a * a).astype(BF16), wd_ref[...])


def _ffn(h, hn, wp, tm, tf):
    m = h.shape[0]
    nf = D_FF // tf
    return pl.pallas_call(
        _ffn_kernel,
        out_shape=jax.ShapeDtypeStruct((m, D_MODEL), F32),
        grid=(m // tm, nf),
        in_specs=[pl.BlockSpec((tm, D_MODEL // nf), lambda i, f: (i, f)),
                  pl.BlockSpec((tm, D_MODEL), lambda i, f: (i, 0)),
                  pl.BlockSpec((D_MODEL, tf), lambda i, f: (0, f)),
                  pl.BlockSpec((tf, D_MODEL), lambda i, f: (f, 0))],
        out_specs=pl.BlockSpec((tm, D_MODEL), lambda i, f: (i, 0)),
        compiler_params=pltpu.CompilerParams(dimension_semantics=("parallel", "arbitrary"),
                                             vmem_limit_bytes=VMEM_LIMIT),
        name="ffn",
    )(h, hn, wp["w_up"], wp["w_down"])


def _rope_tables(pos):
    inv = ROPE_THETA ** (-np.arange(HALF_ROPE, dtype=np.float64) / HALF_ROPE)
    ang = np.asarray(pos, np.float64)[:, None] * inv[None, :]
    cos, sin = np.cos(ang), np.sin(ang)
    z = np.zeros_like(cos)
    z2 = np.zeros((ang.shape[0], LANES - ROPE_DIM))
    tabs = (np.concatenate([cos, cos, z2], axis=1), np.concatenate([-sin, z, z2], axis=1),
            np.concatenate([z, sin, z2], axis=1))
    return tuple(jnp.asarray(t, F32) for t in tabs)


def _head_gain(g_nope, g_rope, scale):
    pad = jnp.zeros((HEAD_PAD - QK_DIM,), F32)
    return (jnp.concatenate([g_nope, g_rope, g_rope, pad]) * scale)[None, :]


def _pad_w_in_kernel(wt_ref, o_ref):
    tr = wt_ref.shape[1]
    split = C_PE0 + ROPE_DIM
    o_ref[:, :C_PE0] = wt_ref[:C_PE0, :].T.astype(BF16)
    pe = wt_ref[C_PE0:C_PE1, :].T
    lane = lax.broadcasted_iota(jnp.int32, (tr, LANES), 1)
    o_ref[:, C_PE0:C_PE1] = jnp.where(lane < ROPE_DIM, pe, 0.0).astype(BF16)
    o_ref[:, C_PE1:] = wt_ref[split:, :].T.astype(BF16)


def _pad_w_in(w_in_t, tr):
    width, rows = w_in_t.shape
    assert width + LANES - ROPE_DIM == IN_PAD
    return pl.pallas_call(
        _pad_w_in_kernel,
        out_shape=jax.ShapeDtypeStruct((rows, IN_PAD), BF16),
        grid=(rows // tr,),
        in_specs=[pl.BlockSpec((width, tr), lambda i: (0, i))],
        out_specs=pl.BlockSpec((tr, IN_PAD), lambda i: (i, 0)),
        compiler_params=pltpu.CompilerParams(dimension_semantics=("parallel",)),
        name="pad_w_in",
    )(w_in_t)


def _prep_weights(norm_mix, w_in, q_lat_norm, kv_lat_norm, w_uq, w_uk, w_uv, q_norm_nope, q_norm_rope,
                  k_norm_nope, k_norm_rope, v_norm, w_spatial, b_spatial, out_norm_attn, out_norm_gmlp,
                  norm_ffn, chunk_len):
    w_in_p = _pad_w_in(w_in.T, 256)
    wq = w_uq.reshape(Q_RANK, N_HEADS, QK_DIM)
    wq = jnp.pad(wq, ((0, 0), (0, 0), (0, HEAD_PAD - QK_DIM))).reshape(Q_RANK, QK_WIDTH).astype(BF16)
    reps = GMLP_CHUNK // chunk_len
    ws = jnp.tile(w_spatial[:, :chunk_len, :chunk_len], (1, reps, reps))
    bst = jnp.tile(b_spatial[:, :chunk_len], (1, reps)).T
    return {
        "gmix": norm_mix[None, :], "w_in": w_in_p, "gql": q_lat_norm[None, :], "gkv": kv_lat_norm[None, :],
        "w_uq": wq, "w_uk": w_uk.astype(BF16), "w_uk_t": w_uk.T.astype(BF16),
        "w_uv": w_uv.astype(BF16), "w_uv_t": w_uv.T.astype(BF16),
        "gq": (_head_gain(q_norm_nope, q_norm_rope, QK_DIM ** -0.5 * math.log2(math.e))
               * _head_gain(k_norm_nope, k_norm_rope, 1.0)),
        "v_norm": v_norm, "ws": ws, "bst": bst, "gog": out_norm_gmlp[None, :],
        "goa": out_norm_attn[None, :], "gffn": norm_ffn[None, :],
    }


def _finish(an, gm, x2d, wp, tm_ffn):
    h, hn = _outproj(an, gm, x2d, wp, 512)
    return _ffn(h, hn, wp, tm_ffn, 1024)


def kernel(x_prompt, x_sample, cache_c_kv, cache_k_rope, norm_mix, w_in, q_lat_norm, kv_lat_norm, w_uq, w_uk, w_uv, q_norm_nope, q_norm_rope, k_norm_nope, k_norm_rope, v_norm, w_spatial, b_spatial, out_norm_attn, out_norm_gmlp, w_out, norm_ffn, w_up, w_down):
    depth = w_in.shape[0]
    assert depth == 1
    batch, seq, _ = x_prompt.shape
    dec_batch, dec_seq, _ = x_sample.shape
    past = cache_c_kv.shape[2]
    assert past % CHUNK == 0 and dec_seq <= CHUNK and GMLP_CHUNK % dec_seq == 0

    weights = (norm_mix[0], w_in[0], q_lat_norm[0], kv_lat_norm[0], w_uq[0], w_uk[0], w_uv[0],
               q_norm_nope[0], q_norm_rope[0], k_norm_nope[0], k_norm_rope[0], v_norm[0], w_spatial[0],
               b_spatial[0], out_norm_attn[0], out_norm_gmlp[0], norm_ffn[0])
    wp = _prep_weights(*weights, chunk_len=GMLP_CHUNK)
    reps = GMLP_CHUNK // dec_seq
    ws_s = jnp.tile(w_spatial[0][:, :dec_seq, :dec_seq], (1, reps, reps))
    bst_s = jnp.tile(b_spatial[0][:, :dec_seq], (1, reps)).T

    tm_p, nt_p = 256, 2
    xp = x_prompt.reshape(batch * seq, D_MODEL)
    tabs_p = _rope_tables(np.arange(seq))
    q, k, vt, ckv_p, kpe_p, gm, wo_b = _proj(
        xp, tabs_p, wp, GMLP_CHUNK, tm_p, nt_p, f32_weights=(w_out[0],))
    an, wu_b, wd_b = _attn_prompt(q, k, vt, wp["goa"], w_up[0], w_down[0], batch, seq, tm_p)
    wp = dict(wp, w_out=wo_b, w_up=wu_b, w_down=wd_b)
    wp_s = dict(wp, ws=ws_s, bst=bst_s)
    y_p = _finish(an, gm, xp, wp, 1024)

    tm_s, nt_s = 256, 2
    xs = x_sample.reshape(dec_batch * dec_seq, D_MODEL)
    tabs_s = _rope_tables(past + np.arange(nt_s * tm_s) % dec_seq)
    qs, kn, vn, ckv_s, kpe_s, gms, vg_s = _proj(xs, tabs_s, wp_s, dec_seq, tm_s, nt_s)
    ans = _attn_sample(cache_c_kv[0].reshape(dec_batch * past, KV_RANK),
                       jnp.swapaxes(cache_k_rope[0], 1, 2),
                       qs, kn, vn, wp, dec_batch, dec_seq, past, nb=4)
    y_s = _finish(ans, gms, xs, wp, 1024)

    return (y_p.reshape(batch, seq, D_MODEL),
            y_s.reshape(dec_batch, dec_seq, D_MODEL),
            ckv_p.reshape(1, batch, seq, KV_RANK),
            jnp.swapaxes(kpe_p, 1, 2)[None],
            ckv_s.reshape(1, dec_batch, dec_seq, KV_RANK),
            kpe_s.reshape(1, dec_batch, dec_seq, ROPE_DIM),
            vg_s.reshape(1, dec_batch, dec_seq, GMLP_WIDTH))
```

```python
import functools
import math

import jax
import jax.numpy as jnp
import numpy as np
from jax import lax
from jax.experimental import pallas as pl
from jax.experimental.pallas import tpu as pltpu

D_MODEL = 2048
CHUNK = 64
N_HEADS = 8
NOPE_DIM = 128
ROPE_DIM = 64
HALF_ROPE = ROPE_DIM // 2
QK_DIM = NOPE_DIM + ROPE_DIM
V_DIM = 128
ATTN_WIDTH = N_HEADS * V_DIM
Q_RANK = 512
KV_RANK = 256
GMLP_GROUPS = 8
GMLP_GROUP_DIM = 128
GMLP_WIDTH = GMLP_GROUPS * GMLP_GROUP_DIM
GMLP_CHUNK = 128
D_FF = 4 * D_MODEL
ROPE_THETA = 10000.0
EPS = 1e-6

LANES = 128
SUBLANES = 8
BF16_SUBLANES = 2 * SUBLANES
HEAD_PAD = 2 * LANES
QK_WIDTH = N_HEADS * HEAD_PAD
C_Q0, C_Q1 = 0, Q_RANK
C_KV0, C_KV1 = C_Q1, C_Q1 + KV_RANK
C_PE0, C_PE1 = C_KV1, C_KV1 + LANES
C_U0, C_U1 = C_PE1, C_PE1 + GMLP_WIDTH
C_V0, C_V1 = C_U1, C_U1 + GMLP_WIDTH
IN_PAD = C_V1
VMEM_LIMIT = 56 * 1024 * 1024

PROJ_ROWS = 256
PAD_ROWS = 256
OUTPROJ_ROWS = 512
FFN_ROWS = 1024
FFN_COLS = 1024
SAMPLE_BATCH_ROWS = 4

BF16 = jnp.bfloat16
F32 = jnp.float32


def _dot(a, b):
    return jnp.dot(a, b, preferred_element_type=F32)


def _dot_nt(a, b):
    return lax.dot_general(a, b, (((1,), (1,)), ((), ())), preferred_element_type=F32)


def _sumsq(x):
    return jnp.sum(x * x, axis=-1, keepdims=True)


def _rms(x):
    return x * lax.rsqrt(jnp.mean(x * x, axis=-1, keepdims=True) + EPS)


def _gelu(x):
    c = math.sqrt(2.0 / math.pi)
    return 0.5 * x * (1.0 + jnp.tanh(c * (x + 0.044715 * (x * x * x))))


def _rope(t, cos, s1, s2):
    return t * cos + pltpu.roll(t, LANES - HALF_ROPE, 1) * s1 + pltpu.roll(t, HALF_ROPE, 1) * s2


def _const_spec(shape):
    nd = len(shape)
    return pl.BlockSpec(shape, lambda *_: (0,) * nd, pipeline_mode=pl.Buffered(1))


def _proj_kernel(chunk_len, prompt, x_ref, gmix_ref, win_ref, gql_ref, gkv_ref, wuq_ref, wuk_ref,
                 wuv_ref, gq_ref, cos_ref, s1_ref, s2_ref, vn_ref, ws_ref, bst_ref, gog_ref, *rest):
    v_transposed = prompt
    emit_v = not prompt
    if prompt:
        (wo_ref, wu_ref, wd_ref, q_ref, k_ref, v_ref, ckv_ref, kpe_ref, gm_ref,
         wo_out, wu_out, wd_out, gate_scr, vgb_scr) = rest
        wo_out[...] = wo_ref[...].astype(BF16)
        wu_out[...] = wu_ref[...].astype(BF16)
        wd_out[...] = wd_ref[...].astype(BF16)
    else:
        q_ref, k_ref, v_ref, ckv_ref, kpe_ref, gm_ref, vg_ref, gate_scr, vgb_scr = rest
    tm = x_ref.shape[0]
    xn = (_rms(x_ref[...]) * gmix_ref[...]).astype(BF16)
    cos, s1, s2 = cos_ref[...], s1_ref[...], s2_ref[...]

    zq = _dot(xn, win_ref[:, C_Q0:C_Q1])
    qln = (_rms(zq) * gql_ref[...]).astype(BF16)
    qraw = _dot(qln, wuq_ref[...])
    for h in range(N_HEADS):
        lo = h * HEAD_PAD
        nope = qraw[:, lo:lo + LANES]
        rp = qraw[:, lo + LANES:lo + HEAD_PAD]
        r = lax.rsqrt((_sumsq(nope) + _sumsq(rp)) * (1.0 / QK_DIM) + EPS)
        rp = _rope(rp, cos, s1, s2)
        q_ref[:, lo:lo + LANES] = (nope * r * gq_ref[:, :LANES]).astype(BF16)
        q_ref[:, lo + LANES:lo + HEAD_PAD] = (rp * r * gq_ref[:, LANES:]).astype(BF16)

    ckv = _rms(_dot(xn, win_ref[:, C_KV0:C_KV1])) * gkv_ref[...]
    ckv_ref[...] = ckv
    cb = ckv.astype(BF16)
    pe = _rope(_dot(xn, win_ref[:, C_PE0:C_PE1]), cos, s1, s2)
    kpe_ref[...] = pe.T[:ROPE_DIM, :] if prompt else pe[:, :ROPE_DIM]
    ss_pe = _sumsq(pe)
    knope = _dot(cb, wuk_ref[...])
    if v_transposed:
        v_ref[...] = _dot_nt(wuv_ref[...], cb).astype(BF16)
    else:
        v_ref[...] = _dot(cb, wuv_ref[...]).astype(BF16)
    for h in range(N_HEADS):
        nope = knope[:, h * LANES:(h + 1) * LANES]
        r = lax.rsqrt((_sumsq(nope) + ss_pe) * (1.0 / QK_DIM) + EPS)
        lo = h * HEAD_PAD
        k_ref[:, lo:lo + LANES] = (nope * r).astype(BF16)
        k_ref[:, lo + LANES:lo + HEAD_PAD] = (pe * r).astype(BF16)

    gv = _gelu(_dot(xn, win_ref[:, C_V0:C_V1]))
    for g in range(GMLP_GROUPS):
        blk = gv[:, g * LANES:(g + 1) * LANES]
        vg = _rms(blk) * vn_ref[g:g + 1, :]
        if emit_v:
            vg_ref[:, g * LANES:(g + 1) * LANES] = vg
        vgb_scr[:, g * LANES:(g + 1) * LANES] = vg.astype(BF16)
    u = _gelu(_dot(xn, win_ref[:, C_U0:C_U1]))
    row = lax.broadcasted_iota(jnp.int32, (GMLP_CHUNK, GMLP_CHUNK), 0)
    col = lax.broadcasted_iota(jnp.int32, (GMLP_CHUNK, GMLP_CHUNK), 1)
    causal = (row // chunk_len == col // chunk_len) & (col <= row)
    for g in range(GMLP_GROUPS):
        wm = jnp.where(causal, ws_ref[g], 0.0).astype(BF16)
        bias = bst_ref[:, g:g + 1]
        for c in range(tm // GMLP_CHUNK):
            rows = slice(c * GMLP_CHUNK, (c + 1) * GMLP_CHUNK)
            cols = slice(g * LANES, (g + 1) * LANES)
            s = _dot(wm, vgb_scr[rows, cols]) + bias
            gate_scr[rows, cols] = u[rows, cols] * s
    gm_ref[...] = (_rms(gate_scr[...]) * gog_ref[...]).astype(BF16)


def _proj(x2d, tabs, wp, chunk_len, tm, f32_weights=None):
    prompt = f32_weights is not None
    m = x2d.shape[0]
    steps = m // tm
    cos, s1, s2 = tabs
    tab_blocks = cos.shape[0] // tm
    row_spec = lambda w: pl.BlockSpec((tm, w), lambda i: (i, 0))
    tab_spec = pl.BlockSpec((tm, LANES), lambda i: (i % tab_blocks, 0))
    consts = [wp["gmix"], wp["w_in"], wp["gql"], wp["gkv"], wp["w_uq"], wp["w_uk"],
              wp["w_uv_t"] if prompt else wp["w_uv"], wp["gq"]]
    consts2 = [wp["v_norm"], wp["ws"], wp["bst"], wp["gog"]]
    in_specs = ([row_spec(D_MODEL)] + [_const_spec(c.shape) for c in consts]
                + [tab_spec] * 3 + [_const_spec(c.shape) for c in consts2])
    v_shape = (steps * ATTN_WIDTH, tm) if prompt else (m, ATTN_WIDTH)
    v_spec = pl.BlockSpec((ATTN_WIDTH, tm), lambda i: (i, 0)) if prompt else row_spec(ATTN_WIDTH)
    kpe_shape = (m // cos.shape[0], ROPE_DIM, cos.shape[0]) if prompt else (m, ROPE_DIM)
    kpe_spec = (pl.BlockSpec((None, ROPE_DIM, tm), lambda i: (i // tab_blocks, 0, i % tab_blocks))
                if prompt else row_spec(ROPE_DIM))
    out_shape = [jax.ShapeDtypeStruct((m, QK_WIDTH), BF16), jax.ShapeDtypeStruct((m, QK_WIDTH), BF16),
                 jax.ShapeDtypeStruct(v_shape, BF16), jax.ShapeDtypeStruct((m, KV_RANK), F32),
                 jax.ShapeDtypeStruct(kpe_shape, F32), jax.ShapeDtypeStruct((m, GMLP_WIDTH), BF16)]
    out_specs = [row_spec(QK_WIDTH), row_spec(QK_WIDTH), v_spec, row_spec(KV_RANK),
                 kpe_spec, row_spec(GMLP_WIDTH)]
    extra_in = []
    if prompt:
        w_out, w_up, w_down = f32_weights
        slabs = [pl.BlockSpec((D_MODEL // steps, D_MODEL), lambda i: (i, 0)),
                 pl.BlockSpec((D_MODEL, D_FF // steps), lambda i: (0, i)),
                 pl.BlockSpec((D_FF // steps, D_MODEL), lambda i: (i, 0))]
        extra_in = [w_out, w_up, w_down]
        in_specs += slabs
        out_shape += [jax.ShapeDtypeStruct(w.shape, BF16) for w in extra_in]
        out_specs += slabs
    else:
        out_shape.append(jax.ShapeDtypeStruct((m, GMLP_WIDTH), F32))
        out_specs.append(row_spec(GMLP_WIDTH))
    return pl.pallas_call(
        functools.partial(_proj_kernel, chunk_len, prompt),
        out_shape=out_shape,
        grid=(steps,),
        in_specs=in_specs,
        out_specs=out_specs,
        scratch_shapes=[pltpu.VMEM((tm, GMLP_WIDTH), F32), pltpu.VMEM((tm, GMLP_WIDTH), BF16)],
        compiler_params=pltpu.CompilerParams(dimension_semantics=("parallel",),
                                             vmem_limit_bytes=VMEM_LIMIT),
        name="proj",
    )(x2d, *consts, cos, s1, s2, *consts2, *extra_in)


NEG = -1e30
SOFTMAX_KEY_CHUNK = 64
KEY_BLOCKS_PER_STEP = 2
SCORE_LEAD = 4


def _attn_prompt_kernel(q_ref, k_ref, vt_ref, go_ref, o_ref, acc_scr, m_scr, l_scr, s_scr, p_scr):
    t = q_ref.shape[0]
    qi = pl.program_id(1)
    m_scr[...] = jnp.full(m_scr.shape, NEG, F32)
    l_scr[...] = jnp.zeros(l_scr.shape, F32)
    acc_scr[...] = jnp.zeros(acc_scr.shape, F32)

    def step(kb0, nblk, last_masked):
        nk = nblk * t
        if last_masked:
            key = lax.broadcasted_iota(jnp.int32, (t, t), 0)
            qry = lax.broadcasted_iota(jnp.int32, (t, t), 1)
            ok = (key // CHUNK) <= (qry // CHUNK)

        m8 = {}

        def scores(h):
            qs = slice(h * HEAD_PAD, (h + 1) * HEAD_PAD)
            mh = None
            for j in range(nblk):
                rows = pl.ds(pl.multiple_of((kb0 + j) * t, t), t)
                s = _dot_nt(k_ref[rows, qs], q_ref[:, qs])
                if last_masked and j == nblk - 1:
                    s = jnp.where(ok, s, NEG)
                s_scr[h, j * t:(j + 1) * t, :] = s
                mj = jnp.max(s.reshape(t // SUBLANES, SUBLANES, t), axis=0)
                mh = mj if mh is None else jnp.maximum(mh, mj)
            m8[h] = mh

        ones = jnp.ones((BF16_SUBLANES, nk), BF16)
        for h in range(min(SCORE_LEAD, N_HEADS)):
            scores(h)
        for h in range(N_HEADS):
            if h + SCORE_LEAD < N_HEADS:
                scores(h + SCORE_LEAD)
            hs = slice(h * V_DIM, (h + 1) * V_DIM)
            m_old = m_scr[h]
            m_new = jnp.maximum(m_old, jnp.max(m8[h], axis=0, keepdims=True))
            alpha = jnp.exp2(m_old - m_new)
            for c in range(nk // SOFTMAX_KEY_CHUNK):
                cs = slice(c * SOFTMAX_KEY_CHUNK, (c + 1) * SOFTMAX_KEY_CHUNK)
                p_scr[h, cs, :] = jnp.exp2(s_scr[h, cs, :] - m_new).astype(BF16)
            m_scr[h] = m_new
            vt = [vt_ref[pl.ds(pl.multiple_of((kb0 + j) * ATTN_WIDTH, ATTN_WIDTH) + h * V_DIM, V_DIM), :]
                  for j in range(nblk)]
            vt = vt[0] if nblk == 1 else jnp.concatenate(vt, axis=1)
            pv = _dot(jnp.concatenate([vt, ones], axis=0), p_scr[h, :nk, :])
            acc_scr[hs, :] = alpha * acc_scr[hs, :] + pv[:V_DIM]
            l_scr[h] = alpha * l_scr[h] + pv[V_DIM:V_DIM + 1]

    def body(j, carry):
        step(j * KEY_BLOCKS_PER_STEP, KEY_BLOCKS_PER_STEP, False)
        return carry

    lax.fori_loop(0, qi // KEY_BLOCKS_PER_STEP, body, 0)
    for rem in range(KEY_BLOCKS_PER_STEP):
        @pl.when(qi % KEY_BLOCKS_PER_STEP == rem)
        def _(rem=rem):
            step(qi - rem, rem + 1, True)

    for h in range(N_HEADS):
        hs = slice(h * V_DIM, (h + 1) * V_DIM)
        acc_scr[hs, :] = acc_scr[hs, :] / l_scr[h]
    o = acc_scr[...].T
    o_ref[...] = (_rms(o) * go_ref[...]).astype(BF16)


def _attn_prompt(q, k, vt, go, batch, seq, t):
    nq = seq // t
    assert vt.shape == (batch * nq * ATTN_WIDTH, t)
    tk = KEY_BLOCKS_PER_STEP * t
    return pl.pallas_call(
        _attn_prompt_kernel,
        out_shape=jax.ShapeDtypeStruct((batch * seq, ATTN_WIDTH), BF16),
        grid=(batch, nq),
        in_specs=[pl.BlockSpec((t, QK_WIDTH), lambda b, i: (b * nq + i, 0)),
                  pl.BlockSpec((seq, QK_WIDTH), lambda b, i: (b, 0)),
                  pl.BlockSpec((nq * ATTN_WIDTH, t), lambda b, i: (b, 0)),
                  _const_spec(go.shape)],
        out_specs=pl.BlockSpec((t, ATTN_WIDTH), lambda b, i: (b * nq + i, 0)),
        scratch_shapes=[pltpu.VMEM((ATTN_WIDTH, t), F32), pltpu.VMEM((N_HEADS, 1, t), F32),
                        pltpu.VMEM((N_HEADS, 1, t), F32), pltpu.VMEM((N_HEADS, tk, t), F32),
                        pltpu.VMEM((N_HEADS, tk, t), BF16)],
        compiler_params=pltpu.CompilerParams(dimension_semantics=("parallel", "arbitrary"),
                                             vmem_limit_bytes=VMEM_LIMIT),
        name="attn_prompt",
    )(q, k, vt, go)


def _attn_sample_kernel(past, c_ref, krt_ref, q_ref, kn_ref, vn_ref, wukt_ref, wuv_ref, go_ref,
                        o_ref, cb_scr, qa_scr, qr_scr, r_scr, sc_scr, sn_scr, pc_scr, acc_scr):
    nb = krt_ref.shape[0]
    t = q_ref.shape[0] // nb
    row = lax.broadcasted_iota(jnp.int32, (t, t), 0)
    col = lax.broadcasted_iota(jnp.int32, (t, t), 1)
    new_ok = ((past + col) // CHUNK) <= ((past + row) // CHUNK)
    denom = [[None] * N_HEADS for _ in range(nb)]
    p_new = [[None] * N_HEADS for _ in range(nb)]

    def norm_factors(j):
        cb_scr[j] = c_ref[j * past:(j + 1) * past, :].astype(BF16)
        krt = krt_ref[j]
        ss_pe = jnp.sum(krt * krt, axis=0, keepdims=True)
        knt = _dot_nt(wukt_ref[...], cb_scr[j])
        for h in range(N_HEADS):
            nope = knt[h * LANES:(h + 1) * LANES, :]
            ss = jnp.sum(nope * nope, axis=0, keepdims=True) + ss_pe
            r_scr[j, h] = lax.rsqrt(ss * (1.0 / QK_DIM) + EPS)

    def scores(j):
        qj = q_ref.at[j * t:(j + 1) * t, :]
        knj = kn_ref.at[j * t:(j + 1) * t, :]
        for h in range(N_HEADS):
            lo = h * HEAD_PAD
            rows = slice(h * t, (h + 1) * t)
            qa_scr[j, rows, :] = _dot(qj[:, lo:lo + NOPE_DIM],
                                      wukt_ref[h * NOPE_DIM:(h + 1) * NOPE_DIM, :]).astype(BF16)
            qr_scr[j, rows, :] = qj[:, lo + NOPE_DIM:lo + QK_DIM]
            sn_scr[j, h] = jnp.where(new_ok, _dot_nt(qj[:, lo:lo + HEAD_PAD], knj[:, lo:lo + HEAD_PAD]), NEG)
        sc_scr[j] = _dot_nt(qa_scr[j], cb_scr[j]) + _dot(qr_scr[j], krt_ref[j].astype(BF16))

    def softmax(j):
        for h in range(N_HEADS):
            s_c, s_n = sc_scr[j, h * t:(h + 1) * t, :] * r_scr[j, h], sn_scr[j, h]
            m = jnp.maximum(jnp.max(s_c, axis=-1, keepdims=True), jnp.max(s_n, axis=-1, keepdims=True))
            p_c = jnp.exp2(s_c - m)
            p_n = jnp.exp2(s_n - m)
            denom[j][h] = jnp.sum(p_c, axis=-1, keepdims=True) + jnp.sum(p_n, axis=-1, keepdims=True)
            p_new[j][h] = p_n.astype(BF16)
            pc_scr[j, h * t:(h + 1) * t, :] = p_c.astype(BF16)

    def values(j):
        vnj = vn_ref.at[j * t:(j + 1) * t, :]
        lat = _dot(pc_scr[j], cb_scr[j]).astype(BF16)
        for h in range(N_HEADS):
            vs = slice(h * V_DIM, (h + 1) * V_DIM)
            acc = _dot(lat[h * t:(h + 1) * t, :], wuv_ref[:, vs]) + _dot(p_new[j][h], vnj[:, vs])
            acc_scr[j, :, vs] = acc / denom[j][h]
        o_ref[j * t:(j + 1) * t, :] = (_rms(acc_scr[j]) * go_ref[...]).astype(BF16)

    stages = (norm_factors, scores, softmax, values)
    for tick in range(nb + len(stages) - 1):
        for j in range(nb):
            if 0 <= tick - j < len(stages):
                stages[tick - j](j)


def _attn_sample(c2d, krt, q, kn, vn, wp, batch, t, past, nb):
    consts = [wp["w_uk_t"], wp["w_uv"], wp["goa"]]
    ht = N_HEADS * t
    return pl.pallas_call(
        functools.partial(_attn_sample_kernel, past),
        out_shape=jax.ShapeDtypeStruct((batch * t, ATTN_WIDTH), BF16),
        grid=(batch // nb,),
        in_specs=[pl.BlockSpec((nb * past, KV_RANK), lambda b: (b, 0)),
                  pl.BlockSpec((nb, ROPE_DIM, past), lambda b: (b, 0, 0)),
                  pl.BlockSpec((nb * t, QK_WIDTH), lambda b: (b, 0)),
                  pl.BlockSpec((nb * t, QK_WIDTH), lambda b: (b, 0)),
                  pl.BlockSpec((nb * t, ATTN_WIDTH), lambda b: (b, 0))]
                 + [_const_spec(c.shape) for c in consts],
        out_specs=pl.BlockSpec((nb * t, ATTN_WIDTH), lambda b: (b, 0)),
        scratch_shapes=[pltpu.VMEM((nb, past, KV_RANK), BF16), pltpu.VMEM((nb, ht, KV_RANK), BF16),
                        pltpu.VMEM((nb, ht, ROPE_DIM), BF16), pltpu.VMEM((nb, N_HEADS, 1, past), F32),
                        pltpu.VMEM((nb, ht, past), F32), pltpu.VMEM((nb, N_HEADS, t, t), F32),
                        pltpu.VMEM((nb, ht, past), BF16), pltpu.VMEM((nb, t, ATTN_WIDTH), F32)],
        compiler_params=pltpu.CompilerParams(dimension_semantics=("parallel",),
                                             vmem_limit_bytes=VMEM_LIMIT),
        name="attn_sample",
    )(c2d, krt, q, kn, vn, *consts)


def _outproj_kernel(an_ref, gm_ref, x_ref, wo_ref, gffn_ref, h_ref, hn_ref):
    h = (x_ref[...] + _dot(an_ref[...], wo_ref[:ATTN_WIDTH, :])
         + _dot(gm_ref[...], wo_ref[ATTN_WIDTH:, :]))
    h_ref[...] = h
    hn_ref[...] = (_rms(h) * gffn_ref[...]).astype(BF16)


def _outproj(an, gm, x2d, wp, tm):
    m = x2d.shape[0]
    row_spec = lambda w: pl.BlockSpec((tm, w), lambda i: (i, 0))
    return pl.pallas_call(
        _outproj_kernel,
        out_shape=[jax.ShapeDtypeStruct((m, D_MODEL), F32), jax.ShapeDtypeStruct((m, D_MODEL), BF16)],
        grid=(m // tm,),
        in_specs=[row_spec(ATTN_WIDTH), row_spec(GMLP_WIDTH), row_spec(D_MODEL),
                  _const_spec(wp["w_out"].shape), _const_spec(wp["gffn"].shape)],
        out_specs=[row_spec(D_MODEL), row_spec(D_MODEL)],
        compiler_params=pltpu.CompilerParams(dimension_semantics=("parallel",),
                                             vmem_limit_bytes=VMEM_LIMIT),
        name="outproj",
    )(an, gm, x2d, wp["w_out"], wp["gffn"])


def _ffn_kernel(h_ref, hn_ref, wu_ref, wd_ref, y_ref):
    f = pl.program_id(1)
    slab = h_ref.shape[1]

    @pl.when(f == 0)
    def _():
        y_ref[...] = jnp.zeros(y_ref.shape, F32)

    for j in range(D_MODEL // slab):
        @pl.when(f == j)
        def _(j=j):
            y_ref[:, j * slab:(j + 1) * slab] += h_ref[...]

    a = jnp.maximum(_dot(hn_ref[...], wu_ref[...]), 0.0)
    y_ref[...] += _dot((a * a).astype(BF16), wd_ref[...])


def _ffn(h, hn, wp, tm, tf):
    m = h.shape[0]
    nf = D_FF // tf
    return pl.pallas_call(
        _ffn_kernel,
        out_shape=jax.ShapeDtypeStruct((m, D_MODEL), F32),
        grid=(m // tm, nf),
        in_specs=[pl.BlockSpec((tm, D_MODEL // nf), lambda i, f: (i, f)),
                  pl.BlockSpec((tm, D_MODEL), lambda i, f: (i, 0)),
                  pl.BlockSpec((D_MODEL, tf), lambda i, f: (0, f)),
                  pl.BlockSpec((tf, D_MODEL), lambda i, f: (f, 0))],
        out_specs=pl.BlockSpec((tm, D_MODEL), lambda i, f: (i, 0)),
        compiler_params=pltpu.CompilerParams(dimension_semantics=("parallel", "arbitrary"),
                                             vmem_limit_bytes=VMEM_LIMIT),
        name="ffn",
    )(h, hn, wp["w_up"], wp["w_down"])


def _rope_tables(pos):
    inv = ROPE_THETA ** (-np.arange(HALF_ROPE, dtype=np.float64) / HALF_ROPE)
    ang = np.asarray(pos, np.float64)[:, None] * inv[None, :]
    cos, sin = np.cos(ang), np.sin(ang)
    z = np.zeros_like(cos)
    z2 = np.zeros((ang.shape[0], LANES - ROPE_DIM))
    tabs = (np.concatenate([cos, cos, z2], axis=1), np.concatenate([-sin, z, z2], axis=1),
            np.concatenate([z, sin, z2], axis=1))
    return tuple(jnp.asarray(t, F32) for t in tabs)


def _head_gain(g_nope, g_rope, scale):
    pad = jnp.zeros((HEAD_PAD - QK_DIM,), F32)
    return (jnp.concatenate([g_nope, g_rope, g_rope, pad]) * scale)[None, :]


def _pad_w_in_kernel(wt_ref, o_ref):
    tr = wt_ref.shape[1]
    split = C_PE0 + ROPE_DIM
    o_ref[:, :C_PE0] = wt_ref[:C_PE0, :].T.astype(BF16)
    pe = wt_ref[C_PE0:C_PE1, :].T
    lane = lax.broadcasted_iota(jnp.int32, (tr, LANES), 1)
    o_ref[:, C_PE0:C_PE1] = jnp.where(lane < ROPE_DIM, pe, 0.0).astype(BF16)
    o_ref[:, C_PE1:] = wt_ref[split:, :].T.astype(BF16)


def _pad_w_in(w_in_t, tr):
    width, rows = w_in_t.shape
    assert width + LANES - ROPE_DIM == IN_PAD
    return pl.pallas_call(
        _pad_w_in_kernel,
        out_shape=jax.ShapeDtypeStruct((rows, IN_PAD), BF16),
        grid=(rows // tr,),
        in_specs=[pl.BlockSpec((width, tr), lambda i: (0, i))],
        out_specs=pl.BlockSpec((tr, IN_PAD), lambda i: (i, 0)),
        compiler_params=pltpu.CompilerParams(dimension_semantics=("parallel",)),
        name="pad_w_in",
    )(w_in_t)


def _prep_weights(norm_mix, w_in, q_lat_norm, kv_lat_norm, w_uq, w_uk, w_uv, q_norm_nope, q_norm_rope,
                  k_norm_nope, k_norm_rope, v_norm, w_spatial, b_spatial, out_norm_attn, out_norm_gmlp,
                  norm_ffn, chunk_len):
    w_in_p = _pad_w_in(w_in.T, PAD_ROWS)
    wq = w_uq.reshape(Q_RANK, N_HEADS, QK_DIM)
    wq = jnp.pad(wq, ((0, 0), (0, 0), (0, HEAD_PAD - QK_DIM))).reshape(Q_RANK, QK_WIDTH).astype(BF16)
    reps = GMLP_CHUNK // chunk_len
    ws = jnp.tile(w_spatial[:, :chunk_len, :chunk_len], (1, reps, reps))
    bst = jnp.tile(b_spatial[:, :chunk_len], (1, reps)).T
    return {
        "gmix": norm_mix[None, :], "w_in": w_in_p, "gql": q_lat_norm[None, :], "gkv": kv_lat_norm[None, :],
        "w_uq": wq, "w_uk": w_uk.astype(BF16), "w_uk_t": w_uk.T.astype(BF16),
        "w_uv": w_uv.astype(BF16), "w_uv_t": w_uv.T.astype(BF16),
        "gq": (_head_gain(q_norm_nope, q_norm_rope, QK_DIM ** -0.5 * math.log2(math.e))
               * _head_gain(k_norm_nope, k_norm_rope, 1.0)),
        "v_norm": v_norm, "ws": ws, "bst": bst, "gog": out_norm_gmlp[None, :],
        "goa": out_norm_attn[None, :], "gffn": norm_ffn[None, :],
    }


def _finish(an, gm, x2d, wp):
    h, hn = _outproj(an, gm, x2d, wp, OUTPROJ_ROWS)
    return _ffn(h, hn, wp, FFN_ROWS, FFN_COLS)


def kernel(x_prompt, x_sample, cache_c_kv, cache_k_rope, norm_mix, w_in, q_lat_norm, kv_lat_norm, w_uq, w_uk, w_uv, q_norm_nope, q_norm_rope, k_norm_nope, k_norm_rope, v_norm, w_spatial, b_spatial, out_norm_attn, out_norm_gmlp, w_out, norm_ffn, w_up, w_down):
    depth = w_in.shape[0]
    assert depth == 1
    batch, seq, _ = x_prompt.shape
    dec_batch, dec_seq, _ = x_sample.shape
    past = cache_c_kv.shape[2]
    assert past % CHUNK == 0 and dec_seq <= CHUNK and GMLP_CHUNK % dec_seq == 0
    n_prompt, n_sample = batch * seq, dec_batch * dec_seq
    assert seq % PROJ_ROWS == 0 and n_sample % PROJ_ROWS == 0 and PROJ_ROWS % dec_seq == 0
    assert n_prompt % FFN_ROWS == 0 and n_sample % FFN_ROWS == 0 and dec_batch % SAMPLE_BATCH_ROWS == 0

    weights = (norm_mix[0], w_in[0], q_lat_norm[0], kv_lat_norm[0], w_uq[0], w_uk[0], w_uv[0],
               q_norm_nope[0], q_norm_rope[0], k_norm_nope[0], k_norm_rope[0], v_norm[0], w_spatial[0],
               b_spatial[0], out_norm_attn[0], out_norm_gmlp[0], norm_ffn[0])
    wp = _prep_weights(*weights, chunk_len=GMLP_CHUNK)
    reps = GMLP_CHUNK // dec_seq
    ws_s = jnp.tile(w_spatial[0][:, :dec_seq, :dec_seq], (1, reps, reps))
    bst_s = jnp.tile(b_spatial[0][:, :dec_seq], (1, reps)).T

    xp = x_prompt.reshape(batch * seq, D_MODEL)
    tabs_p = _rope_tables(np.arange(seq))
    q, k, vt, ckv_p, kpe_p, gm, wo_b, wu_b, wd_b = _proj(
        xp, tabs_p, wp, GMLP_CHUNK, PROJ_ROWS, f32_weights=(w_out[0], w_up[0], w_down[0]))
    wp = dict(wp, w_out=wo_b, w_up=wu_b, w_down=wd_b)
    wp_s = dict(wp, ws=ws_s, bst=bst_s)
    an = _attn_prompt(q, k, vt, wp["goa"], batch, seq, PROJ_ROWS)
    y_p = _finish(an, gm, xp, wp)

    xs = x_sample.reshape(dec_batch * dec_seq, D_MODEL)
    tabs_s = _rope_tables(past + np.arange(PROJ_ROWS) % dec_seq)
    qs, kn, vn, ckv_s, kpe_s, gms, vg_s = _proj(xs, tabs_s, wp_s, dec_seq, PROJ_ROWS)
    ans = _attn_sample(cache_c_kv[0].reshape(dec_batch * past, KV_RANK),
                       jnp.swapaxes(cache_k_rope[0], 1, 2),
                       qs, kn, vn, wp, dec_batch, dec_seq, past, SAMPLE_BATCH_ROWS)
    y_s = _finish(ans, gms, xs, wp)

    return (y_p.reshape(batch, seq, D_MODEL),
            y_s.reshape(dec_batch, dec_seq, D_MODEL),
            ckv_p.reshape(1, batch, seq, KV_RANK),
            jnp.swapaxes(kpe_p, 1, 2)[None],
            ckv_s.reshape(1, dec_batch, dec_seq, KV_RANK),
            kpe_s.reshape(1, dec_batch, dec_seq, ROPE_DIM),
            vg_s.reshape(1, dec_batch, dec_seq, GMLP_WIDTH))
```

```python
import functools
import math

import jax
import jax.numpy as jnp
import numpy as np
from jax import lax
from jax.experimental import pallas as pl
from jax.experimental.pallas import tpu as pltpu

D_MODEL = 2048
CHUNK = 64
N_HEADS = 8
NOPE_DIM = 128
ROPE_DIM = 64
HALF_ROPE = ROPE_DIM // 2
QK_DIM = NOPE_DIM + ROPE_DIM
V_DIM = 128
ATTN_WIDTH = N_HEADS * V_DIM
Q_RANK = 512
KV_RANK = 256
GMLP_GROUPS = 8
GMLP_GROUP_DIM = 128
GMLP_WIDTH = GMLP_GROUPS * GMLP_GROUP_DIM
GMLP_CHUNK = 128
D_FF = 4 * D_MODEL
ROPE_THETA = 10000.0
EPS = 1e-6

LANES = 128
SUBLANES = 8
BF16_SUBLANES = 2 * SUBLANES
HEAD_PAD = 2 * LANES
QK_WIDTH = N_HEADS * HEAD_PAD
C_Q0, C_Q1 = 0, Q_RANK
C_KV0, C_KV1 = C_Q1, C_Q1 + KV_RANK
C_PE0, C_PE1 = C_KV1, C_KV1 + LANES
C_U0, C_U1 = C_PE1, C_PE1 + GMLP_WIDTH
C_V0, C_V1 = C_U1, C_U1 + GMLP_WIDTH
IN_PAD = C_V1
VMEM_LIMIT = 56 * 1024 * 1024

PROJ_ROWS = 256
HEAD_ROWS = 128
PAD_ROWS = 256
OUTPROJ_ROWS = 512
FFN_ROWS = 1024
FFN_COLS = 1024
SAMPLE_BATCH_ROWS = 4

BF16 = jnp.bfloat16
F32 = jnp.float32


def _dot(a, b):
    return jnp.dot(a, b, preferred_element_type=F32)


def _dot_nt(a, b):
    return lax.dot_general(a, b, (((1,), (1,)), ((), ())), preferred_element_type=F32)


def _sum_lanes(x):
    return jnp.sum(x, axis=-1, keepdims=True)


def _sumsq(x):
    return _sum_lanes(x * x)


def _rms(x):
    return x * lax.rsqrt(jnp.mean(x * x, axis=-1, keepdims=True) + EPS)


def _gelu(x):
    c = math.sqrt(2.0 / math.pi)
    return 0.5 * x * (1.0 + jnp.tanh(c * (x + 0.044715 * (x * x * x))))


def _rope(t, cos, s1, s2):
    return t * cos + pltpu.roll(t, LANES - HALF_ROPE, 1) * s1 + pltpu.roll(t, HALF_ROPE, 1) * s2


def _const_spec(shape):
    nd = len(shape)
    return pl.BlockSpec(shape, lambda *_: (0,) * nd, pipeline_mode=pl.Buffered(1))


def _proj_kernel(chunk_len, prompt, x_ref, gmix_ref, win_ref, gql_ref, gkv_ref, wuq_ref, wuk_ref,
                 wuv_ref, gq_ref, cos_ref, s1_ref, s2_ref, vn_ref, ws_ref, bst_ref, gog_ref, *rest):
    v_transposed = prompt
    emit_v = not prompt
    if prompt:
        (wo_ref, wu_ref, wd_ref, q_ref, k_ref, v_ref, ckv_ref, kpe_ref, gm_ref,
         wo_out, wu_out, wd_out, gate_scr, vgb_scr) = rest
        wo_out[...] = wo_ref[...].astype(BF16)
        wu_out[...] = wu_ref[...].astype(BF16)
        wd_out[...] = wd_ref[...].astype(BF16)
    else:
        q_ref, k_ref, v_ref, ckv_ref, kpe_ref, gm_ref, vg_ref, gate_scr, vgb_scr = rest
    tm = x_ref.shape[0]
    xn = (_rms(x_ref[...]) * gmix_ref[...]).astype(BF16)
    cos, s1, s2 = cos_ref[...], s1_ref[...], s2_ref[...]

    zq = _dot(xn, win_ref[:, C_Q0:C_Q1])
    qln = (_rms(zq) * gql_ref[...]).astype(BF16)
    qraw = _dot(qln, wuq_ref[...])
    for h in range(N_HEADS):
        lo = h * HEAD_PAD
        for rc in range(tm // HEAD_ROWS):
            rows = slice(rc * HEAD_ROWS, (rc + 1) * HEAD_ROWS)
            nope = qraw[rows, lo:lo + LANES]
            rp = qraw[rows, lo + LANES:lo + HEAD_PAD]
            r = lax.rsqrt(_sum_lanes(nope * nope + rp * rp) * (1.0 / QK_DIM) + EPS)
            rp = _rope(rp, cos[rows], s1[rows], s2[rows])
            q_ref[rows, lo:lo + LANES] = (nope * r * gq_ref[:, :LANES]).astype(BF16)
            q_ref[rows, lo + LANES:lo + HEAD_PAD] = (rp * r * gq_ref[:, LANES:]).astype(BF16)

    ckv = _rms(_dot(xn, win_ref[:, C_KV0:C_KV1])) * gkv_ref[...]
    ckv_ref[...] = ckv
    cb = ckv.astype(BF16)
    pe = _rope(_dot(xn, win_ref[:, C_PE0:C_PE1]), cos, s1, s2)
    kpe_ref[...] = pe.T[:ROPE_DIM, :] if prompt else pe[:, :ROPE_DIM]
    ss_pe = _sumsq(pe)
    knope = _dot(cb, wuk_ref[...])
    if v_transposed:
        v_ref[...] = _dot_nt(wuv_ref[...], cb).astype(BF16)
    else:
        v_ref[...] = _dot(cb, wuv_ref[...]).astype(BF16)
    for h in range(N_HEADS):
        lo = h * HEAD_PAD
        for rc in range(tm // HEAD_ROWS):
            rows = slice(rc * HEAD_ROWS, (rc + 1) * HEAD_ROWS)
            nope = knope[rows, h * LANES:(h + 1) * LANES]
            r = lax.rsqrt((_sumsq(nope) + ss_pe[rows]) * (1.0 / QK_DIM) + EPS)
            k_ref[rows, lo:lo + LANES] = (nope * r).astype(BF16)
            k_ref[rows, lo + LANES:lo + HEAD_PAD] = (pe[rows] * r).astype(BF16)

    gv = _gelu(_dot(xn, win_ref[:, C_V0:C_V1]))
    for g in range(GMLP_GROUPS):
        cols = slice(g * LANES, (g + 1) * LANES)
        for rc in range(tm // HEAD_ROWS):
            rows = slice(rc * HEAD_ROWS, (rc + 1) * HEAD_ROWS)
            vg = _rms(gv[rows, cols]) * vn_ref[g:g + 1, :]
            if emit_v:
                vg_ref[rows, cols] = vg
            vgb_scr[rows, cols] = vg.astype(BF16)
    u = _gelu(_dot(xn, win_ref[:, C_U0:C_U1]))
    row = lax.broadcasted_iota(jnp.int32, (GMLP_CHUNK, GMLP_CHUNK), 0)
    col = lax.broadcasted_iota(jnp.int32, (GMLP_CHUNK, GMLP_CHUNK), 1)
    causal = (row // chunk_len == col // chunk_len) & (col <= row)
    for g in range(GMLP_GROUPS):
        wm = jnp.where(causal, ws_ref[g], 0.0).astype(BF16)
        bias = bst_ref[:, g:g + 1]
        for c in range(tm // GMLP_CHUNK):
            rows = slice(c * GMLP_CHUNK, (c + 1) * GMLP_CHUNK)
            cols = slice(g * LANES, (g + 1) * LANES)
            s = _dot(wm, vgb_scr[rows, cols]) + bias
            gate_scr[rows, cols] = u[rows, cols] * s
    gm_ref[...] = (_rms(gate_scr[...]) * gog_ref[...]).astype(BF16)


def _proj(x2d, tabs, wp, chunk_len, tm, f32_weights=None):
    prompt = f32_weights is not None
    m = x2d.shape[0]
    steps = m // tm
    cos, s1, s2 = tabs
    tab_blocks = cos.shape[0] // tm
    row_spec = lambda w: pl.BlockSpec((tm, w), lambda i: (i, 0))
    tab_spec = pl.BlockSpec((tm, LANES), lambda i: (i % tab_blocks, 0))
    consts = [wp["gmix"], wp["w_in"], wp["gql"], wp["gkv"], wp["w_uq"], wp["w_uk"],
              wp["w_uv_t"] if prompt else wp["w_uv"], wp["gq"]]
    consts2 = [wp["v_norm"], wp["ws"], wp["bst"], wp["gog"]]
    in_specs = ([row_spec(D_MODEL)] + [_const_spec(c.shape) for c in consts]
                + [tab_spec] * 3 + [_const_spec(c.shape) for c in consts2])
    v_shape = (steps * ATTN_WIDTH, tm) if prompt else (m, ATTN_WIDTH)
    v_spec = pl.BlockSpec((ATTN_WIDTH, tm), lambda i: (i, 0)) if prompt else row_spec(ATTN_WIDTH)
    kpe_shape = (m // cos.shape[0], ROPE_DIM, cos.shape[0]) if prompt else (m, ROPE_DIM)
    kpe_spec = (pl.BlockSpec((None, ROPE_DIM, tm), lambda i: (i // tab_blocks, 0, i % tab_blocks))
                if prompt else row_spec(ROPE_DIM))
    out_shape = [jax.ShapeDtypeStruct((m, QK_WIDTH), BF16), jax.ShapeDtypeStruct((m, QK_WIDTH), BF16),
                 jax.ShapeDtypeStruct(v_shape, BF16), jax.ShapeDtypeStruct((m, KV_RANK), F32),
                 jax.ShapeDtypeStruct(kpe_shape, F32), jax.ShapeDtypeStruct((m, GMLP_WIDTH), BF16)]
    out_specs = [row_spec(QK_WIDTH), row_spec(QK_WIDTH), v_spec, row_spec(KV_RANK),
                 kpe_spec, row_spec(GMLP_WIDTH)]
    extra_in = []
    if prompt:
        w_out, w_up, w_down = f32_weights
        slabs = [pl.BlockSpec((D_MODEL // steps, D_MODEL), lambda i: (i, 0)),
                 pl.BlockSpec((D_MODEL, D_FF // steps), lambda i: (0, i)),
                 pl.BlockSpec((D_FF // steps, D_MODEL), lambda i: (i, 0))]
        extra_in = [w_out, w_up, w_down]
        in_specs += slabs
        out_shape += [jax.ShapeDtypeStruct(w.shape, BF16) for w in extra_in]
        out_specs += slabs
    else:
        out_shape.append(jax.ShapeDtypeStruct((m, GMLP_WIDTH), F32))
        out_specs.append(row_spec(GMLP_WIDTH))
    return pl.pallas_call(
        functools.partial(_proj_kernel, chunk_len, prompt),
        out_shape=out_shape,
        grid=(steps,),
        in_specs=in_specs,
        out_specs=out_specs,
        scratch_shapes=[pltpu.VMEM((tm, GMLP_WIDTH), F32), pltpu.VMEM((tm, GMLP_WIDTH), BF16)],
        compiler_params=pltpu.CompilerParams(dimension_semantics=("parallel",),
                                             vmem_limit_bytes=VMEM_LIMIT),
        name="proj",
    )(x2d, *consts, cos, s1, s2, *consts2, *extra_in)


NEG = -1e30
SOFTMAX_KEY_CHUNK = 64
KEY_BLOCKS_PER_STEP = 2
SCORE_LEAD = 4


def _attn_prompt_kernel(q_ref, k_ref, vt_ref, go_ref, o_ref, acc_scr, m_scr, l_scr, s_scr, p_scr):
    t = q_ref.shape[0]
    qi = pl.program_id(1)
    m_scr[...] = jnp.full(m_scr.shape, NEG, F32)
    l_scr[...] = jnp.zeros(l_scr.shape, F32)
    acc_scr[...] = jnp.zeros(acc_scr.shape, F32)

    def step(kb0, nblk, last_masked):
        nk = nblk * t
        if last_masked:
            key = lax.broadcasted_iota(jnp.int32, (t, t), 0)
            qry = lax.broadcasted_iota(jnp.int32, (t, t), 1)
            ok = (key // CHUNK) <= (qry // CHUNK)

        m8 = {}

        def scores(h):
            qs = slice(h * HEAD_PAD, (h + 1) * HEAD_PAD)
            mh = None
            for j in range(nblk):
                rows = pl.ds(pl.multiple_of((kb0 + j) * t, t), t)
                s = _dot_nt(k_ref[rows, qs], q_ref[:, qs])
                if last_masked and j == nblk - 1:
                    s = jnp.where(ok, s, NEG)
                s_scr[h, j * t:(j + 1) * t, :] = s
                mj = jnp.max(s.reshape(t // SUBLANES, SUBLANES, t), axis=0)
                mh = mj if mh is None else jnp.maximum(mh, mj)
            m8[h] = mh

        ones = jnp.ones((BF16_SUBLANES, nk), BF16)
        for h in range(min(SCORE_LEAD, N_HEADS)):
            scores(h)
        for h in range(N_HEADS):
            if h + SCORE_LEAD < N_HEADS:
                scores(h + SCORE_LEAD)
            hs = slice(h * V_DIM, (h + 1) * V_DIM)
            m_old = m_scr[h]
            m_new = jnp.maximum(m_old, jnp.max(m8[h], axis=0, keepdims=True))
            alpha = jnp.exp2(m_old - m_new)
            for c in range(nk // SOFTMAX_KEY_CHUNK):
                cs = slice(c * SOFTMAX_KEY_CHUNK, (c + 1) * SOFTMAX_KEY_CHUNK)
                p_scr[h, cs, :] = jnp.exp2(s_scr[h, cs, :] - m_new).astype(BF16)
            m_scr[h] = m_new
            vt = [vt_ref[pl.ds(pl.multiple_of((kb0 + j) * ATTN_WIDTH, ATTN_WIDTH) + h * V_DIM, V_DIM), :]
                  for j in range(nblk)]
            vt = vt[0] if nblk == 1 else jnp.concatenate(vt, axis=1)
            pv = _dot(jnp.concatenate([vt, ones], axis=0), p_scr[h, :nk, :])
            acc_scr[hs, :] = alpha * acc_scr[hs, :] + pv[:V_DIM]
            l_scr[h] = alpha * l_scr[h] + pv[V_DIM:V_DIM + 1]

    def body(j, carry):
        step(j * KEY_BLOCKS_PER_STEP, KEY_BLOCKS_PER_STEP, False)
        return carry

    lax.fori_loop(0, qi // KEY_BLOCKS_PER_STEP, body, 0)
    for rem in range(KEY_BLOCKS_PER_STEP):
        @pl.when(qi % KEY_BLOCKS_PER_STEP == rem)
        def _(rem=rem):
            step(qi - rem, rem + 1, True)

    for h in range(N_HEADS):
        hs = slice(h * V_DIM, (h + 1) * V_DIM)
        acc_scr[hs, :] = acc_scr[hs, :] / l_scr[h]
    o = acc_scr[...].T
    o_ref[...] = (_rms(o) * go_ref[...]).astype(BF16)


def _attn_prompt(q, k, vt, go, batch, seq, t):
    nq = seq // t
    assert vt.shape == (batch * nq * ATTN_WIDTH, t)
    tk = KEY_BLOCKS_PER_STEP * t
    return pl.pallas_call(
        _attn_prompt_kernel,
        out_shape=jax.ShapeDtypeStruct((batch * seq, ATTN_WIDTH), BF16),
        grid=(batch, nq),
        in_specs=[pl.BlockSpec((t, QK_WIDTH), lambda b, i: (b * nq + i, 0)),
                  pl.BlockSpec((seq, QK_WIDTH), lambda b, i: (b, 0)),
                  pl.BlockSpec((nq * ATTN_WIDTH, t), lambda b, i: (b, 0)),
                  _const_spec(go.shape)],
        out_specs=pl.BlockSpec((t, ATTN_WIDTH), lambda b, i: (b * nq + i, 0)),
        scratch_shapes=[pltpu.VMEM((ATTN_WIDTH, t), F32), pltpu.VMEM((N_HEADS, 1, t), F32),
                        pltpu.VMEM((N_HEADS, 1, t), F32), pltpu.VMEM((N_HEADS, tk, t), F32),
                        pltpu.VMEM((N_HEADS, tk, t), BF16)],
        compiler_params=pltpu.CompilerParams(dimension_semantics=("parallel", "arbitrary"),
                                             vmem_limit_bytes=VMEM_LIMIT),
        name="attn_prompt",
    )(q, k, vt, go)


def _attn_sample_kernel(past, c_ref, krt_ref, q_ref, kn_ref, vn_ref, wukt_ref, wuv_ref, go_ref,
                        o_ref, cb_scr, qa_scr, qr_scr, r_scr, sc_scr, sn_scr, pc_scr, acc_scr):
    nb = krt_ref.shape[0]
    t = q_ref.shape[0] // nb
    row = lax.broadcasted_iota(jnp.int32, (t, t), 0)
    col = lax.broadcasted_iota(jnp.int32, (t, t), 1)
    new_ok = ((past + col) // CHUNK) <= ((past + row) // CHUNK)
    denom = [[None] * N_HEADS for _ in range(nb)]
    p_new = [[None] * N_HEADS for _ in range(nb)]

    def norm_factors(j):
        cb_scr[j] = c_ref[j * past:(j + 1) * past, :].astype(BF16)
        krt = krt_ref[j]
        ss_pe = jnp.sum(krt * krt, axis=0, keepdims=True)
        knt = _dot_nt(wukt_ref[...], cb_scr[j])
        for h in range(N_HEADS):
            nope = knt[h * LANES:(h + 1) * LANES, :]
            ss = jnp.sum(nope * nope, axis=0, keepdims=True) + ss_pe
            r_scr[j, h] = lax.rsqrt(ss * (1.0 / QK_DIM) + EPS)

    def scores(j):
        qj = q_ref.at[j * t:(j + 1) * t, :]
        knj = kn_ref.at[j * t:(j + 1) * t, :]
        for h in range(N_HEADS):
            lo = h * HEAD_PAD
            rows = slice(h * t, (h + 1) * t)
            qa_scr[j, rows, :] = _dot(qj[:, lo:lo + NOPE_DIM],
                                      wukt_ref[h * NOPE_DIM:(h + 1) * NOPE_DIM, :]).astype(BF16)
            qr_scr[j, rows, :] = qj[:, lo + NOPE_DIM:lo + QK_DIM]
            sn_scr[j, h] = jnp.where(new_ok, _dot_nt(qj[:, lo:lo + HEAD_PAD], knj[:, lo:lo + HEAD_PAD]), NEG)
        sc_scr[j] = _dot_nt(qa_scr[j], cb_scr[j]) + _dot(qr_scr[j], krt_ref[j].astype(BF16))

    def softmax(j):
        for h in range(N_HEADS):
            s_c, s_n = sc_scr[j, h * t:(h + 1) * t, :] * r_scr[j, h], sn_scr[j, h]
            m = jnp.maximum(jnp.max(s_c, axis=-1, keepdims=True), jnp.max(s_n, axis=-1, keepdims=True))
            p_c = jnp.exp2(s_c - m)
            p_n = jnp.exp2(s_n - m)
            denom[j][h] = jnp.sum(p_c, axis=-1, keepdims=True) + jnp.sum(p_n, axis=-1, keepdims=True)
            p_new[j][h] = p_n.astype(BF16)
            pc_scr[j, h * t:(h + 1) * t, :] = p_c.astype(BF16)

    def values(j):
        vnj = vn_ref.at[j * t:(j + 1) * t, :]
        lat = _dot(pc_scr[j], cb_scr[j]).astype(BF16)
        for h in range(N_HEADS):
            vs = slice(h * V_DIM, (h + 1) * V_DIM)
            acc = _dot(lat[h * t:(h + 1) * t, :], wuv_ref[:, vs]) + _dot(p_new[j][h], vnj[:, vs])
            acc_scr[j, :, vs] = acc / denom[j][h]
        o_ref[j * t:(j + 1) * t, :] = (_rms(acc_scr[j]) * go_ref[...]).astype(BF16)

    stages = (norm_factors, scores, softmax, values)
    for tick in range(nb + len(stages) - 1):
        for j in range(nb):
            if 0 <= tick - j < len(stages):
                stages[tick - j](j)


def _attn_sample(c2d, krt, q, kn, vn, wp, batch, t, past, nb):
    consts = [wp["w_uk_t"], wp["w_uv"], wp["goa"]]
    ht = N_HEADS * t
    return pl.pallas_call(
        functools.partial(_attn_sample_kernel, past),
        out_shape=jax.ShapeDtypeStruct((batch * t, ATTN_WIDTH), BF16),
        grid=(batch // nb,),
        in_specs=[pl.BlockSpec((nb * past, KV_RANK), lambda b: (b, 0)),
                  pl.BlockSpec((nb, ROPE_DIM, past), lambda b: (b, 0, 0)),
                  pl.BlockSpec((nb * t, QK_WIDTH), lambda b: (b, 0)),
                  pl.BlockSpec((nb * t, QK_WIDTH), lambda b: (b, 0)),
                  pl.BlockSpec((nb * t, ATTN_WIDTH), lambda b: (b, 0))]
                 + [_const_spec(c.shape) for c in consts],
        out_specs=pl.BlockSpec((nb * t, ATTN_WIDTH), lambda b: (b, 0)),
        scratch_shapes=[pltpu.VMEM((nb, past, KV_RANK), BF16), pltpu.VMEM((nb, ht, KV_RANK), BF16),
                        pltpu.VMEM((nb, ht, ROPE_DIM), BF16), pltpu.VMEM((nb, N_HEADS, 1, past), F32),
                        pltpu.VMEM((nb, ht, past), F32), pltpu.VMEM((nb, N_HEADS, t, t), F32),
                        pltpu.VMEM((nb, ht, past), BF16), pltpu.VMEM((nb, t, ATTN_WIDTH), F32)],
        compiler_params=pltpu.CompilerParams(dimension_semantics=("parallel",),
                                             vmem_limit_bytes=VMEM_LIMIT),
        name="attn_sample",
    )(c2d, krt, q, kn, vn, *consts)


def _outproj_kernel(an_ref, gm_ref, x_ref, wo_ref, gffn_ref, h_ref, hn_ref):
    h = (x_ref[...] + _dot(an_ref[...], wo_ref[:ATTN_WIDTH, :])
         + _dot(gm_ref[...], wo_ref[ATTN_WIDTH:, :]))
    h_ref[...] = h
    hn_ref[...] = (_rms(h) * gffn_ref[...]).astype(BF16)


def _outproj(an, gm, x2d, wp, tm):
    m = x2d.shape[0]
    row_spec = lambda w: pl.BlockSpec((tm, w), lambda i: (i, 0))
    return pl.pallas_call(
        _outproj_kernel,
        out_shape=[jax.ShapeDtypeStruct((m, D_MODEL), F32), jax.ShapeDtypeStruct((m, D_MODEL), BF16)],
        grid=(m // tm,),
        in_specs=[row_spec(ATTN_WIDTH), row_spec(GMLP_WIDTH), row_spec(D_MODEL),
                  _const_spec(wp["w_out"].shape), _const_spec(wp["gffn"].shape)],
        out_specs=[row_spec(D_MODEL), row_spec(D_MODEL)],
        compiler_params=pltpu.CompilerParams(dimension_semantics=("parallel",),
                                             vmem_limit_bytes=VMEM_LIMIT),
        name="outproj",
    )(an, gm, x2d, wp["w_out"], wp["gffn"])


def _ffn_kernel(h_ref, hn_ref, wu_ref, wd_ref, y_ref):
    f = pl.program_id(1)
    slab = h_ref.shape[1]

    @pl.when(f == 0)
    def _():
        y_ref[...] = jnp.zeros(y_ref.shape, F32)

    for j in range(D_MODEL // slab):
        @pl.when(f == j)
        def _(j=j):
            y_ref[:, j * slab:(j + 1) * slab] += h_ref[...]

    a = jnp.maximum(_dot(hn_ref[...], wu_ref[...]), 0.0)
    y_ref[...] += _dot((a * a).astype(BF16), wd_ref[...])


def _ffn(h, hn, wp, tm, tf):
    m = h.shape[0]
    nf = D_FF // tf
    return pl.pallas_call(
        _ffn_kernel,
        out_shape=jax.ShapeDtypeStruct((m, D_MODEL), F32),
        grid=(m // tm, nf),
        in_specs=[pl.BlockSpec((tm, D_MODEL // nf), lambda i, f: (i, f)),
                  pl.BlockSpec((tm, D_MODEL), lambda i, f: (i, 0)),
                  pl.BlockSpec((D_MODEL, tf), lambda i, f: (0, f)),
                  pl.BlockSpec((tf, D_MODEL), lambda i, f: (f, 0))],
        out_specs=pl.BlockSpec((tm, D_MODEL), lambda i, f: (i, 0)),
        compiler_params=pltpu.CompilerParams(dimension_semantics=("parallel", "arbitrary"),
                                             vmem_limit_bytes=VMEM_LIMIT),
        name="ffn",
    )(h, hn, wp["w_up"], wp["w_down"])


def _rope_tables(pos):
    inv = ROPE_THETA ** (-np.arange(HALF_ROPE, dtype=np.float64) / HALF_ROPE)
    ang = np.asarray(pos, np.float64)[:, None] * inv[None, :]
    cos, sin = np.cos(ang), np.sin(ang)
    z = np.zeros_like(cos)
    z2 = np.zeros((ang.shape[0], LANES - ROPE_DIM))
    tabs = (np.concatenate([cos, cos, z2], axis=1), np.concatenate([-sin, z, z2], axis=1),
            np.concatenate([z, sin, z2], axis=1))
    return tuple(jnp.asarray(t, F32) for t in tabs)


def _head_gain(g_nope, g_rope, scale):
    pad = jnp.zeros((HEAD_PAD - QK_DIM,), F32)
    return (jnp.concatenate([g_nope, g_rope, g_rope, pad]) * scale)[None, :]


def _pad_w_in_kernel(wt_ref, o_ref):
    tr = wt_ref.shape[1]
    split = C_PE0 + ROPE_DIM
    o_ref[:, :C_PE0] = wt_ref[:C_PE0, :].T.astype(BF16)
    pe = wt_ref[C_PE0:C_PE1, :].T
    lane = lax.broadcasted_iota(jnp.int32, (tr, LANES), 1)
    o_ref[:, C_PE0:C_PE1] = jnp.where(lane < ROPE_DIM, pe, 0.0).astype(BF16)
    o_ref[:, C_PE1:] = wt_ref[split:, :].T.astype(BF16)


def _pad_w_in(w_in_t, tr):
    width, rows = w_in_t.shape
    assert width + LANES - ROPE_DIM == IN_PAD
    return pl.pallas_call(
        _pad_w_in_kernel,
        out_shape=jax.ShapeDtypeStruct((rows, IN_PAD), BF16),
        grid=(rows // tr,),
        in_specs=[pl.BlockSpec((width, tr), lambda i: (0, i))],
        out_specs=pl.BlockSpec((tr, IN_PAD), lambda i: (i, 0)),
        compiler_params=pltpu.CompilerParams(dimension_semantics=("parallel",)),
        name="pad_w_in",
    )(w_in_t)


def _prep_weights(norm_mix, w_in, q_lat_norm, kv_lat_norm, w_uq, w_uk, w_uv, q_norm_nope, q_norm_rope,
                  k_norm_nope, k_norm_rope, v_norm, w_spatial, b_spatial, out_norm_attn, out_norm_gmlp,
                  norm_ffn, chunk_len):
    w_in_p = _pad_w_in(w_in.T, PAD_ROWS)
    wq = w_uq.reshape(Q_RANK, N_HEADS, QK_DIM)
    wq = jnp.pad(wq, ((0, 0), (0, 0), (0, HEAD_PAD - QK_DIM))).reshape(Q_RANK, QK_WIDTH).astype(BF16)
    reps = GMLP_CHUNK // chunk_len
    ws = jnp.tile(w_spatial[:, :chunk_len, :chunk_len], (1, reps, reps))
    bst = jnp.tile(b_spatial[:, :chunk_len], (1, reps)).T
    return {
        "gmix": norm_mix[None, :], "w_in": w_in_p, "gql": q_lat_norm[None, :], "gkv": kv_lat_norm[None, :],
        "w_uq": wq, "w_uk": w_uk.astype(BF16), "w_uk_t": w_uk.T.astype(BF16),
        "w_uv": w_uv.astype(BF16), "w_uv_t": w_uv.T.astype(BF16),
        "gq": (_head_gain(q_norm_nope, q_norm_rope, QK_DIM ** -0.5 * math.log2(math.e))
               * _head_gain(k_norm_nope, k_norm_rope, 1.0)),
        "v_norm": v_norm, "ws": ws, "bst": bst, "gog": out_norm_gmlp[None, :],
        "goa": out_norm_attn[None, :], "gffn": norm_ffn[None, :],
    }


def _finish(an, gm, x2d, wp):
    h, hn = _outproj(an, gm, x2d, wp, OUTPROJ_ROWS)
    return _ffn(h, hn, wp, FFN_ROWS, FFN_COLS)


def kernel(x_prompt, x_sample, cache_c_kv, cache_k_rope, norm_mix, w_in, q_lat_norm, kv_lat_norm, w_uq, w_uk, w_uv, q_norm_nope, q_norm_rope, k_norm_nope, k_norm_rope, v_norm, w_spatial, b_spatial, out_norm_attn, out_norm_gmlp, w_out, norm_ffn, w_up, w_down):
    depth = w_in.shape[0]
    assert depth == 1
    batch, seq, _ = x_prompt.shape
    dec_batch, dec_seq, _ = x_sample.shape
    past = cache_c_kv.shape[2]
    assert past % CHUNK == 0 and dec_seq <= CHUNK and GMLP_CHUNK % dec_seq == 0
    n_prompt, n_sample = batch * seq, dec_batch * dec_seq
    assert seq % PROJ_ROWS == 0 and n_sample % PROJ_ROWS == 0 and PROJ_ROWS % dec_seq == 0
    assert n_prompt % FFN_ROWS == 0 and n_sample % FFN_ROWS == 0 and dec_batch % SAMPLE_BATCH_ROWS == 0

    weights = (norm_mix[0], w_in[0], q_lat_norm[0], kv_lat_norm[0], w_uq[0], w_uk[0], w_uv[0],
               q_norm_nope[0], q_norm_rope[0], k_norm_nope[0], k_norm_rope[0], v_norm[0], w_spatial[0],
               b_spatial[0], out_norm_attn[0], out_norm_gmlp[0], norm_ffn[0])
    wp = _prep_weights(*weights, chunk_len=GMLP_CHUNK)
    reps = GMLP_CHUNK // dec_seq
    ws_s = jnp.tile(w_spatial[0][:, :dec_seq, :dec_seq], (1, reps, reps))
    bst_s = jnp.tile(b_spatial[0][:, :dec_seq], (1, reps)).T

    xp = x_prompt.reshape(batch * seq, D_MODEL)
    tabs_p = _rope_tables(np.arange(seq))
    q, k, vt, ckv_p, kpe_p, gm, wo_b, wu_b, wd_b = _proj(
        xp, tabs_p, wp, GMLP_CHUNK, PROJ_ROWS, f32_weights=(w_out[0], w_up[0], w_down[0]))
    wp = dict(wp, w_out=wo_b, w_up=wu_b, w_down=wd_b)
    wp_s = dict(wp, ws=ws_s, bst=bst_s)
    an = _attn_prompt(q, k, vt, wp["goa"], batch, seq, PROJ_ROWS)
    y_p = _finish(an, gm, xp, wp)

    xs = x_sample.reshape(dec_batch * dec_seq, D_MODEL)
    tabs_s = _rope_tables(past + np.arange(PROJ_ROWS) % dec_seq)
    qs, kn, vn, ckv_s, kpe_s, gms, vg_s = _proj(xs, tabs_s, wp_s, dec_seq, PROJ_ROWS)
    ans = _attn_sample(cache_c_kv[0].reshape(dec_batch * past, KV_RANK),
                       jnp.swapaxes(cache_k_rope[0], 1, 2),
                       qs, kn, vn, wp, dec_batch, dec_seq, past, SAMPLE_BATCH_ROWS)
    y_s = _finish(ans, gms, xs, wp)

    return (y_p.reshape(batch, seq, D_MODEL),
            y_s.reshape(dec_batch, dec_seq, D_MODEL),
            ckv_p.reshape(1, batch, seq, KV_RANK),
            jnp.swapaxes(kpe_p, 1, 2)[None],
            ckv_s.reshape(1, dec_batch, dec_seq, KV_RANK),
            kpe_s.reshape(1, dec_batch, dec_seq, ROPE_DIM),
            vg_s.reshape(1, dec_batch, dec_seq, GMLP_WIDTH))
```

```python
import functools
import math

import jax
import jax.numpy as jnp
import numpy as np
from jax import lax
from jax.experimental import pallas as pl
from jax.experimental.pallas import tpu as pltpu

D_MODEL = 2048
CHUNK = 64
N_HEADS = 8
NOPE_DIM = 128
ROPE_DIM = 64
HALF_ROPE = ROPE_DIM // 2
QK_DIM = NOPE_DIM + ROPE_DIM
V_DIM = 128
ATTN_WIDTH = N_HEADS * V_DIM
Q_RANK = 512
KV_RANK = 256
GMLP_GROUPS = 8
GMLP_GROUP_DIM = 128
GMLP_WIDTH = GMLP_GROUPS * GMLP_GROUP_DIM
GMLP_CHUNK = 128
D_FF = 4 * D_MODEL
ROPE_THETA = 10000.0
EPS = 1e-6

LANES = 128
SUBLANES = 8
BF16_SUBLANES = 2 * SUBLANES
HEAD_PAD = 2 * LANES
QK_WIDTH = N_HEADS * HEAD_PAD
C_Q0, C_Q1 = 0, Q_RANK
C_KV0, C_KV1 = C_Q1, C_Q1 + KV_RANK
C_PE0, C_PE1 = C_KV1, C_KV1 + LANES
C_U0, C_U1 = C_PE1, C_PE1 + GMLP_WIDTH
C_V0, C_V1 = C_U1, C_U1 + GMLP_WIDTH
IN_PAD = C_V1
VMEM_LIMIT = 56 * 1024 * 1024

PROJ_ROWS = 256
HEAD_ROWS = 128
PAD_ROWS = 256
OUTPROJ_ROWS = 512
FFN_ROWS = 1024
FFN_COLS = 1024
SAMPLE_BATCH_ROWS = 4

BF16 = jnp.bfloat16
F32 = jnp.float32


def _dot(a, b):
    return jnp.dot(a, b, preferred_element_type=F32)


def _dot_nt(a, b):
    return lax.dot_general(a, b, (((1,), (1,)), ((), ())), preferred_element_type=F32)


def _sum_lanes(x):
    return jnp.sum(x, axis=-1, keepdims=True)


def _sumsq(x):
    return _sum_lanes(x * x)


def _rms(x):
    return x * lax.rsqrt(jnp.mean(x * x, axis=-1, keepdims=True) + EPS)


def _gelu(x):
    c = math.sqrt(2.0 / math.pi)
    return 0.5 * x * (1.0 + jnp.tanh(c * (x + 0.044715 * (x * x * x))))


def _rope(t, cos, s1, s2):
    return t * cos + pltpu.roll(t, LANES - HALF_ROPE, 1) * s1 + pltpu.roll(t, HALF_ROPE, 1) * s2


def _const_spec(shape):
    nd = len(shape)
    return pl.BlockSpec(shape, lambda *_: (0,) * nd, pipeline_mode=pl.Buffered(1))


def _proj_kernel(chunk_len, prompt, x_ref, gmix_ref, win_ref, gql_ref, gkv_ref, wuq_ref, wuk_ref,
                 wuv_ref, gq_ref, cos_ref, s1_ref, s2_ref, vn_ref, ws_ref, bst_ref, gog_ref, *rest):
    v_transposed = prompt
    emit_v = not prompt
    if prompt:
        (cost_ref, sint_ref, wo_ref, wu_ref, wd_ref, q_ref, k_ref, v_ref, ckv_ref, kpe_ref, gm_ref,
         wo_out, wu_out, wd_out, gate_scr, vgb_scr) = rest
        wo_out[...] = wo_ref[...].astype(BF16)
        wu_out[...] = wu_ref[...].astype(BF16)
        wd_out[...] = wd_ref[...].astype(BF16)
    else:
        q_ref, k_ref, v_ref, ckv_ref, kpe_ref, gm_ref, vg_ref, gate_scr, vgb_scr = rest
    tm = x_ref.shape[0]
    xn = (_rms(x_ref[...]) * gmix_ref[...]).astype(BF16)
    cos, s1, s2 = cos_ref[...], s1_ref[...], s2_ref[...]

    zq = _dot(xn, win_ref[:, C_Q0:C_Q1])
    qln = (_rms(zq) * gql_ref[...]).astype(BF16)
    if prompt:
        qt = _dot_nt(wuq_ref[...], qln)
        cos_t, sin_t = cost_ref[...], sint_ref[...]
        x1_lo, x2_lo = NOPE_DIM, NOPE_DIM + HALF_ROPE
        zeros = jnp.zeros((HEAD_PAD - QK_DIM, tm), BF16)
        for h in range(N_HEADS):
            lo = h * HEAD_PAD
            nope = qt[lo:lo + NOPE_DIM]
            x1 = qt[lo + x1_lo:lo + x2_lo]
            x2 = qt[lo + x2_lo:lo + QK_DIM]
            ss = (jnp.sum(nope * nope, axis=0, keepdims=True)
                  + jnp.sum(x1 * x1 + x2 * x2, axis=0, keepdims=True))
            r = lax.rsqrt(ss * (1.0 / QK_DIM) + EPS)
            q_ref[lo:lo + NOPE_DIM, :] = (nope * r * gq_ref[:NOPE_DIM]).astype(BF16)
            q_ref[lo + x1_lo:lo + x2_lo, :] = (
                (x1 * cos_t - x2 * sin_t) * r * gq_ref[x1_lo:x2_lo]).astype(BF16)
            q_ref[lo + x2_lo:lo + QK_DIM, :] = (
                (x2 * cos_t + x1 * sin_t) * r * gq_ref[x2_lo:QK_DIM]).astype(BF16)
            q_ref[lo + QK_DIM:lo + HEAD_PAD, :] = zeros
    else:
        qraw = _dot(qln, wuq_ref[...])
        for h in range(N_HEADS):
            lo = h * HEAD_PAD
            for rc in range(tm // HEAD_ROWS):
                rows = slice(rc * HEAD_ROWS, (rc + 1) * HEAD_ROWS)
                nope = qraw[rows, lo:lo + LANES]
                rp = qraw[rows, lo + LANES:lo + HEAD_PAD]
                r = lax.rsqrt(_sum_lanes(nope * nope + rp * rp) * (1.0 / QK_DIM) + EPS)
                rp = _rope(rp, cos[rows], s1[rows], s2[rows])
                q_ref[rows, lo:lo + LANES] = (nope * r * gq_ref[:, :LANES]).astype(BF16)
                q_ref[rows, lo + LANES:lo + HEAD_PAD] = (rp * r * gq_ref[:, LANES:]).astype(BF16)

    ckv = _rms(_dot(xn, win_ref[:, C_KV0:C_KV1])) * gkv_ref[...]
    ckv_ref[...] = ckv
    cb = ckv.astype(BF16)
    pe = _rope(_dot(xn, win_ref[:, C_PE0:C_PE1]), cos, s1, s2)
    kpe_ref[...] = pe.T[:ROPE_DIM, :] if prompt else pe[:, :ROPE_DIM]
    ss_pe = _sumsq(pe)
    knope = _dot(cb, wuk_ref[...])
    if v_transposed:
        v_ref[...] = _dot_nt(wuv_ref[...], cb).astype(BF16)
    else:
        v_ref[...] = _dot(cb, wuv_ref[...]).astype(BF16)
    for h in range(N_HEADS):
        lo = h * HEAD_PAD
        for rc in range(tm // HEAD_ROWS):
            rows = slice(rc * HEAD_ROWS, (rc + 1) * HEAD_ROWS)
            nope = knope[rows, h * LANES:(h + 1) * LANES]
            r = lax.rsqrt((_sumsq(nope) + ss_pe[rows]) * (1.0 / QK_DIM) + EPS)
            k_ref[rows, lo:lo + LANES] = (nope * r).astype(BF16)
            k_ref[rows, lo + LANES:lo + HEAD_PAD] = (pe[rows] * r).astype(BF16)

    gv = _gelu(_dot(xn, win_ref[:, C_V0:C_V1]))
    for g in range(GMLP_GROUPS):
        cols = slice(g * LANES, (g + 1) * LANES)
        for rc in range(tm // HEAD_ROWS):
            rows = slice(rc * HEAD_ROWS, (rc + 1) * HEAD_ROWS)
            vg = _rms(gv[rows, cols]) * vn_ref[g:g + 1, :]
            if emit_v:
                vg_ref[rows, cols] = vg
            vgb_scr[rows, cols] = vg.astype(BF16)
    u = _gelu(_dot(xn, win_ref[:, C_U0:C_U1]))
    row = lax.broadcasted_iota(jnp.int32, (GMLP_CHUNK, GMLP_CHUNK), 0)
    col = lax.broadcasted_iota(jnp.int32, (GMLP_CHUNK, GMLP_CHUNK), 1)
    causal = (row // chunk_len == col // chunk_len) & (col <= row)
    for g in range(GMLP_GROUPS):
        wm = jnp.where(causal, ws_ref[g], 0.0).astype(BF16)
        bias = bst_ref[:, g:g + 1]
        for c in range(tm // GMLP_CHUNK):
            rows = slice(c * GMLP_CHUNK, (c + 1) * GMLP_CHUNK)
            cols = slice(g * LANES, (g + 1) * LANES)
            s = _dot(wm, vgb_scr[rows, cols]) + bias
            gate_scr[rows, cols] = u[rows, cols] * s
    gm_ref[...] = (_rms(gate_scr[...]) * gog_ref[...]).astype(BF16)


def _proj(x2d, tabs, wp, chunk_len, tm, f32_weights=None, tabs_t=None):
    prompt = f32_weights is not None
    m = x2d.shape[0]
    steps = m // tm
    cos, s1, s2 = tabs
    tab_blocks = cos.shape[0] // tm
    row_spec = lambda w: pl.BlockSpec((tm, w), lambda i: (i, 0))
    tab_spec = pl.BlockSpec((tm, LANES), lambda i: (i % tab_blocks, 0))
    consts = [wp["gmix"], wp["w_in"], wp["gql"], wp["gkv"], wp["w_uq_t"] if prompt else wp["w_uq"],
              wp["w_uk"], wp["w_uv_t"] if prompt else wp["w_uv"], wp["gq_t"] if prompt else wp["gq"]]
    consts2 = [wp["v_norm"], wp["ws"], wp["bst"], wp["gog"]]
    in_specs = ([row_spec(D_MODEL)] + [_const_spec(c.shape) for c in consts]
                + [tab_spec] * 3 + [_const_spec(c.shape) for c in consts2])
    q_shape = (steps * QK_WIDTH, tm) if prompt else (m, QK_WIDTH)
    q_spec = pl.BlockSpec((QK_WIDTH, tm), lambda i: (i, 0)) if prompt else row_spec(QK_WIDTH)
    v_shape = (steps * ATTN_WIDTH, tm) if prompt else (m, ATTN_WIDTH)
    v_spec = pl.BlockSpec((ATTN_WIDTH, tm), lambda i: (i, 0)) if prompt else row_spec(ATTN_WIDTH)
    kpe_shape = (m // cos.shape[0], ROPE_DIM, cos.shape[0]) if prompt else (m, ROPE_DIM)
    kpe_spec = (pl.BlockSpec((None, ROPE_DIM, tm), lambda i: (i // tab_blocks, 0, i % tab_blocks))
                if prompt else row_spec(ROPE_DIM))
    out_shape = [jax.ShapeDtypeStruct(q_shape, BF16), jax.ShapeDtypeStruct((m, QK_WIDTH), BF16),
                 jax.ShapeDtypeStruct(v_shape, BF16), jax.ShapeDtypeStruct((m, KV_RANK), F32),
                 jax.ShapeDtypeStruct(kpe_shape, F32), jax.ShapeDtypeStruct((m, GMLP_WIDTH), BF16)]
    out_specs = [q_spec, row_spec(QK_WIDTH), v_spec, row_spec(KV_RANK),
                 kpe_spec, row_spec(GMLP_WIDTH)]
    extra_in = []
    if prompt:
        tab_t_spec = pl.BlockSpec((HALF_ROPE, tm), lambda i: (0, i % tab_blocks))
        w_out, w_up, w_down = f32_weights
        slabs = [pl.BlockSpec((D_MODEL // steps, D_MODEL), lambda i: (i, 0)),
                 pl.BlockSpec((D_MODEL, D_FF // steps), lambda i: (0, i)),
                 pl.BlockSpec((D_FF // steps, D_MODEL), lambda i: (i, 0))]
        extra_in = [*tabs_t, w_out, w_up, w_down]
        in_specs += [tab_t_spec, tab_t_spec] + slabs
        out_shape += [jax.ShapeDtypeStruct(w.shape, BF16) for w in f32_weights]
        out_specs += slabs
    else:
        out_shape.append(jax.ShapeDtypeStruct((m, GMLP_WIDTH), F32))
        out_specs.append(row_spec(GMLP_WIDTH))
    return pl.pallas_call(
        functools.partial(_proj_kernel, chunk_len, prompt),
        out_shape=out_shape,
        grid=(steps,),
        in_specs=in_specs,
        out_specs=out_specs,
        scratch_shapes=[pltpu.VMEM((tm, GMLP_WIDTH), F32), pltpu.VMEM((tm, GMLP_WIDTH), BF16)],
        compiler_params=pltpu.CompilerParams(dimension_semantics=("parallel",),
                                             vmem_limit_bytes=VMEM_LIMIT),
        name="proj",
    )(x2d, *consts, cos, s1, s2, *consts2, *extra_in)


NEG = -1e30
SOFTMAX_KEY_CHUNK = 64
KEY_BLOCKS_PER_STEP = 2
SCORE_LEAD = 4


def _attn_prompt_kernel(q_ref, k_ref, vt_ref, go_ref, o_ref, acc_scr, m_scr, l_scr, s_scr, p_scr):
    t = q_ref.shape[1]
    qi = pl.program_id(1)
    m_scr[...] = jnp.full(m_scr.shape, NEG, F32)
    l_scr[...] = jnp.zeros(l_scr.shape, F32)
    acc_scr[...] = jnp.zeros(acc_scr.shape, F32)

    def step(kb0, nblk, last_masked):
        nk = nblk * t
        if last_masked:
            key = lax.broadcasted_iota(jnp.int32, (t, t), 0)
            qry = lax.broadcasted_iota(jnp.int32, (t, t), 1)
            ok = (key // CHUNK) <= (qry // CHUNK)

        m8 = {}

        def scores(h):
            qs = slice(h * HEAD_PAD, (h + 1) * HEAD_PAD)
            mh = None
            for j in range(nblk):
                rows = pl.ds(pl.multiple_of((kb0 + j) * t, t), t)
                s = _dot(k_ref[rows, qs], q_ref[qs, :])
                if last_masked and j == nblk - 1:
                    s = jnp.where(ok, s, NEG)
                s_scr[h, j * t:(j + 1) * t, :] = s
                mj = jnp.max(s.reshape(t // SUBLANES, SUBLANES, t), axis=0)
                mh = mj if mh is None else jnp.maximum(mh, mj)
            m8[h] = mh

        ones = jnp.ones((BF16_SUBLANES, nk), BF16)
        for h in range(min(SCORE_LEAD, N_HEADS)):
            scores(h)
        for h in range(N_HEADS):
            if h + SCORE_LEAD < N_HEADS:
                scores(h + SCORE_LEAD)
            hs = slice(h * V_DIM, (h + 1) * V_DIM)
            m_old = m_scr[h]
            m_new = jnp.maximum(m_old, jnp.max(m8[h], axis=0, keepdims=True))
            alpha = jnp.exp2(m_old - m_new)
            for c in range(nk // SOFTMAX_KEY_CHUNK):
                cs = slice(c * SOFTMAX_KEY_CHUNK, (c + 1) * SOFTMAX_KEY_CHUNK)
                p_scr[h, cs, :] = jnp.exp2(s_scr[h, cs, :] - m_new).astype(BF16)
            m_scr[h] = m_new
            vt = [vt_ref[pl.ds(pl.multiple_of((kb0 + j) * ATTN_WIDTH, ATTN_WIDTH) + h * V_DIM, V_DIM), :]
                  for j in range(nblk)]
            vt = vt[0] if nblk == 1 else jnp.concatenate(vt, axis=1)
            pv = _dot(jnp.concatenate([vt, ones], axis=0), p_scr[h, :nk, :])
            acc_scr[hs, :] = alpha * acc_scr[hs, :] + pv[:V_DIM]
            l_scr[h] = alpha * l_scr[h] + pv[V_DIM:V_DIM + 1]

    def body(j, carry):
        step(j * KEY_BLOCKS_PER_STEP, KEY_BLOCKS_PER_STEP, False)
        return carry

    lax.fori_loop(0, qi // KEY_BLOCKS_PER_STEP, body, 0)
    for rem in range(KEY_BLOCKS_PER_STEP):
        @pl.when(qi % KEY_BLOCKS_PER_STEP == rem)
        def _(rem=rem):
            step(qi - rem, rem + 1, True)

    for h in range(N_HEADS):
        hs = slice(h * V_DIM, (h + 1) * V_DIM)
        acc_scr[hs, :] = acc_scr[hs, :] / l_scr[h]
    o = acc_scr[...].T
    o_ref[...] = (_rms(o) * go_ref[...]).astype(BF16)


def _attn_prompt(q, k, vt, go, batch, seq, t):
    nq = seq // t
    assert vt.shape == (batch * nq * ATTN_WIDTH, t)
    tk = KEY_BLOCKS_PER_STEP * t
    return pl.pallas_call(
        _attn_prompt_kernel,
        out_shape=jax.ShapeDtypeStruct((batch * seq, ATTN_WIDTH), BF16),
        grid=(batch, nq),
        in_specs=[pl.BlockSpec((QK_WIDTH, t), lambda b, i: (b * nq + i, 0)),
                  pl.BlockSpec((seq, QK_WIDTH), lambda b, i: (b, 0)),
                  pl.BlockSpec((nq * ATTN_WIDTH, t), lambda b, i: (b, 0)),
                  _const_spec(go.shape)],
        out_specs=pl.BlockSpec((t, ATTN_WIDTH), lambda b, i: (b * nq + i, 0)),
        scratch_shapes=[pltpu.VMEM((ATTN_WIDTH, t), F32), pltpu.VMEM((N_HEADS, 1, t), F32),
                        pltpu.VMEM((N_HEADS, 1, t), F32), pltpu.VMEM((N_HEADS, tk, t), F32),
                        pltpu.VMEM((N_HEADS, tk, t), BF16)],
        compiler_params=pltpu.CompilerParams(dimension_semantics=("parallel", "arbitrary"),
                                             vmem_limit_bytes=VMEM_LIMIT),
        name="attn_prompt",
    )(q, k, vt, go)


def _attn_sample_kernel(past, c_ref, krt_ref, q_ref, kn_ref, vn_ref, wukt_ref, wuv_ref, go_ref,
                        o_ref, cb_scr, qa_scr, qr_scr, r_scr, sc_scr, sn_scr, pc_scr, acc_scr):
    nb = krt_ref.shape[0]
    t = q_ref.shape[0] // nb
    row = lax.broadcasted_iota(jnp.int32, (t, t), 0)
    col = lax.broadcasted_iota(jnp.int32, (t, t), 1)
    new_ok = ((past + col) // CHUNK) <= ((past + row) // CHUNK)
    denom = [[None] * N_HEADS for _ in range(nb)]
    p_new = [[None] * N_HEADS for _ in range(nb)]

    def norm_factors(j):
        cb_scr[j] = c_ref[j * past:(j + 1) * past, :].astype(BF16)
        krt = krt_ref[j]
        ss_pe = jnp.sum(krt * krt, axis=0, keepdims=True)
        knt = _dot_nt(wukt_ref[...], cb_scr[j])
        for h in range(N_HEADS):
            nope = knt[h * LANES:(h + 1) * LANES, :]
            ss = jnp.sum(nope * nope, axis=0, keepdims=True) + ss_pe
            r_scr[j, h] = lax.rsqrt(ss * (1.0 / QK_DIM) + EPS)

    def scores(j):
        qj = q_ref.at[j * t:(j + 1) * t, :]
        knj = kn_ref.at[j * t:(j + 1) * t, :]
        for h in range(N_HEADS):
            lo = h * HEAD_PAD
            rows = slice(h * t, (h + 1) * t)
            qa_scr[j, rows, :] = _dot(qj[:, lo:lo + NOPE_DIM],
                                      wukt_ref[h * NOPE_DIM:(h + 1) * NOPE_DIM, :]).astype(BF16)
            qr_scr[j, rows, :] = qj[:, lo + NOPE_DIM:lo + QK_DIM]
            sn_scr[j, h] = jnp.where(new_ok, _dot_nt(qj[:, lo:lo + HEAD_PAD], knj[:, lo:lo + HEAD_PAD]), NEG)
        sc_scr[j] = _dot_nt(qa_scr[j], cb_scr[j]) + _dot(qr_scr[j], krt_ref[j].astype(BF16))

    def softmax(j):
        for h in range(N_HEADS):
            s_c, s_n = sc_scr[j, h * t:(h + 1) * t, :] * r_scr[j, h], sn_scr[j, h]
            m = jnp.maximum(jnp.max(s_c, axis=-1, keepdims=True), jnp.max(s_n, axis=-1, keepdims=True))
            p_c = jnp.exp2(s_c - m)
            p_n = jnp.exp2(s_n - m)
            denom[j][h] = jnp.sum(p_c, axis=-1, keepdims=True) + jnp.sum(p_n, axis=-1, keepdims=True)
            p_new[j][h] = p_n.astype(BF16)
            pc_scr[j, h * t:(h + 1) * t, :] = p_c.astype(BF16)

    def values(j):
        vnj = vn_ref.at[j * t:(j + 1) * t, :]
        lat = _dot(pc_scr[j], cb_scr[j]).astype(BF16)
        for h in range(N_HEADS):
            vs = slice(h * V_DIM, (h + 1) * V_DIM)
            acc = _dot(lat[h * t:(h + 1) * t, :], wuv_ref[:, vs]) + _dot(p_new[j][h], vnj[:, vs])
            acc_scr[j, :, vs] = acc / denom[j][h]
        o_ref[j * t:(j + 1) * t, :] = (_rms(acc_scr[j]) * go_ref[...]).astype(BF16)

    stages = (norm_factors, scores, softmax, values)
    for tick in range(nb + len(stages) - 1):
        for j in range(nb):
            if 0 <= tick - j < len(stages):
                stages[tick - j](j)


def _attn_sample(c2d, krt, q, kn, vn, wp, batch, t, past, nb):
    consts = [wp["w_uk_t"], wp["w_uv"], wp["goa"]]
    ht = N_HEADS * t
    return pl.pallas_call(
        functools.partial(_attn_sample_kernel, past),
        out_shape=jax.ShapeDtypeStruct((batch * t, ATTN_WIDTH), BF16),
        grid=(batch // nb,),
        in_specs=[pl.BlockSpec((nb * past, KV_RANK), lambda b: (b, 0)),
                  pl.BlockSpec((nb, ROPE_DIM, past), lambda b: (b, 0, 0)),
                  pl.BlockSpec((nb * t, QK_WIDTH), lambda b: (b, 0)),
                  pl.BlockSpec((nb * t, QK_WIDTH), lambda b: (b, 0)),
                  pl.BlockSpec((nb * t, ATTN_WIDTH), lambda b: (b, 0))]
                 + [_const_spec(c.shape) for c in consts],
        out_specs=pl.BlockSpec((nb * t, ATTN_WIDTH), lambda b: (b, 0)),
        scratch_shapes=[pltpu.VMEM((nb, past, KV_RANK), BF16), pltpu.VMEM((nb, ht, KV_RANK), BF16),
                        pltpu.VMEM((nb, ht, ROPE_DIM), BF16), pltpu.VMEM((nb, N_HEADS, 1, past), F32),
                        pltpu.VMEM((nb, ht, past), F32), pltpu.VMEM((nb, N_HEADS, t, t), F32),
                        pltpu.VMEM((nb, ht, past), BF16), pltpu.VMEM((nb, t, ATTN_WIDTH), F32)],
        compiler_params=pltpu.CompilerParams(dimension_semantics=("parallel",),
                                             vmem_limit_bytes=VMEM_LIMIT),
        name="attn_sample",
    )(c2d, krt, q, kn, vn, *consts)


def _outproj_kernel(an_ref, gm_ref, x_ref, wo_ref, gffn_ref, h_ref, hn_ref):
    h = (x_ref[...] + _dot(an_ref[...], wo_ref[:ATTN_WIDTH, :])
         + _dot(gm_ref[...], wo_ref[ATTN_WIDTH:, :]))
    h_ref[...] = h
    hn_ref[...] = (_rms(h) * gffn_ref[...]).astype(BF16)


def _outproj(an, gm, x2d, wp, tm):
    m = x2d.shape[0]
    row_spec = lambda w: pl.BlockSpec((tm, w), lambda i: (i, 0))
    return pl.pallas_call(
        _outproj_kernel,
        out_shape=[jax.ShapeDtypeStruct((m, D_MODEL), F32), jax.ShapeDtypeStruct((m, D_MODEL), BF16)],
        grid=(m // tm,),
        in_specs=[row_spec(ATTN_WIDTH), row_spec(GMLP_WIDTH), row_spec(D_MODEL),
                  _const_spec(wp["w_out"].shape), _const_spec(wp["gffn"].shape)],
        out_specs=[row_spec(D_MODEL), row_spec(D_MODEL)],
        compiler_params=pltpu.CompilerParams(dimension_semantics=("parallel",),
                                             vmem_limit_bytes=VMEM_LIMIT),
        name="outproj",
    )(an, gm, x2d, wp["w_out"], wp["gffn"])


def _ffn_kernel(h_ref, hn_ref, wu_ref, wd_ref, y_ref):
    f = pl.program_id(1)
    slab = h_ref.shape[1]

    @pl.when(f == 0)
    def _():
        y_ref[...] = jnp.zeros(y_ref.shape, F32)

    for j in range(D_MODEL // slab):
        @pl.when(f == j)
        def _(j=j):
            y_ref[:, j * slab:(j + 1) * slab] += h_ref[...]

    a = jnp.maximum(_dot(hn_ref[...], wu_ref[...]), 0.0)
    y_ref[...] += _dot((a * a).astype(BF16), wd_ref[...])


def _ffn(h, hn, wp, tm, tf):
    m = h.shape[0]
    nf = D_FF // tf
    return pl.pallas_call(
        _ffn_kernel,
        out_shape=jax.ShapeDtypeStruct((m, D_MODEL), F32),
        grid=(m // tm, nf),
        in_specs=[pl.BlockSpec((tm, D_MODEL // nf), lambda i, f: (i, f)),
                  pl.BlockSpec((tm, D_MODEL), lambda i, f: (i, 0)),
                  pl.BlockSpec((D_MODEL, tf), lambda i, f: (0, f)),
                  pl.BlockSpec((tf, D_MODEL), lambda i, f: (f, 0))],
        out_specs=pl.BlockSpec((tm, D_MODEL), lambda i, f: (i, 0)),
        compiler_params=pltpu.CompilerParams(dimension_semantics=("parallel", "arbitrary"),
                                             vmem_limit_bytes=VMEM_LIMIT),
        name="ffn",
    )(h, hn, wp["w_up"], wp["w_down"])


def _rope_tables(pos):
    inv = ROPE_THETA ** (-np.arange(HALF_ROPE, dtype=np.float64) / HALF_ROPE)
    ang = np.asarray(pos, np.float64)[:, None] * inv[None, :]
    cos, sin = np.cos(ang), np.sin(ang)
    z = np.zeros_like(cos)
    z2 = np.zeros((ang.shape[0], LANES - ROPE_DIM))
    tabs = (np.concatenate([cos, cos, z2], axis=1), np.concatenate([-sin, z, z2], axis=1),
            np.concatenate([z, sin, z2], axis=1))
    tabs_t = (cos.T, sin.T)
    return tuple(jnp.asarray(t, F32) for t in tabs), tuple(jnp.asarray(t, F32) for t in tabs_t)


def _head_gain(g_nope, g_rope, scale):
    pad = jnp.zeros((HEAD_PAD - QK_DIM,), F32)
    return (jnp.concatenate([g_nope, g_rope, g_rope, pad]) * scale)[None, :]


def _pad_w_in_kernel(wt_ref, o_ref):
    tr = wt_ref.shape[1]
    split = C_PE0 + ROPE_DIM
    o_ref[:, :C_PE0] = wt_ref[:C_PE0, :].T.astype(BF16)
    pe = wt_ref[C_PE0:C_PE1, :].T
    lane = lax.broadcasted_iota(jnp.int32, (tr, LANES), 1)
    o_ref[:, C_PE0:C_PE1] = jnp.where(lane < ROPE_DIM, pe, 0.0).astype(BF16)
    o_ref[:, C_PE1:] = wt_ref[split:, :].T.astype(BF16)


def _pad_w_in(w_in_t, tr):
    width, rows = w_in_t.shape
    assert width + LANES - ROPE_DIM == IN_PAD
    return pl.pallas_call(
        _pad_w_in_kernel,
        out_shape=jax.ShapeDtypeStruct((rows, IN_PAD), BF16),
        grid=(rows // tr,),
        in_specs=[pl.BlockSpec((width, tr), lambda i: (0, i))],
        out_specs=pl.BlockSpec((tr, IN_PAD), lambda i: (i, 0)),
        compiler_params=pltpu.CompilerParams(dimension_semantics=("parallel",)),
        name="pad_w_in",
    )(w_in_t)


def _prep_weights(norm_mix, w_in, q_lat_norm, kv_lat_norm, w_uq, w_uk, w_uv, q_norm_nope, q_norm_rope,
                  k_norm_nope, k_norm_rope, v_norm, w_spatial, b_spatial, out_norm_attn, out_norm_gmlp,
                  norm_ffn, chunk_len):
    w_in_p = _pad_w_in(w_in.T, PAD_ROWS)
    wq = w_uq.reshape(Q_RANK, N_HEADS, QK_DIM)
    wq = jnp.pad(wq, ((0, 0), (0, 0), (0, HEAD_PAD - QK_DIM))).reshape(Q_RANK, QK_WIDTH).astype(BF16)
    reps = GMLP_CHUNK // chunk_len
    ws = jnp.tile(w_spatial[:, :chunk_len, :chunk_len], (1, reps, reps))
    bst = jnp.tile(b_spatial[:, :chunk_len], (1, reps)).T
    gq = (_head_gain(q_norm_nope, q_norm_rope, QK_DIM ** -0.5 * math.log2(math.e))
          * _head_gain(k_norm_nope, k_norm_rope, 1.0))
    return {
        "gmix": norm_mix[None, :], "w_in": w_in_p, "gql": q_lat_norm[None, :], "gkv": kv_lat_norm[None, :],
        "w_uq": wq, "w_uq_t": wq.T, "w_uk": w_uk.astype(BF16), "w_uk_t": w_uk.T.astype(BF16),
        "w_uv": w_uv.astype(BF16), "w_uv_t": w_uv.T.astype(BF16),
        "gq": gq, "gq_t": jnp.broadcast_to(gq.T, (HEAD_PAD, PROJ_ROWS)),
        "v_norm": v_norm, "ws": ws, "bst": bst, "gog": out_norm_gmlp[None, :],
        "goa": out_norm_attn[None, :], "gffn": norm_ffn[None, :],
    }


def _finish(an, gm, x2d, wp):
    h, hn = _outproj(an, gm, x2d, wp, OUTPROJ_ROWS)
    return _ffn(h, hn, wp, FFN_ROWS, FFN_COLS)


def kernel(x_prompt, x_sample, cache_c_kv, cache_k_rope, norm_mix, w_in, q_lat_norm, kv_lat_norm, w_uq, w_uk, w_uv, q_norm_nope, q_norm_rope, k_norm_nope, k_norm_rope, v_norm, w_spatial, b_spatial, out_norm_attn, out_norm_gmlp, w_out, norm_ffn, w_up, w_down):
    depth = w_in.shape[0]
    assert depth == 1
    batch, seq, _ = x_prompt.shape
    dec_batch, dec_seq, _ = x_sample.shape
    past = cache_c_kv.shape[2]
    assert past % CHUNK == 0 and dec_seq <= CHUNK and GMLP_CHUNK % dec_seq == 0
    n_prompt, n_sample = batch * seq, dec_batch * dec_seq
    assert seq % PROJ_ROWS == 0 and n_sample % PROJ_ROWS == 0 and PROJ_ROWS % dec_seq == 0
    assert n_prompt % FFN_ROWS == 0 and n_sample % FFN_ROWS == 0 and dec_batch % SAMPLE_BATCH_ROWS == 0

    weights = (norm_mix[0], w_in[0], q_lat_norm[0], kv_lat_norm[0], w_uq[0], w_uk[0], w_uv[0],
               q_norm_nope[0], q_norm_rope[0], k_norm_nope[0], k_norm_rope[0], v_norm[0], w_spatial[0],
               b_spatial[0], out_norm_attn[0], out_norm_gmlp[0], norm_ffn[0])
    wp = _prep_weights(*weights, chunk_len=GMLP_CHUNK)
    reps = GMLP_CHUNK // dec_seq
    ws_s = jnp.tile(w_spatial[0][:, :dec_seq, :dec_seq], (1, reps, reps))
    bst_s = jnp.tile(b_spatial[0][:, :dec_seq], (1, reps)).T

    xp = x_prompt.reshape(batch * seq, D_MODEL)
    tabs_p, tabs_pt = _rope_tables(np.arange(seq))
    qt, k, vt, ckv_p, kpe_p, gm, wo_b, wu_b, wd_b = _proj(
        xp, tabs_p, wp, GMLP_CHUNK, PROJ_ROWS, f32_weights=(w_out[0], w_up[0], w_down[0]),
        tabs_t=tabs_pt)
    wp = dict(wp, w_out=wo_b, w_up=wu_b, w_down=wd_b)
    wp_s = dict(wp, ws=ws_s, bst=bst_s)
    an = _attn_prompt(qt, k, vt, wp["goa"], batch, seq, PROJ_ROWS)
    y_p = _finish(an, gm, xp, wp)

    xs = x_sample.reshape(dec_batch * dec_seq, D_MODEL)
    tabs_s, _ = _rope_tables(past + np.arange(PROJ_ROWS) % dec_seq)
    qs, kn, vn, ckv_s, kpe_s, gms, vg_s = _proj(xs, tabs_s, wp_s, dec_seq, PROJ_ROWS)
    ans = _attn_sample(cache_c_kv[0].reshape(dec_batch * past, KV_RANK),
                       jnp.swapaxes(cache_k_rope[0], 1, 2),
                       qs, kn, vn, wp, dec_batch, dec_seq, past, SAMPLE_BATCH_ROWS)
    y_s = _finish(ans, gms, xs, wp)

    return (y_p.reshape(batch, seq, D_MODEL),
            y_s.reshape(dec_batch, dec_seq, D_MODEL),
            ckv_p.reshape(1, batch, seq, KV_RANK),
            jnp.swapaxes(kpe_p, 1, 2)[None],
            ckv_s.reshape(1, dec_batch, dec_seq, KV_RANK),
            kpe_s.reshape(1, dec_batch, dec_seq, ROPE_DIM),
            vg_s.reshape(1, dec_batch, dec_seq, GMLP_WIDTH))
```

```python
import functools
import math

import jax
import jax.numpy as jnp
import numpy as np
from jax import lax
from jax.experimental import pallas as pl
from jax.experimental.pallas import tpu as pltpu

D_MODEL = 2048
CHUNK = 64
N_HEADS = 8
NOPE_DIM = 128
ROPE_DIM = 64
HALF_ROPE = ROPE_DIM // 2
QK_DIM = NOPE_DIM + ROPE_DIM
V_DIM = 128
ATTN_WIDTH = N_HEADS * V_DIM
Q_RANK = 512
KV_RANK = 256
GMLP_GROUPS = 8
GMLP_GROUP_DIM = 128
GMLP_WIDTH = GMLP_GROUPS * GMLP_GROUP_DIM
GMLP_CHUNK = 128
D_FF = 4 * D_MODEL
ROPE_THETA = 10000.0
EPS = 1e-6

LANES = 128
SUBLANES = 8
BF16_SUBLANES = 2 * SUBLANES
HEAD_PAD = 2 * LANES
QK_WIDTH = N_HEADS * HEAD_PAD
C_Q0, C_Q1 = 0, Q_RANK
C_KV0, C_KV1 = C_Q1, C_Q1 + KV_RANK
C_PE0, C_PE1 = C_KV1, C_KV1 + LANES
C_U0, C_U1 = C_PE1, C_PE1 + GMLP_WIDTH
C_V0, C_V1 = C_U1, C_U1 + GMLP_WIDTH
IN_PAD = C_V1
VMEM_LIMIT = 56 * 1024 * 1024

PROJ_ROWS = 256
HEAD_ROWS = 128
PAD_ROWS = 256
OUTPROJ_ROWS = 512
FFN_ROWS = 1024
FFN_COLS = 1024
SAMPLE_BATCH_ROWS = 4

BF16 = jnp.bfloat16
F32 = jnp.float32


def _dot(a, b):
    return jnp.dot(a, b, preferred_element_type=F32)


def _dot_nt(a, b):
    return lax.dot_general(a, b, (((1,), (1,)), ((), ())), preferred_element_type=F32)


def _sum_lanes(x):
    return jnp.sum(x, axis=-1, keepdims=True)


def _sumsq(x):
    return _sum_lanes(x * x)


def _rms(x):
    return x * lax.rsqrt(jnp.mean(x * x, axis=-1, keepdims=True) + EPS)


def _gelu(x):
    c = math.sqrt(2.0 / math.pi)
    return 0.5 * x * (1.0 + jnp.tanh(c * (x + 0.044715 * (x * x * x))))


def _rope(t, cos, s1, s2):
    return t * cos + pltpu.roll(t, LANES - HALF_ROPE, 1) * s1 + pltpu.roll(t, HALF_ROPE, 1) * s2


def _const_spec(shape):
    nd = len(shape)
    return pl.BlockSpec(shape, lambda *_: (0,) * nd, pipeline_mode=pl.Buffered(1))


def _proj_kernel(chunk_len, prompt, x_ref, gmix_ref, win_ref, gql_ref, gkv_ref, wuq_ref, wuk_ref,
                 wuv_ref, gq_ref, cos_ref, s1_ref, s2_ref, vn_ref, ws_ref, bst_ref, gog_ref, *rest):
    v_transposed = prompt
    emit_v = not prompt
    if prompt:
        (cost_ref, sint_ref, wo_ref, wu_ref, wd_ref, q_ref, k_ref, v_ref, ckv_ref, kpe_ref, gm_ref,
         wo_out, wu_out, wd_out, gate_scr, vgb_scr) = rest
        wo_out[...] = wo_ref[...].astype(BF16)
        wu_out[...] = wu_ref[...].astype(BF16)
        wd_out[...] = wd_ref[...].astype(BF16)
    else:
        q_ref, k_ref, v_ref, ckv_ref, kpe_ref, gm_ref, vg_ref, gate_scr, vgb_scr = rest
    tm = x_ref.shape[0]
    xn = (_rms(x_ref[...]) * gmix_ref[...]).astype(BF16)
    cos, s1, s2 = cos_ref[...], s1_ref[...], s2_ref[...]

    zq = _dot(xn, win_ref[:, C_Q0:C_Q1])
    qln = (_rms(zq) * gql_ref[...]).astype(BF16)
    if prompt:
        qt = _dot_nt(wuq_ref[...], qln)
        cos_t, sin_t = cost_ref[...], sint_ref[...]
        x1_lo, x2_lo = NOPE_DIM, NOPE_DIM + HALF_ROPE
        zeros = jnp.zeros((HEAD_PAD - QK_DIM, tm), BF16)
        for h in range(N_HEADS):
            lo = h * HEAD_PAD
            nope = qt[lo:lo + NOPE_DIM]
            x1 = qt[lo + x1_lo:lo + x2_lo]
            x2 = qt[lo + x2_lo:lo + QK_DIM]
            ss = (jnp.sum(nope * nope, axis=0, keepdims=True)
                  + jnp.sum(x1 * x1 + x2 * x2, axis=0, keepdims=True))
            r = lax.rsqrt(ss * (1.0 / QK_DIM) + EPS)
            q_ref[lo:lo + NOPE_DIM, :] = (nope * r * gq_ref[:NOPE_DIM]).astype(BF16)
            q_ref[lo + x1_lo:lo + x2_lo, :] = (
                (x1 * cos_t - x2 * sin_t) * r * gq_ref[x1_lo:x2_lo]).astype(BF16)
            q_ref[lo + x2_lo:lo + QK_DIM, :] = (
                (x2 * cos_t + x1 * sin_t) * r * gq_ref[x2_lo:QK_DIM]).astype(BF16)
            q_ref[lo + QK_DIM:lo + HEAD_PAD, :] = zeros
    else:
        qraw = _dot(qln, wuq_ref[...])
        for h in range(N_HEADS):
            lo = h * HEAD_PAD
            for rc in range(tm // HEAD_ROWS):
                rows = slice(rc * HEAD_ROWS, (rc + 1) * HEAD_ROWS)
                nope = qraw[rows, lo:lo + LANES]
                rp = qraw[rows, lo + LANES:lo + HEAD_PAD]
                r = lax.rsqrt(_sum_lanes(nope * nope + rp * rp) * (1.0 / QK_DIM) + EPS)
                rp = _rope(rp, cos[rows], s1[rows], s2[rows])
                q_ref[rows, lo:lo + LANES] = (nope * r * gq_ref[:, :LANES]).astype(BF16)
                q_ref[rows, lo + LANES:lo + HEAD_PAD] = (rp * r * gq_ref[:, LANES:]).astype(BF16)

    ckv = _rms(_dot(xn, win_ref[:, C_KV0:C_KV1])) * gkv_ref[...]
    ckv_ref[...] = ckv
    cb = ckv.astype(BF16)
    pe = _rope(_dot(xn, win_ref[:, C_PE0:C_PE1]), cos, s1, s2)
    kpe_ref[...] = pe.T[:ROPE_DIM, :] if prompt else pe[:, :ROPE_DIM]
    ss_pe = _sumsq(pe)
    knope = _dot(cb, wuk_ref[...])
    if v_transposed:
        v_ref[...] = _dot_nt(wuv_ref[...], cb).astype(BF16)
    else:
        v_ref[...] = _dot(cb, wuv_ref[...]).astype(BF16)
    for h in range(N_HEADS):
        lo = h * HEAD_PAD
        for rc in range(tm // HEAD_ROWS):
            rows = slice(rc * HEAD_ROWS, (rc + 1) * HEAD_ROWS)
            nope = knope[rows, h * LANES:(h + 1) * LANES]
            r = lax.rsqrt((_sumsq(nope) + ss_pe[rows]) * (1.0 / QK_DIM) + EPS)
            k_ref[rows, lo:lo + LANES] = (nope * r).astype(BF16)
            k_ref[rows, lo + LANES:lo + HEAD_PAD] = (pe[rows] * r).astype(BF16)

    gv = _gelu(_dot(xn, win_ref[:, C_V0:C_V1]))
    for g in range(GMLP_GROUPS):
        cols = slice(g * LANES, (g + 1) * LANES)
        for rc in range(tm // HEAD_ROWS):
            rows = slice(rc * HEAD_ROWS, (rc + 1) * HEAD_ROWS)
            vg = _rms(gv[rows, cols]) * vn_ref[g:g + 1, :]
            if emit_v:
                vg_ref[rows, cols] = vg
            vgb_scr[rows, cols] = vg.astype(BF16)
    u = _gelu(_dot(xn, win_ref[:, C_U0:C_U1]))
    row = lax.broadcasted_iota(jnp.int32, (GMLP_CHUNK, GMLP_CHUNK), 0)
    col = lax.broadcasted_iota(jnp.int32, (GMLP_CHUNK, GMLP_CHUNK), 1)
    causal = (row // chunk_len == col // chunk_len) & (col <= row)
    for g in range(GMLP_GROUPS):
        wm = jnp.where(causal, ws_ref[g], 0.0).astype(BF16)
        bias = bst_ref[:, g:g + 1]
        for c in range(tm // GMLP_CHUNK):
            rows = slice(c * GMLP_CHUNK, (c + 1) * GMLP_CHUNK)
            cols = slice(g * LANES, (g + 1) * LANES)
            s = _dot(wm, vgb_scr[rows, cols]) + bias
            gate_scr[rows, cols] = u[rows, cols] * s
    gm_ref[...] = (_rms(gate_scr[...]) * gog_ref[...]).astype(BF16)


def _proj(x2d, tabs, wp, chunk_len, tm, f32_weights=None, tabs_t=None):
    prompt = f32_weights is not None
    m = x2d.shape[0]
    steps = m // tm
    cos, s1, s2 = tabs
    tab_blocks = cos.shape[0] // tm
    row_spec = lambda w: pl.BlockSpec((tm, w), lambda i: (i, 0))
    tab_spec = pl.BlockSpec((tm, LANES), lambda i: (i % tab_blocks, 0))
    consts = [wp["gmix"], wp["w_in"], wp["gql"], wp["gkv"], wp["w_uq_t"] if prompt else wp["w_uq"],
              wp["w_uk"], wp["w_uv_t"] if prompt else wp["w_uv"], wp["gq_t"] if prompt else wp["gq"]]
    consts2 = [wp["v_norm"], wp["ws"], wp["bst"], wp["gog"]]
    in_specs = ([row_spec(D_MODEL)] + [_const_spec(c.shape) for c in consts]
                + [tab_spec] * 3 + [_const_spec(c.shape) for c in consts2])
    q_shape = (steps * QK_WIDTH, tm) if prompt else (m, QK_WIDTH)
    q_spec = pl.BlockSpec((QK_WIDTH, tm), lambda i: (i, 0)) if prompt else row_spec(QK_WIDTH)
    v_shape = (steps * ATTN_WIDTH, tm) if prompt else (m, ATTN_WIDTH)
    v_spec = pl.BlockSpec((ATTN_WIDTH, tm), lambda i: (i, 0)) if prompt else row_spec(ATTN_WIDTH)
    kpe_shape = (m // cos.shape[0], ROPE_DIM, cos.shape[0]) if prompt else (m, ROPE_DIM)
    kpe_spec = (pl.BlockSpec((None, ROPE_DIM, tm), lambda i: (i // tab_blocks, 0, i % tab_blocks))
                if prompt else row_spec(ROPE_DIM))
    out_shape = [jax.ShapeDtypeStruct(q_shape, BF16), jax.ShapeDtypeStruct((m, QK_WIDTH), BF16),
                 jax.ShapeDtypeStruct(v_shape, BF16), jax.ShapeDtypeStruct((m, KV_RANK), F32),
                 jax.ShapeDtypeStruct(kpe_shape, F32), jax.ShapeDtypeStruct((m, GMLP_WIDTH), BF16)]
    out_specs = [q_spec, row_spec(QK_WIDTH), v_spec, row_spec(KV_RANK),
                 kpe_spec, row_spec(GMLP_WIDTH)]
    extra_in = []
    if prompt:
        tab_t_spec = pl.BlockSpec((HALF_ROPE, tm), lambda i: (0, i % tab_blocks))
        w_out, w_up, w_down = f32_weights
        slabs = [pl.BlockSpec((D_MODEL // steps, D_MODEL), lambda i: (i, 0)),
                 pl.BlockSpec((D_MODEL, D_FF // steps), lambda i: (0, i)),
                 pl.BlockSpec((D_FF // steps, D_MODEL), lambda i: (i, 0))]
        extra_in = [*tabs_t, w_out, w_up, w_down]
        in_specs += [tab_t_spec, tab_t_spec] + slabs
        out_shape += [jax.ShapeDtypeStruct(w.shape, BF16) for w in f32_weights]
        out_specs += slabs
    else:
        out_shape.append(jax.ShapeDtypeStruct((m, GMLP_WIDTH), F32))
        out_specs.append(row_spec(GMLP_WIDTH))
    return pl.pallas_call(
        functools.partial(_proj_kernel, chunk_len, prompt),
        out_shape=out_shape,
        grid=(steps,),
        in_specs=in_specs,
        out_specs=out_specs,
        scratch_shapes=[pltpu.VMEM((tm, GMLP_WIDTH), F32), pltpu.VMEM((tm, GMLP_WIDTH), BF16)],
        compiler_params=pltpu.CompilerParams(dimension_semantics=("parallel",),
                                             vmem_limit_bytes=VMEM_LIMIT),
        name="proj",
    )(x2d, *consts, cos, s1, s2, *consts2, *extra_in)


NEG = -1e30
SOFTMAX_KEY_CHUNK = 64
KEY_BLOCKS_PER_STEP = 2
SCORE_LEAD = 4


def _attn_prompt_kernel(q_ref, k_ref, vt_ref, go_ref, o_ref, acc_scr, m_scr, l_scr, s_scr, p_scr):
    t = q_ref.shape[1]
    qi = pl.program_id(1)
    m_scr[...] = jnp.full(m_scr.shape, NEG, F32)
    l_scr[...] = jnp.zeros(l_scr.shape, F32)
    acc_scr[...] = jnp.zeros(acc_scr.shape, F32)

    def step(kb0, nblk, last_masked):
        nk = nblk * t
        if last_masked:
            key = lax.broadcasted_iota(jnp.int32, (t, t), 0)
            qry = lax.broadcasted_iota(jnp.int32, (t, t), 1)
            ok = (key // CHUNK) <= (qry // CHUNK)

        m8 = {}

        def scores(h):
            qs = slice(h * HEAD_PAD, (h + 1) * HEAD_PAD)
            mh = None
            for j in range(nblk):
                rows = pl.ds(pl.multiple_of((kb0 + j) * t, t), t)
                s = _dot(k_ref[rows, qs], q_ref[qs, :])
                if last_masked and j == nblk - 1:
                    s = jnp.where(ok, s, NEG)
                s_scr[h, j * t:(j + 1) * t, :] = s
                mj = jnp.max(s.reshape(t // SUBLANES, SUBLANES, t), axis=0)
                mh = mj if mh is None else jnp.maximum(mh, mj)
            m8[h] = mh

        ones = jnp.ones((BF16_SUBLANES, nk), BF16)
        for h in range(min(SCORE_LEAD, N_HEADS)):
            scores(h)
        for h in range(N_HEADS):
            if h + SCORE_LEAD < N_HEADS:
                scores(h + SCORE_LEAD)
            hs = slice(h * V_DIM, (h + 1) * V_DIM)
            m_old = m_scr[h]
            m_new = jnp.maximum(m_old, jnp.max(m8[h], axis=0, keepdims=True))
            alpha = jnp.exp2(m_old - m_new)
            for c in range(nk // SOFTMAX_KEY_CHUNK):
                cs = slice(c * SOFTMAX_KEY_CHUNK, (c + 1) * SOFTMAX_KEY_CHUNK)
                p_scr[h, cs, :] = jnp.exp2(s_scr[h, cs, :] - m_new).astype(BF16)
            m_scr[h] = m_new
            vt = [vt_ref[pl.ds(pl.multiple_of((kb0 + j) * ATTN_WIDTH, ATTN_WIDTH) + h * V_DIM, V_DIM), :]
                  for j in range(nblk)]
            vt = vt[0] if nblk == 1 else jnp.concatenate(vt, axis=1)
            pv = _dot(jnp.concatenate([vt, ones], axis=0), p_scr[h, :nk, :])
            acc_scr[hs, :] = alpha * acc_scr[hs, :] + pv[:V_DIM]
            l_scr[h] = alpha * l_scr[h] + pv[V_DIM:V_DIM + 1]

    def body(j, carry):
        step(j * KEY_BLOCKS_PER_STEP, KEY_BLOCKS_PER_STEP, False)
        return carry

    lax.fori_loop(0, qi // KEY_BLOCKS_PER_STEP, body, 0)
    for rem in range(KEY_BLOCKS_PER_STEP):
        @pl.when(qi % KEY_BLOCKS_PER_STEP == rem)
        def _(rem=rem):
            step(qi - rem, rem + 1, True)

    for h in range(N_HEADS):
        hs = slice(h * V_DIM, (h + 1) * V_DIM)
        acc_scr[hs, :] = acc_scr[hs, :] / l_scr[h]
    o = acc_scr[...].T
    o_ref[...] = (_rms(o) * go_ref[...]).astype(BF16)


def _attn_prompt(q, k, vt, go, batch, seq, t):
    nq = seq // t
    assert vt.shape == (batch * nq * ATTN_WIDTH, t)
    tk = KEY_BLOCKS_PER_STEP * t
    return pl.pallas_call(
        _attn_prompt_kernel,
        out_shape=jax.ShapeDtypeStruct((batch * seq, ATTN_WIDTH), BF16),
        grid=(batch, nq),
        in_specs=[pl.BlockSpec((QK_WIDTH, t), lambda b, i: (b * nq + i, 0)),
                  pl.BlockSpec((seq, QK_WIDTH), lambda b, i: (b, 0)),
                  pl.BlockSpec((nq * ATTN_WIDTH, t), lambda b, i: (b, 0)),
                  _const_spec(go.shape)],
        out_specs=pl.BlockSpec((t, ATTN_WIDTH), lambda b, i: (b * nq + i, 0)),
        scratch_shapes=[pltpu.VMEM((ATTN_WIDTH, t), F32), pltpu.VMEM((N_HEADS, 1, t), F32),
                        pltpu.VMEM((N_HEADS, 1, t), F32), pltpu.VMEM((N_HEADS, tk, t), F32),
                        pltpu.VMEM((N_HEADS, tk, t), BF16)],
        compiler_params=pltpu.CompilerParams(dimension_semantics=("parallel", "arbitrary"),
                                             vmem_limit_bytes=VMEM_LIMIT),
        name="attn_prompt",
    )(q, k, vt, go)


def _attn_sample_kernel(past, c_ref, krt_ref, q_ref, kn_ref, vn_ref, wukt_ref, wuv_ref, go_ref,
                        o_ref, cb_scr, qa_scr, qr_scr, r_scr, sc_scr, sn_scr, pc_scr, acc_scr):
    nb = krt_ref.shape[0]
    t = q_ref.shape[0] // nb
    row = lax.broadcasted_iota(jnp.int32, (t, t), 0)
    col = lax.broadcasted_iota(jnp.int32, (t, t), 1)
    new_ok = ((past + col) // CHUNK) <= ((past + row) // CHUNK)
    denom = [[None] * N_HEADS for _ in range(nb)]
    p_new = [[None] * N_HEADS for _ in range(nb)]

    def norm_factors(j):
        cb_scr[j] = c_ref[j * past:(j + 1) * past, :].astype(BF16)
        krt = krt_ref[j]
        ss_pe = jnp.sum(krt * krt, axis=0, keepdims=True)
        knt = _dot_nt(wukt_ref[...], cb_scr[j])
        for h in range(N_HEADS):
            nope = knt[h * LANES:(h + 1) * LANES, :]
            ss = jnp.sum(nope * nope, axis=0, keepdims=True) + ss_pe
            r_scr[j, h] = lax.rsqrt(ss * (1.0 / QK_DIM) + EPS)

    def scores(j):
        qj = q_ref.at[j * t:(j + 1) * t, :]
        knj = kn_ref.at[j * t:(j + 1) * t, :]
        for h in range(N_HEADS):
            lo = h * HEAD_PAD
            rows = slice(h * t, (h + 1) * t)
            qa_scr[j, rows, :] = _dot(qj[:, lo:lo + NOPE_DIM],
                                      wukt_ref[h * NOPE_DIM:(h + 1) * NOPE_DIM, :]).astype(BF16)
            qr_scr[j, rows, :] = qj[:, lo + NOPE_DIM:lo + QK_DIM]
            sn_scr[j, h] = jnp.where(new_ok, _dot_nt(qj[:, lo:lo + HEAD_PAD], knj[:, lo:lo + HEAD_PAD]), NEG)
        sc_scr[j] = _dot_nt(qa_scr[j], cb_scr[j]) + _dot(qr_scr[j], krt_ref[j].astype(BF16))

    def softmax(j):
        for h in range(N_HEADS):
            s_c, s_n = sc_scr[j, h * t:(h + 1) * t, :] * r_scr[j, h], sn_scr[j, h]
            m = jnp.maximum(jnp.max(s_c, axis=-1, keepdims=True), jnp.max(s_n, axis=-1, keepdims=True))
            p_c = jnp.exp2(s_c - m)
            p_n = jnp.exp2(s_n - m)
            denom[j][h] = jnp.sum(p_c, axis=-1, keepdims=True) + jnp.sum(p_n, axis=-1, keepdims=True)
            p_new[j][h] = p_n.astype(BF16)
            pc_scr[j, h * t:(h + 1) * t, :] = p_c.astype(BF16)

    def values(j):
        vnj = vn_ref.at[j * t:(j + 1) * t, :]
        lat = _dot(pc_scr[j], cb_scr[j]).astype(BF16)
        for h in range(N_HEADS):
            vs = slice(h * V_DIM, (h + 1) * V_DIM)
            acc = _dot(lat[h * t:(h + 1) * t, :], wuv_ref[:, vs]) + _dot(p_new[j][h], vnj[:, vs])
            acc_scr[j, :, vs] = acc / denom[j][h]
        o_ref[j * t:(j + 1) * t, :] = (_rms(acc_scr[j]) * go_ref[...]).astype(BF16)

    stages = (norm_factors, scores, softmax, values)
    for tick in range(nb + len(stages) - 1):
        for j in range(nb):
            if 0 <= tick - j < len(stages):
                stages[tick - j](j)


def _attn_sample(c2d, krt, q, kn, vn, wp, batch, t, past, nb):
    consts = [wp["w_uk_t"], wp["w_uv"], wp["goa"]]
    ht = N_HEADS * t
    return pl.pallas_call(
        functools.partial(_attn_sample_kernel, past),
        out_shape=jax.ShapeDtypeStruct((batch * t, ATTN_WIDTH), BF16),
        grid=(batch // nb,),
        in_specs=[pl.BlockSpec((nb * past, KV_RANK), lambda b: (b, 0)),
                  pl.BlockSpec((nb, ROPE_DIM, past), lambda b: (b, 0, 0)),
                  pl.BlockSpec((nb * t, QK_WIDTH), lambda b: (b, 0)),
                  pl.BlockSpec((nb * t, QK_WIDTH), lambda b: (b, 0)),
                  pl.BlockSpec((nb * t, ATTN_WIDTH), lambda b: (b, 0))]
                 + [_const_spec(c.shape) for c in consts],
        out_specs=pl.BlockSpec((nb * t, ATTN_WIDTH), lambda b: (b, 0)),
        scratch_shapes=[pltpu.VMEM((nb, past, KV_RANK), BF16), pltpu.VMEM((nb, ht, KV_RANK), BF16),
                        pltpu.VMEM((nb, ht, ROPE_DIM), BF16), pltpu.VMEM((nb, N_HEADS, 1, past), F32),
                        pltpu.VMEM((nb, ht, past), F32), pltpu.VMEM((nb, N_HEADS, t, t), F32),
                        pltpu.VMEM((nb, ht, past), BF16), pltpu.VMEM((nb, t, ATTN_WIDTH), F32)],
        compiler_params=pltpu.CompilerParams(dimension_semantics=("parallel",),
                                             vmem_limit_bytes=VMEM_LIMIT),
        name="attn_sample",
    )(c2d, krt, q, kn, vn, *consts)


def _outproj_kernel(an_ref, gm_ref, x_ref, wo_ref, gffn_ref, h_ref, hn_ref):
    h = (x_ref[...] + _dot(an_ref[...], wo_ref[:ATTN_WIDTH, :])
         + _dot(gm_ref[...], wo_ref[ATTN_WIDTH:, :]))
    h_ref[...] = h
    hn_ref[...] = (_rms(h) * gffn_ref[...]).astype(BF16)


def _outproj(an, gm, x2d, wp, tm):
    m = x2d.shape[0]
    row_spec = lambda w: pl.BlockSpec((tm, w), lambda i: (i, 0))
    return pl.pallas_call(
        _outproj_kernel,
        out_shape=[jax.ShapeDtypeStruct((m, D_MODEL), F32), jax.ShapeDtypeStruct((m, D_MODEL), BF16)],
        grid=(m // tm,),
        in_specs=[row_spec(ATTN_WIDTH), row_spec(GMLP_WIDTH), row_spec(D_MODEL),
                  _const_spec(wp["w_out"].shape), _const_spec(wp["gffn"].shape)],
        out_specs=[row_spec(D_MODEL), row_spec(D_MODEL)],
        compiler_params=pltpu.CompilerParams(dimension_semantics=("parallel",),
                                             vmem_limit_bytes=VMEM_LIMIT),
        name="outproj",
    )(an, gm, x2d, wp["w_out"], wp["gffn"])


def _ffn_kernel(h_ref, hn_ref, wu_ref, wd_ref, y_ref):
    f = pl.program_id(1)
    slab = h_ref.shape[1]

    def down():
        a = jnp.maximum(_dot(hn_ref[...], wu_ref[...]), 0.0)
        return _dot((a * a).astype(BF16), wd_ref[...])

    @pl.when(f == 0)
    def _():
        d = down()
        y_ref[:, :slab] = d[:, :slab] + h_ref[...]
        y_ref[:, slab:] = d[:, slab:]

    @pl.when(f > 0)
    def _():
        cols = pl.ds(pl.multiple_of(f * slab, slab), slab)
        y_ref[:, cols] += h_ref[...]
        y_ref[...] += down()


def _ffn(h, hn, wp, tm, tf):
    m = h.shape[0]
    nf = D_FF // tf
    return pl.pallas_call(
        _ffn_kernel,
        out_shape=jax.ShapeDtypeStruct((m, D_MODEL), F32),
        grid=(m // tm, nf),
        in_specs=[pl.BlockSpec((tm, D_MODEL // nf), lambda i, f: (i, f)),
                  pl.BlockSpec((tm, D_MODEL), lambda i, f: (i, 0)),
                  pl.BlockSpec((D_MODEL, tf), lambda i, f: (0, f)),
                  pl.BlockSpec((tf, D_MODEL), lambda i, f: (f, 0))],
        out_specs=pl.BlockSpec((tm, D_MODEL), lambda i, f: (i, 0)),
        compiler_params=pltpu.CompilerParams(dimension_semantics=("parallel", "arbitrary"),
                                             vmem_limit_bytes=VMEM_LIMIT),
        name="ffn",
    )(h, hn, wp["w_up"], wp["w_down"])


def _rope_tables(pos):
    inv = ROPE_THETA ** (-np.arange(HALF_ROPE, dtype=np.float64) / HALF_ROPE)
    ang = np.asarray(pos, np.float64)[:, None] * inv[None, :]
    cos, sin = np.cos(ang), np.sin(ang)
    z = np.zeros_like(cos)
    z2 = np.zeros((ang.shape[0], LANES - ROPE_DIM))
    tabs = (np.concatenate([cos, cos, z2], axis=1), np.concatenate([-sin, z, z2], axis=1),
            np.concatenate([z, sin, z2], axis=1))
    tabs_t = (cos.T, sin.T)
    return tuple(jnp.asarray(t, F32) for t in tabs), tuple(jnp.asarray(t, F32) for t in tabs_t)


def _head_gain(g_nope, g_rope, scale):
    pad = jnp.zeros((HEAD_PAD - QK_DIM,), F32)
    return (jnp.concatenate([g_nope, g_rope, g_rope, pad]) * scale)[None, :]


def _pad_w_in_kernel(wt_ref, o_ref):
    tr = wt_ref.shape[1]
    split = C_PE0 + ROPE_DIM
    o_ref[:, :C_PE0] = wt_ref[:C_PE0, :].T.astype(BF16)
    pe = wt_ref[C_PE0:C_PE1, :].T
    lane = lax.broadcasted_iota(jnp.int32, (tr, LANES), 1)
    o_ref[:, C_PE0:C_PE1] = jnp.where(lane < ROPE_DIM, pe, 0.0).astype(BF16)
    o_ref[:, C_PE1:] = wt_ref[split:, :].T.astype(BF16)


def _pad_w_in(w_in_t, tr):
    width, rows = w_in_t.shape
    assert width + LANES - ROPE_DIM == IN_PAD
    return pl.pallas_call(
        _pad_w_in_kernel,
        out_shape=jax.ShapeDtypeStruct((rows, IN_PAD), BF16),
        grid=(rows // tr,),
        in_specs=[pl.BlockSpec((width, tr), lambda i: (0, i))],
        out_specs=pl.BlockSpec((tr, IN_PAD), lambda i: (i, 0)),
        compiler_params=pltpu.CompilerParams(dimension_semantics=("parallel",)),
        name="pad_w_in",
    )(w_in_t)


def _prep_weights(norm_mix, w_in, q_lat_norm, kv_lat_norm, w_uq, w_uk, w_uv, q_norm_nope, q_norm_rope,
                  k_norm_nope, k_norm_rope, v_norm, w_spatial, b_spatial, out_norm_attn, out_norm_gmlp,
                  norm_ffn, chunk_len):
    w_in_p = _pad_w_in(w_in.T, PAD_ROWS)
    wq = w_uq.reshape(Q_RANK, N_HEADS, QK_DIM)
    wq = jnp.pad(wq, ((0, 0), (0, 0), (0, HEAD_PAD - QK_DIM))).reshape(Q_RANK, QK_WIDTH).astype(BF16)
    reps = GMLP_CHUNK // chunk_len
    ws = jnp.tile(w_spatial[:, :chunk_len, :chunk_len], (1, reps, reps))
    bst = jnp.tile(b_spatial[:, :chunk_len], (1, reps)).T
    gq = (_head_gain(q_norm_nope, q_norm_rope, QK_DIM ** -0.5 * math.log2(math.e))
          * _head_gain(k_norm_nope, k_norm_rope, 1.0))
    return {
        "gmix": norm_mix[None, :], "w_in": w_in_p, "gql": q_lat_norm[None, :], "gkv": kv_lat_norm[None, :],
        "w_uq": wq, "w_uq_t": wq.T, "w_uk": w_uk.astype(BF16), "w_uk_t": w_uk.T.astype(BF16),
        "w_uv": w_uv.astype(BF16), "w_uv_t": w_uv.T.astype(BF16),
        "gq": gq, "gq_t": jnp.broadcast_to(gq.T, (HEAD_PAD, PROJ_ROWS)),
        "v_norm": v_norm, "ws": ws, "bst": bst, "gog": out_norm_gmlp[None, :],
        "goa": out_norm_attn[None, :], "gffn": norm_ffn[None, :],
    }


def _finish(an, gm, x2d, wp):
    h, hn = _outproj(an, gm, x2d, wp, OUTPROJ_ROWS)
    return _ffn(h, hn, wp, FFN_ROWS, FFN_COLS)


def kernel(x_prompt, x_sample, cache_c_kv, cache_k_rope, norm_mix, w_in, q_lat_norm, kv_lat_norm, w_uq, w_uk, w_uv, q_norm_nope, q_norm_rope, k_norm_nope, k_norm_rope, v_norm, w_spatial, b_spatial, out_norm_attn, out_norm_gmlp, w_out, norm_ffn, w_up, w_down):
    depth = w_in.shape[0]
    assert depth == 1
    batch, seq, _ = x_prompt.shape
    dec_batch, dec_seq, _ = x_sample.shape
    past = cache_c_kv.shape[2]
    assert past % CHUNK == 0 and dec_seq <= CHUNK and GMLP_CHUNK % dec_seq == 0
    n_prompt, n_sample = batch * seq, dec_batch * dec_seq
    assert seq % PROJ_ROWS == 0 and n_sample % PROJ_ROWS == 0 and PROJ_ROWS % dec_seq == 0
    assert n_prompt % FFN_ROWS == 0 and n_sample % FFN_ROWS == 0 and dec_batch % SAMPLE_BATCH_ROWS == 0

    weights = (norm_mix[0], w_in[0], q_lat_norm[0], kv_lat_norm[0], w_uq[0], w_uk[0], w_uv[0],
               q_norm_nope[0], q_norm_rope[0], k_norm_nope[0], k_norm_rope[0], v_norm[0], w_spatial[0],
               b_spatial[0], out_norm_attn[0], out_norm_gmlp[0], norm_ffn[0])
    wp = _prep_weights(*weights, chunk_len=GMLP_CHUNK)
    reps = GMLP_CHUNK // dec_seq
    ws_s = jnp.tile(w_spatial[0][:, :dec_seq, :dec_seq], (1, reps, reps))
    bst_s = jnp.tile(b_spatial[0][:, :dec_seq], (1, reps)).T

    xp = x_prompt.reshape(batch * seq, D_MODEL)
    tabs_p, tabs_pt = _rope_tables(np.arange(seq))
    qt, k, vt, ckv_p, kpe_p, gm, wo_b, wu_b, wd_b = _proj(
        xp, tabs_p, wp, GMLP_CHUNK, PROJ_ROWS, f32_weights=(w_out[0], w_up[0], w_down[0]),
        tabs_t=tabs_pt)
    wp = dict(wp, w_out=wo_b, w_up=wu_b, w_down=wd_b)
    wp_s = dict(wp, ws=ws_s, bst=bst_s)
    an = _attn_prompt(qt, k, vt, wp["goa"], batch, seq, PROJ_ROWS)
    y_p = _finish(an, gm, xp, wp)

    xs = x_sample.reshape(dec_batch * dec_seq, D_MODEL)
    tabs_s, _ = _rope_tables(past + np.arange(PROJ_ROWS) % dec_seq)
    qs, kn, vn, ckv_s, kpe_s, gms, vg_s = _proj(xs, tabs_s, wp_s, dec_seq, PROJ_ROWS)
    ans = _attn_sample(cache_c_kv[0].reshape(dec_batch * past, KV_RANK),
                       jnp.swapaxes(cache_k_rope[0], 1, 2),
                       qs, kn, vn, wp, dec_batch, dec_seq, past, SAMPLE_BATCH_ROWS)
    y_s = _finish(ans, gms, xs, wp)

    return (y_p.reshape(batch, seq, D_MODEL),
            y_s.reshape(dec_batch, dec_seq, D_MODEL),
            ckv_p.reshape(1, batch, seq, KV_RANK),
            jnp.swapaxes(kpe_p, 1, 2)[None],
            ckv_s.reshape(1, dec_batch, dec_seq, KV_RANK),
            kpe_s.reshape(1, dec_batch, dec_seq, ROPE_DIM),
            vg_s.reshape(1, dec_batch, dec_seq, GMLP_WIDTH))
```

```python
import functools
import math

import jax
import jax.numpy as jnp
import numpy as np
from jax import lax
from jax.experimental import pallas as pl
from jax.experimental.pallas import tpu as pltpu

D_MODEL = 2048
CHUNK = 64
N_HEADS = 8
NOPE_DIM = 128
ROPE_DIM = 64
HALF_ROPE = ROPE_DIM // 2
QK_DIM = NOPE_DIM + ROPE_DIM
V_DIM = 128
ATTN_WIDTH = N_HEADS * V_DIM
Q_RANK = 512
KV_RANK = 256
GMLP_GROUPS = 8
GMLP_GROUP_DIM = 128
GMLP_WIDTH = GMLP_GROUPS * GMLP_GROUP_DIM
GMLP_CHUNK = 128
D_FF = 4 * D_MODEL
ROPE_THETA = 10000.0
EPS = 1e-6

LANES = 128
SUBLANES = 8
BF16_SUBLANES = 2 * SUBLANES
HEAD_PAD = 2 * LANES
QK_WIDTH = N_HEADS * HEAD_PAD
C_Q0, C_Q1 = 0, Q_RANK
C_KV0, C_KV1 = C_Q1, C_Q1 + KV_RANK
C_PE0, C_PE1 = C_KV1, C_KV1 + LANES
C_U0, C_U1 = C_PE1, C_PE1 + GMLP_WIDTH
C_V0, C_V1 = C_U1, C_U1 + GMLP_WIDTH
IN_PAD = C_V1
VMEM_LIMIT = 56 * 1024 * 1024

PROJ_ROWS = 256
HEAD_ROWS = 128
PAD_ROWS = 256
OUTPROJ_ROWS = 512
FFN_ROWS = 1024
FFN_COLS = 1024
SAMPLE_BATCH_ROWS = 4

BF16 = jnp.bfloat16
F32 = jnp.float32


def _dot(a, b):
    return jnp.dot(a, b, preferred_element_type=F32)


def _dot_nt(a, b):
    return lax.dot_general(a, b, (((1,), (1,)), ((), ())), preferred_element_type=F32)


def _sum_lanes(x):
    return jnp.sum(x, axis=-1, keepdims=True)


def _sumsq(x):
    return _sum_lanes(x * x)


def _rms(x):
    return x * lax.rsqrt(jnp.mean(x * x, axis=-1, keepdims=True) + EPS)


def _gelu(x):
    c = math.sqrt(2.0 / math.pi)
    return 0.5 * x * (1.0 + jnp.tanh(c * (x + 0.044715 * (x * x * x))))


def _rope(t, cos, s1, s2):
    return t * cos + pltpu.roll(t, LANES - HALF_ROPE, 1) * s1 + pltpu.roll(t, HALF_ROPE, 1) * s2


def _const_spec(shape):
    nd = len(shape)
    return pl.BlockSpec(shape, lambda *_: (0,) * nd, pipeline_mode=pl.Buffered(1))


def _proj_kernel(chunk_len, prompt, x_ref, gmix_ref, win_ref, gql_ref, gkv_ref, wuq_ref, wuk_ref,
                 wuv_ref, gq_ref, cos_ref, s1_ref, s2_ref, vn_ref, ws_ref, bst_ref, gog_ref, *rest):
    v_transposed = prompt
    emit_v = not prompt
    if prompt:
        (cost_ref, sint_ref, wo_ref, wu_ref, wd_ref, q_ref, k_ref, v_ref, ckv_ref, kpe_ref, gm_ref,
         wo_out, wu_out, wd_out, gate_scr, vgb_scr) = rest
        wo_out[...] = wo_ref[...].astype(BF16)
        wu_out[...] = wu_ref[...].astype(BF16)
        wd_out[...] = wd_ref[...].astype(BF16)
    else:
        q_ref, k_ref, v_ref, ckv_ref, kpe_ref, gm_ref, vg_ref, gate_scr, vgb_scr = rest
    tm = x_ref.shape[0]
    xn = (_rms(x_ref[...]) * gmix_ref[...]).astype(BF16)
    cos, s1, s2 = cos_ref[...], s1_ref[...], s2_ref[...]

    zq = _dot(xn, win_ref[:, C_Q0:C_Q1])
    qln = (_rms(zq) * gql_ref[...]).astype(BF16)
    if prompt:
        qt = _dot_nt(wuq_ref[...], qln)
        cos_t, sin_t = cost_ref[...], sint_ref[...]
        x1_lo, x2_lo = NOPE_DIM, NOPE_DIM + HALF_ROPE
        zeros = jnp.zeros((HEAD_PAD - QK_DIM, tm), BF16)
        for h in range(N_HEADS):
            lo = h * HEAD_PAD
            nope = qt[lo:lo + NOPE_DIM]
            x1 = qt[lo + x1_lo:lo + x2_lo]
            x2 = qt[lo + x2_lo:lo + QK_DIM]
            ss = (jnp.sum(nope * nope, axis=0, keepdims=True)
                  + jnp.sum(x1 * x1 + x2 * x2, axis=0, keepdims=True))
            r = lax.rsqrt(ss * (1.0 / QK_DIM) + EPS)
            q_ref[lo:lo + NOPE_DIM, :] = (nope * r * gq_ref[:NOPE_DIM]).astype(BF16)
            q_ref[lo + x1_lo:lo + x2_lo, :] = (
                (x1 * cos_t - x2 * sin_t) * r * gq_ref[x1_lo:x2_lo]).astype(BF16)
            q_ref[lo + x2_lo:lo + QK_DIM, :] = (
                (x2 * cos_t + x1 * sin_t) * r * gq_ref[x2_lo:QK_DIM]).astype(BF16)
            q_ref[lo + QK_DIM:lo + HEAD_PAD, :] = zeros
    else:
        qraw = _dot(qln, wuq_ref[...])
        for h in range(N_HEADS):
            lo = h * HEAD_PAD
            for rc in range(tm // HEAD_ROWS):
                rows = slice(rc * HEAD_ROWS, (rc + 1) * HEAD_ROWS)
                nope = qraw[rows, lo:lo + LANES]
                rp = qraw[rows, lo + LANES:lo + HEAD_PAD]
                r = lax.rsqrt(_sum_lanes(nope * nope + rp * rp) * (1.0 / QK_DIM) + EPS)
                rp = _rope(rp, cos[rows], s1[rows], s2[rows])
                q_ref[rows, lo:lo + LANES] = (nope * r * gq_ref[:, :LANES]).astype(BF16)
                q_ref[rows, lo + LANES:lo + HEAD_PAD] = (rp * r * gq_ref[:, LANES:]).astype(BF16)

    ckv = _rms(_dot(xn, win_ref[:, C_KV0:C_KV1])) * gkv_ref[...]
    ckv_ref[...] = ckv
    cb = ckv.astype(BF16)
    pe = _rope(_dot(xn, win_ref[:, C_PE0:C_PE1]), cos, s1, s2)
    kpe_ref[...] = pe.T[:ROPE_DIM, :] if prompt else pe[:, :ROPE_DIM]
    ss_pe = _sumsq(pe)
    knope = _dot(cb, wuk_ref[...])
    if v_transposed:
        v_ref[...] = _dot_nt(wuv_ref[...], cb).astype(BF16)
    else:
        v_ref[...] = _dot(cb, wuv_ref[...]).astype(BF16)
    for h in range(N_HEADS):
        lo = h * HEAD_PAD
        for rc in range(tm // HEAD_ROWS):
            rows = slice(rc * HEAD_ROWS, (rc + 1) * HEAD_ROWS)
            nope = knope[rows, h * LANES:(h + 1) * LANES]
            r = lax.rsqrt((_sumsq(nope) + ss_pe[rows]) * (1.0 / QK_DIM) + EPS)
            k_ref[rows, lo:lo + LANES] = (nope * r).astype(BF16)
            k_ref[rows, lo + LANES:lo + HEAD_PAD] = (pe[rows] * r).astype(BF16)

    gv = _gelu(_dot(xn, win_ref[:, C_V0:C_V1]))
    for g in range(GMLP_GROUPS):
        cols = slice(g * LANES, (g + 1) * LANES)
        for rc in range(tm // HEAD_ROWS):
            rows = slice(rc * HEAD_ROWS, (rc + 1) * HEAD_ROWS)
            vg = _rms(gv[rows, cols]) * vn_ref[g:g + 1, :]
            if emit_v:
                vg_ref[rows, cols] = vg
            vgb_scr[rows, cols] = vg.astype(BF16)
    u = _gelu(_dot(xn, win_ref[:, C_U0:C_U1]))
    row = lax.broadcasted_iota(jnp.int32, (GMLP_CHUNK, GMLP_CHUNK), 0)
    col = lax.broadcasted_iota(jnp.int32, (GMLP_CHUNK, GMLP_CHUNK), 1)
    causal = (row // chunk_len == col // chunk_len) & (col <= row)
    for g in range(GMLP_GROUPS):
        wm = jnp.where(causal, ws_ref[g], 0.0).astype(BF16)
        bias = bst_ref[:, g:g + 1]
        for c in range(tm // GMLP_CHUNK):
            rows = slice(c * GMLP_CHUNK, (c + 1) * GMLP_CHUNK)
            cols = slice(g * LANES, (g + 1) * LANES)
            s = _dot(wm, vgb_scr[rows, cols]) + bias
            gate_scr[rows, cols] = u[rows, cols] * s
    gm_ref[...] = (_rms(gate_scr[...]) * gog_ref[...]).astype(BF16)


def _proj(x2d, tabs, wp, chunk_len, tm, f32_weights=None, tabs_t=None):
    prompt = f32_weights is not None
    m = x2d.shape[0]
    steps = m // tm
    cos, s1, s2 = tabs
    tab_blocks = cos.shape[0] // tm
    row_spec = lambda w: pl.BlockSpec((tm, w), lambda i: (i, 0))
    tab_spec = pl.BlockSpec((tm, LANES), lambda i: (i % tab_blocks, 0))
    consts = [wp["gmix"], wp["w_in"], wp["gql"], wp["gkv"], wp["w_uq_t"] if prompt else wp["w_uq"],
              wp["w_uk"], wp["w_uv_t"] if prompt else wp["w_uv"], wp["gq_t"] if prompt else wp["gq"]]
    consts2 = [wp["v_norm"], wp["ws"], wp["bst"], wp["gog"]]
    in_specs = ([row_spec(D_MODEL)] + [_const_spec(c.shape) for c in consts]
                + [tab_spec] * 3 + [_const_spec(c.shape) for c in consts2])
    q_shape = (steps * QK_WIDTH, tm) if prompt else (m, QK_WIDTH)
    q_spec = pl.BlockSpec((QK_WIDTH, tm), lambda i: (i, 0)) if prompt else row_spec(QK_WIDTH)
    v_shape = (steps * ATTN_WIDTH, tm) if prompt else (m, ATTN_WIDTH)
    v_spec = pl.BlockSpec((ATTN_WIDTH, tm), lambda i: (i, 0)) if prompt else row_spec(ATTN_WIDTH)
    kpe_shape = (m // cos.shape[0], ROPE_DIM, cos.shape[0]) if prompt else (m, ROPE_DIM)
    kpe_spec = (pl.BlockSpec((None, ROPE_DIM, tm), lambda i: (i // tab_blocks, 0, i % tab_blocks))
                if prompt else row_spec(ROPE_DIM))
    out_shape = [jax.ShapeDtypeStruct(q_shape, BF16), jax.ShapeDtypeStruct((m, QK_WIDTH), BF16),
                 jax.ShapeDtypeStruct(v_shape, BF16), jax.ShapeDtypeStruct((m, KV_RANK), F32),
                 jax.ShapeDtypeStruct(kpe_shape, F32), jax.ShapeDtypeStruct((m, GMLP_WIDTH), BF16)]
    out_specs = [q_spec, row_spec(QK_WIDTH), v_spec, row_spec(KV_RANK),
                 kpe_spec, row_spec(GMLP_WIDTH)]
    extra_in = []
    if prompt:
        tab_t_spec = pl.BlockSpec((HALF_ROPE, tm), lambda i: (0, i % tab_blocks))
        w_out, w_up, w_down = f32_weights
        slabs = [pl.BlockSpec((D_MODEL // steps, D_MODEL), lambda i: (i, 0)),
                 pl.BlockSpec((D_MODEL, D_FF // steps), lambda i: (0, i)),
                 pl.BlockSpec((D_FF // steps, D_MODEL), lambda i: (i, 0))]
        extra_in = [*tabs_t, w_out, w_up, w_down]
        in_specs += [tab_t_spec, tab_t_spec] + slabs
        out_shape += [jax.ShapeDtypeStruct(w.shape, BF16) for w in f32_weights]
        out_specs += slabs
    else:
        out_shape.append(jax.ShapeDtypeStruct((m, GMLP_WIDTH), F32))
        out_specs.append(row_spec(GMLP_WIDTH))
    return pl.pallas_call(
        functools.partial(_proj_kernel, chunk_len, prompt),
        out_shape=out_shape,
        grid=(steps,),
        in_specs=in_specs,
        out_specs=out_specs,
        scratch_shapes=[pltpu.VMEM((tm, GMLP_WIDTH), F32), pltpu.VMEM((tm, GMLP_WIDTH), BF16)],
        compiler_params=pltpu.CompilerParams(dimension_semantics=("parallel",),
                                             vmem_limit_bytes=VMEM_LIMIT),
        name="proj",
    )(x2d, *consts, cos, s1, s2, *consts2, *extra_in)


NEG = -1e30
SOFTMAX_KEY_CHUNK = 64
KEY_BLOCKS_PER_STEP = 2
SCORE_LEAD = 4


def _attn_prompt_kernel(q_ref, k_ref, vt_ref, go_ref, o_ref, acc_scr, m_scr, l_scr, s_scr, p_scr):
    t = q_ref.shape[1]
    qi = pl.program_id(1)
    m_scr[...] = jnp.full(m_scr.shape, NEG, F32)
    l_scr[...] = jnp.zeros(l_scr.shape, F32)
    acc_scr[...] = jnp.zeros(acc_scr.shape, F32)

    def step(kb0, nblk, last_masked):
        nk = nblk * t
        if last_masked:
            key = lax.broadcasted_iota(jnp.int32, (t, t), 0)
            qry = lax.broadcasted_iota(jnp.int32, (t, t), 1)
            ok = (key // CHUNK) <= (qry // CHUNK)

        m8 = {}

        def scores(h):
            qs = slice(h * HEAD_PAD, (h + 1) * HEAD_PAD)
            mh = None
            for j in range(nblk):
                rows = pl.ds(pl.multiple_of((kb0 + j) * t, t), t)
                s = _dot(k_ref[rows, qs], q_ref[qs, :])
                if last_masked and j == nblk - 1:
                    s = jnp.where(ok, s, NEG)
                s_scr[h, j * t:(j + 1) * t, :] = s
                mj = jnp.max(s.reshape(t // SUBLANES, SUBLANES, t), axis=0)
                mh = mj if mh is None else jnp.maximum(mh, mj)
            m8[h] = mh

        ones = jnp.ones((BF16_SUBLANES, nk), BF16)
        for h in range(min(SCORE_LEAD, N_HEADS)):
            scores(h)
        for h in range(N_HEADS):
            if h + SCORE_LEAD < N_HEADS:
                scores(h + SCORE_LEAD)
            hs = slice(h * V_DIM, (h + 1) * V_DIM)
            m_old = m_scr[h]
            m_new = jnp.maximum(m_old, jnp.max(m8[h], axis=0, keepdims=True))
            alpha = jnp.exp2(m_old - m_new)
            for c in range(nk // SOFTMAX_KEY_CHUNK):
                cs = slice(c * SOFTMAX_KEY_CHUNK, (c + 1) * SOFTMAX_KEY_CHUNK)
                p_scr[h, cs, :] = jnp.exp2(s_scr[h, cs, :] - m_new).astype(BF16)
            m_scr[h] = m_new
            vt = [vt_ref[pl.ds(pl.multiple_of((kb0 + j) * ATTN_WIDTH, ATTN_WIDTH) + h * V_DIM, V_DIM), :]
                  for j in range(nblk)]
            vt = vt[0] if nblk == 1 else jnp.concatenate(vt, axis=1)
            pv = _dot(jnp.concatenate([vt, ones], axis=0), p_scr[h, :nk, :])
            acc_scr[hs, :] = alpha * acc_scr[hs, :] + pv[:V_DIM]
            l_scr[h] = alpha * l_scr[h] + pv[V_DIM:V_DIM + 1]

    def body(j, carry):
        step(j * KEY_BLOCKS_PER_STEP, KEY_BLOCKS_PER_STEP, False)
        return carry

    lax.fori_loop(0, qi // KEY_BLOCKS_PER_STEP, body, 0)
    for rem in range(KEY_BLOCKS_PER_STEP):
        @pl.when(qi % KEY_BLOCKS_PER_STEP == rem)
        def _(rem=rem):
            step(qi - rem, rem + 1, True)

    for h in range(N_HEADS):
        hs = slice(h * V_DIM, (h + 1) * V_DIM)
        acc_scr[hs, :] = acc_scr[hs, :] / l_scr[h]
    o = acc_scr[...].T
    o_ref[...] = (_rms(o) * go_ref[...]).astype(BF16)


def _attn_prompt(q, k, vt, go, batch, seq, t):
    nq = seq // t
    assert vt.shape == (batch * nq * ATTN_WIDTH, t)
    tk = KEY_BLOCKS_PER_STEP * t
    return pl.pallas_call(
        _attn_prompt_kernel,
        out_shape=jax.ShapeDtypeStruct((batch * seq, ATTN_WIDTH), BF16),
        grid=(batch, nq),
        in_specs=[pl.BlockSpec((QK_WIDTH, t), lambda b, i: (b * nq + i, 0)),
                  pl.BlockSpec((seq, QK_WIDTH), lambda b, i: (b, 0)),
                  pl.BlockSpec((nq * ATTN_WIDTH, t), lambda b, i: (b, 0)),
                  _const_spec(go.shape)],
        out_specs=pl.BlockSpec((t, ATTN_WIDTH), lambda b, i: (b * nq + i, 0)),
        scratch_shapes=[pltpu.VMEM((ATTN_WIDTH, t), F32), pltpu.VMEM((N_HEADS, 1, t), F32),
                        pltpu.VMEM((N_HEADS, 1, t), F32), pltpu.VMEM((N_HEADS, tk, t), F32),
                        pltpu.VMEM((N_HEADS, tk, t), BF16)],
        compiler_params=pltpu.CompilerParams(dimension_semantics=("parallel", "arbitrary"),
                                             vmem_limit_bytes=VMEM_LIMIT),
        name="attn_prompt",
    )(q, k, vt, go)


def _attn_sample_kernel(past, c_ref, krt_ref, q_ref, kn_ref, vn_ref, wukt_ref, wuv_ref, go_ref,
                        o_ref, cb_scr, qa_scr, qr_scr, r_scr, sc_scr, sn_scr, pc_scr, acc_scr):
    nb = krt_ref.shape[0]
    t = q_ref.shape[0] // nb
    row = lax.broadcasted_iota(jnp.int32, (t, t), 0)
    col = lax.broadcasted_iota(jnp.int32, (t, t), 1)
    new_ok = ((past + col) // CHUNK) <= ((past + row) // CHUNK)
    denom = [[None] * N_HEADS for _ in range(nb)]
    p_new = [[None] * N_HEADS for _ in range(nb)]

    def norm_factors(j):
        cb_scr[j] = c_ref[j * past:(j + 1) * past, :].astype(BF16)
        krt = krt_ref[j]
        ss_pe = jnp.sum(krt * krt, axis=0, keepdims=True)
        knt = _dot_nt(wukt_ref[...], cb_scr[j])
        for h in range(N_HEADS):
            nope = knt[h * LANES:(h + 1) * LANES, :]
            ss = jnp.sum(nope * nope, axis=0, keepdims=True) + ss_pe
            r_scr[j, h] = lax.rsqrt(ss * (1.0 / QK_DIM) + EPS)

    def scores(j):
        qj = q_ref.at[j * t:(j + 1) * t, :]
        knj = kn_ref.at[j * t:(j + 1) * t, :]
        for h in range(N_HEADS):
            lo = h * HEAD_PAD
            rows = slice(h * t, (h + 1) * t)
            qa_scr[j, rows, :] = _dot(qj[:, lo:lo + NOPE_DIM],
                                      wukt_ref[h * NOPE_DIM:(h + 1) * NOPE_DIM, :]).astype(BF16)
            qr_scr[j, rows, :] = qj[:, lo + NOPE_DIM:lo + QK_DIM]
            sn_scr[j, h] = jnp.where(new_ok, _dot_nt(qj[:, lo:lo + HEAD_PAD], knj[:, lo:lo + HEAD_PAD]), NEG)
        sc_scr[j] = _dot_nt(qa_scr[j], cb_scr[j]) + _dot(qr_scr[j], krt_ref[j].astype(BF16))

    def softmax(j):
        for h in range(N_HEADS):
            s_c, s_n = sc_scr[j, h * t:(h + 1) * t, :] * r_scr[j, h], sn_scr[j, h]
            m = jnp.maximum(jnp.max(s_c, axis=-1, keepdims=True), jnp.max(s_n, axis=-1, keepdims=True))
            p_c = jnp.exp2(s_c - m)
            p_n = jnp.exp2(s_n - m)
            denom[j][h] = jnp.sum(p_c, axis=-1, keepdims=True) + jnp.sum(p_n, axis=-1, keepdims=True)
            p_new[j][h] = p_n.astype(BF16)
            pc_scr[j, h * t:(h + 1) * t, :] = p_c.astype(BF16)

    def values(j):
        vnj = vn_ref.at[j * t:(j + 1) * t, :]
        lat = _dot(pc_scr[j], cb_scr[j]).astype(BF16)
        for h in range(N_HEADS):
            vs = slice(h * V_DIM, (h + 1) * V_DIM)
            acc = _dot(lat[h * t:(h + 1) * t, :], wuv_ref[:, vs]) + _dot(p_new[j][h], vnj[:, vs])
            acc_scr[j, :, vs] = acc / denom[j][h]
        o_ref[j * t:(j + 1) * t, :] = (_rms(acc_scr[j]) * go_ref[...]).astype(BF16)

    stages = (norm_factors, scores, softmax, values)
    for tick in range(nb + len(stages) - 1):
        for j in range(nb):
            if 0 <= tick - j < len(stages):
                stages[tick - j](j)


def _attn_sample(c2d, krt, q, kn, vn, wp, batch, t, past, nb):
    consts = [wp["w_uk_t"], wp["w_uv"], wp["goa"]]
    ht = N_HEADS * t
    return pl.pallas_call(
        functools.partial(_attn_sample_kernel, past),
        out_shape=jax.ShapeDtypeStruct((batch * t, ATTN_WIDTH), BF16),
        grid=(batch // nb,),
        in_specs=[pl.BlockSpec((nb * past, KV_RANK), lambda b: (b, 0)),
                  pl.BlockSpec((nb, ROPE_DIM, past), lambda b: (b, 0, 0)),
                  pl.BlockSpec((nb * t, QK_WIDTH), lambda b: (b, 0)),
                  pl.BlockSpec((nb * t, QK_WIDTH), lambda b: (b, 0)),
                  pl.BlockSpec((nb * t, ATTN_WIDTH), lambda b: (b, 0))]
                 + [_const_spec(c.shape) for c in consts],
        out_specs=pl.BlockSpec((nb * t, ATTN_WIDTH), lambda b: (b, 0)),
        scratch_shapes=[pltpu.VMEM((nb, past, KV_RANK), BF16), pltpu.VMEM((nb, ht, KV_RANK), BF16),
                        pltpu.VMEM((nb, ht, ROPE_DIM), BF16), pltpu.VMEM((nb, N_HEADS, 1, past), F32),
                        pltpu.VMEM((nb, ht, past), F32), pltpu.VMEM((nb, N_HEADS, t, t), F32),
                        pltpu.VMEM((nb, ht, past), BF16), pltpu.VMEM((nb, t, ATTN_WIDTH), F32)],
        compiler_params=pltpu.CompilerParams(dimension_semantics=("parallel",),
                                             vmem_limit_bytes=VMEM_LIMIT),
        name="attn_sample",
    )(c2d, krt, q, kn, vn, *consts)


def _outproj_kernel(an_ref, gm_ref, x_ref, wo_ref, gffn_ref, h_ref, hn_ref):
    h = (x_ref[...] + _dot(an_ref[...], wo_ref[:ATTN_WIDTH, :])
         + _dot(gm_ref[...], wo_ref[ATTN_WIDTH:, :]))
    h_ref[...] = h
    hn_ref[...] = (_rms(h) * gffn_ref[...]).astype(BF16)


def _outproj(an, gm, x2d, wp, tm):
    m = x2d.shape[0]
    row_spec = lambda w: pl.BlockSpec((tm, w), lambda i: (i, 0))
    return pl.pallas_call(
        _outproj_kernel,
        out_shape=[jax.ShapeDtypeStruct((m, D_MODEL), F32), jax.ShapeDtypeStruct((m, D_MODEL), BF16)],
        grid=(m // tm,),
        in_specs=[row_spec(ATTN_WIDTH), row_spec(GMLP_WIDTH), row_spec(D_MODEL),
                  _const_spec(wp["w_out"].shape), _const_spec(wp["gffn"].shape)],
        out_specs=[row_spec(D_MODEL), row_spec(D_MODEL)],
        compiler_params=pltpu.CompilerParams(dimension_semantics=("parallel",),
                                             vmem_limit_bytes=VMEM_LIMIT),
        name="outproj",
    )(an, gm, x2d, wp["w_out"], wp["gffn"])


def _ffn_kernel(h_ref, hn_ref, wu_ref, wd_ref, y_ref):
    f = pl.program_id(1)
    slab = h_ref.shape[1]

    def down():
        a = jnp.maximum(_dot(hn_ref[...], wu_ref[...]), 0.0)
        return _dot((a * a).astype(BF16), wd_ref[...])

    @pl.when(f == 0)
    def _():
        d = down()
        y_ref[:, :slab] = d[:, :slab] + h_ref[...]
        y_ref[:, slab:] = d[:, slab:]

    @pl.when(f > 0)
    def _():
        cols = pl.ds(pl.multiple_of(f * slab, slab), slab)
        y_ref[:, cols] += h_ref[...]
        y_ref[...] += down()


def _ffn(h, hn, wp, tm, tf):
    m = h.shape[0]
    nf = D_FF // tf
    return pl.pallas_call(
        _ffn_kernel,
        out_shape=jax.ShapeDtypeStruct((m, D_MODEL), F32),
        grid=(m // tm, nf),
        in_specs=[pl.BlockSpec((tm, D_MODEL // nf), lambda i, f: (i, f)),
                  pl.BlockSpec((tm, D_MODEL), lambda i, f: (i, 0)),
                  pl.BlockSpec((D_MODEL, tf), lambda i, f: (0, f)),
                  pl.BlockSpec((tf, D_MODEL), lambda i, f: (f, 0))],
        out_specs=pl.BlockSpec((tm, D_MODEL), lambda i, f: (i, 0)),
        compiler_params=pltpu.CompilerParams(dimension_semantics=("parallel", "arbitrary"),
                                             vmem_limit_bytes=VMEM_LIMIT),
        name="ffn",
    )(h, hn, wp["w_up"], wp["w_down"])


def _rope_tables(pos):
    inv = ROPE_THETA ** (-np.arange(HALF_ROPE, dtype=np.float64) / HALF_ROPE)
    ang = np.asarray(pos, np.float64)[:, None] * inv[None, :]
    cos, sin = np.cos(ang), np.sin(ang)
    z = np.zeros_like(cos)
    z2 = np.zeros((ang.shape[0], LANES - ROPE_DIM))
    tabs = (np.concatenate([cos, cos, z2], axis=1), np.concatenate([-sin, z, z2], axis=1),
            np.concatenate([z, sin, z2], axis=1))
    tabs_t = (cos.T, sin.T)
    return tuple(jnp.asarray(t, F32) for t in tabs), tuple(jnp.asarray(t, F32) for t in tabs_t)


def _tile_lanes(x, width):
    if width == LANES:
        return x
    lane = lax.broadcasted_iota(jnp.int32, x.shape, 1)
    x = jnp.where(lane < width, x, 0.0)
    out = x
    for r in range(1, LANES // width):
        out = out + pltpu.roll(x, r * width, 1)
    return out


def _pad_w_in_kernel(dec_seq, wt_ref, uq_ref, uk_ref, uv_ref, wsp_ref, bsp_ref, qn_ref, qr_ref, kn_ref,
                     kr_ref, o_ref, wq_ref, wqt_ref, wuk_ref, wukt_ref, wuv_ref, wuvt_ref, wss_ref,
                     bstp_ref, bsts_ref, gq_ref, gqt_ref, pad_scr):
    tr = wt_ref.shape[1]
    split = C_PE0 + ROPE_DIM
    o_ref[:, :C_PE0] = wt_ref[:C_PE0, :].T.astype(BF16)
    pe = wt_ref[C_PE0:C_PE1, :].T
    lane = lax.broadcasted_iota(jnp.int32, (tr, LANES), 1)
    o_ref[:, C_PE0:C_PE1] = jnp.where(lane < ROPE_DIM, pe, 0.0).astype(BF16)
    o_ref[:, C_PE1:] = wt_ref[split:, :].T.astype(BF16)

    i = pl.program_id(0)

    @pl.when(i == 0)
    def _():
        uq_t = uq_ref[...].T
        for h in range(N_HEADS):
            lo = h * HEAD_PAD
            pad_scr[lo:lo + QK_DIM, :] = uq_t[h * QK_DIM:(h + 1) * QK_DIM, :]
            pad_scr[lo + QK_DIM:lo + HEAD_PAD, :] = jnp.zeros((HEAD_PAD - QK_DIM, Q_RANK), F32)
        wqt_ref[...] = pad_scr[...].astype(BF16)

    @pl.when(i == 1)
    def _():
        wq_ref[...] = pad_scr[...].T.astype(BF16)

    for step, (src, dst, dst_t) in enumerate(((uk_ref, wuk_ref, wukt_ref), (uv_ref, wuv_ref, wuvt_ref))):
        @pl.when(i == 2 + step)
        def _(src=src, dst=dst, dst_t=dst_t):
            w = src[...]
            dst[...] = w.astype(BF16)
            dst_t[...] = w.T.astype(BF16)

    @pl.when(i == 4)
    def _():
        for g in range(GMLP_GROUPS):
            w = _tile_lanes(wsp_ref[g], dec_seq)
            wss_ref[g] = jnp.concatenate([w[:dec_seq]] * (GMLP_CHUNK // dec_seq), axis=0)
        b = bsp_ref[...]
        fill = jnp.zeros((LANES - GMLP_GROUPS, LANES), F32)
        for dst, width in ((bstp_ref, GMLP_CHUNK), (bsts_ref, dec_seq)):
            dst[...] = jnp.concatenate([_tile_lanes(b, width), fill], axis=0).T

    @pl.when(i == 5)
    def _():
        scale = QK_DIM ** -0.5 * math.log2(math.e)
        nope = (qn_ref[...] * scale) * kn_ref[...]
        rope = (qr_ref[...] * scale) * kr_ref[...]
        rope = jnp.concatenate([rope, rope, jnp.zeros((1, HEAD_PAD - QK_DIM), F32)], axis=1)
        for half, g in enumerate((nope, rope)):
            cols = slice(half * LANES, (half + 1) * LANES)
            gq_ref[:, cols] = g
            col = jnp.broadcast_to(g, (LANES, LANES)).T
            gqt_ref[cols, :] = jnp.concatenate([col] * (gqt_ref.shape[1] // LANES), axis=1)


def _pad_w_in(w_in_t, w_uq, w_uk, w_uv, w_spatial, b_spatial, q_nope, q_rope, k_nope, k_rope, dec_seq,
              tr):
    width, rows = w_in_t.shape
    assert width + LANES - ROPE_DIM == IN_PAD and rows // tr >= 6
    assert w_uq.shape == (Q_RANK, N_HEADS * QK_DIM) and GMLP_CHUNK == LANES
    assert w_spatial.shape == (GMLP_GROUPS, GMLP_CHUNK, GMLP_CHUNK)
    small = (w_uq, w_uk, w_uv, w_spatial, b_spatial, q_nope[None, :], q_rope[None, :],
             k_nope[None, :], k_rope[None, :])
    outs = [(Q_RANK, QK_WIDTH), (QK_WIDTH, Q_RANK), w_uk.shape, w_uk.shape[::-1], w_uv.shape,
            w_uv.shape[::-1]]
    outs_f32 = [w_spatial.shape, (LANES, LANES), (LANES, LANES), (1, HEAD_PAD), (HEAD_PAD, PROJ_ROWS)]
    return pl.pallas_call(
        functools.partial(_pad_w_in_kernel, dec_seq),
        out_shape=[jax.ShapeDtypeStruct((rows, IN_PAD), BF16)]
        + [jax.ShapeDtypeStruct(s, BF16) for s in outs]
        + [jax.ShapeDtypeStruct(s, F32) for s in outs_f32],
        grid=(rows // tr,),
        in_specs=[pl.BlockSpec((width, tr), lambda i: (0, i))] + [_const_spec(w.shape) for w in small],
        out_specs=[pl.BlockSpec((tr, IN_PAD), lambda i: (i, 0))]
        + [_const_spec(s) for s in outs + outs_f32],
        scratch_shapes=[pltpu.VMEM((QK_WIDTH, Q_RANK), F32)],
        compiler_params=pltpu.CompilerParams(dimension_semantics=("arbitrary",),
                                             vmem_limit_bytes=VMEM_LIMIT),
        name="pad_w_in",
    )(w_in_t, *small)


def _prep_weights(norm_mix, w_in, q_lat_norm, kv_lat_norm, w_uq, w_uk, w_uv, q_norm_nope, q_norm_rope,
                  k_norm_nope, k_norm_rope, v_norm, w_spatial, b_spatial, out_norm_attn, out_norm_gmlp,
                  norm_ffn, dec_seq):
    w_in_p, wq, wq_t, wuk, wuk_t, wuv, wuv_t, ws_s, bst, bst_s, gq, gq_t = _pad_w_in(
        w_in.T, w_uq, w_uk, w_uv, w_spatial, b_spatial, q_norm_nope, q_norm_rope, k_norm_nope,
        k_norm_rope, dec_seq, PAD_ROWS)
    return {
        "gmix": norm_mix[None, :], "w_in": w_in_p, "gql": q_lat_norm[None, :], "gkv": kv_lat_norm[None, :],
        "w_uq": wq, "w_uq_t": wq_t, "w_uk": wuk, "w_uk_t": wuk_t, "w_uv": wuv, "w_uv_t": wuv_t,
        "gq": gq, "gq_t": gq_t, "v_norm": v_norm, "ws": w_spatial, "bst": bst, "ws_s": ws_s,
        "bst_s": bst_s, "gog": out_norm_gmlp[None, :], "goa": out_norm_attn[None, :],
        "gffn": norm_ffn[None, :],
    }


def _finish(an, gm, x2d, wp):
    h, hn = _outproj(an, gm, x2d, wp, OUTPROJ_ROWS)
    return _ffn(h, hn, wp, FFN_ROWS, FFN_COLS)


def kernel(x_prompt, x_sample, cache_c_kv, cache_k_rope, norm_mix, w_in, q_lat_norm, kv_lat_norm, w_uq, w_uk, w_uv, q_norm_nope, q_norm_rope, k_norm_nope, k_norm_rope, v_norm, w_spatial, b_spatial, out_norm_attn, out_norm_gmlp, w_out, norm_ffn, w_up, w_down):
    depth = w_in.shape[0]
    assert depth == 1
    batch, seq, _ = x_prompt.shape
    dec_batch, dec_seq, _ = x_sample.shape
    past = cache_c_kv.shape[2]
    assert past % CHUNK == 0 and dec_seq <= CHUNK and GMLP_CHUNK % dec_seq == 0
    n_prompt, n_sample = batch * seq, dec_batch * dec_seq
    assert seq % PROJ_ROWS == 0 and n_sample % PROJ_ROWS == 0 and PROJ_ROWS % dec_seq == 0
    assert n_prompt % FFN_ROWS == 0 and n_sample % FFN_ROWS == 0 and dec_batch % SAMPLE_BATCH_ROWS == 0

    weights = (norm_mix[0], w_in[0], q_lat_norm[0], kv_lat_norm[0], w_uq[0], w_uk[0], w_uv[0],
               q_norm_nope[0], q_norm_rope[0], k_norm_nope[0], k_norm_rope[0], v_norm[0], w_spatial[0],
               b_spatial[0], out_norm_attn[0], out_norm_gmlp[0], norm_ffn[0])
    wp = _prep_weights(*weights, dec_seq=dec_seq)

    xp = x_prompt.reshape(batch * seq, D_MODEL)
    tabs_p, tabs_pt = _rope_tables(np.arange(seq))
    qt, k, vt, ckv_p, kpe_p, gm, wo_b, wu_b, wd_b = _proj(
        xp, tabs_p, wp, GMLP_CHUNK, PROJ_ROWS, f32_weights=(w_out[0], w_up[0], w_down[0]),
        tabs_t=tabs_pt)
    wp = dict(wp, w_out=wo_b, w_up=wu_b, w_down=wd_b)
    wp_s = dict(wp, ws=wp["ws_s"], bst=wp["bst_s"])
    an = _attn_prompt(qt, k, vt, wp["goa"], batch, seq, PROJ_ROWS)
    y_p = _finish(an, gm, xp, wp)

    xs = x_sample.reshape(dec_batch * dec_seq, D_MODEL)
    tabs_s, _ = _rope_tables(past + np.arange(PROJ_ROWS) % dec_seq)
    qs, kn, vn, ckv_s, kpe_s, gms, vg_s = _proj(xs, tabs_s, wp_s, dec_seq, PROJ_ROWS)
    ans = _attn_sample(cache_c_kv[0].reshape(dec_batch * past, KV_RANK),
                       jnp.swapaxes(cache_k_rope[0], 1, 2),
                       qs, kn, vn, wp, dec_batch, dec_seq, past, SAMPLE_BATCH_ROWS)
    y_s = _finish(ans, gms, xs, wp)

    return (y_p.reshape(batch, seq, D_MODEL),
            y_s.reshape(dec_batch, dec_seq, D_MODEL),
            ckv_p.reshape(1, batch, seq, KV_RANK),
            jnp.swapaxes(kpe_p, 1, 2)[None],
            ckv_s.reshape(1, dec_batch, dec_seq, KV_RANK),
            kpe_s.reshape(1, dec_batch, dec_seq, ROPE_DIM),
            vg_s.reshape(1, dec_batch, dec_seq, GMLP_WIDTH))
```

```python
import functools
import math

import jax
import jax.numpy as jnp
import numpy as np
from jax import lax
from jax.experimental import pallas as pl
from jax.experimental.pallas import tpu as pltpu

D_MODEL = 2048
CHUNK = 64
N_HEADS = 8
NOPE_DIM = 128
ROPE_DIM = 64
HALF_ROPE = ROPE_DIM // 2
QK_DIM = NOPE_DIM + ROPE_DIM
V_DIM = 128
ATTN_WIDTH = N_HEADS * V_DIM
Q_RANK = 512
KV_RANK = 256
GMLP_GROUPS = 8
GMLP_GROUP_DIM = 128
GMLP_WIDTH = GMLP_GROUPS * GMLP_GROUP_DIM
GMLP_CHUNK = 128
D_FF = 4 * D_MODEL
ROPE_THETA = 10000.0
EPS = 1e-6

LANES = 128
SUBLANES = 8
BF16_SUBLANES = 2 * SUBLANES
HEAD_PAD = 2 * LANES
QK_WIDTH = N_HEADS * HEAD_PAD
C_Q0, C_Q1 = 0, Q_RANK
C_KV0, C_KV1 = C_Q1, C_Q1 + KV_RANK
C_PE0, C_PE1 = C_KV1, C_KV1 + LANES
C_U0, C_U1 = C_PE1, C_PE1 + GMLP_WIDTH
C_V0, C_V1 = C_U1, C_U1 + GMLP_WIDTH
IN_PAD = C_V1
VMEM_LIMIT = 56 * 1024 * 1024

PROJ_ROWS = 256
HEAD_ROWS = 128
PAD_ROWS = 256
OUTPROJ_ROWS = 512
FFN_ROWS = 1024
FFN_COLS = 1024
SAMPLE_BATCH_ROWS = 8

BF16 = jnp.bfloat16
F32 = jnp.float32


def _dot(a, b):
    return jnp.dot(a, b, preferred_element_type=F32)


def _dot_nt(a, b):
    return lax.dot_general(a, b, (((1,), (1,)), ((), ())), preferred_element_type=F32)


def _sum_lanes(x):
    return jnp.sum(x, axis=-1, keepdims=True)


def _sumsq(x):
    return _sum_lanes(x * x)


def _rms(x):
    return x * lax.rsqrt(jnp.mean(x * x, axis=-1, keepdims=True) + EPS)


def _gelu(x):
    c = math.sqrt(2.0 / math.pi)
    return 0.5 * x * (1.0 + jnp.tanh(c * (x + 0.044715 * (x * x * x))))


def _rope(t, cos, s1, s2):
    return t * cos + pltpu.roll(t, LANES - HALF_ROPE, 1) * s1 + pltpu.roll(t, HALF_ROPE, 1) * s2


def _const_spec(shape):
    nd = len(shape)
    return pl.BlockSpec(shape, lambda *_: (0,) * nd, pipeline_mode=pl.Buffered(1))


def _proj_kernel(chunk_len, prompt, x_ref, gmix_ref, win_ref, gql_ref, gkv_ref, wuq_ref, wuk_ref,
                 wuv_ref, gq_ref, cos_ref, s1_ref, s2_ref, vn_ref, ws_ref, bst_ref, gog_ref, *rest):
    v_transposed = prompt
    emit_v = not prompt
    if prompt:
        (cost_ref, sint_ref, wo_ref, wu_ref, wd_ref, q_ref, k_ref, v_ref, ckv_ref, kpe_ref, gm_ref,
         wo_out, wu_out, wd_out, gate_scr, vgb_scr) = rest
        wo_out[...] = wo_ref[...].astype(BF16)
        wu_out[...] = wu_ref[...].astype(BF16)
        wd_out[...] = wd_ref[...].astype(BF16)
    else:
        q_ref, k_ref, v_ref, ckv_ref, kpe_ref, gm_ref, vg_ref, gate_scr, vgb_scr = rest
    tm = x_ref.shape[0]
    xn = (_rms(x_ref[...]) * gmix_ref[...]).astype(BF16)
    cos, s1, s2 = cos_ref[...], s1_ref[...], s2_ref[...]

    zq = _dot(xn, win_ref[:, C_Q0:C_Q1])
    qln = (_rms(zq) * gql_ref[...]).astype(BF16)
    if prompt:
        qt = _dot_nt(wuq_ref[...], qln)
        cos_t, sin_t = cost_ref[...], sint_ref[...]
        x1_lo, x2_lo = NOPE_DIM, NOPE_DIM + HALF_ROPE
        zeros = jnp.zeros((HEAD_PAD - QK_DIM, tm), BF16)
        for h in range(N_HEADS):
            lo = h * HEAD_PAD
            nope = qt[lo:lo + NOPE_DIM]
            x1 = qt[lo + x1_lo:lo + x2_lo]
            x2 = qt[lo + x2_lo:lo + QK_DIM]
            ss = (jnp.sum(nope * nope, axis=0, keepdims=True)
                  + jnp.sum(x1 * x1 + x2 * x2, axis=0, keepdims=True))
            r = lax.rsqrt(ss * (1.0 / QK_DIM) + EPS)
            q_ref[lo:lo + NOPE_DIM, :] = (nope * r * gq_ref[:NOPE_DIM]).astype(BF16)
            q_ref[lo + x1_lo:lo + x2_lo, :] = (
                (x1 * cos_t - x2 * sin_t) * r * gq_ref[x1_lo:x2_lo]).astype(BF16)
            q_ref[lo + x2_lo:lo + QK_DIM, :] = (
                (x2 * cos_t + x1 * sin_t) * r * gq_ref[x2_lo:QK_DIM]).astype(BF16)
            q_ref[lo + QK_DIM:lo + HEAD_PAD, :] = zeros
    else:
        qraw = _dot(qln, wuq_ref[...])
        for h in range(N_HEADS):
            lo = h * HEAD_PAD
            for rc in range(tm // HEAD_ROWS):
                rows = slice(rc * HEAD_ROWS, (rc + 1) * HEAD_ROWS)
                nope = qraw[rows, lo:lo + LANES]
                rp = qraw[rows, lo + LANES:lo + HEAD_PAD]
                r = lax.rsqrt(_sum_lanes(nope * nope + rp * rp) * (1.0 / QK_DIM) + EPS)
                rp = _rope(rp, cos[rows], s1[rows], s2[rows])
                q_ref[rows, lo:lo + LANES] = (nope * r * gq_ref[:, :LANES]).astype(BF16)
                q_ref[rows, lo + LANES:lo + HEAD_PAD] = (rp * r * gq_ref[:, LANES:]).astype(BF16)

    ckv = _rms(_dot(xn, win_ref[:, C_KV0:C_KV1])) * gkv_ref[...]
    ckv_ref[...] = ckv
    cb = ckv.astype(BF16)
    pe = _rope(_dot(xn, win_ref[:, C_PE0:C_PE1]), cos, s1, s2)
    kpe_ref[...] = pe.T[:ROPE_DIM, :] if prompt else pe[:, :ROPE_DIM]
    ss_pe = _sumsq(pe)
    knope = _dot(cb, wuk_ref[...])
    if v_transposed:
        v_ref[...] = _dot_nt(wuv_ref[...], cb).astype(BF16)
    else:
        v_ref[...] = _dot(cb, wuv_ref[...]).astype(BF16)
    for h in range(N_HEADS):
        lo = h * HEAD_PAD
        for rc in range(tm // HEAD_ROWS):
            rows = slice(rc * HEAD_ROWS, (rc + 1) * HEAD_ROWS)
            nope = knope[rows, h * LANES:(h + 1) * LANES]
            r = lax.rsqrt((_sumsq(nope) + ss_pe[rows]) * (1.0 / QK_DIM) + EPS)
            k_ref[rows, lo:lo + LANES] = (nope * r).astype(BF16)
            k_ref[rows, lo + LANES:lo + HEAD_PAD] = (pe[rows] * r).astype(BF16)

    gv = _gelu(_dot(xn, win_ref[:, C_V0:C_V1]))
    for g in range(GMLP_GROUPS):
        cols = slice(g * LANES, (g + 1) * LANES)
        for rc in range(tm // HEAD_ROWS):
            rows = slice(rc * HEAD_ROWS, (rc + 1) * HEAD_ROWS)
            vg = _rms(gv[rows, cols]) * vn_ref[g:g + 1, :]
            if emit_v:
                vg_ref[rows, cols] = vg
            vgb_scr[rows, cols] = vg.astype(BF16)
    u = _gelu(_dot(xn, win_ref[:, C_U0:C_U1]))
    row = lax.broadcasted_iota(jnp.int32, (GMLP_CHUNK, GMLP_CHUNK), 0)
    col = lax.broadcasted_iota(jnp.int32, (GMLP_CHUNK, GMLP_CHUNK), 1)
    causal = (row // chunk_len == col // chunk_len) & (col <= row)
    for g in range(GMLP_GROUPS):
        wm = jnp.where(causal, ws_ref[g], 0.0).astype(BF16)
        bias = bst_ref[:, g:g + 1]
        for c in range(tm // GMLP_CHUNK):
            rows = slice(c * GMLP_CHUNK, (c + 1) * GMLP_CHUNK)
            cols = slice(g * LANES, (g + 1) * LANES)
            s = _dot(wm, vgb_scr[rows, cols]) + bias
            gate_scr[rows, cols] = u[rows, cols] * s
    gm_ref[...] = (_rms(gate_scr[...]) * gog_ref[...]).astype(BF16)


def _proj(x2d, tabs, wp, chunk_len, tm, f32_weights=None, tabs_t=None):
    prompt = f32_weights is not None
    m = x2d.shape[0]
    steps = m // tm
    cos, s1, s2 = tabs
    tab_blocks = cos.shape[0] // tm
    row_spec = lambda w: pl.BlockSpec((tm, w), lambda i: (i, 0))
    tab_spec = pl.BlockSpec((tm, LANES), lambda i: (i % tab_blocks, 0))
    consts = [wp["gmix"], wp["w_in"], wp["gql"], wp["gkv"], wp["w_uq_t"] if prompt else wp["w_uq"],
              wp["w_uk"], wp["w_uv_t"] if prompt else wp["w_uv"], wp["gq_t"] if prompt else wp["gq"]]
    consts2 = [wp["v_norm"], wp["ws"], wp["bst"], wp["gog"]]
    in_specs = ([row_spec(D_MODEL)] + [_const_spec(c.shape) for c in consts]
                + [tab_spec] * 3 + [_const_spec(c.shape) for c in consts2])
    q_shape = (steps * QK_WIDTH, tm) if prompt else (m, QK_WIDTH)
    q_spec = pl.BlockSpec((QK_WIDTH, tm), lambda i: (i, 0)) if prompt else row_spec(QK_WIDTH)
    v_shape = (steps * ATTN_WIDTH, tm) if prompt else (m, ATTN_WIDTH)
    v_spec = pl.BlockSpec((ATTN_WIDTH, tm), lambda i: (i, 0)) if prompt else row_spec(ATTN_WIDTH)
    kpe_shape = (m // cos.shape[0], ROPE_DIM, cos.shape[0]) if prompt else (m, ROPE_DIM)
    kpe_spec = (pl.BlockSpec((None, ROPE_DIM, tm), lambda i: (i // tab_blocks, 0, i % tab_blocks))
                if prompt else row_spec(ROPE_DIM))
    out_shape = [jax.ShapeDtypeStruct(q_shape, BF16), jax.ShapeDtypeStruct((m, QK_WIDTH), BF16),
                 jax.ShapeDtypeStruct(v_shape, BF16), jax.ShapeDtypeStruct((m, KV_RANK), F32),
                 jax.ShapeDtypeStruct(kpe_shape, F32), jax.ShapeDtypeStruct((m, GMLP_WIDTH), BF16)]
    out_specs = [q_spec, row_spec(QK_WIDTH), v_spec, row_spec(KV_RANK),
                 kpe_spec, row_spec(GMLP_WIDTH)]
    extra_in = []
    if prompt:
        tab_t_spec = pl.BlockSpec((HALF_ROPE, tm), lambda i: (0, i % tab_blocks))
        w_out, w_up, w_down = f32_weights
        slabs = [pl.BlockSpec((D_MODEL // steps, D_MODEL), lambda i: (i, 0)),
                 pl.BlockSpec((D_MODEL, D_FF // steps), lambda i: (0, i)),
                 pl.BlockSpec((D_FF // steps, D_MODEL), lambda i: (i, 0))]
        extra_in = [*tabs_t, w_out, w_up, w_down]
        in_specs += [tab_t_spec, tab_t_spec] + slabs
        out_shape += [jax.ShapeDtypeStruct(w.shape, BF16) for w in f32_weights]
        out_specs += slabs
    else:
        out_shape.append(jax.ShapeDtypeStruct((m, GMLP_WIDTH), F32))
        out_specs.append(row_spec(GMLP_WIDTH))
    return pl.pallas_call(
        functools.partial(_proj_kernel, chunk_len, prompt),
        out_shape=out_shape,
        grid=(steps,),
        in_specs=in_specs,
        out_specs=out_specs,
        scratch_shapes=[pltpu.VMEM((tm, GMLP_WIDTH), F32), pltpu.VMEM((tm, GMLP_WIDTH), BF16)],
        compiler_params=pltpu.CompilerParams(dimension_semantics=("parallel",),
                                             vmem_limit_bytes=VMEM_LIMIT),
        name="proj",
    )(x2d, *consts, cos, s1, s2, *consts2, *extra_in)


NEG = -1e30
SOFTMAX_KEY_CHUNK = 64
KEY_BLOCKS_PER_STEP = 2
SCORE_LEAD = 4


def _attn_prompt_kernel(q_ref, k_ref, vt_ref, go_ref, o_ref, acc_scr, m_scr, l_scr, s_scr, p_scr):
    t = q_ref.shape[1]
    qi = pl.program_id(1)
    m_scr[...] = jnp.full(m_scr.shape, NEG, F32)
    l_scr[...] = jnp.zeros(l_scr.shape, F32)
    acc_scr[...] = jnp.zeros(acc_scr.shape, F32)

    def step(kb0, nblk, last_masked):
        nk = nblk * t
        if last_masked:
            key = lax.broadcasted_iota(jnp.int32, (t, t), 0)
            qry = lax.broadcasted_iota(jnp.int32, (t, t), 1)
            ok = (key // CHUNK) <= (qry // CHUNK)

        m8 = {}

        def scores(h):
            qs = slice(h * HEAD_PAD, (h + 1) * HEAD_PAD)
            mh = None
            for j in range(nblk):
                rows = pl.ds(pl.multiple_of((kb0 + j) * t, t), t)
                s = _dot(k_ref[rows, qs], q_ref[qs, :])
                if last_masked and j == nblk - 1:
                    s = jnp.where(ok, s, NEG)
                s_scr[h, j * t:(j + 1) * t, :] = s
                mj = jnp.max(s.reshape(t // SUBLANES, SUBLANES, t), axis=0)
                mh = mj if mh is None else jnp.maximum(mh, mj)
            m8[h] = mh

        ones = jnp.ones((BF16_SUBLANES, nk), BF16)
        for h in range(min(SCORE_LEAD, N_HEADS)):
            scores(h)
        for h in range(N_HEADS):
            if h + SCORE_LEAD < N_HEADS:
                scores(h + SCORE_LEAD)
            hs = slice(h * V_DIM, (h + 1) * V_DIM)
            m_old = m_scr[h]
            m_new = jnp.maximum(m_old, jnp.max(m8[h], axis=0, keepdims=True))
            alpha = jnp.exp2(m_old - m_new)
            for c in range(nk // SOFTMAX_KEY_CHUNK):
                cs = slice(c * SOFTMAX_KEY_CHUNK, (c + 1) * SOFTMAX_KEY_CHUNK)
                p_scr[h, cs, :] = jnp.exp2(s_scr[h, cs, :] - m_new).astype(BF16)
            m_scr[h] = m_new
            vt = [vt_ref[pl.ds(pl.multiple_of((kb0 + j) * ATTN_WIDTH, ATTN_WIDTH) + h * V_DIM, V_DIM), :]
                  for j in range(nblk)]
            vt = vt[0] if nblk == 1 else jnp.concatenate(vt, axis=1)
            pv = _dot(jnp.concatenate([vt, ones], axis=0), p_scr[h, :nk, :])
            acc_scr[hs, :] = alpha * acc_scr[hs, :] + pv[:V_DIM]
            l_scr[h] = alpha * l_scr[h] + pv[V_DIM:V_DIM + 1]

    def body(j, carry):
        step(j * KEY_BLOCKS_PER_STEP, KEY_BLOCKS_PER_STEP, False)
        return carry

    lax.fori_loop(0, qi // KEY_BLOCKS_PER_STEP, body, 0)
    for rem in range(KEY_BLOCKS_PER_STEP):
        @pl.when(qi % KEY_BLOCKS_PER_STEP == rem)
        def _(rem=rem):
            step(qi - rem, rem + 1, True)

    for h in range(N_HEADS):
        hs = slice(h * V_DIM, (h + 1) * V_DIM)
        acc_scr[hs, :] = acc_scr[hs, :] / l_scr[h]
    o = acc_scr[...].T
    o_ref[...] = (_rms(o) * go_ref[...]).astype(BF16)


def _attn_prompt(q, k, vt, go, batch, seq, t):
    nq = seq // t
    assert vt.shape == (batch * nq * ATTN_WIDTH, t)
    tk = KEY_BLOCKS_PER_STEP * t
    return pl.pallas_call(
        _attn_prompt_kernel,
        out_shape=jax.ShapeDtypeStruct((batch * seq, ATTN_WIDTH), BF16),
        grid=(batch, nq),
        in_specs=[pl.BlockSpec((QK_WIDTH, t), lambda b, i: (b * nq + i, 0)),
                  pl.BlockSpec((seq, QK_WIDTH), lambda b, i: (b, 0)),
                  pl.BlockSpec((nq * ATTN_WIDTH, t), lambda b, i: (b, 0)),
                  _const_spec(go.shape)],
        out_specs=pl.BlockSpec((t, ATTN_WIDTH), lambda b, i: (b * nq + i, 0)),
        scratch_shapes=[pltpu.VMEM((ATTN_WIDTH, t), F32), pltpu.VMEM((N_HEADS, 1, t), F32),
                        pltpu.VMEM((N_HEADS, 1, t), F32), pltpu.VMEM((N_HEADS, tk, t), F32),
                        pltpu.VMEM((N_HEADS, tk, t), BF16)],
        compiler_params=pltpu.CompilerParams(dimension_semantics=("parallel", "arbitrary"),
                                             vmem_limit_bytes=VMEM_LIMIT),
        name="attn_prompt",
    )(q, k, vt, go)


def _attn_sample_kernel(past, c_ref, krt_ref, q_ref, kn_ref, vn_ref, wukt_ref, wuv_ref, go_ref,
                        o_ref, cb_scr, qa_scr, qr_scr, r_scr, sc_scr, sn_scr, pc_scr, acc_scr,
                        lat_scr, pn_scr):
    nb = krt_ref.shape[0]
    t = q_ref.shape[0] // nb
    row = lax.broadcasted_iota(jnp.int32, (nb * t, nb * t), 0)
    col = lax.broadcasted_iota(jnp.int32, (nb * t, nb * t), 1)
    new_ok = (row // t == col // t) & (((past + col % t) // CHUNK) <= ((past + row % t) // CHUNK))
    denom = [[None] * N_HEADS for _ in range(nb)]

    def norm_factors(j):
        cb_scr[j] = c_ref[j * past:(j + 1) * past, :].astype(BF16)
        krt = krt_ref[j]
        ss_pe = jnp.sum(krt * krt, axis=0, keepdims=True)
        knt = _dot_nt(wukt_ref[...], cb_scr[j])
        for h in range(N_HEADS):
            nope = knt[h * LANES:(h + 1) * LANES, :]
            ss = jnp.sum(nope * nope, axis=0, keepdims=True) + ss_pe
            r_scr[j, h] = lax.rsqrt(ss * (1.0 / QK_DIM) + EPS)

    def queries():
        for h in range(N_HEADS):
            lo = h * HEAD_PAD
            qa = _dot(q_ref[:, lo:lo + NOPE_DIM],
                      wukt_ref[h * NOPE_DIM:(h + 1) * NOPE_DIM, :]).astype(BF16)
            for j in range(nb):
                rows = slice(h * t, (h + 1) * t)
                qa_scr[j, rows, :] = qa[j * t:(j + 1) * t, :]
                qr_scr[j, rows, :] = q_ref[j * t:(j + 1) * t, lo + NOPE_DIM:lo + QK_DIM]
            sn = _dot_nt(q_ref[:, lo:lo + HEAD_PAD], kn_ref[:, lo:lo + HEAD_PAD])
            sn_scr[h] = jnp.where(new_ok, sn, NEG)

    def scores(j):
        sc_scr[j] = _dot_nt(qa_scr[j], cb_scr[j]) + _dot(qr_scr[j], krt_ref[j].astype(BF16))

    def softmax(j):
        for h in range(N_HEADS):
            s_c = sc_scr[j, h * t:(h + 1) * t, :] * r_scr[j, h]
            s_n = sn_scr[h, j * t:(j + 1) * t, :]
            m = jnp.maximum(jnp.max(s_c, axis=-1, keepdims=True), jnp.max(s_n, axis=-1, keepdims=True))
            p_c = jnp.exp2(s_c - m)
            p_n = jnp.exp2(s_n - m)
            denom[j][h] = jnp.sum(p_c, axis=-1, keepdims=True) + jnp.sum(p_n, axis=-1, keepdims=True)
            pn_scr[h, j * t:(j + 1) * t, :] = p_n.astype(BF16)
            pc_scr[j, h * t:(h + 1) * t, :] = p_c.astype(BF16)

    def values(j):
        lat = _dot(pc_scr[j], cb_scr[j]).astype(BF16)
        for h in range(N_HEADS):
            lat_scr[h, j * t:(j + 1) * t, :] = lat[h * t:(h + 1) * t, :]

    def outputs():
        for h in range(N_HEADS):
            vs = slice(h * V_DIM, (h + 1) * V_DIM)
            acc = _dot(lat_scr[h], wuv_ref[:, vs]) + _dot(pn_scr[h], vn_ref[:, vs])
            for j in range(nb):
                acc_scr[j, :, vs] = acc[j * t:(j + 1) * t, :] / denom[j][h]
        for j in range(nb):
            o_ref[j * t:(j + 1) * t, :] = (_rms(acc_scr[j]) * go_ref[...]).astype(BF16)

    stages = (norm_factors, scores, softmax, values)
    queries()
    for tick in range(nb + len(stages) - 1):
        for j in range(nb):
            if 0 <= tick - j < len(stages):
                stages[tick - j](j)
    outputs()


def _attn_sample(c2d, krt, q, kn, vn, wp, batch, t, past, nb):
    consts = [wp["w_uk_t"], wp["w_uv"], wp["goa"]]
    ht = N_HEADS * t
    return pl.pallas_call(
        functools.partial(_attn_sample_kernel, past),
        out_shape=jax.ShapeDtypeStruct((batch * t, ATTN_WIDTH), BF16),
        grid=(batch // nb,),
        in_specs=[pl.BlockSpec((nb * past, KV_RANK), lambda b: (b, 0)),
                  pl.BlockSpec((nb, ROPE_DIM, past), lambda b: (b, 0, 0)),
                  pl.BlockSpec((nb * t, QK_WIDTH), lambda b: (b, 0)),
                  pl.BlockSpec((nb * t, QK_WIDTH), lambda b: (b, 0)),
                  pl.BlockSpec((nb * t, ATTN_WIDTH), lambda b: (b, 0))]
                 + [_const_spec(c.shape) for c in consts],
        out_specs=pl.BlockSpec((nb * t, ATTN_WIDTH), lambda b: (b, 0)),
        scratch_shapes=[pltpu.VMEM((nb, past, KV_RANK), BF16), pltpu.VMEM((nb, ht, KV_RANK), BF16),
                        pltpu.VMEM((nb, ht, ROPE_DIM), BF16), pltpu.VMEM((nb, N_HEADS, 1, past), F32),
                        pltpu.VMEM((nb, ht, past), F32), pltpu.VMEM((N_HEADS, nb * t, nb * t), F32),
                        pltpu.VMEM((nb, ht, past), BF16), pltpu.VMEM((nb, t, ATTN_WIDTH), F32),
                        pltpu.VMEM((N_HEADS, nb * t, KV_RANK), BF16),
                        pltpu.VMEM((N_HEADS, nb * t, nb * t), BF16)],
        compiler_params=pltpu.CompilerParams(dimension_semantics=("parallel",),
                                             vmem_limit_bytes=VMEM_LIMIT),
        name="attn_sample",
    )(c2d, krt, q, kn, vn, *consts)


def _outproj_kernel(an_ref, gm_ref, x_ref, wo_ref, gffn_ref, h_ref, hn_ref):
    h = (x_ref[...] + _dot(an_ref[...], wo_ref[:ATTN_WIDTH, :])
         + _dot(gm_ref[...], wo_ref[ATTN_WIDTH:, :]))
    h_ref[...] = h
    hn_ref[...] = (_rms(h) * gffn_ref[...]).astype(BF16)


def _outproj(an, gm, x2d, wp, tm):
    m = x2d.shape[0]
    row_spec = lambda w: pl.BlockSpec((tm, w), lambda i: (i, 0))
    return pl.pallas_call(
        _outproj_kernel,
        out_shape=[jax.ShapeDtypeStruct((m, D_MODEL), F32), jax.ShapeDtypeStruct((m, D_MODEL), BF16)],
        grid=(m // tm,),
        in_specs=[row_spec(ATTN_WIDTH), row_spec(GMLP_WIDTH), row_spec(D_MODEL),
                  _const_spec(wp["w_out"].shape), _const_spec(wp["gffn"].shape)],
        out_specs=[row_spec(D_MODEL), row_spec(D_MODEL)],
        compiler_params=pltpu.CompilerParams(dimension_semantics=("parallel",),
                                             vmem_limit_bytes=VMEM_LIMIT),
        name="outproj",
    )(an, gm, x2d, wp["w_out"], wp["gffn"])


def _ffn_kernel(h_ref, hn_ref, wu_ref, wd_ref, y_ref):
    f = pl.program_id(1)
    slab = h_ref.shape[1]

    def down():
        a = jnp.maximum(_dot(hn_ref[...], wu_ref[...]), 0.0)
        return _dot((a * a).astype(BF16), wd_ref[...])

    @pl.when(f == 0)
    def _():
        d = down()
        y_ref[:, :slab] = d[:, :slab] + h_ref[...]
        y_ref[:, slab:] = d[:, slab:]

    @pl.when(f > 0)
    def _():
        cols = pl.ds(pl.multiple_of(f * slab, slab), slab)
        y_ref[:, cols] += h_ref[...]
        y_ref[...] += down()


def _ffn(h, hn, wp, tm, tf):
    m = h.shape[0]
    nf = D_FF // tf
    return pl.pallas_call(
        _ffn_kernel,
        out_shape=jax.ShapeDtypeStruct((m, D_MODEL), F32),
        grid=(m // tm, nf),
        in_specs=[pl.BlockSpec((tm, D_MODEL // nf), lambda i, f: (i, f)),
                  pl.BlockSpec((tm, D_MODEL), lambda i, f: (i, 0)),
                  pl.BlockSpec((D_MODEL, tf), lambda i, f: (0, f)),
                  pl.BlockSpec((tf, D_MODEL), lambda i, f: (f, 0))],
        out_specs=pl.BlockSpec((tm, D_MODEL), lambda i, f: (i, 0)),
        compiler_params=pltpu.CompilerParams(dimension_semantics=("parallel", "arbitrary"),
                                             vmem_limit_bytes=VMEM_LIMIT),
        name="ffn",
    )(h, hn, wp["w_up"], wp["w_down"])


def _rope_tables(pos):
    inv = ROPE_THETA ** (-np.arange(HALF_ROPE, dtype=np.float64) / HALF_ROPE)
    ang = np.asarray(pos, np.float64)[:, None] * inv[None, :]
    cos, sin = np.cos(ang), np.sin(ang)
    z = np.zeros_like(cos)
    z2 = np.zeros((ang.shape[0], LANES - ROPE_DIM))
    tabs = (np.concatenate([cos, cos, z2], axis=1), np.concatenate([-sin, z, z2], axis=1),
            np.concatenate([z, sin, z2], axis=1))
    tabs_t = (cos.T, sin.T)
    return tuple(jnp.asarray(t, F32) for t in tabs), tuple(jnp.asarray(t, F32) for t in tabs_t)


def _tile_lanes(x, width):
    if width == LANES:
        return x
    lane = lax.broadcasted_iota(jnp.int32, x.shape, 1)
    x = jnp.where(lane < width, x, 0.0)
    out = x
    for r in range(1, LANES // width):
        out = out + pltpu.roll(x, r * width, 1)
    return out


def _pad_w_in_kernel(dec_seq, wt_ref, uq_ref, uk_ref, uv_ref, wsp_ref, bsp_ref, qn_ref, qr_ref, kn_ref,
                     kr_ref, o_ref, wq_ref, wqt_ref, wuk_ref, wukt_ref, wuv_ref, wuvt_ref, wss_ref,
                     bstp_ref, bsts_ref, gq_ref, gqt_ref, pad_scr):
    tr = wt_ref.shape[1]
    split = C_PE0 + ROPE_DIM
    o_ref[:, :C_PE0] = wt_ref[:C_PE0, :].T.astype(BF16)
    pe = wt_ref[C_PE0:C_PE1, :].T
    lane = lax.broadcasted_iota(jnp.int32, (tr, LANES), 1)
    o_ref[:, C_PE0:C_PE1] = jnp.where(lane < ROPE_DIM, pe, 0.0).astype(BF16)
    o_ref[:, C_PE1:] = wt_ref[split:, :].T.astype(BF16)

    i = pl.program_id(0)

    @pl.when(i == 0)
    def _():
        uq_t = uq_ref[...].T
        for h in range(N_HEADS):
            lo = h * HEAD_PAD
            pad_scr[lo:lo + QK_DIM, :] = uq_t[h * QK_DIM:(h + 1) * QK_DIM, :]
            pad_scr[lo + QK_DIM:lo + HEAD_PAD, :] = jnp.zeros((HEAD_PAD - QK_DIM, Q_RANK), F32)
        wqt_ref[...] = pad_scr[...].astype(BF16)

    @pl.when(i == 1)
    def _():
        wq_ref[...] = pad_scr[...].T.astype(BF16)

    for step, (src, dst, dst_t) in enumerate(((uk_ref, wuk_ref, wukt_ref), (uv_ref, wuv_ref, wuvt_ref))):
        @pl.when(i == 2 + step)
        def _(src=src, dst=dst, dst_t=dst_t):
            w = src[...]
            dst[...] = w.astype(BF16)
            dst_t[...] = w.T.astype(BF16)

    @pl.when(i == 4)
    def _():
        for g in range(GMLP_GROUPS):
            w = _tile_lanes(wsp_ref[g], dec_seq)
            wss_ref[g] = jnp.concatenate([w[:dec_seq]] * (GMLP_CHUNK // dec_seq), axis=0)
        b = bsp_ref[...]
        fill = jnp.zeros((LANES - GMLP_GROUPS, LANES), F32)
        for dst, width in ((bstp_ref, GMLP_CHUNK), (bsts_ref, dec_seq)):
            dst[...] = jnp.concatenate([_tile_lanes(b, width), fill], axis=0).T

    @pl.when(i == 5)
    def _():
        scale = QK_DIM ** -0.5 * math.log2(math.e)
        nope = (qn_ref[...] * scale) * kn_ref[...]
        rope = (qr_ref[...] * scale) * kr_ref[...]
        rope = jnp.concatenate([rope, rope, jnp.zeros((1, HEAD_PAD - QK_DIM), F32)], axis=1)
        for half, g in enumerate((nope, rope)):
            cols = slice(half * LANES, (half + 1) * LANES)
            gq_ref[:, cols] = g
            col = jnp.broadcast_to(g, (LANES, LANES)).T
            gqt_ref[cols, :] = jnp.concatenate([col] * (gqt_ref.shape[1] // LANES), axis=1)


def _pad_w_in(w_in_t, w_uq, w_uk, w_uv, w_spatial, b_spatial, q_nope, q_rope, k_nope, k_rope, dec_seq,
              tr):
    width, rows = w_in_t.shape
    assert width + LANES - ROPE_DIM == IN_PAD and rows // tr >= 6
    assert w_uq.shape == (Q_RANK, N_HEADS * QK_DIM) and GMLP_CHUNK == LANES
    assert w_spatial.shape == (GMLP_GROUPS, GMLP_CHUNK, GMLP_CHUNK)
    small = (w_uq, w_uk, w_uv, w_spatial, b_spatial, q_nope[None, :], q_rope[None, :],
             k_nope[None, :], k_rope[None, :])
    outs = [(Q_RANK, QK_WIDTH), (QK_WIDTH, Q_RANK), w_uk.shape, w_uk.shape[::-1], w_uv.shape,
            w_uv.shape[::-1]]
    outs_f32 = [w_spatial.shape, (LANES, LANES), (LANES, LANES), (1, HEAD_PAD), (HEAD_PAD, PROJ_ROWS)]
    return pl.pallas_call(
        functools.partial(_pad_w_in_kernel, dec_seq),
        out_shape=[jax.ShapeDtypeStruct((rows, IN_PAD), BF16)]
        + [jax.ShapeDtypeStruct(s, BF16) for s in outs]
        + [jax.ShapeDtypeStruct(s, F32) for s in outs_f32],
        grid=(rows // tr,),
        in_specs=[pl.BlockSpec((width, tr), lambda i: (0, i))] + [_const_spec(w.shape) for w in small],
        out_specs=[pl.BlockSpec((tr, IN_PAD), lambda i: (i, 0))]
        + [_const_spec(s) for s in outs + outs_f32],
        scratch_shapes=[pltpu.VMEM((QK_WIDTH, Q_RANK), F32)],
        compiler_params=pltpu.CompilerParams(dimension_semantics=("arbitrary",),
                                             vmem_limit_bytes=VMEM_LIMIT),
        name="pad_w_in",
    )(w_in_t, *small)


def _prep_weights(norm_mix, w_in, q_lat_norm, kv_lat_norm, w_uq, w_uk, w_uv, q_norm_nope, q_norm_rope,
                  k_norm_nope, k_norm_rope, v_norm, w_spatial, b_spatial, out_norm_attn, out_norm_gmlp,
                  norm_ffn, dec_seq):
    w_in_p, wq, wq_t, wuk, wuk_t, wuv, wuv_t, ws_s, bst, bst_s, gq, gq_t = _pad_w_in(
        w_in.T, w_uq, w_uk, w_uv, w_spatial, b_spatial, q_norm_nope, q_norm_rope, k_norm_nope,
        k_norm_rope, dec_seq, PAD_ROWS)
    return {
        "gmix": norm_mix[None, :], "w_in": w_in_p, "gql": q_lat_norm[None, :], "gkv": kv_lat_norm[None, :],
        "w_uq": wq, "w_uq_t": wq_t, "w_uk": wuk, "w_uk_t": wuk_t, "w_uv": wuv, "w_uv_t": wuv_t,
        "gq": gq, "gq_t": gq_t, "v_norm": v_norm, "ws": w_spatial, "bst": bst, "ws_s": ws_s,
        "bst_s": bst_s, "gog": out_norm_gmlp[None, :], "goa": out_norm_attn[None, :],
        "gffn": norm_ffn[None, :],
    }


def _finish(an, gm, x2d, wp):
    h, hn = _outproj(an, gm, x2d, wp, OUTPROJ_ROWS)
    return _ffn(h, hn, wp, FFN_ROWS, FFN_COLS)


def kernel(x_prompt, x_sample, cache_c_kv, cache_k_rope, norm_mix, w_in, q_lat_norm, kv_lat_norm, w_uq, w_uk, w_uv, q_norm_nope, q_norm_rope, k_norm_nope, k_norm_rope, v_norm, w_spatial, b_spatial, out_norm_attn, out_norm_gmlp, w_out, norm_ffn, w_up, w_down):
    depth = w_in.shape[0]
    assert depth == 1
    batch, seq, _ = x_prompt.shape
    dec_batch, dec_seq, _ = x_sample.shape
    past = cache_c_kv.shape[2]
    assert past % CHUNK == 0 and dec_seq <= CHUNK and GMLP_CHUNK % dec_seq == 0
    n_prompt, n_sample = batch * seq, dec_batch * dec_seq
    assert seq % PROJ_ROWS == 0 and n_sample % PROJ_ROWS == 0 and PROJ_ROWS % dec_seq == 0
    assert n_prompt % FFN_ROWS == 0 and n_sample % FFN_ROWS == 0 and dec_batch % SAMPLE_BATCH_ROWS == 0

    weights = (norm_mix[0], w_in[0], q_lat_norm[0], kv_lat_norm[0], w_uq[0], w_uk[0], w_uv[0],
               q_norm_nope[0], q_norm_rope[0], k_norm_nope[0], k_norm_rope[0], v_norm[0], w_spatial[0],
               b_spatial[0], out_norm_attn[0], out_norm_gmlp[0], norm_ffn[0])
    wp = _prep_weights(*weights, dec_seq=dec_seq)

    xp = x_prompt.reshape(batch * seq, D_MODEL)
    tabs_p, tabs_pt = _rope_tables(np.arange(seq))
    qt, k, vt, ckv_p, kpe_p, gm, wo_b, wu_b, wd_b = _proj(
        xp, tabs_p, wp, GMLP_CHUNK, PROJ_ROWS, f32_weights=(w_out[0], w_up[0], w_down[0]),
        tabs_t=tabs_pt)
    wp = dict(wp, w_out=wo_b, w_up=wu_b, w_down=wd_b)
    wp_s = dict(wp, ws=wp["ws_s"], bst=wp["bst_s"])
    an = _attn_prompt(qt, k, vt, wp["goa"], batch, seq, PROJ_ROWS)
    y_p = _finish(an, gm, xp, wp)

    xs = x_sample.reshape(dec_batch * dec_seq, D_MODEL)
    tabs_s, _ = _rope_tables(past + np.arange(PROJ_ROWS) % dec_seq)
    qs, kn, vn, ckv_s, kpe_s, gms, vg_s = _proj(xs, tabs_s, wp_s, dec_seq, PROJ_ROWS)
    ans = _attn_sample(cache_c_kv[0].reshape(dec_batch * past, KV_RANK),
                       jnp.swapaxes(cache_k_rope[0], 1, 2),
                       qs, kn, vn, wp, dec_batch, dec_seq, past, SAMPLE_BATCH_ROWS)
    y_s = _finish(ans, gms, xs, wp)

    return (y_p.reshape(batch, seq, D_MODEL),
            y_s.reshape(dec_batch, dec_seq, D_MODEL),
            ckv_p.reshape(1, batch, seq, KV_RANK),
            jnp.swapaxes(kpe_p, 1, 2)[None],
            ckv_s.reshape(1, dec_batch, dec_seq, KV_RANK),
            kpe_s.reshape(1, dec_batch, dec_seq, ROPE_DIM),
            vg_s.reshape(1, dec_batch, dec_seq, GMLP_WIDTH))
```

```python
import functools
import math

import jax
import jax.numpy as jnp
import numpy as np
from jax import lax
from jax.experimental import pallas as pl
from jax.experimental.pallas import tpu as pltpu

D_MODEL = 2048
CHUNK = 64
N_HEADS = 8
NOPE_DIM = 128
ROPE_DIM = 64
HALF_ROPE = ROPE_DIM // 2
QK_DIM = NOPE_DIM + ROPE_DIM
V_DIM = 128
ATTN_WIDTH = N_HEADS * V_DIM
Q_RANK = 512
KV_RANK = 256
GMLP_GROUPS = 8
GMLP_GROUP_DIM = 128
GMLP_WIDTH = GMLP_GROUPS * GMLP_GROUP_DIM
GMLP_CHUNK = 128
D_FF = 4 * D_MODEL
ROPE_THETA = 10000.0
EPS = 1e-6

LANES = 128
SUBLANES = 8
BF16_SUBLANES = 2 * SUBLANES
HEAD_PAD = 2 * LANES
QK_WIDTH = N_HEADS * HEAD_PAD
C_Q0, C_Q1 = 0, Q_RANK
C_KV0, C_KV1 = C_Q1, C_Q1 + KV_RANK
C_PE0, C_PE1 = C_KV1, C_KV1 + LANES
C_U0, C_U1 = C_PE1, C_PE1 + GMLP_WIDTH
C_V0, C_V1 = C_U1, C_U1 + GMLP_WIDTH
IN_PAD = C_V1
VMEM_LIMIT = 56 * 1024 * 1024

PROJ_ROWS = 256
HEAD_ROWS = 128
PAD_ROWS = 256
OUTPROJ_ROWS = 512
FFN_ROWS = 1024
FFN_COLS = 1024
SAMPLE_BATCH_ROWS = 8

BF16 = jnp.bfloat16
F32 = jnp.float32


def _dot(a, b):
    return jnp.dot(a, b, preferred_element_type=F32)


def _dot_nt(a, b):
    return lax.dot_general(a, b, (((1,), (1,)), ((), ())), preferred_element_type=F32)


def _sum_lanes(x):
    return jnp.sum(x, axis=-1, keepdims=True)


def _sumsq(x):
    return _sum_lanes(x * x)


def _rms(x):
    return x * lax.rsqrt(jnp.mean(x * x, axis=-1, keepdims=True) + EPS)


def _gelu(x):
    c = math.sqrt(2.0 / math.pi)
    return 0.5 * x * (1.0 + jnp.tanh(c * (x + 0.044715 * (x * x * x))))


def _rope(t, cos, s1, s2):
    return t * cos + pltpu.roll(t, LANES - HALF_ROPE, 1) * s1 + pltpu.roll(t, HALF_ROPE, 1) * s2


def _const_spec(shape):
    nd = len(shape)
    return pl.BlockSpec(shape, lambda *_: (0,) * nd, pipeline_mode=pl.Buffered(1))


def _proj_kernel(chunk_len, prompt, x_ref, gmix_ref, win_ref, gql_ref, gkv_ref, wuq_ref, wuk_ref,
                 wuv_ref, gq_ref, cos_ref, s1_ref, s2_ref, vn_ref, ws_ref, bst_ref, gog_ref, *rest):
    v_transposed = prompt
    emit_v = not prompt
    if prompt:
        (cost_ref, sint_ref, wo_ref, wu_ref, wd_ref, q_ref, k_ref, v_ref, ckv_ref, kpe_ref, gm_ref,
         wo_out, wu_out, wd_out, gate_scr, vgb_scr) = rest
        wo_out[...] = wo_ref[...].astype(BF16)
        wu_out[...] = wu_ref[...].astype(BF16)
        wd_out[...] = wd_ref[...].astype(BF16)
    else:
        q_ref, k_ref, v_ref, ckv_ref, kpe_ref, gm_ref, vg_ref, gate_scr, vgb_scr = rest
    tm = x_ref.shape[0]
    xn = (_rms(x_ref[...]) * gmix_ref[...]).astype(BF16)
    cos, s1, s2 = cos_ref[...], s1_ref[...], s2_ref[...]

    zq = _dot(xn, win_ref[:, C_Q0:C_Q1])
    qln = (_rms(zq) * gql_ref[...]).astype(BF16)
    if prompt:
        qt = _dot_nt(wuq_ref[...], qln)
        cos_t, sin_t = cost_ref[...], sint_ref[...]
        x1_lo, x2_lo = NOPE_DIM, NOPE_DIM + HALF_ROPE
        zeros = jnp.zeros((HEAD_PAD - QK_DIM, tm), BF16)
        for h in range(N_HEADS):
            lo = h * HEAD_PAD
            nope = qt[lo:lo + NOPE_DIM]
            x1 = qt[lo + x1_lo:lo + x2_lo]
            x2 = qt[lo + x2_lo:lo + QK_DIM]
            ss = (jnp.sum(nope * nope, axis=0, keepdims=True)
                  + jnp.sum(x1 * x1 + x2 * x2, axis=0, keepdims=True))
            r = lax.rsqrt(ss * (1.0 / QK_DIM) + EPS)
            q_ref[lo:lo + NOPE_DIM, :] = (nope * r * gq_ref[:NOPE_DIM]).astype(BF16)
            q_ref[lo + x1_lo:lo + x2_lo, :] = (
                (x1 * cos_t - x2 * sin_t) * r * gq_ref[x1_lo:x2_lo]).astype(BF16)
            q_ref[lo + x2_lo:lo + QK_DIM, :] = (
                (x2 * cos_t + x1 * sin_t) * r * gq_ref[x2_lo:QK_DIM]).astype(BF16)
            q_ref[lo + QK_DIM:lo + HEAD_PAD, :] = zeros
    else:
        qraw = _dot(qln, wuq_ref[...])
        for h in range(N_HEADS):
            lo = h * HEAD_PAD
            for rc in range(tm // HEAD_ROWS):
                rows = slice(rc * HEAD_ROWS, (rc + 1) * HEAD_ROWS)
                nope = qraw[rows, lo:lo + LANES]
                rp = qraw[rows, lo + LANES:lo + HEAD_PAD]
                r = lax.rsqrt(_sum_lanes(nope * nope + rp * rp) * (1.0 / QK_DIM) + EPS)
                rp = _rope(rp, cos[rows], s1[rows], s2[rows])
                q_ref[rows, lo:lo + LANES] = (nope * r * gq_ref[:, :LANES]).astype(BF16)
                q_ref[rows, lo + LANES:lo + HEAD_PAD] = (rp * r * gq_ref[:, LANES:]).astype(BF16)

    ckv = _rms(_dot(xn, win_ref[:, C_KV0:C_KV1])) * gkv_ref[...]
    ckv_ref[...] = ckv
    cb = ckv.astype(BF16)
    pe = _rope(_dot(xn, win_ref[:, C_PE0:C_PE1]), cos, s1, s2)
    kpe_ref[...] = pe.T[:ROPE_DIM, :] if prompt else pe[:, :ROPE_DIM]
    ss_pe = _sumsq(pe)
    knope = _dot(cb, wuk_ref[...])
    if v_transposed:
        v_ref[...] = _dot_nt(wuv_ref[...], cb).astype(BF16)
    else:
        v_ref[...] = _dot(cb, wuv_ref[...]).astype(BF16)
    for h in range(N_HEADS):
        lo = h * HEAD_PAD
        for rc in range(tm // HEAD_ROWS):
            rows = slice(rc * HEAD_ROWS, (rc + 1) * HEAD_ROWS)
            nope = knope[rows, h * LANES:(h + 1) * LANES]
            r = lax.rsqrt((_sumsq(nope) + ss_pe[rows]) * (1.0 / QK_DIM) + EPS)
            k_ref[rows, lo:lo + LANES] = (nope * r).astype(BF16)
            k_ref[rows, lo + LANES:lo + HEAD_PAD] = (pe[rows] * r).astype(BF16)

    gv = _gelu(_dot(xn, win_ref[:, C_V0:C_V1]))
    for g in range(GMLP_GROUPS):
        cols = slice(g * LANES, (g + 1) * LANES)
        for rc in range(tm // HEAD_ROWS):
            rows = slice(rc * HEAD_ROWS, (rc + 1) * HEAD_ROWS)
            vg = _rms(gv[rows, cols]) * vn_ref[g:g + 1, :]
            if emit_v:
                vg_ref[rows, cols] = vg
            vgb_scr[rows, cols] = vg.astype(BF16)
    u = _gelu(_dot(xn, win_ref[:, C_U0:C_U1]))
    row = lax.broadcasted_iota(jnp.int32, (GMLP_CHUNK, GMLP_CHUNK), 0)
    col = lax.broadcasted_iota(jnp.int32, (GMLP_CHUNK, GMLP_CHUNK), 1)
    causal = (row // chunk_len == col // chunk_len) & (col <= row)
    for g in range(GMLP_GROUPS):
        wm = jnp.where(causal, ws_ref[g], 0.0).astype(BF16)
        bias = bst_ref[:, g:g + 1]
        for c in range(tm // GMLP_CHUNK):
            rows = slice(c * GMLP_CHUNK, (c + 1) * GMLP_CHUNK)
            cols = slice(g * LANES, (g + 1) * LANES)
            s = _dot(wm, vgb_scr[rows, cols]) + bias
            gate_scr[rows, cols] = u[rows, cols] * s
    gm_ref[...] = (_rms(gate_scr[...]) * gog_ref[...]).astype(BF16)


def _proj(x2d, tabs, wp, chunk_len, tm, f32_weights=None, tabs_t=None):
    prompt = f32_weights is not None
    m = x2d.shape[0]
    steps = m // tm
    cos, s1, s2 = tabs
    tab_blocks = cos.shape[0] // tm
    row_spec = lambda w: pl.BlockSpec((tm, w), lambda i: (i, 0))
    tab_spec = pl.BlockSpec((tm, LANES), lambda i: (i % tab_blocks, 0))
    consts = [wp["gmix"], wp["w_in"], wp["gql"], wp["gkv"], wp["w_uq_t"] if prompt else wp["w_uq"],
              wp["w_uk"], wp["w_uv_t"] if prompt else wp["w_uv"], wp["gq_t"] if prompt else wp["gq"]]
    consts2 = [wp["v_norm"], wp["ws"], wp["bst"], wp["gog"]]
    in_specs = ([row_spec(D_MODEL)] + [_const_spec(c.shape) for c in consts]
                + [tab_spec] * 3 + [_const_spec(c.shape) for c in consts2])
    q_shape = (steps * QK_WIDTH, tm) if prompt else (m, QK_WIDTH)
    q_spec = pl.BlockSpec((QK_WIDTH, tm), lambda i: (i, 0)) if prompt else row_spec(QK_WIDTH)
    v_shape = (steps * ATTN_WIDTH, tm) if prompt else (m, ATTN_WIDTH)
    v_spec = pl.BlockSpec((ATTN_WIDTH, tm), lambda i: (i, 0)) if prompt else row_spec(ATTN_WIDTH)
    kpe_shape = (m // cos.shape[0], ROPE_DIM, cos.shape[0]) if prompt else (m, ROPE_DIM)
    kpe_spec = (pl.BlockSpec((None, ROPE_DIM, tm), lambda i: (i // tab_blocks, 0, i % tab_blocks))
                if prompt else row_spec(ROPE_DIM))
    out_shape = [jax.ShapeDtypeStruct(q_shape, BF16), jax.ShapeDtypeStruct((m, QK_WIDTH), BF16),
                 jax.ShapeDtypeStruct(v_shape, BF16), jax.ShapeDtypeStruct((m, KV_RANK), F32),
                 jax.ShapeDtypeStruct(kpe_shape, F32), jax.ShapeDtypeStruct((m, GMLP_WIDTH), BF16)]
    out_specs = [q_spec, row_spec(QK_WIDTH), v_spec, row_spec(KV_RANK),
                 kpe_spec, row_spec(GMLP_WIDTH)]
    extra_in = []
    if prompt:
        tab_t_spec = pl.BlockSpec((HALF_ROPE, tm), lambda i: (0, i % tab_blocks))
        w_out, w_up, w_down = f32_weights
        slabs = [pl.BlockSpec((D_MODEL // steps, D_MODEL), lambda i: (i, 0)),
                 pl.BlockSpec((D_MODEL, D_FF // steps), lambda i: (0, i)),
                 pl.BlockSpec((D_FF // steps, D_MODEL), lambda i: (i, 0))]
        extra_in = [*tabs_t, w_out, w_up, w_down]
        in_specs += [tab_t_spec, tab_t_spec] + slabs
        out_shape += [jax.ShapeDtypeStruct(w.shape, BF16) for w in f32_weights]
        out_specs += slabs
    else:
        out_shape.append(jax.ShapeDtypeStruct((m, GMLP_WIDTH), F32))
        out_specs.append(row_spec(GMLP_WIDTH))
    return pl.pallas_call(
        functools.partial(_proj_kernel, chunk_len, prompt),
        out_shape=out_shape,
        grid=(steps,),
        in_specs=in_specs,
        out_specs=out_specs,
        scratch_shapes=[pltpu.VMEM((tm, GMLP_WIDTH), F32), pltpu.VMEM((tm, GMLP_WIDTH), BF16)],
        compiler_params=pltpu.CompilerParams(dimension_semantics=("parallel",),
                                             vmem_limit_bytes=VMEM_LIMIT),
        name="proj",
    )(x2d, *consts, cos, s1, s2, *consts2, *extra_in)


NEG = -1e30
SOFTMAX_KEY_CHUNK = 64
KEY_BLOCKS_PER_STEP = 2
SCORE_LEAD = 4


def _attn_prompt_kernel(q_ref, k_ref, vt_ref, go_ref, o_ref, acc_scr, m_scr, l_scr, s_scr, p_scr):
    t = q_ref.shape[1]
    qi = pl.program_id(1)
    m_scr[...] = jnp.full(m_scr.shape, NEG, F32)
    l_scr[...] = jnp.zeros(l_scr.shape, F32)
    acc_scr[...] = jnp.zeros(acc_scr.shape, F32)

    def step(kb0, nblk, last_masked):
        nk = nblk * t
        if last_masked:
            key = lax.broadcasted_iota(jnp.int32, (t, t), 0)
            qry = lax.broadcasted_iota(jnp.int32, (t, t), 1)
            ok = (key // CHUNK) <= (qry // CHUNK)

        m8 = {}

        def scores(h):
            qs = slice(h * HEAD_PAD, (h + 1) * HEAD_PAD)
            mh = None
            for j in range(nblk):
                rows = pl.ds(pl.multiple_of((kb0 + j) * t, t), t)
                s = _dot(k_ref[rows, qs], q_ref[qs, :])
                if last_masked and j == nblk - 1:
                    s = jnp.where(ok, s, NEG)
                s_scr[h, j * t:(j + 1) * t, :] = s
                mj = jnp.max(s.reshape(t // SUBLANES, SUBLANES, t), axis=0)
                mh = mj if mh is None else jnp.maximum(mh, mj)
            m8[h] = mh

        ones = jnp.ones((BF16_SUBLANES, nk), BF16)
        for h in range(min(SCORE_LEAD, N_HEADS)):
            scores(h)
        for h in range(N_HEADS):
            if h + SCORE_LEAD < N_HEADS:
                scores(h + SCORE_LEAD)
            hs = slice(h * V_DIM, (h + 1) * V_DIM)
            m_old = m_scr[h]
            m_new = jnp.maximum(m_old, jnp.max(m8[h], axis=0, keepdims=True))
            alpha = jnp.exp2(m_old - m_new)
            for c in range(nk // SOFTMAX_KEY_CHUNK):
                cs = slice(c * SOFTMAX_KEY_CHUNK, (c + 1) * SOFTMAX_KEY_CHUNK)
                p_scr[h, cs, :] = jnp.exp2(s_scr[h, cs, :] - m_new).astype(BF16)
            m_scr[h] = m_new
            vt = [vt_ref[pl.ds(pl.multiple_of((kb0 + j) * ATTN_WIDTH, ATTN_WIDTH) + h * V_DIM, V_DIM), :]
                  for j in range(nblk)]
            vt = vt[0] if nblk == 1 else jnp.concatenate(vt, axis=1)
            pv = _dot(jnp.concatenate([vt, ones], axis=0), p_scr[h, :nk, :])
            acc_scr[hs, :] = alpha * acc_scr[hs, :] + pv[:V_DIM]
            l_scr[h] = alpha * l_scr[h] + pv[V_DIM:V_DIM + 1]

    def body(j, carry):
        step(j * KEY_BLOCKS_PER_STEP, KEY_BLOCKS_PER_STEP, False)
        return carry

    lax.fori_loop(0, qi // KEY_BLOCKS_PER_STEP, body, 0)
    for rem in range(KEY_BLOCKS_PER_STEP):
        @pl.when(qi % KEY_BLOCKS_PER_STEP == rem)
        def _(rem=rem):
            step(qi - rem, rem + 1, True)

    for h in range(N_HEADS):
        hs = slice(h * V_DIM, (h + 1) * V_DIM)
        acc_scr[hs, :] = acc_scr[hs, :] / l_scr[h]
    o = acc_scr[...].T
    o_ref[...] = (_rms(o) * go_ref[...]).astype(BF16)


def _attn_prompt(q, k, vt, go, batch, seq, t):
    nq = seq // t
    assert vt.shape == (batch * nq * ATTN_WIDTH, t)
    tk = KEY_BLOCKS_PER_STEP * t
    return pl.pallas_call(
        _attn_prompt_kernel,
        out_shape=jax.ShapeDtypeStruct((batch * seq, ATTN_WIDTH), BF16),
        grid=(batch, nq),
        in_specs=[pl.BlockSpec((QK_WIDTH, t), lambda b, i: (b * nq + i, 0)),
                  pl.BlockSpec((seq, QK_WIDTH), lambda b, i: (b, 0)),
                  pl.BlockSpec((nq * ATTN_WIDTH, t), lambda b, i: (b, 0)),
                  _const_spec(go.shape)],
        out_specs=pl.BlockSpec((t, ATTN_WIDTH), lambda b, i: (b * nq + i, 0)),
        scratch_shapes=[pltpu.VMEM((ATTN_WIDTH, t), F32), pltpu.VMEM((N_HEADS, 1, t), F32),
                        pltpu.VMEM((N_HEADS, 1, t), F32), pltpu.VMEM((N_HEADS, tk, t), F32),
                        pltpu.VMEM((N_HEADS, tk, t), BF16)],
        compiler_params=pltpu.CompilerParams(dimension_semantics=("parallel", "arbitrary"),
                                             vmem_limit_bytes=VMEM_LIMIT),
        name="attn_prompt",
    )(q, k, vt, go)


def _attn_sample_kernel(past, c_ref, krt_ref, q_ref, kn_ref, vn_ref, wukt_ref, wuv_ref, go_ref,
                        o_ref, cb_scr, qa_scr, qr_scr, r_scr, sc_scr, sn_scr, pc_scr, acc_scr,
                        lat_scr, pn_scr):
    nb = krt_ref.shape[0]
    t = q_ref.shape[0] // nb
    row = lax.broadcasted_iota(jnp.int32, (nb * t, nb * t), 0)
    col = lax.broadcasted_iota(jnp.int32, (nb * t, nb * t), 1)
    new_ok = (row // t == col // t) & (((past + col % t) // CHUNK) <= ((past + row % t) // CHUNK))
    denom = [[None] * N_HEADS for _ in range(nb)]

    def norm_factors(j):
        cb_scr[j] = c_ref[j * past:(j + 1) * past, :].astype(BF16)
        krt = krt_ref[j]
        ss_pe = jnp.sum(krt * krt, axis=0, keepdims=True)
        knt = _dot_nt(wukt_ref[...], cb_scr[j])
        for h in range(N_HEADS):
            nope = knt[h * LANES:(h + 1) * LANES, :]
            ss = jnp.sum(nope * nope, axis=0, keepdims=True) + ss_pe
            r_scr[j, h] = lax.rsqrt(ss * (1.0 / QK_DIM) + EPS)

    def queries():
        for h in range(N_HEADS):
            lo = h * HEAD_PAD
            qa = _dot(q_ref[:, lo:lo + NOPE_DIM],
                      wukt_ref[h * NOPE_DIM:(h + 1) * NOPE_DIM, :]).astype(BF16)
            for j in range(nb):
                rows = slice(h * t, (h + 1) * t)
                qa_scr[j, rows, :] = qa[j * t:(j + 1) * t, :]
                qr_scr[j, rows, :] = q_ref[j * t:(j + 1) * t, lo + NOPE_DIM:lo + QK_DIM]
            sn = _dot_nt(q_ref[:, lo:lo + HEAD_PAD], kn_ref[:, lo:lo + HEAD_PAD])
            sn_scr[h] = jnp.where(new_ok, sn, NEG)

    def scores(j):
        sc_scr[j] = _dot_nt(qa_scr[j], cb_scr[j]) + _dot(qr_scr[j], krt_ref[j].astype(BF16))

    def softmax(j):
        for h in range(N_HEADS):
            s_c = sc_scr[j, h * t:(h + 1) * t, :] * r_scr[j, h]
            s_n = sn_scr[h, j * t:(j + 1) * t, :]
            m = jnp.maximum(jnp.max(s_c, axis=-1, keepdims=True), jnp.max(s_n, axis=-1, keepdims=True))
            p_c = jnp.exp2(s_c - m)
            p_n = jnp.exp2(s_n - m)
            denom[j][h] = jnp.sum(p_c, axis=-1, keepdims=True) + jnp.sum(p_n, axis=-1, keepdims=True)
            pn_scr[h, j * t:(j + 1) * t, :] = p_n.astype(BF16)
            pc_scr[j, h * t:(h + 1) * t, :] = p_c.astype(BF16)

    def values(j):
        lat = _dot(pc_scr[j], cb_scr[j]).astype(BF16)
        for h in range(N_HEADS):
            lat_scr[h, j * t:(j + 1) * t, :] = lat[h * t:(h + 1) * t, :]

    def outputs():
        for h in range(N_HEADS):
            vs = slice(h * V_DIM, (h + 1) * V_DIM)
            acc = _dot(lat_scr[h], wuv_ref[:, vs]) + _dot(pn_scr[h], vn_ref[:, vs])
            for j in range(nb):
                acc_scr[j, :, vs] = acc[j * t:(j + 1) * t, :] / denom[j][h]
        for j in range(nb):
            o_ref[j * t:(j + 1) * t, :] = (_rms(acc_scr[j]) * go_ref[...]).astype(BF16)

    stages = (norm_factors, scores, softmax, values)
    queries()
    for tick in range(nb + len(stages) - 1):
        for j in range(nb):
            if 0 <= tick - j < len(stages):
                stages[tick - j](j)
    outputs()


def _attn_sample(c2d, krt, q, kn, vn, wp, batch, t, past, nb):
    consts = [wp["w_uk_t"], wp["w_uv"], wp["goa"]]
    ht = N_HEADS * t
    return pl.pallas_call(
        functools.partial(_attn_sample_kernel, past),
        out_shape=jax.ShapeDtypeStruct((batch * t, ATTN_WIDTH), BF16),
        grid=(batch // nb,),
        in_specs=[pl.BlockSpec((nb * past, KV_RANK), lambda b: (b, 0)),
                  pl.BlockSpec((nb, ROPE_DIM, past), lambda b: (b, 0, 0)),
                  pl.BlockSpec((nb * t, QK_WIDTH), lambda b: (b, 0)),
                  pl.BlockSpec((nb * t, QK_WIDTH), lambda b: (b, 0)),
                  pl.BlockSpec((nb * t, ATTN_WIDTH), lambda b: (b, 0))]
                 + [_const_spec(c.shape) for c in consts],
        out_specs=pl.BlockSpec((nb * t, ATTN_WIDTH), lambda b: (b, 0)),
        scratch_shapes=[pltpu.VMEM((nb, past, KV_RANK), BF16), pltpu.VMEM((nb, ht, KV_RANK), BF16),
                        pltpu.VMEM((nb, ht, ROPE_DIM), BF16), pltpu.VMEM((nb, N_HEADS, 1, past), F32),
                        pltpu.VMEM((nb, ht, past), F32), pltpu.VMEM((N_HEADS, nb * t, nb * t), F32),
                        pltpu.VMEM((nb, ht, past), BF16), pltpu.VMEM((nb, t, ATTN_WIDTH), F32),
                        pltpu.VMEM((N_HEADS, nb * t, KV_RANK), BF16),
                        pltpu.VMEM((N_HEADS, nb * t, nb * t), BF16)],
        compiler_params=pltpu.CompilerParams(dimension_semantics=("parallel",),
                                             vmem_limit_bytes=VMEM_LIMIT),
        name="attn_sample",
    )(c2d, krt, q, kn, vn, *consts)


X_SLOTS = 3


def _outproj_kernel(n_steps, an_ref, gm_ref, x_hbm, wo_ref, gffn_ref, h_ref, hn_ref, x_buf, x_sem):
    s = pl.program_id(0)
    tm = h_ref.shape[0]

    def x_copy(step, slot):
        rows = pl.ds(pl.multiple_of(step * tm, tm), tm)
        return pltpu.make_async_copy(x_hbm.at[rows, :], x_buf.at[slot], x_sem.at[slot])

    @pl.when(s == 0)
    def _():
        for step in range(min(X_SLOTS - 1, n_steps)):
            x_copy(step, step).start()

    @pl.when(s + (X_SLOTS - 1) < n_steps)
    def _():
        nxt = s + (X_SLOTS - 1)
        x_copy(nxt, lax.rem(nxt, X_SLOTS)).start()

    slot = lax.rem(s, X_SLOTS)
    x_copy(s, slot).wait()
    h = (x_buf[slot] + _dot(an_ref[...], wo_ref[:ATTN_WIDTH, :])
         + _dot(gm_ref[...], wo_ref[ATTN_WIDTH:, :]))
    h_ref[...] = h
    hn_ref[...] = (_rms(h) * gffn_ref[...]).astype(BF16)


def _outproj(an, gm, x2d, wp, tm):
    m = x2d.shape[0]
    row_spec = lambda w: pl.BlockSpec((tm, w), lambda i: (i, 0))
    return pl.pallas_call(
        functools.partial(_outproj_kernel, m // tm),
        out_shape=[jax.ShapeDtypeStruct((m, D_MODEL), F32), jax.ShapeDtypeStruct((m, D_MODEL), BF16)],
        grid=(m // tm,),
        in_specs=[row_spec(ATTN_WIDTH), row_spec(GMLP_WIDTH), pl.BlockSpec(memory_space=pl.ANY),
                  _const_spec(wp["w_out"].shape), _const_spec(wp["gffn"].shape)],
        out_specs=[row_spec(D_MODEL), row_spec(D_MODEL)],
        scratch_shapes=[pltpu.VMEM((X_SLOTS, tm, D_MODEL), F32), pltpu.SemaphoreType.DMA((X_SLOTS,))],
        compiler_params=pltpu.CompilerParams(dimension_semantics=("arbitrary",),
                                             vmem_limit_bytes=VMEM_LIMIT),
        name="outproj",
    )(an, gm, x2d, wp["w_out"], wp["gffn"])


def _ffn_kernel(h_ref, hn_ref, wu_ref, wd_ref, y_ref):
    f = pl.program_id(1)
    slab = h_ref.shape[1]

    def down():
        a = jnp.maximum(_dot(hn_ref[...], wu_ref[...]), 0.0)
        return _dot((a * a).astype(BF16), wd_ref[...])

    @pl.when(f == 0)
    def _():
        d = down()
        y_ref[:, :slab] = d[:, :slab] + h_ref[...]
        y_ref[:, slab:] = d[:, slab:]

    @pl.when(f > 0)
    def _():
        cols = pl.ds(pl.multiple_of(f * slab, slab), slab)
        y_ref[:, cols] += h_ref[...]
        y_ref[...] += down()


def _ffn(h, hn, wp, tm, tf):
    m = h.shape[0]
    nf = D_FF // tf
    return pl.pallas_call(
        _ffn_kernel,
        out_shape=jax.ShapeDtypeStruct((m, D_MODEL), F32),
        grid=(m // tm, nf),
        in_specs=[pl.BlockSpec((tm, D_MODEL // nf), lambda i, f: (i, f)),
                  pl.BlockSpec((tm, D_MODEL), lambda i, f: (i, 0)),
                  pl.BlockSpec((D_MODEL, tf), lambda i, f: (0, f)),
                  pl.BlockSpec((tf, D_MODEL), lambda i, f: (f, 0))],
        out_specs=pl.BlockSpec((tm, D_MODEL), lambda i, f: (i, 0)),
        compiler_params=pltpu.CompilerParams(dimension_semantics=("parallel", "arbitrary"),
                                             vmem_limit_bytes=VMEM_LIMIT),
        name="ffn",
    )(h, hn, wp["w_up"], wp["w_down"])


def _rope_tables(pos):
    inv = ROPE_THETA ** (-np.arange(HALF_ROPE, dtype=np.float64) / HALF_ROPE)
    ang = np.asarray(pos, np.float64)[:, None] * inv[None, :]
    cos, sin = np.cos(ang), np.sin(ang)
    z = np.zeros_like(cos)
    z2 = np.zeros((ang.shape[0], LANES - ROPE_DIM))
    tabs = (np.concatenate([cos, cos, z2], axis=1), np.concatenate([-sin, z, z2], axis=1),
            np.concatenate([z, sin, z2], axis=1))
    tabs_t = (cos.T, sin.T)
    return tuple(jnp.asarray(t, F32) for t in tabs), tuple(jnp.asarray(t, F32) for t in tabs_t)


def _tile_lanes(x, width):
    if width == LANES:
        return x
    lane = lax.broadcasted_iota(jnp.int32, x.shape, 1)
    x = jnp.where(lane < width, x, 0.0)
    out = x
    for r in range(1, LANES // width):
        out = out + pltpu.roll(x, r * width, 1)
    return out


def _pad_w_in_kernel(dec_seq, wt_ref, uq_ref, uk_ref, uv_ref, wsp_ref, bsp_ref, qn_ref, qr_ref, kn_ref,
                     kr_ref, o_ref, wq_ref, wqt_ref, wuk_ref, wukt_ref, wuv_ref, wuvt_ref, wss_ref,
                     bstp_ref, bsts_ref, gq_ref, gqt_ref, pad_scr):
    tr = wt_ref.shape[1]
    split = C_PE0 + ROPE_DIM
    o_ref[:, :C_PE0] = wt_ref[:C_PE0, :].T.astype(BF16)
    pe = wt_ref[C_PE0:C_PE1, :].T
    lane = lax.broadcasted_iota(jnp.int32, (tr, LANES), 1)
    o_ref[:, C_PE0:C_PE1] = jnp.where(lane < ROPE_DIM, pe, 0.0).astype(BF16)
    o_ref[:, C_PE1:] = wt_ref[split:, :].T.astype(BF16)

    i = pl.program_id(0)

    @pl.when(i == 0)
    def _():
        uq_t = uq_ref[...].T
        for h in range(N_HEADS):
            lo = h * HEAD_PAD
            pad_scr[lo:lo + QK_DIM, :] = uq_t[h * QK_DIM:(h + 1) * QK_DIM, :]
            pad_scr[lo + QK_DIM:lo + HEAD_PAD, :] = jnp.zeros((HEAD_PAD - QK_DIM, Q_RANK), F32)
        wqt_ref[...] = pad_scr[...].astype(BF16)

    @pl.when(i == 1)
    def _():
        wq_ref[...] = pad_scr[...].T.astype(BF16)

    for step, (src, dst, dst_t) in enumerate(((uk_ref, wuk_ref, wukt_ref), (uv_ref, wuv_ref, wuvt_ref))):
        @pl.when(i == 2 + step)
        def _(src=src, dst=dst, dst_t=dst_t):
            w = src[...]
            dst[...] = w.astype(BF16)
            dst_t[...] = w.T.astype(BF16)

    @pl.when(i == 4)
    def _():
        for g in range(GMLP_GROUPS):
            w = _tile_lanes(wsp_ref[g], dec_seq)
            wss_ref[g] = jnp.concatenate([w[:dec_seq]] * (GMLP_CHUNK // dec_seq), axis=0)
        b = bsp_ref[...]
        fill = jnp.zeros((LANES - GMLP_GROUPS, LANES), F32)
        for dst, width in ((bstp_ref, GMLP_CHUNK), (bsts_ref, dec_seq)):
            dst[...] = jnp.concatenate([_tile_lanes(b, width), fill], axis=0).T

    @pl.when(i == 5)
    def _():
        scale = QK_DIM ** -0.5 * math.log2(math.e)
        nope = (qn_ref[...] * scale) * kn_ref[...]
        rope = (qr_ref[...] * scale) * kr_ref[...]
        rope = jnp.concatenate([rope, rope, jnp.zeros((1, HEAD_PAD - QK_DIM), F32)], axis=1)
        for half, g in enumerate((nope, rope)):
            cols = slice(half * LANES, (half + 1) * LANES)
            gq_ref[:, cols] = g
            col = jnp.broadcast_to(g, (LANES, LANES)).T
            gqt_ref[cols, :] = jnp.concatenate([col] * (gqt_ref.shape[1] // LANES), axis=1)


def _pad_w_in(w_in_t, w_uq, w_uk, w_uv, w_spatial, b_spatial, q_nope, q_rope, k_nope, k_rope, dec_seq,
              tr):
    width, rows = w_in_t.shape
    assert width + LANES - ROPE_DIM == IN_PAD and rows // tr >= 6
    assert w_uq.shape == (Q_RANK, N_HEADS * QK_DIM) and GMLP_CHUNK == LANES
    assert w_spatial.shape == (GMLP_GROUPS, GMLP_CHUNK, GMLP_CHUNK)
    small = (w_uq, w_uk, w_uv, w_spatial, b_spatial, q_nope[None, :], q_rope[None, :],
             k_nope[None, :], k_rope[None, :])
    outs = [(Q_RANK, QK_WIDTH), (QK_WIDTH, Q_RANK), w_uk.shape, w_uk.shape[::-1], w_uv.shape,
            w_uv.shape[::-1]]
    outs_f32 = [w_spatial.shape, (LANES, LANES), (LANES, LANES), (1, HEAD_PAD), (HEAD_PAD, PROJ_ROWS)]
    return pl.pallas_call(
        functools.partial(_pad_w_in_kernel, dec_seq),
        out_shape=[jax.ShapeDtypeStruct((rows, IN_PAD), BF16)]
        + [jax.ShapeDtypeStruct(s, BF16) for s in outs]
        + [jax.ShapeDtypeStruct(s, F32) for s in outs_f32],
        grid=(rows // tr,),
        in_specs=[pl.BlockSpec((width, tr), lambda i: (0, i))] + [_const_spec(w.shape) for w in small],
        out_specs=[pl.BlockSpec((tr, IN_PAD), lambda i: (i, 0))]
        + [_const_spec(s) for s in outs + outs_f32],
        scratch_shapes=[pltpu.VMEM((QK_WIDTH, Q_RANK), F32)],
        compiler_params=pltpu.CompilerParams(dimension_semantics=("arbitrary",),
                                             vmem_limit_bytes=VMEM_LIMIT),
        name="pad_w_in",
    )(w_in_t, *small)


def _prep_weights(norm_mix, w_in, q_lat_norm, kv_lat_norm, w_uq, w_uk, w_uv, q_norm_nope, q_norm_rope,
                  k_norm_nope, k_norm_rope, v_norm, w_spatial, b_spatial, out_norm_attn, out_norm_gmlp,
                  norm_ffn, dec_seq):
    w_in_p, wq, wq_t, wuk, wuk_t, wuv, wuv_t, ws_s, bst, bst_s, gq, gq_t = _pad_w_in(
        w_in.T, w_uq, w_uk, w_uv, w_spatial, b_spatial, q_norm_nope, q_norm_rope, k_norm_nope,
        k_norm_rope, dec_seq, PAD_ROWS)
    return {
        "gmix": norm_mix[None, :], "w_in": w_in_p, "gql": q_lat_norm[None, :], "gkv": kv_lat_norm[None, :],
        "w_uq": wq, "w_uq_t": wq_t, "w_uk": wuk, "w_uk_t": wuk_t, "w_uv": wuv, "w_uv_t": wuv_t,
        "gq": gq, "gq_t": gq_t, "v_norm": v_norm, "ws": w_spatial, "bst": bst, "ws_s": ws_s,
        "bst_s": bst_s, "gog": out_norm_gmlp[None, :], "goa": out_norm_attn[None, :],
        "gffn": norm_ffn[None, :],
    }


def _finish(an, gm, x2d, wp):
    h, hn = _outproj(an, gm, x2d, wp, OUTPROJ_ROWS)
    return _ffn(h, hn, wp, FFN_ROWS, FFN_COLS)


def kernel(x_prompt, x_sample, cache_c_kv, cache_k_rope, norm_mix, w_in, q_lat_norm, kv_lat_norm, w_uq, w_uk, w_uv, q_norm_nope, q_norm_rope, k_norm_nope, k_norm_rope, v_norm, w_spatial, b_spatial, out_norm_attn, out_norm_gmlp, w_out, norm_ffn, w_up, w_down):
    depth = w_in.shape[0]
    assert depth == 1
    batch, seq, _ = x_prompt.shape
    dec_batch, dec_seq, _ = x_sample.shape
    past = cache_c_kv.shape[2]
    assert past % CHUNK == 0 and dec_seq <= CHUNK and GMLP_CHUNK % dec_seq == 0
    n_prompt, n_sample = batch * seq, dec_batch * dec_seq
    assert seq % PROJ_ROWS == 0 and n_sample % PROJ_ROWS == 0 and PROJ_ROWS % dec_seq == 0
    assert n_prompt % FFN_ROWS == 0 and n_sample % FFN_ROWS == 0 and dec_batch % SAMPLE_BATCH_ROWS == 0

    weights = (norm_mix[0], w_in[0], q_lat_norm[0], kv_lat_norm[0], w_uq[0], w_uk[0], w_uv[0],
               q_norm_nope[0], q_norm_rope[0], k_norm_nope[0], k_norm_rope[0], v_norm[0], w_spatial[0],
               b_spatial[0], out_norm_attn[0], out_norm_gmlp[0], norm_ffn[0])
    wp = _prep_weights(*weights, dec_seq=dec_seq)

    xp = x_prompt.reshape(batch * seq, D_MODEL)
    tabs_p, tabs_pt = _rope_tables(np.arange(seq))
    qt, k, vt, ckv_p, kpe_p, gm, wo_b, wu_b, wd_b = _proj(
        xp, tabs_p, wp, GMLP_CHUNK, PROJ_ROWS, f32_weights=(w_out[0], w_up[0], w_down[0]),
        tabs_t=tabs_pt)
    wp = dict(wp, w_out=wo_b, w_up=wu_b, w_down=wd_b)
    wp_s = dict(wp, ws=wp["ws_s"], bst=wp["bst_s"])
    an = _attn_prompt(qt, k, vt, wp["goa"], batch, seq, PROJ_ROWS)
    y_p = _finish(an, gm, xp, wp)

    xs = x_sample.reshape(dec_batch * dec_seq, D_MODEL)
    tabs_s, _ = _rope_tables(past + np.arange(PROJ_ROWS) % dec_seq)
    qs, kn, vn, ckv_s, kpe_s, gms, vg_s = _proj(xs, tabs_s, wp_s, dec_seq, PROJ_ROWS)
    ans = _attn_sample(cache_c_kv[0].reshape(dec_batch * past, KV_RANK),
                       jnp.swapaxes(cache_k_rope[0], 1, 2),
                       qs, kn, vn, wp, dec_batch, dec_seq, past, SAMPLE_BATCH_ROWS)
    y_s = _finish(ans, gms, xs, wp)

    return (y_p.reshape(batch, seq, D_MODEL),
            y_s.reshape(dec_batch, dec_seq, D_MODEL),
            ckv_p.reshape(1, batch, seq, KV_RANK),
            jnp.swapaxes(kpe_p, 1, 2)[None],
            ckv_s.reshape(1, dec_batch, dec_seq, KV_RANK),
            kpe_s.reshape(1, dec_batch, dec_seq, ROPE_DIM),
            vg_s.reshape(1, dec_batch, dec_seq, GMLP_WIDTH))
```

```python
import functools
import math

import jax
import jax.numpy as jnp
import numpy as np
from jax import lax
from jax.experimental import pallas as pl
from jax.experimental.pallas import tpu as pltpu

D_MODEL = 2048
CHUNK = 64
N_HEADS = 8
NOPE_DIM = 128
ROPE_DIM = 64
HALF_ROPE = ROPE_DIM // 2
QK_DIM = NOPE_DIM + ROPE_DIM
V_DIM = 128
ATTN_WIDTH = N_HEADS * V_DIM
Q_RANK = 512
KV_RANK = 256
GMLP_GROUPS = 8
GMLP_GROUP_DIM = 128
GMLP_WIDTH = GMLP_GROUPS * GMLP_GROUP_DIM
GMLP_CHUNK = 128
D_FF = 4 * D_MODEL
ROPE_THETA = 10000.0
EPS = 1e-6

LANES = 128
SUBLANES = 8
BF16_SUBLANES = 2 * SUBLANES
HEAD_PAD = 2 * LANES
QK_WIDTH = N_HEADS * HEAD_PAD
C_Q0, C_Q1 = 0, Q_RANK
C_KV0, C_KV1 = C_Q1, C_Q1 + KV_RANK
C_PE0, C_PE1 = C_KV1, C_KV1 + LANES
C_U0, C_U1 = C_PE1, C_PE1 + GMLP_WIDTH
C_V0, C_V1 = C_U1, C_U1 + GMLP_WIDTH
IN_PAD = C_V1
VMEM_LIMIT = 56 * 1024 * 1024

PROJ_ROWS = 256
HEAD_ROWS = 128
PAD_ROWS = 256
OUTPROJ_ROWS = 512
FFN_ROWS = 1024
FFN_COLS = 1024
SAMPLE_BATCH_ROWS = 8

BF16 = jnp.bfloat16
F32 = jnp.float32


def _dot(a, b):
    return jnp.dot(a, b, preferred_element_type=F32)


def _dot_nt(a, b):
    return lax.dot_general(a, b, (((1,), (1,)), ((), ())), preferred_element_type=F32)


def _sum_lanes(x):
    return jnp.sum(x, axis=-1, keepdims=True)


def _sumsq(x):
    return _sum_lanes(x * x)


def _rms(x):
    return x * lax.rsqrt(jnp.mean(x * x, axis=-1, keepdims=True) + EPS)


def _gelu(x):
    c = math.sqrt(2.0 / math.pi)
    return 0.5 * x * (1.0 + jnp.tanh(c * (x + 0.044715 * (x * x * x))))


def _rope(t, cos, s1, s2):
    return t * cos + pltpu.roll(t, LANES - HALF_ROPE, 1) * s1 + pltpu.roll(t, HALF_ROPE, 1) * s2


def _const_spec(shape):
    nd = len(shape)
    return pl.BlockSpec(shape, lambda *_: (0,) * nd, pipeline_mode=pl.Buffered(1))


def _proj_kernel(chunk_len, prompt, x_ref, gmix_ref, win_ref, gql_ref, gkv_ref, wuq_ref, wuk_ref,
                 wuv_ref, gq_ref, cos_ref, s1_ref, s2_ref, vn_ref, ws_ref, bst_ref, gog_ref, *rest):
    v_transposed = prompt
    emit_v = not prompt
    if prompt:
        (cost_ref, sint_ref, wo_ref, wu_ref, wd_ref, q_ref, k_ref, v_ref, ckv_ref, kpe_ref, gm_ref,
         wo_out, wu_out, wd_out, gate_scr, vgb_scr) = rest
        wo_out[...] = wo_ref[...].astype(BF16)
        wu_out[...] = wu_ref[...].astype(BF16)
        wd_out[...] = wd_ref[...].astype(BF16)
    else:
        q_ref, k_ref, v_ref, ckv_ref, kpe_ref, gm_ref, vg_ref, gate_scr, vgb_scr = rest
    tm = x_ref.shape[0]
    xn = (_rms(x_ref[...]) * gmix_ref[...]).astype(BF16)
    cos, s1, s2 = cos_ref[...], s1_ref[...], s2_ref[...]

    zq = _dot(xn, win_ref[:, C_Q0:C_Q1])
    qln = (_rms(zq) * gql_ref[...]).astype(BF16)
    if prompt:
        qt = _dot_nt(wuq_ref[...], qln)
        cos_t, sin_t = cost_ref[...], sint_ref[...]
        x1_lo, x2_lo = NOPE_DIM, NOPE_DIM + HALF_ROPE
        zeros = jnp.zeros((HEAD_PAD - QK_DIM, tm), BF16)
        for h in range(N_HEADS):
            lo = h * HEAD_PAD
            nope = qt[lo:lo + NOPE_DIM]
            x1 = qt[lo + x1_lo:lo + x2_lo]
            x2 = qt[lo + x2_lo:lo + QK_DIM]
            ss = (jnp.sum(nope * nope, axis=0, keepdims=True)
                  + jnp.sum(x1 * x1 + x2 * x2, axis=0, keepdims=True))
            r = lax.rsqrt(ss * (1.0 / QK_DIM) + EPS)
            q_ref[lo:lo + NOPE_DIM, :] = (nope * r * gq_ref[:NOPE_DIM]).astype(BF16)
            q_ref[lo + x1_lo:lo + x2_lo, :] = (
                (x1 * cos_t - x2 * sin_t) * r * gq_ref[x1_lo:x2_lo]).astype(BF16)
            q_ref[lo + x2_lo:lo + QK_DIM, :] = (
                (x2 * cos_t + x1 * sin_t) * r * gq_ref[x2_lo:QK_DIM]).astype(BF16)
            q_ref[lo + QK_DIM:lo + HEAD_PAD, :] = zeros
    else:
        qraw = _dot(qln, wuq_ref[...])
        for h in range(N_HEADS):
            lo = h * HEAD_PAD
            for rc in range(tm // HEAD_ROWS):
                rows = slice(rc * HEAD_ROWS, (rc + 1) * HEAD_ROWS)
                nope = qraw[rows, lo:lo + LANES]
                rp = qraw[rows, lo + LANES:lo + HEAD_PAD]
                r = lax.rsqrt(_sum_lanes(nope * nope + rp * rp) * (1.0 / QK_DIM) + EPS)
                rp = _rope(rp, cos[rows], s1[rows], s2[rows])
                q_ref[rows, lo:lo + LANES] = (nope * r * gq_ref[:, :LANES]).astype(BF16)
                q_ref[rows, lo + LANES:lo + HEAD_PAD] = (rp * r * gq_ref[:, LANES:]).astype(BF16)

    ckv = _rms(_dot(xn, win_ref[:, C_KV0:C_KV1])) * gkv_ref[...]
    ckv_ref[...] = ckv
    cb = ckv.astype(BF16)
    pe = _rope(_dot(xn, win_ref[:, C_PE0:C_PE1]), cos, s1, s2)
    kpe_ref[...] = pe.T[:ROPE_DIM, :] if prompt else pe[:, :ROPE_DIM]
    ss_pe = _sumsq(pe)
    knope = _dot(cb, wuk_ref[...])
    if v_transposed:
        v_ref[...] = _dot_nt(wuv_ref[...], cb).astype(BF16)
    else:
        v_ref[...] = _dot(cb, wuv_ref[...]).astype(BF16)
    for h in range(N_HEADS):
        lo = h * HEAD_PAD
        for rc in range(tm // HEAD_ROWS):
            rows = slice(rc * HEAD_ROWS, (rc + 1) * HEAD_ROWS)
            nope = knope[rows, h * LANES:(h + 1) * LANES]
            r = lax.rsqrt((_sumsq(nope) + ss_pe[rows]) * (1.0 / QK_DIM) + EPS)
            k_ref[rows, lo:lo + LANES] = (nope * r).astype(BF16)
            k_ref[rows, lo + LANES:lo + HEAD_PAD] = (pe[rows] * r).astype(BF16)

    gv = _gelu(_dot(xn, win_ref[:, C_V0:C_V1]))
    for g in range(GMLP_GROUPS):
        cols = slice(g * LANES, (g + 1) * LANES)
        for rc in range(tm // HEAD_ROWS):
            rows = slice(rc * HEAD_ROWS, (rc + 1) * HEAD_ROWS)
            vg = _rms(gv[rows, cols]) * vn_ref[g:g + 1, :]
            if emit_v:
                vg_ref[rows, cols] = vg
            vgb_scr[rows, cols] = vg.astype(BF16)
    u = _gelu(_dot(xn, win_ref[:, C_U0:C_U1]))
    row = lax.broadcasted_iota(jnp.int32, (GMLP_CHUNK, GMLP_CHUNK), 0)
    col = lax.broadcasted_iota(jnp.int32, (GMLP_CHUNK, GMLP_CHUNK), 1)
    causal = (row // chunk_len == col // chunk_len) & (col <= row)
    for g in range(GMLP_GROUPS):
        wm = jnp.where(causal, ws_ref[g], 0.0).astype(BF16)
        bias = bst_ref[:, g:g + 1]
        for c in range(tm // GMLP_CHUNK):
            rows = slice(c * GMLP_CHUNK, (c + 1) * GMLP_CHUNK)
            cols = slice(g * LANES, (g + 1) * LANES)
            s = _dot(wm, vgb_scr[rows, cols]) + bias
            gate_scr[rows, cols] = u[rows, cols] * s
    gm_ref[...] = (_rms(gate_scr[...]) * gog_ref[...]).astype(BF16)


def _proj(x2d, tabs, wp, chunk_len, tm, f32_weights=None, tabs_t=None):
    prompt = f32_weights is not None
    m = x2d.shape[0]
    steps = m // tm
    cos, s1, s2 = tabs
    tab_blocks = cos.shape[0] // tm
    row_spec = lambda w: pl.BlockSpec((tm, w), lambda i: (i, 0))
    tab_spec = pl.BlockSpec((tm, LANES), lambda i: (i % tab_blocks, 0))
    consts = [wp["gmix"], wp["w_in"], wp["gql"], wp["gkv"], wp["w_uq_t"] if prompt else wp["w_uq"],
              wp["w_uk"], wp["w_uv_t"] if prompt else wp["w_uv"], wp["gq_t"] if prompt else wp["gq"]]
    consts2 = [wp["v_norm"], wp["ws"], wp["bst"], wp["gog"]]
    in_specs = ([row_spec(D_MODEL)] + [_const_spec(c.shape) for c in consts]
                + [tab_spec] * 3 + [_const_spec(c.shape) for c in consts2])
    q_shape = (steps * QK_WIDTH, tm) if prompt else (m, QK_WIDTH)
    q_spec = pl.BlockSpec((QK_WIDTH, tm), lambda i: (i, 0)) if prompt else row_spec(QK_WIDTH)
    v_shape = (steps * ATTN_WIDTH, tm) if prompt else (m, ATTN_WIDTH)
    v_spec = pl.BlockSpec((ATTN_WIDTH, tm), lambda i: (i, 0)) if prompt else row_spec(ATTN_WIDTH)
    kpe_shape = (m // cos.shape[0], ROPE_DIM, cos.shape[0]) if prompt else (m, ROPE_DIM)
    kpe_spec = (pl.BlockSpec((None, ROPE_DIM, tm), lambda i: (i // tab_blocks, 0, i % tab_blocks))
                if prompt else row_spec(ROPE_DIM))
    out_shape = [jax.ShapeDtypeStruct(q_shape, BF16), jax.ShapeDtypeStruct((m, QK_WIDTH), BF16),
                 jax.ShapeDtypeStruct(v_shape, BF16), jax.ShapeDtypeStruct((m, KV_RANK), F32),
                 jax.ShapeDtypeStruct(kpe_shape, F32), jax.ShapeDtypeStruct((m, GMLP_WIDTH), BF16)]
    out_specs = [q_spec, row_spec(QK_WIDTH), v_spec, row_spec(KV_RANK),
                 kpe_spec, row_spec(GMLP_WIDTH)]
    extra_in = []
    if prompt:
        tab_t_spec = pl.BlockSpec((HALF_ROPE, tm), lambda i: (0, i % tab_blocks))
        w_out, w_up, w_down = f32_weights
        slabs = [pl.BlockSpec((D_MODEL // steps, D_MODEL), lambda i: (i, 0)),
                 pl.BlockSpec((D_MODEL, D_FF // steps), lambda i: (0, i)),
                 pl.BlockSpec((D_FF // steps, D_MODEL), lambda i: (i, 0))]
        extra_in = [*tabs_t, w_out, w_up, w_down]
        in_specs += [tab_t_spec, tab_t_spec] + slabs
        out_shape += [jax.ShapeDtypeStruct(w.shape, BF16) for w in f32_weights]
        out_specs += slabs
    else:
        out_shape.append(jax.ShapeDtypeStruct((m, GMLP_WIDTH), F32))
        out_specs.append(row_spec(GMLP_WIDTH))
    return pl.pallas_call(
        functools.partial(_proj_kernel, chunk_len, prompt),
        out_shape=out_shape,
        grid=(steps,),
        in_specs=in_specs,
        out_specs=out_specs,
        scratch_shapes=[pltpu.VMEM((tm, GMLP_WIDTH), F32), pltpu.VMEM((tm, GMLP_WIDTH), BF16)],
        compiler_params=pltpu.CompilerParams(dimension_semantics=("parallel",),
                                             vmem_limit_bytes=VMEM_LIMIT),
        name="proj",
    )(x2d, *consts, cos, s1, s2, *consts2, *extra_in)


NEG = -1e30
SOFTMAX_KEY_CHUNK = 64
KEY_BLOCKS_PER_STEP = 2
SCORE_BANKS = 3
SCORE_LEAD = 5


def _attn_prompt_kernel(q_ref, k_ref, vt_ref, go_ref, o_ref, acc_scr, m_scr, l_scr, p_scr, *s_scrs):
    t = q_ref.shape[1]
    qi = pl.program_id(1)
    zero = jnp.minimum(qi, 0)
    m_scr[...] = jnp.full(m_scr.shape, NEG, F32)
    l_scr[...] = jnp.zeros(l_scr.shape, F32)
    acc_scr[...] = jnp.zeros(acc_scr.shape, F32)

    def step(kb0, nblk, last_masked):
        nk = nblk * t
        if last_masked:
            key = lax.broadcasted_iota(jnp.int32, (t, t), 0)
            qry = lax.broadcasted_iota(jnp.int32, (t, t), 1)
            ok = (key // CHUNK) <= (qry // CHUNK)

        m8 = {}

        def scores(h):
            qs = slice(h * HEAD_PAD, (h + 1) * HEAD_PAD)
            mh = None
            for j in range(nblk):
                rows = pl.ds(pl.multiple_of((kb0 + j) * t, t), t)
                s = _dot(k_ref[rows, qs], q_ref[qs, :])
                if last_masked and j == nblk - 1:
                    s = jnp.where(ok, s, NEG)
                s_scrs[h % SCORE_BANKS][h // SCORE_BANKS, j * t:(j + 1) * t, :] = s
                mj = jnp.max(s.reshape(t // SUBLANES, SUBLANES, t), axis=0)
                mh = mj if mh is None else jnp.maximum(mh, mj)
            m8[h] = mh

        ones = jnp.ones((BF16_SUBLANES, nk), BF16)
        for h in range(min(SCORE_LEAD, N_HEADS)):
            scores(h)
        for h in range(N_HEADS):
            if h + SCORE_LEAD < N_HEADS:
                scores(h + SCORE_LEAD)
            hs = slice(h * V_DIM, (h + 1) * V_DIM)
            m_old = m_scr[h]
            m_new = jnp.maximum(m_old, jnp.max(m8[h], axis=0, keepdims=True))
            alpha = jnp.exp2(m_old - m_new)
            for c in range(nk // SOFTMAX_KEY_CHUNK):
                cs = slice(c * SOFTMAX_KEY_CHUNK, (c + 1) * SOFTMAX_KEY_CHUNK)
                s_c = s_scrs[h % SCORE_BANKS][h // SCORE_BANKS + zero, cs, :]
                p_scr[h, cs, :] = jnp.exp2(s_c - m_new).astype(BF16)
            m_scr[h] = m_new
            vt = [vt_ref[pl.ds(pl.multiple_of((kb0 + j) * ATTN_WIDTH, ATTN_WIDTH) + h * V_DIM, V_DIM), :]
                  for j in range(nblk)]
            vt = vt[0] if nblk == 1 else jnp.concatenate(vt, axis=1)
            pv = _dot(jnp.concatenate([vt, ones], axis=0), p_scr[h, :nk, :])
            acc_scr[hs, :] = alpha * acc_scr[hs, :] + pv[:V_DIM]
            l_scr[h] = alpha * l_scr[h] + pv[V_DIM:V_DIM + 1]

    def body(j, carry):
        step(j * KEY_BLOCKS_PER_STEP, KEY_BLOCKS_PER_STEP, False)
        return carry

    lax.fori_loop(0, qi // KEY_BLOCKS_PER_STEP, body, 0)
    for rem in range(KEY_BLOCKS_PER_STEP):
        @pl.when(qi % KEY_BLOCKS_PER_STEP == rem)
        def _(rem=rem):
            step(qi - rem, rem + 1, True)

    for h in range(N_HEADS):
        hs = slice(h * V_DIM, (h + 1) * V_DIM)
        acc_scr[hs, :] = acc_scr[hs, :] / l_scr[h]
    o = acc_scr[...].T
    o_ref[...] = (_rms(o) * go_ref[...]).astype(BF16)


def _attn_prompt(q, k, vt, go, batch, seq, t):
    nq = seq // t
    assert vt.shape == (batch * nq * ATTN_WIDTH, t)
    tk = KEY_BLOCKS_PER_STEP * t
    return pl.pallas_call(
        _attn_prompt_kernel,
        out_shape=jax.ShapeDtypeStruct((batch * seq, ATTN_WIDTH), BF16),
        grid=(batch, nq),
        in_specs=[pl.BlockSpec((QK_WIDTH, t), lambda b, i: (b * nq + i, 0)),
                  pl.BlockSpec((seq, QK_WIDTH), lambda b, i: (b, 0)),
                  pl.BlockSpec((nq * ATTN_WIDTH, t), lambda b, i: (b, 0)),
                  _const_spec(go.shape)],
        out_specs=pl.BlockSpec((t, ATTN_WIDTH), lambda b, i: (b * nq + i, 0)),
        scratch_shapes=[pltpu.VMEM((ATTN_WIDTH, t), F32), pltpu.VMEM((N_HEADS, 1, t), F32),
                        pltpu.VMEM((N_HEADS, 1, t), F32), pltpu.VMEM((N_HEADS, tk, t), BF16)]
        + [pltpu.VMEM((-(-N_HEADS // SCORE_BANKS), tk, t), F32)] * SCORE_BANKS,
        compiler_params=pltpu.CompilerParams(dimension_semantics=("parallel", "arbitrary"),
                                             vmem_limit_bytes=VMEM_LIMIT),
        name="attn_prompt",
    )(q, k, vt, go)


def _attn_sample_kernel(past, c_ref, krt_ref, q_ref, kn_ref, vn_ref, wukt_ref, wuv_ref, go_ref,
                        o_ref, cb_scr, qa_scr, qr_scr, r_scr, sc_scr, sn_scr, pc_scr, acc_scr,
                        lat_scr, pn_scr):
    nb = krt_ref.shape[0]
    t = q_ref.shape[0] // nb
    row = lax.broadcasted_iota(jnp.int32, (nb * t, nb * t), 0)
    col = lax.broadcasted_iota(jnp.int32, (nb * t, nb * t), 1)
    new_ok = (row // t == col // t) & (((past + col % t) // CHUNK) <= ((past + row % t) // CHUNK))
    denom = [[None] * N_HEADS for _ in range(nb)]

    def norm_factors(j):
        cb_scr[j] = c_ref[j * past:(j + 1) * past, :].astype(BF16)
        krt = krt_ref[j]
        ss_pe = jnp.sum(krt * krt, axis=0, keepdims=True)
        knt = _dot_nt(wukt_ref[...], cb_scr[j])
        for h in range(N_HEADS):
            nope = knt[h * LANES:(h + 1) * LANES, :]
            ss = jnp.sum(nope * nope, axis=0, keepdims=True) + ss_pe
            r_scr[j, h] = lax.rsqrt(ss * (1.0 / QK_DIM) + EPS)

    def queries():
        for h in range(N_HEADS):
            lo = h * HEAD_PAD
            qa = _dot(q_ref[:, lo:lo + NOPE_DIM],
                      wukt_ref[h * NOPE_DIM:(h + 1) * NOPE_DIM, :]).astype(BF16)
            for j in range(nb):
                rows = slice(h * t, (h + 1) * t)
                qa_scr[j, rows, :] = qa[j * t:(j + 1) * t, :]
                qr_scr[j, rows, :] = q_ref[j * t:(j + 1) * t, lo + NOPE_DIM:lo + QK_DIM]
            sn = _dot_nt(q_ref[:, lo:lo + HEAD_PAD], kn_ref[:, lo:lo + HEAD_PAD])
            sn_scr[h] = jnp.where(new_ok, sn, NEG)

    def scores(j):
        sc_scr[j] = _dot_nt(qa_scr[j], cb_scr[j]) + _dot(qr_scr[j], krt_ref[j].astype(BF16))

    def softmax(j):
        for h in range(N_HEADS):
            s_c = sc_scr[j, h * t:(h + 1) * t, :] * r_scr[j, h]
            s_n = sn_scr[h, j * t:(j + 1) * t, :]
            m = jnp.maximum(jnp.max(s_c, axis=-1, keepdims=True), jnp.max(s_n, axis=-1, keepdims=True))
            p_c = jnp.exp2(s_c - m)
            p_n = jnp.exp2(s_n - m)
            denom[j][h] = jnp.sum(p_c, axis=-1, keepdims=True) + jnp.sum(p_n, axis=-1, keepdims=True)
            pn_scr[h, j * t:(j + 1) * t, :] = p_n.astype(BF16)
            pc_scr[j, h * t:(h + 1) * t, :] = p_c.astype(BF16)

    def values(j):
        lat = _dot(pc_scr[j], cb_scr[j]).astype(BF16)
        for h in range(N_HEADS):
            lat_scr[h, j * t:(j + 1) * t, :] = lat[h * t:(h + 1) * t, :]

    def outputs():
        for h in range(N_HEADS):
            vs = slice(h * V_DIM, (h + 1) * V_DIM)
            acc = _dot(lat_scr[h], wuv_ref[:, vs]) + _dot(pn_scr[h], vn_ref[:, vs])
            for j in range(nb):
                acc_scr[j, :, vs] = acc[j * t:(j + 1) * t, :] / denom[j][h]
        for j in range(nb):
            o_ref[j * t:(j + 1) * t, :] = (_rms(acc_scr[j]) * go_ref[...]).astype(BF16)

    stages = (norm_factors, scores, softmax, values)
    queries()
    for tick in range(nb + len(stages) - 1):
        for j in range(nb):
            if 0 <= tick - j < len(stages):
                stages[tick - j](j)
    outputs()


def _attn_sample(c2d, krt, q, kn, vn, wp, batch, t, past, nb):
    consts = [wp["w_uk_t"], wp["w_uv"], wp["goa"]]
    ht = N_HEADS * t
    return pl.pallas_call(
        functools.partial(_attn_sample_kernel, past),
        out_shape=jax.ShapeDtypeStruct((batch * t, ATTN_WIDTH), BF16),
        grid=(batch // nb,),
        in_specs=[pl.BlockSpec((nb * past, KV_RANK), lambda b: (b, 0)),
                  pl.BlockSpec((nb, ROPE_DIM, past), lambda b: (b, 0, 0)),
                  pl.BlockSpec((nb * t, QK_WIDTH), lambda b: (b, 0)),
                  pl.BlockSpec((nb * t, QK_WIDTH), lambda b: (b, 0)),
                  pl.BlockSpec((nb * t, ATTN_WIDTH), lambda b: (b, 0))]
                 + [_const_spec(c.shape) for c in consts],
        out_specs=pl.BlockSpec((nb * t, ATTN_WIDTH), lambda b: (b, 0)),
        scratch_shapes=[pltpu.VMEM((nb, past, KV_RANK), BF16), pltpu.VMEM((nb, ht, KV_RANK), BF16),
                        pltpu.VMEM((nb, ht, ROPE_DIM), BF16), pltpu.VMEM((nb, N_HEADS, 1, past), F32),
                        pltpu.VMEM((nb, ht, past), F32), pltpu.VMEM((N_HEADS, nb * t, nb * t), F32),
                        pltpu.VMEM((nb, ht, past), BF16), pltpu.VMEM((nb, t, ATTN_WIDTH), F32),
                        pltpu.VMEM((N_HEADS, nb * t, KV_RANK), BF16),
                        pltpu.VMEM((N_HEADS, nb * t, nb * t), BF16)],
        compiler_params=pltpu.CompilerParams(dimension_semantics=("parallel",),
                                             vmem_limit_bytes=VMEM_LIMIT),
        name="attn_sample",
    )(c2d, krt, q, kn, vn, *consts)


def _outproj_kernel(an_ref, gm_ref, x_ref, wo_ref, gffn_ref, h_ref, hn_ref):
    h = (x_ref[...] + _dot(an_ref[...], wo_ref[:ATTN_WIDTH, :])
         + _dot(gm_ref[...], wo_ref[ATTN_WIDTH:, :]))
    h_ref[...] = h
    hn_ref[...] = (_rms(h) * gffn_ref[...]).astype(BF16)


def _outproj(an, gm, x2d, wp, tm):
    m = x2d.shape[0]
    row_spec = lambda w: pl.BlockSpec((tm, w), lambda i: (i, 0))
    return pl.pallas_call(
        _outproj_kernel,
        out_shape=[jax.ShapeDtypeStruct((m, D_MODEL), F32), jax.ShapeDtypeStruct((m, D_MODEL), BF16)],
        grid=(m // tm,),
        in_specs=[row_spec(ATTN_WIDTH), row_spec(GMLP_WIDTH), row_spec(D_MODEL),
                  _const_spec(wp["w_out"].shape), _const_spec(wp["gffn"].shape)],
        out_specs=[row_spec(D_MODEL), row_spec(D_MODEL)],
        compiler_params=pltpu.CompilerParams(dimension_semantics=("parallel",),
                                             vmem_limit_bytes=VMEM_LIMIT),
        name="outproj",
    )(an, gm, x2d, wp["w_out"], wp["gffn"])


def _ffn_kernel(h_ref, hn_ref, wu_ref, wd_ref, y_ref):
    f = pl.program_id(1)
    slab = h_ref.shape[1]

    def down():
        a = jnp.maximum(_dot(hn_ref[...], wu_ref[...]), 0.0)
        return _dot((a * a).astype(BF16), wd_ref[...])

    @pl.when(f == 0)
    def _():
        d = down()
        y_ref[:, :slab] = d[:, :slab] + h_ref[...]
        y_ref[:, slab:] = d[:, slab:]

    @pl.when(f > 0)
    def _():
        cols = pl.ds(pl.multiple_of(f * slab, slab), slab)
        y_ref[:, cols] += h_ref[...]
        y_ref[...] += down()


def _ffn(h, hn, wp, tm, tf):
    m = h.shape[0]
    nf = D_FF // tf
    return pl.pallas_call(
        _ffn_kernel,
        out_shape=jax.ShapeDtypeStruct((m, D_MODEL), F32),
        grid=(m // tm, nf),
        in_specs=[pl.BlockSpec((tm, D_MODEL // nf), lambda i, f: (i, f)),
                  pl.BlockSpec((tm, D_MODEL), lambda i, f: (i, 0)),
                  pl.BlockSpec((D_MODEL, tf), lambda i, f: (0, f)),
                  pl.BlockSpec((tf, D_MODEL), lambda i, f: (f, 0))],
        out_specs=pl.BlockSpec((tm, D_MODEL), lambda i, f: (i, 0)),
        compiler_params=pltpu.CompilerParams(dimension_semantics=("parallel", "arbitrary"),
                                             vmem_limit_bytes=VMEM_LIMIT),
        name="ffn",
    )(h, hn, wp["w_up"], wp["w_down"])


def _rope_tables(pos):
    inv = ROPE_THETA ** (-np.arange(HALF_ROPE, dtype=np.float64) / HALF_ROPE)
    ang = np.asarray(pos, np.float64)[:, None] * inv[None, :]
    cos, sin = np.cos(ang), np.sin(ang)
    z = np.zeros_like(cos)
    z2 = np.zeros((ang.shape[0], LANES - ROPE_DIM))
    tabs = (np.concatenate([cos, cos, z2], axis=1), np.concatenate([-sin, z, z2], axis=1),
            np.concatenate([z, sin, z2], axis=1))
    tabs_t = (cos.T, sin.T)
    return tuple(jnp.asarray(t, F32) for t in tabs), tuple(jnp.asarray(t, F32) for t in tabs_t)


def _tile_lanes(x, width):
    if width == LANES:
        return x
    lane = lax.broadcasted_iota(jnp.int32, x.shape, 1)
    x = jnp.where(lane < width, x, 0.0)
    out = x
    for r in range(1, LANES // width):
        out = out + pltpu.roll(x, r * width, 1)
    return out


def _pad_w_in_kernel(dec_seq, wt_ref, uq_ref, uk_ref, uv_ref, wsp_ref, bsp_ref, qn_ref, qr_ref, kn_ref,
                     kr_ref, o_ref, wq_ref, wqt_ref, wuk_ref, wukt_ref, wuv_ref, wuvt_ref, wss_ref,
                     bstp_ref, bsts_ref, gq_ref, gqt_ref, pad_scr):
    tr = wt_ref.shape[1]
    split = C_PE0 + ROPE_DIM
    o_ref[:, :C_PE0] = wt_ref[:C_PE0, :].T.astype(BF16)
    pe = wt_ref[C_PE0:C_PE1, :].T
    lane = lax.broadcasted_iota(jnp.int32, (tr, LANES), 1)
    o_ref[:, C_PE0:C_PE1] = jnp.where(lane < ROPE_DIM, pe, 0.0).astype(BF16)
    o_ref[:, C_PE1:] = wt_ref[split:, :].T.astype(BF16)

    i = pl.program_id(0)

    @pl.when(i == 0)
    def _():
        uq_t = uq_ref[...].T
        for h in range(N_HEADS):
            lo = h * HEAD_PAD
            pad_scr[lo:lo + QK_DIM, :] = uq_t[h * QK_DIM:(h + 1) * QK_DIM, :]
            pad_scr[lo + QK_DIM:lo + HEAD_PAD, :] = jnp.zeros((HEAD_PAD - QK_DIM, Q_RANK), F32)
        wqt_ref[...] = pad_scr[...].astype(BF16)

    @pl.when(i == 1)
    def _():
        wq_ref[...] = pad_scr[...].T.astype(BF16)

    for step, (src, dst, dst_t) in enumerate(((uk_ref, wuk_ref, wukt_ref), (uv_ref, wuv_ref, wuvt_ref))):
        @pl.when(i == 2 + step)
        def _(src=src, dst=dst, dst_t=dst_t):
            w = src[...]
            dst[...] = w.astype(BF16)
            dst_t[...] = w.T.astype(BF16)

    @pl.when(i == 4)
    def _():
        for g in range(GMLP_GROUPS):
            w = _tile_lanes(wsp_ref[g], dec_seq)
            wss_ref[g] = jnp.concatenate([w[:dec_seq]] * (GMLP_CHUNK // dec_seq), axis=0)
        b = bsp_ref[...]
        fill = jnp.zeros((LANES - GMLP_GROUPS, LANES), F32)
        for dst, width in ((bstp_ref, GMLP_CHUNK), (bsts_ref, dec_seq)):
            dst[...] = jnp.concatenate([_tile_lanes(b, width), fill], axis=0).T

    @pl.when(i == 5)
    def _():
        scale = QK_DIM ** -0.5 * math.log2(math.e)
        nope = (qn_ref[...] * scale) * kn_ref[...]
        rope = (qr_ref[...] * scale) * kr_ref[...]
        rope = jnp.concatenate([rope, rope, jnp.zeros((1, HEAD_PAD - QK_DIM), F32)], axis=1)
        for half, g in enumerate((nope, rope)):
            cols = slice(half * LANES, (half + 1) * LANES)
            gq_ref[:, cols] = g
            col = jnp.broadcast_to(g, (LANES, LANES)).T
            gqt_ref[cols, :] = jnp.concatenate([col] * (gqt_ref.shape[1] // LANES), axis=1)


def _pad_w_in(w_in_t, w_uq, w_uk, w_uv, w_spatial, b_spatial, q_nope, q_rope, k_nope, k_rope, dec_seq,
              tr):
    width, rows = w_in_t.shape
    assert width + LANES - ROPE_DIM == IN_PAD and rows // tr >= 6
    assert w_uq.shape == (Q_RANK, N_HEADS * QK_DIM) and GMLP_CHUNK == LANES
    assert w_spatial.shape == (GMLP_GROUPS, GMLP_CHUNK, GMLP_CHUNK)
    small = (w_uq, w_uk, w_uv, w_spatial, b_spatial, q_nope[None, :], q_rope[None, :],
             k_nope[None, :], k_rope[None, :])
    outs = [(Q_RANK, QK_WIDTH), (QK_WIDTH, Q_RANK), w_uk.shape, w_uk.shape[::-1], w_uv.shape,
            w_uv.shape[::-1]]
    outs_f32 = [w_spatial.shape, (LANES, LANES), (LANES, LANES), (1, HEAD_PAD), (HEAD_PAD, PROJ_ROWS)]
    return pl.pallas_call(
        functools.partial(_pad_w_in_kernel, dec_seq),
        out_shape=[jax.ShapeDtypeStruct((rows, IN_PAD), BF16)]
        + [jax.ShapeDtypeStruct(s, BF16) for s in outs]
        + [jax.ShapeDtypeStruct(s, F32) for s in outs_f32],
        grid=(rows // tr,),
        in_specs=[pl.BlockSpec((width, tr), lambda i: (0, i))] + [_const_spec(w.shape) for w in small],
        out_specs=[pl.BlockSpec((tr, IN_PAD), lambda i: (i, 0))]
        + [_const_spec(s) for s in outs + outs_f32],
        scratch_shapes=[pltpu.VMEM((QK_WIDTH, Q_RANK), F32)],
        compiler_params=pltpu.CompilerParams(dimension_semantics=("arbitrary",),
                                             vmem_limit_bytes=VMEM_LIMIT),
        name="pad_w_in",
    )(w_in_t, *small)


def _prep_weights(norm_mix, w_in, q_lat_norm, kv_lat_norm, w_uq, w_uk, w_uv, q_norm_nope, q_norm_rope,
                  k_norm_nope, k_norm_rope, v_norm, w_spatial, b_spatial, out_norm_attn, out_norm_gmlp,
                  norm_ffn, dec_seq):
    w_in_p, wq, wq_t, wuk, wuk_t, wuv, wuv_t, ws_s, bst, bst_s, gq, gq_t = _pad_w_in(
        w_in.T, w_uq, w_uk, w_uv, w_spatial, b_spatial, q_norm_nope, q_norm_rope, k_norm_nope,
        k_norm_rope, dec_seq, PAD_ROWS)
    return {
        "gmix": norm_mix[None, :], "w_in": w_in_p, "gql": q_lat_norm[None, :], "gkv": kv_lat_norm[None, :],
        "w_uq": wq, "w_uq_t": wq_t, "w_uk": wuk, "w_uk_t": wuk_t, "w_uv": wuv, "w_uv_t": wuv_t,
        "gq": gq, "gq_t": gq_t, "v_norm": v_norm, "ws": w_spatial, "bst": bst, "ws_s": ws_s,
        "bst_s": bst_s, "gog": out_norm_gmlp[None, :], "goa": out_norm_attn[None, :],
        "gffn": norm_ffn[None, :],
    }


def _finish(an, gm, x2d, wp):
    h, hn = _outproj(an, gm, x2d, wp, OUTPROJ_ROWS)
    return _ffn(h, hn, wp, FFN_ROWS, FFN_COLS)


def kernel(x_prompt, x_sample, cache_c_kv, cache_k_rope, norm_mix, w_in, q_lat_norm, kv_lat_norm, w_uq, w_uk, w_uv, q_norm_nope, q_norm_rope, k_norm_nope, k_norm_rope, v_norm, w_spatial, b_spatial, out_norm_attn, out_norm_gmlp, w_out, norm_ffn, w_up, w_down):
    depth = w_in.shape[0]
    assert depth == 1
    batch, seq, _ = x_prompt.shape
    dec_batch, dec_seq, _ = x_sample.shape
    past = cache_c_kv.shape[2]
    assert past % CHUNK == 0 and dec_seq <= CHUNK and GMLP_CHUNK % dec_seq == 0
    n_prompt, n_sample = batch * seq, dec_batch * dec_seq
    assert seq % PROJ_ROWS == 0 and n_sample % PROJ_ROWS == 0 and PROJ_ROWS % dec_seq == 0
    assert n_prompt % FFN_ROWS == 0 and n_sample % FFN_ROWS == 0 and dec_batch % SAMPLE_BATCH_ROWS == 0

    weights = (norm_mix[0], w_in[0], q_lat_norm[0], kv_lat_norm[0], w_uq[0], w_uk[0], w_uv[0],
               q_norm_nope[0], q_norm_rope[0], k_norm_nope[0], k_norm_rope[0], v_norm[0], w_spatial[0],
               b_spatial[0], out_norm_attn[0], out_norm_gmlp[0], norm_ffn[0])
    wp = _prep_weights(*weights, dec_seq=dec_seq)

    xp = x_prompt.reshape(batch * seq, D_MODEL)
    tabs_p, tabs_pt = _rope_tables(np.arange(seq))
    qt, k, vt, ckv_p, kpe_p, gm, wo_b, wu_b, wd_b = _proj(
        xp, tabs_p, wp, GMLP_CHUNK, PROJ_ROWS, f32_weights=(w_out[0], w_up[0], w_down[0]),
        tabs_t=tabs_pt)
    wp = dict(wp, w_out=wo_b, w_up=wu_b, w_down=wd_b)
    wp_s = dict(wp, ws=wp["ws_s"], bst=wp["bst_s"])
    an = _attn_prompt(qt, k, vt, wp["goa"], batch, seq, PROJ_ROWS)
    y_p = _finish(an, gm, xp, wp)

    xs = x_sample.reshape(dec_batch * dec_seq, D_MODEL)
    tabs_s, _ = _rope_tables(past + np.arange(PROJ_ROWS) % dec_seq)
    qs, kn, vn, ckv_s, kpe_s, gms, vg_s = _proj(xs, tabs_s, wp_s, dec_seq, PROJ_ROWS)
    ans = _attn_sample(cache_c_kv[0].reshape(dec_batch * past, KV_RANK),
                       jnp.swapaxes(cache_k_rope[0], 1, 2),
                       qs, kn, vn, wp, dec_batch, dec_seq, past, SAMPLE_BATCH_ROWS)
    y_s = _finish(ans, gms, xs, wp)

    return (y_p.reshape(batch, seq, D_MODEL),
            y_s.reshape(dec_batch, dec_seq, D_MODEL),
            ckv_p.reshape(1, batch, seq, KV_RANK),
            jnp.swapaxes(kpe_p, 1, 2)[None],
            ckv_s.reshape(1, dec_batch, dec_seq, KV_RANK),
            kpe_s.reshape(1, dec_batch, dec_seq, ROPE_DIM),
            vg_s.reshape(1, dec_batch, dec_seq, GMLP_WIDTH))
```
